```python
import jax, jax.numpy as jnp
from jax import lax
import numpy as np

D_MODEL = 2048
BATCH = 4
SEQ = 8192
DEPTH = 2

CHUNK = 64
D_MIX = D_MODEL
F32 = jnp.float32

A_HEAD_DIM = 128
A_HEADS = (D_MIX // 2) // A_HEAD_DIM
IDX_HEADS = 16
IDX_DIM = 64
IDX_W_SCALE = (IDX_HEADS * IDX_DIM) ** -0.5
TOPK_MAX = 256
Q_BLOCK = 128
ROPE_THETA = 500000.0
ROPE_FRAC = 4

R_HEADS = 4
R_V_DIM = (D_MIX // 4) // R_HEADS
R_QK_DIM = R_V_DIM // 2
R_THETA = 10000.0

W_HEAD_DIM = 64
W_HEADS = (D_MIX // 4) // W_HEAD_DIM
DECAY_LORA = 96
AAA_LORA = 96
MV_LORA = 64
GATE_LORA = 256
LNX_EPS = 64e-5

D_FF = -(-(8 * D_MODEL) // (3 * 256)) * 256

A_W = A_HEADS * A_HEAD_DIM
B_QK_W = R_HEADS * R_QK_DIM
B_V_W = R_HEADS * R_V_DIM
C_W = W_HEADS * W_HEAD_DIM
SPLIT_AB = [A_W, A_W, A_W, IDX_HEADS * IDX_DIM, IDX_DIM, IDX_HEADS, B_QK_W, B_QK_W, B_V_W, B_V_W]
SPLIT_C = [C_W, C_W, C_W, DECAY_LORA, AAA_LORA, GATE_LORA]
N_AB = sum(SPLIT_AB)
N_C = sum(SPLIT_C)
N_IN = N_AB + N_C

kernel_name = 'hybrid_dsa_retention_rwkv7_chunk_encoder'


def split_cols(t, sizes):
    offs, acc = [], 0
    for s in sizes[:-1]:
        acc += s
        offs.append(acc)
    return jnp.split(t, offs, axis=-1)


def heads(t, n, d):
    return t.reshape(t.shape[0], t.shape[1], n, d)


def rms_norm(x, g, eps=1e-5):
    xf = x.astype(F32)
    y = xf * lax.rsqrt(jnp.mean(xf * xf, axis=-1, keepdims=True) + eps)
    return (y * g.astype(F32)).astype(x.dtype)


def head_norm(y, eps):
    yf = y.astype(F32)
    mu = jnp.mean(yf, axis=-1, keepdims=True)
    var = jnp.mean(jnp.square(yf - mu), axis=-1, keepdims=True)
    return (yf - mu) * lax.rsqrt(var + eps)


def rope(x, pos, rot_dim, theta):
    half = rot_dim // 2
    freqs = jnp.power(F32(theta), -jnp.arange(half, dtype=F32) / half)
    ang = pos.astype(F32)[:, None] * freqs[None, :]
    cos = jnp.cos(ang)[:, None, :]
    sin = jnp.sin(ang)[:, None, :]
    xf = x.astype(F32)
    x1, x2, rest = xf[..., :half], xf[..., half:rot_dim], xf[..., rot_dim:]
    return jnp.concatenate([x1 * cos - x2 * sin, x2 * cos + x1 * sin, rest], axis=-1).astype(x.dtype)


def token_shift(p, mu):
    prev = jnp.pad(p, ((0, 0), (1, 0), (0, 0)))[:, :-1]
    return p + (prev - p) * mu


def sparse_indexer_attention(q, k, v, q_idx, k_idx, w_idx):
    bsz, seq = q.shape[0], q.shape[1]
    top_k = min(TOPK_MAX, seq // 4)
    n_blk = seq // Q_BLOCK
    key_chunk = jnp.arange(seq) // CHUNK
    qpos = jnp.arange(seq).reshape(n_blk, Q_BLOCK)
    scale = A_HEAD_DIM ** -0.5
    gather = jax.vmap(lambda src, ii: src[ii])

    def to_blocks(t):
        return jnp.moveaxis(t.reshape((bsz, n_blk, Q_BLOCK) + t.shape[2:]), 1, 0)

    def one_block(args):
        qb, qib, wb, pb = args
        q_chunk = pb // CHUNK
        rel = jax.nn.relu(jnp.einsum('bqhd,bsd->bqhs', qib, k_idx).astype(F32))
        score = jnp.einsum('bqhs,bqh->bqs', rel, wb.astype(F32))
        adm = key_chunk[None, :] <= q_chunk[:, None]
        score = jnp.where(adm[None], score, -jnp.inf)
        _, idx = lax.top_k(score, top_k)
        valid = (idx // CHUNK) <= q_chunk[None, :, None]
        kg = gather(k, idx)
        vg = gather(v, idx)
        s = jnp.einsum('bqhd,bqkhd->bhqk', qb, kg).astype(F32) * scale
        s = jnp.where(valid[:, None], s, -jnp.inf)
        p = jax.nn.softmax(s, axis=-1).astype(v.dtype)
        return jnp.einsum('bhqk,bqkhd->bqhd', p, vg)

    out = lax.map(one_block, (to_blocks(q), to_blocks(q_idx), to_blocks(w_idx), qpos))
    return jnp.moveaxis(out, 0, 1).reshape(bsz, seq, -1)


def retention(q, k, v):
    bsz, seq, nh, dk = q.shape
    dv = v.shape[-1]
    nc = seq // CHUNK
    log_g = jnp.log1p(-jnp.power(2.0, -5.0 - jnp.arange(nh, dtype=F32)))
    pos = jnp.arange(CHUNK, dtype=F32)
    diff = pos[:, None] - pos[None, :]
    decay = jnp.where(diff >= 0, jnp.exp(jnp.maximum(diff, 0.0)[None] * log_g[:, None, None]), 0.0)
    zeta = jnp.exp((CHUNK - 1.0 - pos)[None, :] * log_g[:, None])
    xi = jnp.exp((pos + 1.0)[None, :] * log_g[:, None])
    g_chunk = jnp.exp(CHUNK * log_g)
    qc = q.astype(F32).reshape(bsz, nc, CHUNK, nh, dk)
    kc = k.astype(F32).reshape(bsz, nc, CHUNK, nh, dk) * dk ** -0.5
    vc = v.astype(F32).reshape(bsz, nc, CHUNK, nh, dv)
    scores = jnp.einsum('bcnhd,bcmhd->bchnm', qc, kc) * decay
    intra = jnp.einsum('bchnm,bcmhe->bcnhe', scores, vc)
    kv = jnp.einsum('bcmhd,bcmhe,hm->bchde', kc, vc, zeta)

    def step(state, kv_c):
        return state * g_chunk[None, :, None, None] + kv_c, state

    _, prev = lax.scan(step, jnp.zeros((bsz, nh, dk, dv), F32), jnp.moveaxis(kv, 1, 0))
    cross = jnp.einsum('bcnhd,cbhde,hn->bcnhe', qc, prev, xi)
    return (intra + cross).reshape(bsz, seq, nh, dv)


def wkv7_scan(r, w, k, v, a, b):
    bsz, _, nh, n = r.shape

    def step(state, inp):
        r_t, w_t, k_t, v_t, a_t, b_t = inp
        sa = jnp.einsum('bhij,bhj->bhi', state, a_t)
        state = state * w_t[:, :, None, :] + sa[..., None] * b_t[:, :, None, :] + v_t[..., None] * k_t[:, :, None, :]
        return state, jnp.einsum('bhij,bhj->bhi', state, r_t)

    xs = tuple(jnp.moveaxis(t, 1, 0) for t in (r, w, k, v, a, b))
    _, y = lax.scan(step, jnp.zeros((bsz, nh, n, n), F32), xs)
    return jnp.moveaxis(y, 0, 1)


def rwkv7_branch(c, mu, w0, w_lb, a0, a_lb, g_lb, k_k, k_a, r_k, lnx_g, lnx_b, v_first, v0, v_lb):
    bsz, seq = c.shape[0], c.shape[1]
    c = token_shift(c, mu)
    has_vres = v0 is not None
    parts = split_cols(c, (SPLIT_C + [MV_LORA]) if has_vres else SPLIT_C)
    r, k, v, wl, al, gl = parts[:6]
    w_log = -jax.nn.softplus(-(w0 + jnp.tanh(wl) @ w_lb)) - 0.5
    decay = jnp.exp(-jnp.exp(w_log.astype(F32)))
    a = jax.nn.sigmoid(a0 + al @ a_lb)
    g = jax.nn.sigmoid(gl) @ g_lb
    if has_vres:
        v = v + (v_first - v) * jax.nn.sigmoid(v0 + parts[6] @ v_lb)
    else:
        v_first = v
    hs = lambda t: t.reshape(bsz, seq, W_HEADS, W_HEAD_DIM).astype(F32)
    kk = hs(k * k_k)
    kk = kk / jnp.maximum(jnp.sqrt(jnp.sum(kk * kk, axis=-1, keepdims=True)), 1e-12)
    k_h = hs(k * (1.0 + (a - 1.0) * k_a))
    r_h, v_h, a_h = hs(r), hs(v), hs(a)
    y = wkv7_scan(r_h, hs(decay), k_h, v_h, -kk, kk * a_h)
    y = head_norm(y, LNX_EPS).reshape(bsz, seq, C_W) * lnx_g + lnx_b
    bonus = jnp.sum(r_h * k_h * r_k, axis=-1, keepdims=True) * v_h
    out = (y + bonus.reshape(bsz, seq, C_W)) * g
    return out.astype(c.dtype), v_first


def setup_inputs(seed: int = 0) -> dict:
    key = jax.random.key(seed)
    ks = iter(jax.random.split(key, 32))
    nrm = lambda shape, scale: jax.random.normal(next(ks), shape, F32) * scale
    uni = lambda shape, lo, hi: jax.random.uniform(next(ks), shape, F32, lo, hi)
    return {
        'x': nrm((BATCH, SEQ, D_MODEL), 1.0),
        'norm_mix_g': 1.0 + nrm((DEPTH, D_MODEL), 0.02),
        'w_in': nrm((DEPTH, D_MODEL, N_IN), D_MODEL ** -0.5),
        'w_in_vres': nrm((DEPTH - 1, D_MODEL, MV_LORA), D_MODEL ** -0.5),
        'rwkv_mu': uni((DEPTH, N_C), 0.0, 1.0),
        'rwkv_mu_vres': uni((DEPTH - 1, MV_LORA), 0.0, 1.0),
        'rwkv_w0': uni((DEPTH, C_W), -6.0, -1.0),
        'rwkv_w_lora_b': nrm((DEPTH, DECAY_LORA, C_W), 0.1 * DECAY_LORA ** -0.5),
        'rwkv_a0': nrm((DEPTH, C_W), 0.1),
        'rwkv_a_lora_b': nrm((DEPTH, AAA_LORA, C_W), AAA_LORA ** -0.5),
        'rwkv_v0': 1.0 + nrm((DEPTH - 1, C_W), 0.1),
        'rwkv_v_lora_b': nrm((DEPTH - 1, MV_LORA, C_W), MV_LORA ** -0.5),
        'rwkv_g_lora_b': nrm((DEPTH, GATE_LORA, C_W), GATE_LORA ** -0.5),
        'rwkv_k_k': 0.85 + nrm((DEPTH, C_W), 0.02),
        'rwkv_k_a': 1.0 + nrm((DEPTH, C_W), 0.02),
        'rwkv_r_k': nrm((DEPTH, W_HEADS, W_HEAD_DIM), 0.1),
        'rwkv_lnx_g': 1.0 + nrm((DEPTH, C_W), 0.02),
        'rwkv_lnx_b': nrm((DEPTH, C_W), 0.02),
        'w_out': nrm((DEPTH, D_MIX, D_MODEL), D_MIX ** -0.5),
        'norm_ffn_g': 1.0 + nrm((DEPTH, D_MODEL), 0.02),
        'w_gate_up': nrm((DEPTH, D_MODEL, 2 * D_FF), D_MODEL ** -0.5),
        'w_down': nrm((DEPTH, D_FF, D_MODEL), D_FF ** -0.5),
        'final_norm_g': 1.0 + nrm((D_MODEL,), 0.02),
    }


def reference(x, norm_mix_g, w_in, w_in_vres, rwkv_mu, rwkv_mu_vres, rwkv_w0, rwkv_w_lora_b,
              rwkv_a0, rwkv_a_lora_b, rwkv_v0, rwkv_v_lora_b, rwkv_g_lora_b, rwkv_k_k, rwkv_k_a,
              rwkv_r_k, rwkv_lnx_g, rwkv_lnx_b, w_out, norm_ffn_g, w_gate_up, w_down, final_norm_g):
    bsz, seq = x.shape[0], x.shape[1]
    positions = jnp.arange(seq)
    v_first = None
    for l in range(DEPTH):
        h = rms_norm(x, norm_mix_g[l])
        w_full = w_in[l] if l == 0 else jnp.concatenate([w_in[l], w_in_vres[l - 1]], axis=1)
        proj = h @ w_full
        a_q, a_k, a_v, a_qi, a_ki, a_wi, b_q, b_k, b_v, b_g = split_cols(proj[..., :N_AB], SPLIT_AB)
        c_cols = proj[..., N_AB:]

        q = rope(heads(a_q, A_HEADS, A_HEAD_DIM), positions, A_HEAD_DIM // ROPE_FRAC, ROPE_THETA)
        k = rope(heads(a_k, A_HEADS, A_HEAD_DIM), positions, A_HEAD_DIM // ROPE_FRAC, ROPE_THETA)
        v = heads(a_v, A_HEADS, A_HEAD_DIM)
        qi = rope(heads(a_qi, IDX_HEADS, IDX_DIM), positions, IDX_DIM // ROPE_FRAC, ROPE_THETA)
        ki = rope(a_ki[:, :, None, :], positions, IDX_DIM // ROPE_FRAC, ROPE_THETA)[:, :, 0]
        out_a = sparse_indexer_attention(q, k, v, qi, ki, a_wi * IDX_W_SCALE)

        rq = rope(heads(b_q, R_HEADS, R_QK_DIM), positions, R_QK_DIM, R_THETA)
        rk = rope(heads(b_k, R_HEADS, R_QK_DIM), positions, R_QK_DIM, R_THETA)
        ret = retention(rq, rk, heads(b_v, R_HEADS, R_V_DIM))
        out_b = (head_norm(ret, 1e-5).reshape(bsz, seq, B_V_W) * jax.nn.silu(b_g.astype(F32))).astype(x.dtype)

        if l == 0:
            mu_l, v0_l, v_lb_l = rwkv_mu[l], None, None
        else:
            mu_l = jnp.concatenate([rwkv_mu[l], rwkv_mu_vres[l - 1]])
            v0_l, v_lb_l = rwkv_v0[l - 1], rwkv_v_lora_b[l - 1]
        out_c, v_first = rwkv7_branch(c_cols, mu_l, rwkv_w0[l], rwkv_w_lora_b[l], rwkv_a0[l],
                                      rwkv_a_lora_b[l], rwkv_g_lora_b[l], rwkv_k_k[l], rwkv_k_a[l],
                                      rwkv_r_k[l], rwkv_lnx_g[l], rwkv_lnx_b[l], v_first, v0_l, v_lb_l)

        x = x + jnp.concatenate([out_a, out_b, out_c], axis=-1) @ w_out[l]

        h = rms_norm(x, norm_ffn_g[l])
        gate, up = jnp.split(h @ w_gate_up[l], 2, axis=-1)
        x = x + (jax.nn.silu(gate) * up) @ w_down[l]
    return rms_norm(x, final_norm_g)
```

```python
import functools
import math

import jax
import jax.numpy as jnp
from jax import lax
from jax.experimental import pallas as pl
from jax.experimental.pallas import tpu as pltpu

F32 = jnp.float32
BF16 = jnp.bfloat16
I32 = jnp.int32
HIGHEST = lax.Precision.HIGHEST

CHUNK = 64
A_HEAD_DIM = 128
IDX_HEADS = 16
IDX_DIM = 64
IDX_W_SCALE = (IDX_HEADS * IDX_DIM) ** -0.5
TOPK_MAX = 256
ROPE_THETA = 500000.0
ROPE_FRAC = 4
R_HEADS = 4
R_THETA = 10000.0
W_HEAD_DIM = 64
DECAY_LORA = 96
AAA_LORA = 96
MV_LORA = 64
GATE_LORA = 256
LNX_EPS = 64e-5
RMS_EPS = 1e-5

LANE = 128
VMEM_LIMIT = 56 * 1024 * 1024
NEG_BIAS = -1e30
M_INIT = -1e20
INT_MIN = -2147483648


def _cparams(sem):
    return pltpu.CompilerParams(dimension_semantics=sem, vmem_limit_bytes=VMEM_LIMIT)


def _pick_tile(n, cap):
    best = LANE
    t = LANE
    while t <= min(n, cap):
        if n % t == 0:
            best = t
        t += LANE
    return best


def _rmsnorm_kernel(x_ref, g_ref, o_ref):
    x = x_ref[...]
    ms = jnp.mean(x * x, axis=-1, keepdims=True)
    o_ref[...] = (x * lax.rsqrt(ms + RMS_EPS) * g_ref[...]).astype(o_ref.dtype)


def _rmsnorm(x2d, g, out_dtype, tm=512):
    m, d = x2d.shape
    return pl.pallas_call(
        _rmsnorm_kernel,
        grid=(m // tm,),
        in_specs=[pl.BlockSpec((tm, d), lambda i: (i, 0)),
                  pl.BlockSpec((1, d), lambda i: (0, 0))],
        out_specs=pl.BlockSpec((tm, d), lambda i: (i, 0)),
        out_shape=jax.ShapeDtypeStruct((m, d), out_dtype),
        compiler_params=_cparams(("parallel",)),
        name="rmsnorm",
    )(x2d, g.reshape(1, d).astype(F32))


def _mm_plain_kernel(a_ref, w_ref, o_ref):
    o_ref[...] = jnp.dot(a_ref[...], w_ref[...], preferred_element_type=F32).astype(o_ref.dtype)


def _mm_plain(a, w, out_dtype, tm=512, tn_cap=1024):
    m, k = a.shape
    n = w.shape[1]
    tn = _pick_tile(n, tn_cap)
    return pl.pallas_call(
        _mm_plain_kernel,
        grid=(m // tm, n // tn),
        in_specs=[pl.BlockSpec((tm, k), lambda i, j: (i, 0)),
                  pl.BlockSpec((k, tn), lambda i, j: (0, j))],
        out_specs=pl.BlockSpec((tm, tn), lambda i, j: (i, j)),
        out_shape=jax.ShapeDtypeStruct((m, n), out_dtype),
        compiler_params=_cparams(("parallel", "arbitrary")),
        name="proj_plain",
    )(a, w)


def _mm_rope_kernel(a_ref, w_ref, c_ref, s1_ref, s2_ref, o_ref, *, half):
    acc = jnp.dot(a_ref[...], w_ref[...], preferred_element_type=F32)
    c, s1, s2 = c_ref[...], s1_ref[...], s2_ref[...]
    for blk in range(acc.shape[1] // LANE):
        x = acc[:, blk * LANE:(blk + 1) * LANE]
        up = pltpu.roll(x, LANE - half, axis=1)
        dn = pltpu.roll(x, half, axis=1)
        o_ref[:, blk * LANE:(blk + 1) * LANE] = (x * c + up * s1 + dn * s2).astype(o_ref.dtype)


def _mm_rope(a, w, tables, half, seq, tn, out_dtype, tm=512):
    m, k = a.shape
    n = w.shape[1]
    tpb = seq // tm
    tab_spec = pl.BlockSpec((None, tm, LANE), lambda i, j: (j, i % tpb, 0))
    return pl.pallas_call(
        functools.partial(_mm_rope_kernel, half=half),
        grid=(m // tm, n // tn),
        in_specs=[pl.BlockSpec((tm, k), lambda i, j: (i, 0)),
                  pl.BlockSpec((k, tn), lambda i, j: (0, j)),
                  tab_spec, tab_spec, tab_spec],
        out_specs=pl.BlockSpec((tm, tn), lambda i, j: (i, j)),
        out_shape=jax.ShapeDtypeStruct((m, n), out_dtype),
        compiler_params=_cparams(("parallel", "arbitrary")),
        name="proj_rope",
    )(a, w, *tables)


def _rope_tables(seq, group, rot_dim, theta, scale=1.0, pass_scale=None):
    half = rot_dim // 2
    freqs = jnp.power(F32(theta), -jnp.arange(half, dtype=F32) / half)
    ang = jnp.arange(seq, dtype=F32)[:, None] * freqs[None, :]
    cos, sin = jnp.cos(ang), jnp.sin(ang)
    lane = jnp.arange(LANE) % group
    idx = lane % half
    cosl, sinl = cos[:, idx], sin[:, idx]
    passv = jnp.ones((LANE,), F32) if pass_scale is None else pass_scale
    c = jnp.where(lane < rot_dim, cosl, passv[None, :])
    s1 = jnp.where(lane < half, -sinl, 0.0)
    s2 = jnp.where((lane >= half) & (lane < rot_dim), sinl, 0.0)
    return c * scale, s1 * scale, s2 * scale


def _indexer_kernel(qi_ref, kit_ref, wi_ref, bias_ref, key_ref, *, tq, tki, top_k, seq):
    qb = pl.program_id(1)
    nkb = ((qb + 1) * tq + tki - 1) // tki
    q_chunk = (qb * tq + lax.broadcasted_iota(I32, (tq, 1), 0)) // CHUNK
    w = wi_ref[...]

    def score_block(kb, carry):
        k0 = pl.multiple_of(kb * tki, tki)
        kblk = kit_ref[:, pl.ds(k0, tki)]
        acc = jnp.zeros((tq, tki), F32)
        for h in range(IDX_HEADS):
            x = jnp.dot(qi_ref[:, h * IDX_DIM:(h + 1) * IDX_DIM], kblk, preferred_element_type=F32)
            acc = acc + jnp.maximum(x, 0.0) * w[:, h:h + 1]
        k_chunk = (k0 + lax.broadcasted_iota(I32, (1, tki), 1)) // CHUNK
        bits = lax.bitcast_convert_type(acc, I32)
        key = bits ^ ((bits >> 31) & jnp.int32(0x7FFFFFFF))
        key_ref[:, pl.ds(k0, tki)] = jnp.where(k_chunk <= q_chunk, key, jnp.int32(INT_MIN))
        return carry

    lax.fori_loop(0, nkb, score_block, 0)

    def radix_pass(it, prefix):
        cand = prefix + (jnp.int32(1) << (31 - it))
        candb = jnp.broadcast_to(cand, (tq, LANE))

        def count_block(kb, cnt):
            k0 = pl.multiple_of(kb * tki, tki)
            for c in range(tki // LANE):
                blk = key_ref[:, pl.ds(k0 + c * LANE, LANE)]
                cnt = cnt + jnp.where(blk >= candb, 1, 0)
            return cnt

        cnt = lax.fori_loop(0, nkb, count_block, jnp.zeros((tq, LANE), I32))
        tot = jnp.sum(cnt, axis=1, keepdims=True)
        return jnp.where(tot >= top_k, cand, prefix)

    thr = lax.fori_loop(0, 32, radix_pass, jnp.full((tq, 1), INT_MIN, I32))
    thr = jnp.maximum(thr, jnp.int32(INT_MIN + 1))
    thrb = jnp.broadcast_to(thr, (tq, LANE))

    def write_block(kb, carry):
        k0 = pl.multiple_of(kb * tki, tki)
        for c in range(tki // LANE):
            blk = key_ref[:, pl.ds(k0 + c * LANE, LANE)]
            bias_ref[:, pl.ds(k0 + c * LANE, LANE)] = jnp.where(blk >= thrb, 0.0, NEG_BIAS)
        return carry

    lax.fori_loop(0, nkb, write_block, 0)

    def fill_block(kb, carry):
        k0 = pl.multiple_of(kb * tki, tki)
        bias_ref[:, pl.ds(k0, tki)] = jnp.full((tq, tki), NEG_BIAS, F32)
        return carry

    lax.fori_loop(nkb, seq // tki, fill_block, 0)


def _indexer_bias(qi, kit, wi, top_k, tq=128, tki=512):
    bsz, seq, _ = qi.shape
    tki = min(tki, seq)
    return pl.pallas_call(
        functools.partial(_indexer_kernel, tq=tq, tki=tki, top_k=top_k, seq=seq),
        grid=(bsz, seq // tq),
        in_specs=[pl.BlockSpec((None, tq, IDX_HEADS * IDX_DIM), lambda b, i: (b, i, 0)),
                  pl.BlockSpec((None, IDX_DIM, seq), lambda b, i: (b, 0, 0)),
                  pl.BlockSpec((None, tq, IDX_HEADS), lambda b, i: (b, i, 0))],
        out_specs=pl.BlockSpec((None, tq, seq), lambda b, i: (b, i, 0)),
        out_shape=jax.ShapeDtypeStruct((bsz, seq, seq), F32),
        scratch_shapes=[pltpu.VMEM((tq, seq), I32)],
        compiler_params=_cparams(("parallel", "arbitrary")),
        name="indexer_select",
    )(qi, kit, wi)


def _attn_kernel(qidx_ref, kidx_ref, q_ref, kt_ref, v_ref, b_ref, o_ref, m_ref, l_ref, acc_ref,
                 *, nheads, hd, tk):
    p = pl.program_id(1)
    qb = qidx_ref[p]
    kb = kidx_ref[p]

    @pl.when(kb == 0)
    def _():
        m_ref[...] = jnp.full(m_ref.shape, M_INIT, F32)
        l_ref[...] = jnp.zeros(l_ref.shape, F32)
        acc_ref[...] = jnp.zeros(acc_ref.shape, F32)

    bias = b_ref[...]
    for h in range(nheads):
        sl = slice(h * hd, (h + 1) * hd)
        s = jnp.dot(q_ref[:, sl], kt_ref[sl, :], preferred_element_type=F32) + bias
        m_old = m_ref[h]
        m_new = jnp.maximum(m_old, jnp.max(s, axis=1, keepdims=True))
        alpha = jnp.exp(m_old - m_new)
        pr = jnp.exp(s - pltpu.repeat(m_new, tk // LANE, axis=1))
        l_ref[h] = alpha * l_ref[h] + jnp.sum(pr, axis=1, keepdims=True)
        m_ref[h] = m_new
        acc_ref[:, sl] = alpha * acc_ref[:, sl] + jnp.dot(pr.astype(BF16), v_ref[:, sl],
                                                          preferred_element_type=F32)

    @pl.when(kb == qb)
    def _():
        for h in range(nheads):
            sl = slice(h * hd, (h + 1) * hd)
            o_ref[:, sl] = (acc_ref[:, sl] / l_ref[h]).astype(o_ref.dtype)


def _masked_attention(q, kt, v, bias, tq=256):
    bsz, seq, width = q.shape
    nheads = width // A_HEAD_DIM
    nq = seq // tq
    pairs = [(i, j) for i in range(nq) for j in range(i + 1)]
    qidx = jnp.asarray([pq for pq, _ in pairs], I32)
    kidx = jnp.asarray([pk for _, pk in pairs], I32)
    grid_spec = pltpu.PrefetchScalarGridSpec(
        num_scalar_prefetch=2,
        grid=(bsz, len(pairs)),
        in_specs=[pl.BlockSpec((None, tq, width), lambda b, p, qi, ki: (b, qi[p], 0)),
                  pl.BlockSpec((None, width, tq), lambda b, p, qi, ki: (b, 0, ki[p])),
                  pl.BlockSpec((None, tq, width), lambda b, p, qi, ki: (b, ki[p], 0)),
                  pl.BlockSpec((None, tq, tq), lambda b, p, qi, ki: (b, qi[p], ki[p]))],
        out_specs=pl.BlockSpec((None, tq, width), lambda b, p, qi, ki: (b, qi[p], 0)),
        scratch_shapes=[pltpu.VMEM((nheads, tq, LANE), F32),
                        pltpu.VMEM((nheads, tq, LANE), F32),
                        pltpu.VMEM((tq, width), F32)],
    )
    return pl.pallas_call(
        functools.partial(_attn_kernel, nheads=nheads, hd=A_HEAD_DIM, tk=tq),
        grid_spec=grid_spec,
        out_shape=jax.ShapeDtypeStruct((bsz, seq, width), BF16),
        compiler_params=_cparams(("parallel", "arbitrary")),
        name="masked_attention",
    )(qidx, kidx, q, kt, v, bias)


def _retention_kernel(qk_ref, v_ref, g_ref, o_ref, state_ref, *, tc, dk, dv):
    @pl.when(pl.program_id(1) == 0)
    def _():
        state_ref[...] = jnp.zeros(state_ref.shape, F32)

    row = lax.broadcasted_iota(I32, (tc, tc), 0)
    col = lax.broadcasted_iota(I32, (tc, tc), 1)
    diff = (row - col).astype(F32)
    pos = lax.broadcasted_iota(I32, (tc, 1), 0).astype(F32)
    for h in range(R_HEADS):
        log_g = math.log1p(-(2.0 ** (-5.0 - h)))
        q = qk_ref[:, h * dk:(h + 1) * dk]
        k = qk_ref[:, (R_HEADS + h) * dk:(R_HEADS + h + 1) * dk]
        v = v_ref[:, h * dv:(h + 1) * dv]
        decay = jnp.where(diff >= 0, jnp.exp(jnp.maximum(diff, 0.0) * log_g), 0.0)
        s = lax.dot_general(q, k, (((1,), (1,)), ((), ())), preferred_element_type=F32) * decay
        intra = jnp.dot(s.astype(BF16), v, preferred_element_type=F32)
        xi = jnp.exp((pos + 1.0) * log_g)
        zeta = jnp.exp((tc - 1.0 - pos) * log_g)
        state = state_ref[h]
        cross = jnp.dot((q.astype(F32) * xi).astype(BF16), state.astype(BF16), preferred_element_type=F32)
        kz = (k.astype(F32) * zeta).astype(BF16)
        kv = lax.dot_general(kz, v, (((0,), (0,)), ((), ())), preferred_element_type=F32)
        state_ref[h] = state * math.exp(tc * log_g) + kv
        ret = intra + cross
        mu = jnp.mean(ret, axis=1, keepdims=True)
        d = ret - mu
        var = jnp.mean(d * d, axis=1, keepdims=True)
        gate = g_ref[:, h * dv:(h + 1) * dv]
        gate = gate / (1.0 + jnp.exp(-gate))
        o_ref[:, h * dv:(h + 1) * dv] = (d * lax.rsqrt(var + 1e-5) * gate).astype(o_ref.dtype)


def _retention(bqk, bv, bg, tc=256):
    bsz, seq, w2 = bqk.shape
    dk = w2 // (2 * R_HEADS)
    dv = bv.shape[2] // R_HEADS
    tc = min(tc, seq)
    return pl.pallas_call(
        functools.partial(_retention_kernel, tc=tc, dk=dk, dv=dv),
        grid=(bsz, seq // tc),
        in_specs=[pl.BlockSpec((None, tc, w2), lambda b, c: (b, c, 0)),
                  pl.BlockSpec((None, tc, R_HEADS * dv), lambda b, c: (b, c, 0)),
                  pl.BlockSpec((None, tc, R_HEADS * dv), lambda b, c: (b, c, 0))],
        out_specs=pl.BlockSpec((None, tc, R_HEADS * dv), lambda b, c: (b, c, 0)),
        out_shape=jax.ShapeDtypeStruct((bsz, seq, R_HEADS * dv), BF16),
        scratch_shapes=[pltpu.VMEM((R_HEADS, dk, dv), F32)],
        compiler_params=_cparams(("parallel", "arbitrary")),
        name="retention",
    )(bqk, bv, bg)


def _head_sum_matrix(width, hd):
    r = jnp.arange(width)
    return (r[:, None] // hd == r[None, :] // hd).astype(F32)


def _sigmoid(x):
    return 1.0 / (1.0 + jnp.exp(-x))


def _rwkv_prep_kernel(*refs, cw, has_vres):
    if has_vres:
        (c_ref, mu_ref, vec_ref, wlb_ref, alb_ref, glb_ref, hs_ref, vlb_ref, vfirst_ref,
         r_o, lw_o, k_o, v_o, al_o, be_o, g_o, bo_o, carry_ref) = refs
    else:
        (c_ref, mu_ref, vec_ref, wlb_ref, alb_ref, glb_ref, hs_ref,
         r_o, lw_o, k_o, v_o, al_o, be_o, g_o, bo_o, carry_ref) = refs
    tr = c_ref.shape[0]

    @pl.when(pl.program_id(1) == 0)
    def _():
        carry_ref[...] = jnp.zeros(carry_ref.shape, F32)

    c = c_ref[...]
    row = lax.broadcasted_iota(I32, (tr, 1), 0)
    prev = jnp.where(row == 0, carry_ref[0:1, :], pltpu.roll(c, 1, axis=0))
    carry_ref[0:1, :] = c[tr - 1:tr, :]
    cs = c + (prev - c) * mu_ref[...]

    r = cs[:, 0:cw]
    k = cs[:, cw:2 * cw]
    v = cs[:, 2 * cw:3 * cw]
    o = 3 * cw
    wl = cs[:, o:o + LANE]
    al = cs[:, o + LANE:o + 2 * LANE]
    gl = cs[:, o + 2 * LANE:o + 2 * LANE + GATE_LORA]
    w0, a0, k_k, k_a, r_k, v0 = (vec_ref[i:i + 1, :] for i in range(6))

    z = -(w0 + jnp.dot(jnp.tanh(wl).astype(BF16), wlb_ref[...], preferred_element_type=F32))
    softplus = jnp.maximum(z, 0.0) + jnp.log(1.0 + jnp.exp(-jnp.abs(z)))
    lw = -jnp.exp(-softplus - 0.5)
    a = _sigmoid(a0 + jnp.dot(al.astype(BF16), alb_ref[...], preferred_element_type=F32))
    g = jnp.dot(_sigmoid(gl).astype(BF16), glb_ref[...], preferred_element_type=F32)
    if has_vres:
        vr = cs[:, o + 2 * LANE + GATE_LORA:o + 2 * LANE + GATE_LORA + vlb_ref.shape[0]]
        mix = _sigmoid(v0 + jnp.dot(vr.astype(BF16), vlb_ref[...], preferred_element_type=F32))
        v = v + (vfirst_ref[...] - v) * mix
    hs = hs_ref[...]
    kk = k * k_k
    ss = jnp.dot(kk * kk, hs, precision=HIGHEST, preferred_element_type=F32)
    kk = kk / jnp.maximum(jnp.sqrt(ss), 1e-12)
    kh = k * (1.0 + (a - 1.0) * k_a)
    rk = jnp.dot(r * kh * r_k, hs, precision=HIGHEST, preferred_element_type=F32)

    r_o[...] = r
    lw_o[...] = lw
    k_o[...] = kh
    v_o[...] = v
    al_o[...] = -kk
    be_o[...] = kk * a
    g_o[...] = g
    bo_o[...] = rk * v


def _rwkv_prep(cproj, mu, vecs, wlb, alb, glb, vlb, v_first, cw, tr=256):
    bsz, seq, wc = cproj.shape
    tr = min(tr, seq)
    has_vres = vlb is not None
    hs = _head_sum_matrix(cw, W_HEAD_DIM)
    full = lambda a: pl.BlockSpec(a.shape, lambda b, t: (0,) * a.ndim)
    tok = pl.BlockSpec((None, tr, cw), lambda b, t: (b, t, 0))
    ins = [cproj, mu, vecs, wlb, alb, glb, hs]
    in_specs = [pl.BlockSpec((None, tr, wc), lambda b, t: (b, t, 0)),
                full(mu), full(vecs), full(wlb), full(alb), full(glb), full(hs)]
    if has_vres:
        ins += [vlb, v_first]
        in_specs += [full(vlb), tok]
    out = jax.ShapeDtypeStruct((bsz, seq, cw), F32)
    return pl.pallas_call(
        functools.partial(_rwkv_prep_kernel, cw=cw, has_vres=has_vres),
        grid=(bsz, seq // tr),
        in_specs=in_specs,
        out_specs=[tok] * 8,
        out_shape=[out] * 8,
        scratch_shapes=[pltpu.VMEM((8, wc), F32)],
        compiler_params=_cparams(("parallel", "arbitrary")),
        name="rwkv_prep",
    )(*ins)


def _dot_hi(a, b):
    return jnp.dot(a, b, precision=HIGHEST, preferred_element_type=F32)


def _rwkv_chunk_kernel(r_ref, lw_ref, k_ref, v_ref, al_ref, be_ref, g_o, y0_o, m_o, z0_o, *, tc, npairs):
    row = lax.broadcasted_iota(I32, (tc, tc), 0)
    col = lax.broadcasted_iota(I32, (tc, tc), 1)
    incl = row >= col
    strict = row > col
    tri = incl.astype(F32)
    eye = (row == col).astype(F32)
    lane = lax.broadcasted_iota(I32, (1, LANE), 1)
    prow = lax.broadcasted_iota(I32, (LANE, LANE), 0)
    pcol = lax.broadcasted_iota(I32, (LANE, LANE), 1)
    same_head = (prow // W_HEAD_DIM) == (pcol // W_HEAD_DIM)
    for p in range(npairs):
        sl = slice(p * LANE, (p + 1) * LANE)
        lw = lw_ref[:, sl]
        cum = _dot_hi(tri, lw)
        tot = cum[tc - 1:tc, :]
        p_in = jnp.exp(cum)
        p_ex = jnp.exp(cum - lw)
        p_inv = jnp.exp(-cum)
        p_end = jnp.exp(tot - cum)
        at = al_ref[:, sl] * p_ex
        rt = r_ref[:, sl] * p_in
        bt = be_ref[:, sl] * p_inv
        kt = k_ref[:, sl] * p_inv
        bh = be_ref[:, sl] * p_end
        khat = k_ref[:, sl] * p_end
        v = v_ref[:, sl]
        rhs = jnp.concatenate([bt, kt], axis=0)
        quads = []
        w_parts, u0_parts = [], []
        for h in range(2):
            mh = (lane // W_HEAD_DIM) == h
            lhs = jnp.concatenate([jnp.where(mh, at, 0.0), jnp.where(mh, rt, 0.0)], axis=0)
            x = lax.dot_general(lhs, rhs, (((1,), (1,)), ((), ())), precision=HIGHEST,
                                preferred_element_type=F32)
            a_ab = jnp.where(strict, x[:tc, :tc], 0.0)
            a_ak = jnp.where(strict, x[:tc, tc:], 0.0)
            q_b = jnp.where(incl, x[tc:, :tc], 0.0)
            q_k = jnp.where(incl, x[tc:, tc:], 0.0)
            t_inv = eye + a_ab
            lp = a_ab
            n = 2
            while n < tc:
                lp = _dot_hi(lp, lp)
                t_inv = t_inv + _dot_hi(t_inv, lp)
                n *= 2
            w_parts.append(_dot_hi(t_inv, at))
            u0_parts.append(_dot_hi(t_inv, _dot_hi(a_ak, v)))
            quads.append((q_b, q_k))
        mh0 = (lane // W_HEAD_DIM) == 0
        w = jnp.where(mh0, w_parts[0], w_parts[1])
        u0 = jnp.where(mh0, u0_parts[0], u0_parts[1])
        y0_parts, gw_parts = [], []
        for h in range(2):
            q_b, q_k = quads[h]
            y0_parts.append(_dot_hi(q_k, v) + _dot_hi(q_b, u0))
            gw_parts.append(_dot_hi(q_b, w))
        y0_o[:, sl] = jnp.where(mh0, y0_parts[0], y0_parts[1])
        g_o[:, sl] = rt + jnp.where(mh0, gw_parts[0], gw_parts[1])
        tn = (((0,), (0,)), ((), ()))
        m_mat = lax.dot_general(bh, w, tn, precision=HIGHEST, preferred_element_type=F32)
        m_mat = jnp.where(same_head, m_mat, 0.0) + jnp.where(
            prow == pcol, jnp.broadcast_to(jnp.exp(tot), (LANE, LANE)), 0.0)
        z0 = (lax.dot_general(bh, u0, tn, precision=HIGHEST, preferred_element_type=F32)
              + lax.dot_general(khat, v, tn, precision=HIGHEST, preferred_element_type=F32))
        m_o[p] = m_mat
        z0_o[p] = jnp.where(same_head, z0, 0.0)


def _rwkv_chunk_ops(r, lw, kh, v, alpha, beta, tc=CHUNK):
    bsz, seq, cw = r.shape
    npairs = cw // LANE
    nc = seq // tc
    tok = pl.BlockSpec((None, tc, cw), lambda b, c: (b, c, 0))
    mat = pl.BlockSpec((None, None, npairs, LANE, LANE), lambda b, c: (b, c, 0, 0, 0))
    tok_shape = jax.ShapeDtypeStruct((bsz, seq, cw), F32)
    mat_shape = jax.ShapeDtypeStruct((bsz, nc, npairs, LANE, LANE), F32)
    return pl.pallas_call(
        functools.partial(_rwkv_chunk_kernel, tc=tc, npairs=npairs),
        grid=(bsz, nc),
        in_specs=[tok] * 6,
        out_specs=[tok, tok, mat, mat],
        out_shape=[tok_shape, tok_shape, mat_shape, mat_shape],
        compiler_params=_cparams(("parallel", "parallel")),
        name="rwkv_chunk_ops",
    )(r, lw, kh, v, alpha, beta)


def _rwkv_scan_kernel(g_ref, y0_ref, m_ref, z0_ref, bo_ref, gate_ref, ln_ref, hs_ref, o_ref, state_ref,
                      *, tc, nch, npairs):
    @pl.when(pl.program_id(1) == 0)
    def _():
        state_ref[...] = jnp.zeros(state_ref.shape, F32)

    hs = hs_ref[...]
    inv_n = 1.0 / W_HEAD_DIM
    for p in range(npairs):
        sl = slice(p * LANE, (p + 1) * LANE)
        st = state_ref[p]
        for ch in range(nch):
            rows = slice(ch * tc, (ch + 1) * tc)
            y = _dot_hi(g_ref[rows, sl], st) + y0_ref[rows, sl]
            st = _dot_hi(m_ref[ch, p], st) + z0_ref[ch, p]
            mu = _dot_hi(y, hs) * inv_n
            d = y - mu
            var = _dot_hi(d * d, hs) * inv_n
            yn = d * lax.rsqrt(var + LNX_EPS) * ln_ref[0:1, sl] + ln_ref[1:2, sl]
            o_ref[rows, sl] = ((yn + bo_ref[rows, sl]) * gate_ref[rows, sl]).astype(o_ref.dtype)
        state_ref[p] = st


def _rwkv_scan(g, y0, m, z0, bonus, gate, ln, tc=CHUNK, nch=4):
    bsz, seq, cw = g.shape
    npairs = cw // LANE
    nch = min(nch, seq // tc)
    hs = _head_sum_matrix(LANE, W_HEAD_DIM)
    tok = pl.BlockSpec((None, tc * nch, cw), lambda b, c: (b, c, 0))
    mat = pl.BlockSpec((None, nch, npairs, LANE, LANE), lambda b, c: (b, c, 0, 0, 0))
    return pl.pallas_call(
        functools.partial(_rwkv_scan_kernel, tc=tc, nch=nch, npairs=npairs),
        grid=(bsz, seq // (tc * nch)),
        in_specs=[tok, tok, mat, mat, tok, tok,
                  pl.BlockSpec(ln.shape, lambda b, c: (0, 0)),
                  pl.BlockSpec(hs.shape, lambda b, c: (0, 0))],
        out_specs=tok,
        out_shape=jax.ShapeDtypeStruct((bsz, seq, cw), BF16),
        scratch_shapes=[pltpu.VMEM((npairs, LANE, LANE), F32)],
        compiler_params=_cparams(("parallel", "arbitrary")),
        name="rwkv_scan",
    )(g, y0, m, z0, bonus, gate, ln, hs)


def _out_proj_kernel(x_ref, a_ref, b_ref, c_ref, w_ref, o_ref, *, wa, wb):
    acc = jnp.dot(a_ref[...], w_ref[0:wa, :], preferred_element_type=F32)
    acc = acc + jnp.dot(b_ref[...], w_ref[wa:wa + wb, :], preferred_element_type=F32)
    acc = acc + jnp.dot(c_ref[...], w_ref[wa + wb:, :], preferred_element_type=F32)
    o_ref[...] = x_ref[...] + acc


def _out_proj(x2d, oa, ob, oc, w, tm=512, tn=1024):
    m, d = x2d.shape
    wa, wb, wc = oa.shape[1], ob.shape[1], oc.shape[1]
    tn = min(tn, d)
    return pl.pallas_call(
        functools.partial(_out_proj_kernel, wa=wa, wb=wb),
        grid=(m // tm, d // tn),
        in_specs=[pl.BlockSpec((tm, tn), lambda i, j: (i, j)),
                  pl.BlockSpec((tm, wa), lambda i, j: (i, 0)),
                  pl.BlockSpec((tm, wb), lambda i, j: (i, 0)),
                  pl.BlockSpec((tm, wc), lambda i, j: (i, 0)),
                  pl.BlockSpec((wa + wb + wc, tn), lambda i, j: (0, j))],
        out_specs=pl.BlockSpec((tm, tn), lambda i, j: (i, j)),
        out_shape=jax.ShapeDtypeStruct((m, d), F32),
        compiler_params=_cparams(("parallel", "arbitrary")),
        name="out_proj",
    )(x2d, oa, ob, oc, w)


def _gate_up_kernel(h_ref, wg_ref, wu_ref, o_ref):
    h = h_ref[...]
    gate = jnp.dot(h, wg_ref[...], preferred_element_type=F32)
    up = jnp.dot(h, wu_ref[...], preferred_element_type=F32)
    o_ref[...] = (gate / (1.0 + jnp.exp(-gate)) * up).astype(o_ref.dtype)


def _gate_up(h, w_gate_up, tm=512, tf_cap=512):
    m, d = h.shape
    dff = w_gate_up.shape[1] // 2
    tf = _pick_tile(dff, tf_cap)
    nf = dff // tf
    return pl.pallas_call(
        _gate_up_kernel,
        grid=(m // tm, nf),
        in_specs=[pl.BlockSpec((tm, d), lambda i, j: (i, 0)),
                  pl.BlockSpec((d, tf), lambda i, j: (0, j)),
                  pl.BlockSpec((d, tf), lambda i, j: (0, j + nf))],
        out_specs=pl.BlockSpec((tm, tf), lambda i, j: (i, j)),
        out_shape=jax.ShapeDtypeStruct((m, dff), BF16),
        compiler_params=_cparams(("parallel", "arbitrary")),
        name="ffn_gate_up",
    )(h, w_gate_up, w_gate_up)


def _down_kernel(x_ref, a_ref, w_ref, o_ref):
    o_ref[...] = x_ref[...] + jnp.dot(a_ref[...], w_ref[...], preferred_element_type=F32)


def _down_proj(x2d, act, w, tm=512, tn=512):
    m, d = x2d.shape
    dff = act.shape[1]
    tn = min(tn, d)
    return pl.pallas_call(
        _down_kernel,
        grid=(m // tm, d // tn),
        in_specs=[pl.BlockSpec((tm, tn), lambda i, j: (i, j)),
                  pl.BlockSpec((tm, dff), lambda i, j: (i, 0)),
                  pl.BlockSpec((dff, tn), lambda i, j: (0, j))],
        out_specs=pl.BlockSpec((tm, tn), lambda i, j: (i, j)),
        out_shape=jax.ShapeDtypeStruct((m, d), F32),
        compiler_params=_cparams(("parallel", "arbitrary")),
        name="ffn_down",
    )(x2d, act, w)


def _pad_cols(w, width):
    return jnp.pad(w, ((0, 0), (0, width - w.shape[1])))


def _pad_rows(w, height):
    return jnp.pad(w, ((0, height - w.shape[0]), (0, 0)))


def _pad_vec(v, width):
    return jnp.pad(v, (0, width - v.shape[0]))


def kernel(x, norm_mix_g, w_in, w_in_vres, rwkv_mu, rwkv_mu_vres, rwkv_w0, rwkv_w_lora_b, rwkv_a0,
           rwkv_a_lora_b, rwkv_v0, rwkv_v_lora_b, rwkv_g_lora_b, rwkv_k_k, rwkv_k_a, rwkv_r_k,
           rwkv_lnx_g, rwkv_lnx_b, w_out, norm_ffn_g, w_gate_up, w_down, final_norm_g):
    bsz, seq, d_model = x.shape
    depth = w_in.shape[0]
    m = bsz * seq
    d_mix = w_out.shape[1]
    a_w = d_mix // 2
    qi_w = IDX_HEADS * IDX_DIM
    b_v_w = d_mix // 4
    b_qk_w = b_v_w // 2
    c_w = d_mix // 4
    r_qk_dim = b_qk_w // R_HEADS
    top_k = min(TOPK_MAX, seq // 4)
    vres_pad = 2 * LANE

    o_q, o_k, o_v = 0, a_w, 2 * a_w
    o_qi = 3 * a_w
    o_ki = o_qi + qi_w
    o_wi = o_ki + IDX_DIM
    o_bq = o_wi + IDX_HEADS
    o_bk = o_bq + b_qk_w
    o_bv = o_bk + b_qk_w
    o_bg = o_bv + b_v_w
    o_c = o_bg + b_v_w
    o_wl = o_c + 3 * c_w
    o_al = o_wl + DECAY_LORA
    o_gl = o_al + AAA_LORA
    n_in = o_gl + GATE_LORA

    a_rot = A_HEAD_DIM // ROPE_FRAC
    tq_ = _rope_tables(seq, A_HEAD_DIM, a_rot, ROPE_THETA, scale=A_HEAD_DIM ** -0.5)
    tk_ = _rope_tables(seq, A_HEAD_DIM, a_rot, ROPE_THETA)
    tab_qk = tuple(jnp.stack([a, b]) for a, b in zip(tq_, tk_))
    i_rot = IDX_DIM // ROPE_FRAC
    tab_qi = tuple(t[None] for t in _rope_tables(seq, IDX_DIM, i_rot, ROPE_THETA))
    lane = jnp.arange(LANE)
    kiwi_pass = jnp.where(lane < IDX_DIM, 1.0, IDX_W_SCALE).astype(F32)
    c_kw, s1_kw, s2_kw = _rope_tables(seq, LANE, i_rot, ROPE_THETA, pass_scale=kiwi_pass)
    tab_kiwi = (c_kw[None], s1_kw[None], s2_kw[None])
    tbq = _rope_tables(seq, r_qk_dim, r_qk_dim, R_THETA)
    tbk = _rope_tables(seq, r_qk_dim, r_qk_dim, R_THETA, scale=r_qk_dim ** -0.5)
    tab_bqk = tuple(jnp.stack([a, b]) for a, b in zip(tbq, tbk))

    x2d = x.reshape(m, d_model)
    v_first = None
    for l in range(depth):
        wl_ = w_in[l]
        cols = lambda o, n: wl_[:, o:o + n]
        w_qk = wl_[:, o_q:o_v].astype(BF16)
        w_v = jnp.concatenate([cols(o_v, a_w), cols(o_bv, b_v_w)], axis=1).astype(BF16)
        w_qi = cols(o_qi, qi_w).astype(BF16)
        w_kiwi = _pad_cols(cols(o_ki, IDX_DIM + IDX_HEADS), LANE).astype(BF16)
        w_bqk = cols(o_bq, 2 * b_qk_w).astype(BF16)
        w_bg = cols(o_bg, b_v_w).astype(BF16)
        vres_w = (w_in_vres[l - 1] if l > 0 else jnp.zeros((d_model, MV_LORA), F32))
        w_c = jnp.concatenate([cols(o_c, 3 * c_w), _pad_cols(cols(o_wl, DECAY_LORA), LANE),
                               _pad_cols(cols(o_al, AAA_LORA), LANE), cols(o_gl, GATE_LORA),
                               _pad_cols(vres_w, vres_pad)], axis=1).astype(BF16)
        mu = rwkv_mu[l]
        mu_vres = rwkv_mu_vres[l - 1] if l > 0 else jnp.zeros((MV_LORA,), F32)
        mu_c = jnp.concatenate([mu[:3 * c_w], _pad_vec(mu[3 * c_w:3 * c_w + DECAY_LORA], LANE),
                                _pad_vec(mu[3 * c_w + DECAY_LORA:3 * c_w + DECAY_LORA + AAA_LORA], LANE),
                                mu[3 * c_w + DECAY_LORA + AAA_LORA:], _pad_vec(mu_vres, vres_pad)])[None, :]

        h = _rmsnorm(x2d, norm_mix_g[l], BF16)
        qk = _mm_rope(h, w_qk, tab_qk, a_rot // 2, seq, a_w, BF16).reshape(bsz, seq, 2 * a_w)
        vv = _mm_plain(h, w_v, BF16, tn_cap=512).reshape(bsz, seq, a_w + b_v_w)
        qi = _mm_rope(h, w_qi, tab_qi, i_rot // 2, seq, qi_w, BF16).reshape(bsz, seq, qi_w)
        kiwi = _mm_rope(h, w_kiwi, tab_kiwi, i_rot // 2, seq, LANE, F32).reshape(bsz, seq, LANE)
        bqk = _mm_rope(h, w_bqk, tab_bqk, r_qk_dim // 2, seq, b_qk_w, BF16).reshape(bsz, seq, 2 * b_qk_w)
        bg = _mm_plain(h, w_bg, F32).reshape(bsz, seq, b_v_w)
        cproj = _mm_plain(h, w_c, F32, tn_cap=768).reshape(bsz, seq, w_c.shape[1])

        kit = jnp.swapaxes(kiwi[:, :, :IDX_DIM], 1, 2).astype(BF16)
        wi = kiwi[:, :, IDX_DIM:IDX_DIM + IDX_HEADS]
        bias = _indexer_bias(qi, kit, wi, top_k)
        q = qk[:, :, :a_w]
        kt = jnp.swapaxes(qk[:, :, a_w:], 1, 2)
        out_a = _masked_attention(q, kt, vv[:, :, :a_w], bias)

        out_b = _retention(bqk, vv[:, :, a_w:], bg)

        vecs = jnp.stack([rwkv_w0[l], rwkv_a0[l], rwkv_k_k[l], rwkv_k_a[l], rwkv_r_k[l].reshape(-1),
                          rwkv_v0[l - 1] if l > 0 else jnp.zeros((c_w,), F32),
                          jnp.zeros((c_w,), F32), jnp.zeros((c_w,), F32)])
        wlb = _pad_rows(rwkv_w_lora_b[l], LANE).astype(BF16)
        alb = _pad_rows(rwkv_a_lora_b[l], LANE).astype(BF16)
        glb = rwkv_g_lora_b[l].astype(BF16)
        vlb = _pad_rows(rwkv_v_lora_b[l - 1], vres_pad).astype(BF16) if l > 0 else None
        r_, lw_, kh_, v_, al_, be_, g_, bo_ = _rwkv_prep(cproj, mu_c, vecs, wlb, alb, glb, vlb, v_first, c_w)
        if l == 0:
            v_first = v_
        gm, y0, mm, z0 = _rwkv_chunk_ops(r_, lw_, kh_, v_, al_, be_)
        ln = jnp.stack([rwkv_lnx_g[l], rwkv_lnx_b[l]] + [jnp.zeros((c_w,), F32)] * 6)
        out_c = _rwkv_scan(gm, y0, mm, z0, bo_, g_, ln)

        x2d = _out_proj(x2d, out_a.reshape(m, a_w), out_b.reshape(m, b_v_w), out_c.reshape(m, c_w),
                        w_out[l].astype(BF16))

        h = _rmsnorm(x2d, norm_ffn_g[l], BF16)
        act = _gate_up(h, w_gate_up[l].astype(BF16))
        x2d = _down_proj(x2d, act, w_down[l].astype(BF16))

    return _rmsnorm(x2d, final_norm_g, F32).reshape(bsz, seq, d_model)
```

```python
import functools
import math

import jax
import jax.numpy as jnp
from jax import lax
from jax.experimental import pallas as pl
from jax.experimental.pallas import tpu as pltpu

F32 = jnp.float32
BF16 = jnp.bfloat16
I32 = jnp.int32
HIGHEST = lax.Precision.HIGHEST

CHUNK = 64
A_HEAD_DIM = 128
IDX_HEADS = 16
IDX_DIM = 64
IDX_W_SCALE = (IDX_HEADS * IDX_DIM) ** -0.5
TOPK_MAX = 256
ROPE_THETA = 500000.0
ROPE_FRAC = 4
R_HEADS = 4
R_THETA = 10000.0
W_HEAD_DIM = 64
DECAY_LORA = 96
AAA_LORA = 96
MV_LORA = 64
GATE_LORA = 256
LNX_EPS = 64e-5
RMS_EPS = 1e-5

LANE = 128
VMEM_LIMIT = 56 * 1024 * 1024
NEG_BIAS = -1e30
M_INIT = -1e20
INT_MIN = -2147483648


def _cparams(sem):
    return pltpu.CompilerParams(dimension_semantics=sem, vmem_limit_bytes=VMEM_LIMIT)


def _pick_tile(n, cap):
    best = LANE
    t = LANE
    while t <= min(n, cap):
        if n % t == 0:
            best = t
        t += LANE
    return best


def _rmsnorm_kernel(x_ref, g_ref, o_ref):
    x = x_ref[...]
    ms = jnp.mean(x * x, axis=-1, keepdims=True)
    o_ref[...] = (x * lax.rsqrt(ms + RMS_EPS) * g_ref[...]).astype(o_ref.dtype)


def _rmsnorm(x2d, g, out_dtype, tm=512):
    m, d = x2d.shape
    return pl.pallas_call(
        _rmsnorm_kernel,
        grid=(m // tm,),
        in_specs=[pl.BlockSpec((tm, d), lambda i: (i, 0)),
                  pl.BlockSpec((1, d), lambda i: (0, 0))],
        out_specs=pl.BlockSpec((tm, d), lambda i: (i, 0)),
        out_shape=jax.ShapeDtypeStruct((m, d), out_dtype),
        compiler_params=_cparams(("parallel",)),
        name="rmsnorm",
    )(x2d, g.reshape(1, d).astype(F32))


def _mm_plain_kernel(a_ref, w_ref, o_ref):
    o_ref[...] = jnp.dot(a_ref[...], w_ref[...], preferred_element_type=F32).astype(o_ref.dtype)


def _mm_plain(a, w, out_dtype, tm=512, tn_cap=1024):
    m, k = a.shape
    n = w.shape[1]
    tn = _pick_tile(n, tn_cap)
    return pl.pallas_call(
        _mm_plain_kernel,
        grid=(m // tm, n // tn),
        in_specs=[pl.BlockSpec((tm, k), lambda i, j: (i, 0)),
                  pl.BlockSpec((k, tn), lambda i, j: (0, j))],
        out_specs=pl.BlockSpec((tm, tn), lambda i, j: (i, j)),
        out_shape=jax.ShapeDtypeStruct((m, n), out_dtype),
        compiler_params=_cparams(("parallel", "arbitrary")),
        name="proj_plain",
    )(a, w)


def _mm_rope_kernel(a_ref, w_ref, c_ref, s1_ref, s2_ref, o_ref, *, half):
    acc = jnp.dot(a_ref[...], w_ref[...], preferred_element_type=F32)
    c, s1, s2 = c_ref[...], s1_ref[...], s2_ref[...]
    for blk in range(acc.shape[1] // LANE):
        x = acc[:, blk * LANE:(blk + 1) * LANE]
        up = pltpu.roll(x, LANE - half, axis=1)
        dn = pltpu.roll(x, half, axis=1)
        o_ref[:, blk * LANE:(blk + 1) * LANE] = (x * c + up * s1 + dn * s2).astype(o_ref.dtype)


def _mm_rope(a, w, tables, half, seq, tn, out_dtype, tm=512):
    m, k = a.shape
    n = w.shape[1]
    tpb = seq // tm
    tab_spec = pl.BlockSpec((None, tm, LANE), lambda i, j: (j, i % tpb, 0))
    return pl.pallas_call(
        functools.partial(_mm_rope_kernel, half=half),
        grid=(m // tm, n // tn),
        in_specs=[pl.BlockSpec((tm, k), lambda i, j: (i, 0)),
                  pl.BlockSpec((k, tn), lambda i, j: (0, j)),
                  tab_spec, tab_spec, tab_spec],
        out_specs=pl.BlockSpec((tm, tn), lambda i, j: (i, j)),
        out_shape=jax.ShapeDtypeStruct((m, n), out_dtype),
        compiler_params=_cparams(("parallel", "arbitrary")),
        name="proj_rope",
    )(a, w, *tables)


def _rope_tables(seq, group, rot_dim, theta, scale=1.0, pass_scale=None):
    half = rot_dim // 2
    freqs = jnp.power(F32(theta), -jnp.arange(half, dtype=F32) / half)
    ang = jnp.arange(seq, dtype=F32)[:, None] * freqs[None, :]
    cos, sin = jnp.cos(ang), jnp.sin(ang)
    lane = jnp.arange(LANE) % group
    idx = lane % half
    cosl, sinl = cos[:, idx], sin[:, idx]
    passv = jnp.ones((LANE,), F32) if pass_scale is None else pass_scale
    c = jnp.where(lane < rot_dim, cosl, passv[None, :])
    s1 = jnp.where(lane < half, -sinl, 0.0)
    s2 = jnp.where((lane >= half) & (lane < rot_dim), sinl, 0.0)
    return c * scale, s1 * scale, s2 * scale


def _indexer_kernel(qi_ref, kit_ref, wi_ref, bias_ref, key_ref, *, tq, tki, top_k, seq):
    qb = pl.program_id(1)
    nkb = ((qb + 1) * tq + tki - 1) // tki
    q_chunk = (qb * tq + lax.broadcasted_iota(I32, (tq, 1), 0)) // CHUNK
    w = wi_ref[...]

    def score_block(kb, rmax):
        k0 = pl.multiple_of(kb * tki, tki)
        kblk = kit_ref[:, pl.ds(k0, tki)]
        acc = jnp.zeros((tq, tki), F32)
        for h in range(IDX_HEADS):
            x = jnp.dot(qi_ref[:, h * IDX_DIM:(h + 1) * IDX_DIM], kblk, preferred_element_type=F32)
            acc = acc + jnp.maximum(x, 0.0) * w[:, h:h + 1]
        k_chunk = (k0 + lax.broadcasted_iota(I32, (1, tki), 1)) // CHUNK
        bits = lax.bitcast_convert_type(acc, I32)
        key = bits ^ ((bits >> 31) & jnp.int32(0x7FFFFFFF))
        key = jnp.where(k_chunk <= q_chunk, key, jnp.int32(INT_MIN))
        key_ref[:, pl.ds(k0, tki)] = key
        for c in range(tki // LANE):
            rmax = jnp.maximum(rmax, key[:, c * LANE:(c + 1) * LANE])
        return rmax

    rmax = lax.fori_loop(0, nkb, score_block, jnp.full((tq, LANE), INT_MIN, I32))
    rowmax = jnp.max(rmax, axis=1, keepdims=True)

    def count_ge(cand):
        candb = jnp.broadcast_to(cand, (tq, LANE))

        def count_block(kb, cnt):
            k0 = pl.multiple_of(kb * tki, tki)
            for c in range(tki // LANE):
                blk = key_ref[:, pl.ds(k0 + c * LANE, LANE)]
                cnt = cnt + jnp.where(blk >= candb, 1, 0)
            return cnt

        cnt = lax.fori_loop(0, nkb, count_block, jnp.zeros((tq, LANE), I32))
        return jnp.sum(cnt, axis=1, keepdims=True)

    probe = jnp.where(rowmax > jnp.int32(INT_MIN + (1 << 25)), rowmax - jnp.int32(3 << 23),
                      jnp.int32(INT_MIN + 1))

    def bisect_cond(state):
        it, n_active = state[0], state[1]
        return (n_active > 0) & (it < 40)

    def bisect_body(state):
        it, _, lo, hi, cnt_lo, cnt_hi = state
        mid = (lo >> 1) + (hi >> 1) + (lo & hi & 1)
        mid = jnp.where(it == 0, probe, mid)
        cnt = count_ge(mid)
        ge = cnt >= top_k
        exact = cnt == top_k
        new_lo = jnp.where(ge, mid, lo)
        new_hi = jnp.where(exact, mid + 1, jnp.where(ge, hi, mid))
        cnt_lo = jnp.where(ge, cnt, cnt_lo)
        cnt_hi = jnp.where(ge, cnt_hi, cnt)
        active = (new_lo + 1) < new_hi
        n_active = jnp.max(active.astype(I32))
        return it + 1, n_active, new_lo, new_hi, cnt_lo, cnt_hi

    big = jnp.int32(1 << 30)
    init = (jnp.int32(0), jnp.int32(1), jnp.full((tq, 1), INT_MIN, I32), rowmax + 1,
            jnp.full((tq, 1), big, I32), jnp.zeros((tq, 1), I32))
    _, _, lo, _, cnt_lo, cnt_hi = lax.while_loop(bisect_cond, bisect_body, init)

    thr = jnp.maximum(lo, jnp.int32(INT_MIN + 1))
    thrb = jnp.broadcast_to(thr, (tq, LANE))
    tie_row = (cnt_lo > top_k) & (lo > jnp.int32(INT_MIN))
    any_tie = jnp.max(tie_row.astype(I32))

    @pl.when(any_tie == 0)
    def _():
        def write_block(kb, carry):
            k0 = pl.multiple_of(kb * tki, tki)
            for c in range(tki // LANE):
                blk = key_ref[:, pl.ds(k0 + c * LANE, LANE)]
                bias_ref[:, pl.ds(k0 + c * LANE, LANE)] = jnp.where(blk >= thrb, 0.0, NEG_BIAS)
            return carry

        lax.fori_loop(0, nkb, write_block, 0)

    @pl.when(any_tie != 0)
    def _():
        need = jnp.where(tie_row, top_k - cnt_hi, big).astype(F32)
        ur = lax.broadcasted_iota(I32, (LANE, LANE), 0)
        uc = lax.broadcasted_iota(I32, (LANE, LANE), 1)
        upper = (ur <= uc).astype(BF16)

        def write_block(kb, run):
            k0 = pl.multiple_of(kb * tki, tki)
            for c in range(tki // LANE):
                blk = key_ref[:, pl.ds(k0 + c * LANE, LANE)]
                eq = blk == thrb
                pc = jnp.dot(jnp.where(eq, 1.0, 0.0).astype(BF16), upper, preferred_element_type=F32)
                keep_eq = jnp.where((run + pc) <= need, 0.0, NEG_BIAS)
                bias_ref[:, pl.ds(k0 + c * LANE, LANE)] = jnp.where(
                    blk > thrb, 0.0, jnp.where(eq, keep_eq, NEG_BIAS))
                run = run + pc[:, LANE - 1:LANE]
            return run

        lax.fori_loop(0, nkb, write_block, jnp.zeros((tq, 1), F32))

    def fill_block(kb, carry):
        k0 = pl.multiple_of(kb * tki, tki)
        bias_ref[:, pl.ds(k0, tki)] = jnp.full((tq, tki), NEG_BIAS, F32)
        return carry

    lax.fori_loop(nkb, seq // tki, fill_block, 0)


def _indexer_bias(qi, kit, wi, top_k, tq=128, tki=512):
    bsz, seq, _ = qi.shape
    tki = min(tki, seq)
    return pl.pallas_call(
        functools.partial(_indexer_kernel, tq=tq, tki=tki, top_k=top_k, seq=seq),
        grid=(bsz, seq // tq),
        in_specs=[pl.BlockSpec((None, tq, IDX_HEADS * IDX_DIM), lambda b, i: (b, i, 0)),
                  pl.BlockSpec((None, IDX_DIM, seq), lambda b, i: (b, 0, 0)),
                  pl.BlockSpec((None, tq, IDX_HEADS), lambda b, i: (b, i, 0))],
        out_specs=pl.BlockSpec((None, tq, seq), lambda b, i: (b, i, 0)),
        out_shape=jax.ShapeDtypeStruct((bsz, seq, seq), F32),
        scratch_shapes=[pltpu.VMEM((tq, seq), I32)],
        compiler_params=_cparams(("parallel", "arbitrary")),
        name="indexer_select",
    )(qi, kit, wi)


def _attn_kernel(qidx_ref, kidx_ref, q_ref, kt_ref, v_ref, b_ref, o_ref, m_ref, l_ref, acc_ref,
                 *, nheads, hd, tk):
    p = pl.program_id(1)
    qb = qidx_ref[p]
    kb = kidx_ref[p]

    @pl.when(kb == 0)
    def _():
        m_ref[...] = jnp.full(m_ref.shape, M_INIT, F32)
        l_ref[...] = jnp.zeros(l_ref.shape, F32)
        acc_ref[...] = jnp.zeros(acc_ref.shape, F32)

    bias = b_ref[...]
    for h in range(nheads):
        sl = slice(h * hd, (h + 1) * hd)
        s = jnp.dot(q_ref[:, sl], kt_ref[sl, :], preferred_element_type=F32) + bias
        m_old = m_ref[h]
        m_new = jnp.maximum(m_old, jnp.max(s, axis=1, keepdims=True))
        alpha = jnp.exp(m_old - m_new)
        pr = jnp.exp(s - pltpu.repeat(m_new, tk // LANE, axis=1))
        l_ref[h] = alpha * l_ref[h] + jnp.sum(pr, axis=1, keepdims=True)
        m_ref[h] = m_new
        acc_ref[:, sl] = alpha * acc_ref[:, sl] + jnp.dot(pr.astype(BF16), v_ref[:, sl],
                                                          preferred_element_type=F32)

    @pl.when(kb == qb)
    def _():
        for h in range(nheads):
            sl = slice(h * hd, (h + 1) * hd)
            o_ref[:, sl] = (acc_ref[:, sl] / l_ref[h]).astype(o_ref.dtype)


def _masked_attention(q, kt, v, bias, tq=256):
    bsz, seq, width = q.shape
    nheads = width // A_HEAD_DIM
    nq = seq // tq
    pairs = [(i, j) for i in range(nq) for j in range(i + 1)]
    qidx = jnp.asarray([pq for pq, _ in pairs], I32)
    kidx = jnp.asarray([pk for _, pk in pairs], I32)
    grid_spec = pltpu.PrefetchScalarGridSpec(
        num_scalar_prefetch=2,
        grid=(bsz, len(pairs)),
        in_specs=[pl.BlockSpec((None, tq, width), lambda b, p, qi, ki: (b, qi[p], 0)),
                  pl.BlockSpec((None, width, tq), lambda b, p, qi, ki: (b, 0, ki[p])),
                  pl.BlockSpec((None, tq, width), lambda b, p, qi, ki: (b, ki[p], 0)),
                  pl.BlockSpec((None, tq, tq), lambda b, p, qi, ki: (b, qi[p], ki[p]))],
        out_specs=pl.BlockSpec((None, tq, width), lambda b, p, qi, ki: (b, qi[p], 0)),
        scratch_shapes=[pltpu.VMEM((nheads, tq, LANE), F32),
                        pltpu.VMEM((nheads, tq, LANE), F32),
                        pltpu.VMEM((tq, width), F32)],
    )
    return pl.pallas_call(
        functools.partial(_attn_kernel, nheads=nheads, hd=A_HEAD_DIM, tk=tq),
        grid_spec=grid_spec,
        out_shape=jax.ShapeDtypeStruct((bsz, seq, width), BF16),
        compiler_params=_cparams(("parallel", "arbitrary")),
        name="masked_attention",
    )(qidx, kidx, q, kt, v, bias)


def _retention_kernel(qk_ref, v_ref, g_ref, o_ref, state_ref, *, tc, dk, dv):
    @pl.when(pl.program_id(1) == 0)
    def _():
        state_ref[...] = jnp.zeros(state_ref.shape, F32)

    row = lax.broadcasted_iota(I32, (tc, tc), 0)
    col = lax.broadcasted_iota(I32, (tc, tc), 1)
    diff = (row - col).astype(F32)
    pos = lax.broadcasted_iota(I32, (tc, 1), 0).astype(F32)
    for h in range(R_HEADS):
        log_g = math.log1p(-(2.0 ** (-5.0 - h)))
        q = qk_ref[:, h * dk:(h + 1) * dk]
        k = qk_ref[:, (R_HEADS + h) * dk:(R_HEADS + h + 1) * dk]
        v = v_ref[:, h * dv:(h + 1) * dv]
        decay = jnp.where(diff >= 0, jnp.exp(jnp.maximum(diff, 0.0) * log_g), 0.0)
        s = lax.dot_general(q, k, (((1,), (1,)), ((), ())), preferred_element_type=F32) * decay
        intra = jnp.dot(s.astype(BF16), v, preferred_element_type=F32)
        xi = jnp.exp((pos + 1.0) * log_g)
        zeta = jnp.exp((tc - 1.0 - pos) * log_g)
        state = state_ref[h]
        cross = jnp.dot((q.astype(F32) * xi).astype(BF16), state.astype(BF16), preferred_element_type=F32)
        kz = (k.astype(F32) * zeta).astype(BF16)
        kv = lax.dot_general(kz, v, (((0,), (0,)), ((), ())), preferred_element_type=F32)
        state_ref[h] = state * math.exp(tc * log_g) + kv
        ret = intra + cross
        mu = jnp.mean(ret, axis=1, keepdims=True)
        d = ret - mu
        var = jnp.mean(d * d, axis=1, keepdims=True)
        gate = g_ref[:, h * dv:(h + 1) * dv]
        gate = gate / (1.0 + jnp.exp(-gate))
        o_ref[:, h * dv:(h + 1) * dv] = (d * lax.rsqrt(var + 1e-5) * gate).astype(o_ref.dtype)


def _retention(bqk, bv, bg, tc=256):
    bsz, seq, w2 = bqk.shape
    dk = w2 // (2 * R_HEADS)
    dv = bv.shape[2] // R_HEADS
    tc = min(tc, seq)
    return pl.pallas_call(
        functools.partial(_retention_kernel, tc=tc, dk=dk, dv=dv),
        grid=(bsz, seq // tc),
        in_specs=[pl.BlockSpec((None, tc, w2), lambda b, c: (b, c, 0)),
                  pl.BlockSpec((None, tc, R_HEADS * dv), lambda b, c: (b, c, 0)),
                  pl.BlockSpec((None, tc, R_HEADS * dv), lambda b, c: (b, c, 0))],
        out_specs=pl.BlockSpec((None, tc, R_HEADS * dv), lambda b, c: (b, c, 0)),
        out_shape=jax.ShapeDtypeStruct((bsz, seq, R_HEADS * dv), BF16),
        scratch_shapes=[pltpu.VMEM((R_HEADS, dk, dv), F32)],
        compiler_params=_cparams(("parallel", "arbitrary")),
        name="retention",
    )(bqk, bv, bg)


def _head_sum_matrix(width, hd):
    r = jnp.arange(width)
    return (r[:, None] // hd == r[None, :] // hd).astype(F32)


def _sigmoid(x):
    return 1.0 / (1.0 + jnp.exp(-x))


def _rwkv_prep_kernel(*refs, cw, has_vres):
    if has_vres:
        (c_ref, mu_ref, vec_ref, wlb_ref, alb_ref, glb_ref, hs_ref, vlb_ref, vfirst_ref,
         r_o, lw_o, k_o, v_o, al_o, be_o, g_o, bo_o, carry_ref) = refs
    else:
        (c_ref, mu_ref, vec_ref, wlb_ref, alb_ref, glb_ref, hs_ref,
         r_o, lw_o, k_o, v_o, al_o, be_o, g_o, bo_o, carry_ref) = refs
    tr = c_ref.shape[0]

    @pl.when(pl.program_id(1) == 0)
    def _():
        carry_ref[...] = jnp.zeros(carry_ref.shape, F32)

    c = c_ref[...]
    row = lax.broadcasted_iota(I32, (tr, 1), 0)
    prev = jnp.where(row == 0, carry_ref[0:1, :], pltpu.roll(c, 1, axis=0))
    carry_ref[0:1, :] = c[tr - 1:tr, :]
    cs = c + (prev - c) * mu_ref[...]

    r = cs[:, 0:cw]
    k = cs[:, cw:2 * cw]
    v = cs[:, 2 * cw:3 * cw]
    o = 3 * cw
    wl = cs[:, o:o + LANE]
    al = cs[:, o + LANE:o + 2 * LANE]
    gl = cs[:, o + 2 * LANE:o + 2 * LANE + GATE_LORA]
    w0, a0, k_k, k_a, r_k, v0 = (vec_ref[i:i + 1, :] for i in range(6))

    z = -(w0 + jnp.dot(jnp.tanh(wl).astype(BF16), wlb_ref[...], preferred_element_type=F32))
    softplus = jnp.maximum(z, 0.0) + jnp.log(1.0 + jnp.exp(-jnp.abs(z)))
    lw = -jnp.exp(-softplus - 0.5)
    a = _sigmoid(a0 + jnp.dot(al.astype(BF16), alb_ref[...], preferred_element_type=F32))
    g = jnp.dot(_sigmoid(gl).astype(BF16), glb_ref[...], preferred_element_type=F32)
    if has_vres:
        vr = cs[:, o + 2 * LANE + GATE_LORA:o + 2 * LANE + GATE_LORA + vlb_ref.shape[0]]
        mix = _sigmoid(v0 + jnp.dot(vr.astype(BF16), vlb_ref[...], preferred_element_type=F32))
        v = v + (vfirst_ref[...] - v) * mix
    hs = hs_ref[...]
    kk = k * k_k
    ss = jnp.dot(kk * kk, hs, precision=HIGHEST, preferred_element_type=F32)
    kk = kk / jnp.maximum(jnp.sqrt(ss), 1e-12)
    kh = k * (1.0 + (a - 1.0) * k_a)
    rk = jnp.dot(r * kh * r_k, hs, precision=HIGHEST, preferred_element_type=F32)

    r_o[...] = r
    lw_o[...] = lw
    k_o[...] = kh
    v_o[...] = v
    al_o[...] = -kk
    be_o[...] = kk * a
    g_o[...] = g
    bo_o[...] = rk * v


def _rwkv_prep(cproj, mu, vecs, wlb, alb, glb, vlb, v_first, cw, tr=256):
    bsz, seq, wc = cproj.shape
    tr = min(tr, seq)
    has_vres = vlb is not None
    hs = _head_sum_matrix(cw, W_HEAD_DIM)
    full = lambda a: pl.BlockSpec(a.shape, lambda b, t: (0,) * a.ndim)
    tok = pl.BlockSpec((None, tr, cw), lambda b, t: (b, t, 0))
    ins = [cproj, mu, vecs, wlb, alb, glb, hs]
    in_specs = [pl.BlockSpec((None, tr, wc), lambda b, t: (b, t, 0)),
                full(mu), full(vecs), full(wlb), full(alb), full(glb), full(hs)]
    if has_vres:
        ins += [vlb, v_first]
        in_specs += [full(vlb), tok]
    out = jax.ShapeDtypeStruct((bsz, seq, cw), F32)
    return pl.pallas_call(
        functools.partial(_rwkv_prep_kernel, cw=cw, has_vres=has_vres),
        grid=(bsz, seq // tr),
        in_specs=in_specs,
        out_specs=[tok] * 8,
        out_shape=[out] * 8,
        scratch_shapes=[pltpu.VMEM((8, wc), F32)],
        compiler_params=_cparams(("parallel", "arbitrary")),
        name="rwkv_prep",
    )(*ins)


def _dot_hi(a, b):
    return jnp.dot(a, b, precision=HIGHEST, preferred_element_type=F32)


def _bdot(a, b):
    return jnp.dot(a.astype(BF16), b.astype(BF16), preferred_element_type=F32)


def _bdot_tn(a, b):
    return lax.dot_general(a.astype(BF16), b.astype(BF16), (((0,), (0,)), ((), ())),
                           preferred_element_type=F32)


def _rwkv_chunk_kernel(r_ref, lw_ref, k_ref, v_ref, al_ref, be_ref, g_o, y0_o, m_o, z0_o, *, tc, npairs):
    row = lax.broadcasted_iota(I32, (tc, tc), 0)
    col = lax.broadcasted_iota(I32, (tc, tc), 1)
    incl = row >= col
    strict = row > col
    tri = incl.astype(BF16)
    eye = (row == col).astype(F32)
    lane = lax.broadcasted_iota(I32, (1, LANE), 1)
    mh0 = (lane // W_HEAD_DIM) == 0
    prow = lax.broadcasted_iota(I32, (LANE, LANE), 0)
    pcol = lax.broadcasted_iota(I32, (LANE, LANE), 1)
    same_head = (prow // W_HEAD_DIM) == (pcol // W_HEAD_DIM)

    lw_all = lw_ref[...]
    hi = lw_all.astype(BF16)
    rem = lw_all - hi.astype(F32)
    mid = rem.astype(BF16)
    lo = (rem - mid.astype(F32)).astype(BF16)
    cum_all = (jnp.dot(tri, hi, preferred_element_type=F32) + jnp.dot(tri, mid, preferred_element_type=F32)
               + jnp.dot(tri, lo, preferred_element_type=F32))

    pairs = range(npairs)
    heads = [(p, h) for p in pairs for h in range(2)]
    sls = [slice(p * LANE, (p + 1) * LANE) for p in pairs]
    cum = [cum_all[:, sl] for sl in sls]
    tot = [c[tc - 1:tc, :] for c in cum]
    p_inv = [jnp.exp(-cum[p]) for p in pairs]
    p_end = [jnp.exp(tot[p] - cum[p]) for p in pairs]
    at = [al_ref[:, sls[p]] * jnp.exp(cum[p] - lw_all[:, sls[p]]) for p in pairs]
    rt = [r_ref[:, sls[p]] * jnp.exp(cum[p]) for p in pairs]
    bh = [be_ref[:, sls[p]] * p_end[p] for p in pairs]
    khat = [k_ref[:, sls[p]] * p_end[p] for p in pairs]
    v = [v_ref[:, sl] for sl in sls]
    rhs = [jnp.concatenate([be_ref[:, sls[p]] * p_inv[p], k_ref[:, sls[p]] * p_inv[p]], axis=0).astype(BF16)
           for p in pairs]

    a_ab, a_ak, qcat = {}, {}, {}
    for p, h in heads:
        mh = (lane // W_HEAD_DIM) == h
        lhs = jnp.concatenate([jnp.where(mh, at[p], 0.0), jnp.where(mh, rt[p], 0.0)], axis=0).astype(BF16)
        x = lax.dot_general(lhs, rhs[p], (((1,), (1,)), ((), ())), preferred_element_type=F32)
        a_ab[p, h] = jnp.where(strict, x[:tc, :tc], 0.0)
        a_ak[p, h] = jnp.where(strict, x[:tc, tc:], 0.0)
        qcat[p, h] = jnp.concatenate([jnp.where(incl, x[tc:, tc:], 0.0),
                                      jnp.where(incl, x[tc:, :tc], 0.0)], axis=1)

    t_inv = {ph: eye + a_ab[ph] for ph in heads}
    pw = {ph: _bdot(a_ab[ph], a_ab[ph]) for ph in heads}
    akv = {(p, h): _bdot(a_ak[p, h], v[p]) for p, h in heads}
    n = 2
    while n < tc:
        for ph in heads:
            res = _bdot(jnp.concatenate([pw[ph], t_inv[ph]], axis=0), pw[ph])
            t_inv[ph] = t_inv[ph] + res[tc:]
            pw[ph] = res[:tc]
        n *= 2
    tw = {(p, h): _bdot(t_inv[p, h], jnp.concatenate([at[p], akv[p, h]], axis=1)) for p, h in heads}

    zero = jnp.zeros((tc, LANE), F32)
    w = [jnp.where(mh0, tw[p, 0][:, :LANE], tw[p, 1][:, :LANE]) for p in pairs]
    u0 = [jnp.where(mh0, tw[p, 0][:, LANE:], tw[p, 1][:, LANE:]) for p in pairs]
    vu = [jnp.concatenate([jnp.concatenate([v[p], zero], axis=1),
                           jnp.concatenate([u0[p], w[p]], axis=1)], axis=0) for p in pairs]
    yg = {(p, h): _bdot(qcat[p, h], vu[p]) for p, h in heads}
    m_mat = [_bdot_tn(bh[p], w[p]) for p in pairs]
    z0 = [_bdot_tn(jnp.concatenate([bh[p], khat[p]], axis=0), jnp.concatenate([u0[p], v[p]], axis=0))
          for p in pairs]
    for p in pairs:
        y0_o[:, sls[p]] = jnp.where(mh0, yg[p, 0][:, :LANE], yg[p, 1][:, :LANE])
        g_o[:, sls[p]] = rt[p] + jnp.where(mh0, yg[p, 0][:, LANE:], yg[p, 1][:, LANE:])
        m_o[p] = jnp.where(same_head, m_mat[p], 0.0) + jnp.where(
            prow == pcol, jnp.broadcast_to(jnp.exp(tot[p]), (LANE, LANE)), 0.0)
        z0_o[p] = jnp.where(same_head, z0[p], 0.0)


def _rwkv_chunk_ops(r, lw, kh, v, alpha, beta, tc=CHUNK):
    bsz, seq, cw = r.shape
    npairs = cw // LANE
    nc = seq // tc
    tok = pl.BlockSpec((None, tc, cw), lambda b, c: (b, c, 0))
    mat = pl.BlockSpec((None, None, npairs, LANE, LANE), lambda b, c: (b, c, 0, 0, 0))
    tok_shape = jax.ShapeDtypeStruct((bsz, seq, cw), F32)
    mat_shape = jax.ShapeDtypeStruct((bsz, nc, npairs, LANE, LANE), F32)
    return pl.pallas_call(
        functools.partial(_rwkv_chunk_kernel, tc=tc, npairs=npairs),
        grid=(bsz, nc),
        in_specs=[tok] * 6,
        out_specs=[tok, tok, mat, mat],
        out_shape=[tok_shape, tok_shape, mat_shape, mat_shape],
        compiler_params=_cparams(("parallel", "parallel")),
        name="rwkv_chunk_ops",
    )(r, lw, kh, v, alpha, beta)


def _rwkv_scan_kernel(g_ref, y0_ref, m_ref, z0_ref, bo_ref, gate_ref, ln_ref, hs_ref, o_ref, state_ref,
                      *, tc, nch, npairs):
    @pl.when(pl.program_id(1) == 0)
    def _():
        state_ref[...] = jnp.zeros(state_ref.shape, F32)

    hs = hs_ref[...]
    inv_n = 1.0 / W_HEAD_DIM
    sls = [slice(p * LANE, (p + 1) * LANE) for p in range(npairs)]
    st = [state_ref[p] for p in range(npairs)]
    for ch in range(nch):
        rows = slice(ch * tc, (ch + 1) * tc)
        y = [_dot_hi(g_ref[rows, sls[p]], st[p]) + y0_ref[rows, sls[p]] for p in range(npairs)]
        st = [_dot_hi(m_ref[ch, p], st[p]) + z0_ref[ch, p] for p in range(npairs)]
        mu = [_dot_hi(y[p], hs) * inv_n for p in range(npairs)]
        d = [y[p] - mu[p] for p in range(npairs)]
        var = [_dot_hi(d[p] * d[p], hs) * inv_n for p in range(npairs)]
        for p in range(npairs):
            yn = d[p] * lax.rsqrt(var[p] + LNX_EPS) * ln_ref[0:1, sls[p]] + ln_ref[1:2, sls[p]]
            o_ref[rows, sls[p]] = ((yn + bo_ref[rows, sls[p]]) * gate_ref[rows, sls[p]]).astype(o_ref.dtype)
    for p in range(npairs):
        state_ref[p] = st[p]


def _rwkv_scan(g, y0, m, z0, bonus, gate, ln, tc=CHUNK, nch=4):
    bsz, seq, cw = g.shape
    npairs = cw // LANE
    nch = min(nch, seq // tc)
    hs = _head_sum_matrix(LANE, W_HEAD_DIM)
    tok = pl.BlockSpec((None, tc * nch, cw), lambda b, c: (b, c, 0))
    mat = pl.BlockSpec((None, nch, npairs, LANE, LANE), lambda b, c: (b, c, 0, 0, 0))
    return pl.pallas_call(
        functools.partial(_rwkv_scan_kernel, tc=tc, nch=nch, npairs=npairs),
        grid=(bsz, seq // (tc * nch)),
        in_specs=[tok, tok, mat, mat, tok, tok,
                  pl.BlockSpec(ln.shape, lambda b, c: (0, 0)),
                  pl.BlockSpec(hs.shape, lambda b, c: (0, 0))],
        out_specs=tok,
        out_shape=jax.ShapeDtypeStruct((bsz, seq, cw), BF16),
        scratch_shapes=[pltpu.VMEM((npairs, LANE, LANE), F32)],
        compiler_params=_cparams(("parallel", "arbitrary")),
        name="rwkv_scan",
    )(g, y0, m, z0, bonus, gate, ln, hs)


def _out_proj_kernel(x_ref, a_ref, b_ref, c_ref, w_ref, o_ref, *, wa, wb):
    acc = jnp.dot(a_ref[...], w_ref[0:wa, :], preferred_element_type=F32)
    acc = acc + jnp.dot(b_ref[...], w_ref[wa:wa + wb, :], preferred_element_type=F32)
    acc = acc + jnp.dot(c_ref[...], w_ref[wa + wb:, :], preferred_element_type=F32)
    o_ref[...] = x_ref[...] + acc


def _out_proj(x2d, oa, ob, oc, w, tm=512, tn=1024):
    m, d = x2d.shape
    wa, wb, wc = oa.shape[1], ob.shape[1], oc.shape[1]
    tn = min(tn, d)
    return pl.pallas_call(
        functools.partial(_out_proj_kernel, wa=wa, wb=wb),
        grid=(m // tm, d // tn),
        in_specs=[pl.BlockSpec((tm, tn), lambda i, j: (i, j)),
                  pl.BlockSpec((tm, wa), lambda i, j: (i, 0)),
                  pl.BlockSpec((tm, wb), lambda i, j: (i, 0)),
                  pl.BlockSpec((tm, wc), lambda i, j: (i, 0)),
                  pl.BlockSpec((wa + wb + wc, tn), lambda i, j: (0, j))],
        out_specs=pl.BlockSpec((tm, tn), lambda i, j: (i, j)),
        out_shape=jax.ShapeDtypeStruct((m, d), F32),
        compiler_params=_cparams(("parallel", "arbitrary")),
        name="out_proj",
    )(x2d, oa, ob, oc, w)


def _gate_up_kernel(h_ref, wg_ref, wu_ref, o_ref):
    h = h_ref[...]
    gate = jnp.dot(h, wg_ref[...], preferred_element_type=F32)
    up = jnp.dot(h, wu_ref[...], preferred_element_type=F32)
    o_ref[...] = (gate / (1.0 + jnp.exp(-gate)) * up).astype(o_ref.dtype)


def _gate_up(h, w_gate_up, tm=512, tf_cap=512):
    m, d = h.shape
    dff = w_gate_up.shape[1] // 2
    tf = _pick_tile(dff, tf_cap)
    nf = dff // tf
    return pl.pallas_call(
        _gate_up_kernel,
        grid=(m // tm, nf),
        in_specs=[pl.BlockSpec((tm, d), lambda i, j: (i, 0)),
                  pl.BlockSpec((d, tf), lambda i, j: (0, j)),
                  pl.BlockSpec((d, tf), lambda i, j: (0, j + nf))],
        out_specs=pl.BlockSpec((tm, tf), lambda i, j: (i, j)),
        out_shape=jax.ShapeDtypeStruct((m, dff), BF16),
        compiler_params=_cparams(("parallel", "arbitrary")),
        name="ffn_gate_up",
    )(h, w_gate_up, w_gate_up)


def _down_kernel(x_ref, a_ref, w_ref, o_ref):
    o_ref[...] = x_ref[...] + jnp.dot(a_ref[...], w_ref[...], preferred_element_type=F32)


def _down_proj(x2d, act, w, tm=512, tn=512):
    m, d = x2d.shape
    dff = act.shape[1]
    tn = min(tn, d)
    return pl.pallas_call(
        _down_kernel,
        grid=(m // tm, d // tn),
        in_specs=[pl.BlockSpec((tm, tn), lambda i, j: (i, j)),
                  pl.BlockSpec((tm, dff), lambda i, j: (i, 0)),
                  pl.BlockSpec((dff, tn), lambda i, j: (0, j))],
        out_specs=pl.BlockSpec((tm, tn), lambda i, j: (i, j)),
        out_shape=jax.ShapeDtypeStruct((m, d), F32),
        compiler_params=_cparams(("parallel", "arbitrary")),
        name="ffn_down",
    )(x2d, act, w)


def _pad_cols(w, width):
    return jnp.pad(w, ((0, 0), (0, width - w.shape[1])))


def _pad_rows(w, height):
    return jnp.pad(w, ((0, height - w.shape[0]), (0, 0)))


def _pad_vec(v, width):
    return jnp.pad(v, (0, width - v.shape[0]))


def kernel(x, norm_mix_g, w_in, w_in_vres, rwkv_mu, rwkv_mu_vres, rwkv_w0, rwkv_w_lora_b, rwkv_a0,
           rwkv_a_lora_b, rwkv_v0, rwkv_v_lora_b, rwkv_g_lora_b, rwkv_k_k, rwkv_k_a, rwkv_r_k,
           rwkv_lnx_g, rwkv_lnx_b, w_out, norm_ffn_g, w_gate_up, w_down, final_norm_g):
    bsz, seq, d_model = x.shape
    depth = w_in.shape[0]
    m = bsz * seq
    d_mix = w_out.shape[1]
    a_w = d_mix // 2
    qi_w = IDX_HEADS * IDX_DIM
    b_v_w = d_mix // 4
    b_qk_w = b_v_w // 2
    c_w = d_mix // 4
    r_qk_dim = b_qk_w // R_HEADS
    top_k = min(TOPK_MAX, seq // 4)
    vres_pad = 2 * LANE

    o_q, o_k, o_v = 0, a_w, 2 * a_w
    o_qi = 3 * a_w
    o_ki = o_qi + qi_w
    o_wi = o_ki + IDX_DIM
    o_bq = o_wi + IDX_HEADS
    o_bk = o_bq + b_qk_w
    o_bv = o_bk + b_qk_w
    o_bg = o_bv + b_v_w
    o_c = o_bg + b_v_w
    o_wl = o_c + 3 * c_w
    o_al = o_wl + DECAY_LORA
    o_gl = o_al + AAA_LORA
    n_in = o_gl + GATE_LORA

    a_rot = A_HEAD_DIM // ROPE_FRAC
    tq_ = _rope_tables(seq, A_HEAD_DIM, a_rot, ROPE_THETA, scale=A_HEAD_DIM ** -0.5)
    tk_ = _rope_tables(seq, A_HEAD_DIM, a_rot, ROPE_THETA)
    tab_qk = tuple(jnp.stack([a, b]) for a, b in zip(tq_, tk_))
    i_rot = IDX_DIM // ROPE_FRAC
    tab_qi = tuple(t[None] for t in _rope_tables(seq, IDX_DIM, i_rot, ROPE_THETA))
    lane = jnp.arange(LANE)
    kiwi_pass = jnp.where(lane < IDX_DIM, 1.0, IDX_W_SCALE).astype(F32)
    c_kw, s1_kw, s2_kw = _rope_tables(seq, LANE, i_rot, ROPE_THETA, pass_scale=kiwi_pass)
    tab_kiwi = (c_kw[None], s1_kw[None], s2_kw[None])
    tbq = _rope_tables(seq, r_qk_dim, r_qk_dim, R_THETA)
    tbk = _rope_tables(seq, r_qk_dim, r_qk_dim, R_THETA, scale=r_qk_dim ** -0.5)
    tab_bqk = tuple(jnp.stack([a, b]) for a, b in zip(tbq, tbk))

    x2d = x.reshape(m, d_model)
    v_first = None
    for l in range(depth):
        wl_ = w_in[l]
        cols = lambda o, n: wl_[:, o:o + n]
        w_qk = wl_[:, o_q:o_v].astype(BF16)
        w_v = jnp.concatenate([cols(o_v, a_w), cols(o_bv, b_v_w)], axis=1).astype(BF16)
        w_qi = cols(o_qi, qi_w).astype(BF16)
        w_kiwi = _pad_cols(cols(o_ki, IDX_DIM + IDX_HEADS), LANE).astype(BF16)
        w_bqk = cols(o_bq, 2 * b_qk_w).astype(BF16)
        w_bg = cols(o_bg, b_v_w).astype(BF16)
        vres_w = (w_in_vres[l - 1] if l > 0 else jnp.zeros((d_model, MV_LORA), F32))
        w_c = jnp.concatenate([cols(o_c, 3 * c_w), _pad_cols(cols(o_wl, DECAY_LORA), LANE),
                               _pad_cols(cols(o_al, AAA_LORA), LANE), cols(o_gl, GATE_LORA),
                               _pad_cols(vres_w, vres_pad)], axis=1).astype(BF16)
        mu = rwkv_mu[l]
        mu_vres = rwkv_mu_vres[l - 1] if l > 0 else jnp.zeros((MV_LORA,), F32)
        mu_c = jnp.concatenate([mu[:3 * c_w], _pad_vec(mu[3 * c_w:3 * c_w + DECAY_LORA], LANE),
                                _pad_vec(mu[3 * c_w + DECAY_LORA:3 * c_w + DECAY_LORA + AAA_LORA], LANE),
                                mu[3 * c_w + DECAY_LORA + AAA_LORA:], _pad_vec(mu_vres, vres_pad)])[None, :]

        h = _rmsnorm(x2d, norm_mix_g[l], BF16)
        qk = _mm_rope(h, w_qk, tab_qk, a_rot // 2, seq, a_w, BF16).reshape(bsz, seq, 2 * a_w)
        vv = _mm_plain(h, w_v, BF16, tn_cap=512).reshape(bsz, seq, a_w + b_v_w)
        qi = _mm_rope(h, w_qi, tab_qi, i_rot // 2, seq, qi_w, BF16).reshape(bsz, seq, qi_w)
        kiwi = _mm_rope(h, w_kiwi, tab_kiwi, i_rot // 2, seq, LANE, F32).reshape(bsz, seq, LANE)
        bqk = _mm_rope(h, w_bqk, tab_bqk, r_qk_dim // 2, seq, b_qk_w, BF16).reshape(bsz, seq, 2 * b_qk_w)
        bg = _mm_plain(h, w_bg, F32).reshape(bsz, seq, b_v_w)
        cproj = _mm_plain(h, w_c, F32, tn_cap=768).reshape(bsz, seq, w_c.shape[1])

        kit = jnp.swapaxes(kiwi[:, :, :IDX_DIM], 1, 2).astype(BF16)
        wi = kiwi[:, :, IDX_DIM:IDX_DIM + IDX_HEADS]
        bias = _indexer_bias(qi, kit, wi, top_k)
        q = qk[:, :, :a_w]
        kt = jnp.swapaxes(qk[:, :, a_w:], 1, 2)
        out_a = _masked_attention(q, kt, vv[:, :, :a_w], bias)

        out_b = _retention(bqk, vv[:, :, a_w:], bg)

        vecs = jnp.stack([rwkv_w0[l], rwkv_a0[l], rwkv_k_k[l], rwkv_k_a[l], rwkv_r_k[l].reshape(-1),
                          rwkv_v0[l - 1] if l > 0 else jnp.zeros((c_w,), F32),
                          jnp.zeros((c_w,), F32), jnp.zeros((c_w,), F32)])
        wlb = _pad_rows(rwkv_w_lora_b[l], LANE).astype(BF16)
        alb = _pad_rows(rwkv_a_lora_b[l], LANE).astype(BF16)
        glb = rwkv_g_lora_b[l].astype(BF16)
        vlb = _pad_rows(rwkv_v_lora_b[l - 1], vres_pad).astype(BF16) if l > 0 else None
        r_, lw_, kh_, v_, al_, be_, g_, bo_ = _rwkv_prep(cproj, mu_c, vecs, wlb, alb, glb, vlb, v_first, c_w)
        if l == 0:
            v_first = v_
        gm, y0, mm, z0 = _rwkv_chunk_ops(r_, lw_, kh_, v_, al_, be_)
        ln = jnp.stack([rwkv_lnx_g[l], rwkv_lnx_b[l]] + [jnp.zeros((c_w,), F32)] * 6)
        out_c = _rwkv_scan(gm, y0, mm, z0, bo_, g_, ln)

        x2d = _out_proj(x2d, out_a.reshape(m, a_w), out_b.reshape(m, b_v_w), out_c.reshape(m, c_w),
                        w_out[l].astype(BF16))

        h = _rmsnorm(x2d, norm_ffn_g[l], BF16)
        act = _gate_up(h, w_gate_up[l].astype(BF16))
        x2d = _down_proj(x2d, act, w_down[l].astype(BF16))

    return _rmsnorm(x2d, final_norm_g, F32).reshape(bsz, seq, d_model)
```

```python
import functools
import math

import jax
import jax.numpy as jnp
from jax import lax
from jax.experimental import pallas as pl
from jax.experimental.pallas import tpu as pltpu

F32 = jnp.float32
BF16 = jnp.bfloat16
I32 = jnp.int32
HIGHEST = lax.Precision.HIGHEST

CHUNK = 64
A_HEAD_DIM = 128
IDX_HEADS = 16
IDX_DIM = 64
IDX_W_SCALE = (IDX_HEADS * IDX_DIM) ** -0.5
TOPK_MAX = 256
ROPE_THETA = 500000.0
ROPE_FRAC = 4
R_HEADS = 4
R_THETA = 10000.0
W_HEAD_DIM = 64
DECAY_LORA = 96
AAA_LORA = 96
MV_LORA = 64
GATE_LORA = 256
LNX_EPS = 64e-5
RMS_EPS = 1e-5

LANE = 128
VMEM_LIMIT = 56 * 1024 * 1024
NEG_BIAS = -1e30
M_INIT = -1e20
INT_MIN = -2147483648


def _cparams(sem):
    return pltpu.CompilerParams(dimension_semantics=sem, vmem_limit_bytes=VMEM_LIMIT)


def _pick_tile(n, cap):
    best = LANE
    t = LANE
    while t <= min(n, cap):
        if n % t == 0:
            best = t
        t += LANE
    return best


def _rmsnorm_kernel(x_ref, g_ref, o_ref):
    x = x_ref[...]
    ms = jnp.mean(x * x, axis=-1, keepdims=True)
    o_ref[...] = (x * lax.rsqrt(ms + RMS_EPS) * g_ref[...]).astype(o_ref.dtype)


def _rmsnorm(x2d, g, out_dtype, tm=512):
    m, d = x2d.shape
    return pl.pallas_call(
        _rmsnorm_kernel,
        grid=(m // tm,),
        in_specs=[pl.BlockSpec((tm, d), lambda i: (i, 0)),
                  pl.BlockSpec((1, d), lambda i: (0, 0))],
        out_specs=pl.BlockSpec((tm, d), lambda i: (i, 0)),
        out_shape=jax.ShapeDtypeStruct((m, d), out_dtype),
        compiler_params=_cparams(("parallel",)),
        name="rmsnorm",
    )(x2d, g.reshape(1, d).astype(F32))


def _mm_plain_kernel(a_ref, w_ref, o_ref):
    o_ref[...] = jnp.dot(a_ref[...], w_ref[...], preferred_element_type=F32).astype(o_ref.dtype)


def _mm_plain(a, w, out_dtype, tm=512, tn_cap=1024):
    m, k = a.shape
    n = w.shape[1]
    tn = _pick_tile(n, tn_cap)
    return pl.pallas_call(
        _mm_plain_kernel,
        grid=(m // tm, n // tn),
        in_specs=[pl.BlockSpec((tm, k), lambda i, j: (i, 0)),
                  pl.BlockSpec((k, tn), lambda i, j: (0, j))],
        out_specs=pl.BlockSpec((tm, tn), lambda i, j: (i, j)),
        out_shape=jax.ShapeDtypeStruct((m, n), out_dtype),
        compiler_params=_cparams(("parallel", "arbitrary")),
        name="proj_plain",
    )(a, w)


def _mm_rope_kernel(a_ref, w_ref, c_ref, s1_ref, s2_ref, o_ref, *, half):
    acc = jnp.dot(a_ref[...], w_ref[...], preferred_element_type=F32)
    c, s1, s2 = c_ref[...], s1_ref[...], s2_ref[...]
    for blk in range(acc.shape[1] // LANE):
        x = acc[:, blk * LANE:(blk + 1) * LANE]
        up = pltpu.roll(x, LANE - half, axis=1)
        dn = pltpu.roll(x, half, axis=1)
        o_ref[:, blk * LANE:(blk + 1) * LANE] = (x * c + up * s1 + dn * s2).astype(o_ref.dtype)


def _mm_rope(a, w, tables, half, seq, tn, out_dtype, tm=512):
    m, k = a.shape
    n = w.shape[1]
    tpb = seq // tm
    tab_spec = pl.BlockSpec((None, tm, LANE), lambda i, j: (j, i % tpb, 0))
    return pl.pallas_call(
        functools.partial(_mm_rope_kernel, half=half),
        grid=(m // tm, n // tn),
        in_specs=[pl.BlockSpec((tm, k), lambda i, j: (i, 0)),
                  pl.BlockSpec((k, tn), lambda i, j: (0, j)),
                  tab_spec, tab_spec, tab_spec],
        out_specs=pl.BlockSpec((tm, tn), lambda i, j: (i, j)),
        out_shape=jax.ShapeDtypeStruct((m, n), out_dtype),
        compiler_params=_cparams(("parallel", "arbitrary")),
        name="proj_rope",
    )(a, w, *tables)


def _rope_tables(seq, group, rot_dim, theta, scale=1.0, pass_scale=None):
    half = rot_dim // 2
    freqs = jnp.power(F32(theta), -jnp.arange(half, dtype=F32) / half)
    ang = jnp.arange(seq, dtype=F32)[:, None] * freqs[None, :]
    cos, sin = jnp.cos(ang), jnp.sin(ang)
    lane = jnp.arange(LANE) % group
    idx = lane % half
    cosl, sinl = cos[:, idx], sin[:, idx]
    passv = jnp.ones((LANE,), F32) if pass_scale is None else pass_scale
    c = jnp.where(lane < rot_dim, cosl, passv[None, :])
    s1 = jnp.where(lane < half, -sinl, 0.0)
    s2 = jnp.where((lane >= half) & (lane < rot_dim), sinl, 0.0)
    return c * scale, s1 * scale, s2 * scale


def _indexer_kernel(qi_ref, kit_ref, wi_ref, bias_ref, key_ref, *, tq, ts, tki, tkc, top_k, seq):
    qb = pl.program_id(1)
    nkb = ((qb + 1) * tq + tki - 1) // tki
    q_chunk = (qb * tq + lax.broadcasted_iota(I32, (tq, 1), 0)) // CHUNK
    w = wi_ref[...]

    def score_block(kb, rmax):
        k0 = pl.multiple_of(kb * tki, tki)
        kblk = kit_ref[:, pl.ds(k0, tki)]
        k_chunk = (k0 + lax.broadcasted_iota(I32, (1, tki), 1)) // CHUNK
        new_max = []
        for rh in range(tq // ts):
            rows = slice(rh * ts, (rh + 1) * ts)
            acc = jnp.zeros((ts, tki), F32)
            for h in range(IDX_HEADS):
                x = jnp.dot(qi_ref[rows, h * IDX_DIM:(h + 1) * IDX_DIM], kblk, preferred_element_type=F32)
                acc = acc + jnp.maximum(x, 0.0) * w[rows, h:h + 1]
            bits = lax.bitcast_convert_type(acc, I32)
            key = bits ^ ((bits >> 31) & jnp.int32(0x7FFFFFFF))
            key = jnp.where(k_chunk <= q_chunk[rows], key, jnp.int32(INT_MIN))
            key_ref[rows, pl.ds(k0, tki)] = key
            r = rmax[rows]
            for c in range(tki // LANE):
                r = jnp.maximum(r, key[:, c * LANE:(c + 1) * LANE])
            new_max.append(r)
        return jnp.concatenate(new_max, axis=0)

    rmax = lax.fori_loop(0, nkb, score_block, jnp.full((tq, LANE), INT_MIN, I32))
    rowmax = jnp.max(rmax, axis=1, keepdims=True)

    nkc = (nkb * tki + tkc - 1) // tkc

    def pad_block(kb, carry):
        k0 = pl.multiple_of(kb * tki, tki)
        key_ref[:, pl.ds(k0, tki)] = jnp.full((tq, tki), INT_MIN, I32)
        return carry

    lax.fori_loop(nkb, nkc * (tkc // tki), pad_block, 0)

    def count_ge(cand):
        totals = []
        for rh in range(tq // ts):
            rows = slice(rh * ts, (rh + 1) * ts)
            candb = jnp.broadcast_to(cand[rows], (ts, LANE))

            def count_block(kb, cnt, rows=rows, candb=candb):
                k0 = pl.multiple_of(kb * tkc, tkc)
                for c in range(tkc // LANE):
                    blk = key_ref[rows, pl.ds(k0 + c * LANE, LANE)]
                    cnt = cnt + jnp.where(blk >= candb, 1, 0)
                return cnt

            cnt = lax.fori_loop(0, nkc, count_block, jnp.zeros((ts, LANE), I32))
            totals.append(jnp.sum(cnt, axis=1, keepdims=True))
        return jnp.concatenate(totals, axis=0)

    probe = jnp.where(rowmax > jnp.int32(INT_MIN + (1 << 25)), rowmax - jnp.int32(3 << 23),
                      jnp.int32(INT_MIN + 1))

    def bisect_cond(state):
        it, n_active = state[0], state[1]
        return (n_active > 0) & (it < 40)

    def bisect_body(state):
        it, _, lo, hi, cnt_lo, cnt_hi = state
        mid = (lo >> 1) + (hi >> 1) + (lo & hi & 1)
        mid = jnp.where(it == 0, probe, mid)
        cnt = count_ge(mid)
        ge = cnt >= top_k
        exact = cnt == top_k
        new_lo = jnp.where(ge, mid, lo)
        new_hi = jnp.where(exact, mid + 1, jnp.where(ge, hi, mid))
        cnt_lo = jnp.where(ge, cnt, cnt_lo)
        cnt_hi = jnp.where(ge, cnt_hi, cnt)
        active = (new_lo + 1) < new_hi
        n_active = jnp.max(active.astype(I32))
        return it + 1, n_active, new_lo, new_hi, cnt_lo, cnt_hi

    big = jnp.int32(1 << 30)
    init = (jnp.int32(0), jnp.int32(1), jnp.full((tq, 1), INT_MIN, I32), rowmax + 1,
            jnp.full((tq, 1), big, I32), jnp.zeros((tq, 1), I32))
    _, _, lo, _, cnt_lo, cnt_hi = lax.while_loop(bisect_cond, bisect_body, init)

    thr = jnp.maximum(lo, jnp.int32(INT_MIN + 1))
    thrb = jnp.broadcast_to(thr, (tq, LANE))
    tie_row = (cnt_lo > top_k) & (lo > jnp.int32(INT_MIN))
    any_tie = jnp.max(tie_row.astype(I32))

    @pl.when(any_tie == 0)
    def _():
        def write_block(kb, carry):
            k0 = pl.multiple_of(kb * tki, tki)
            for c in range(tki // LANE):
                blk = key_ref[:, pl.ds(k0 + c * LANE, LANE)]
                bias_ref[:, pl.ds(k0 + c * LANE, LANE)] = jnp.where(blk >= thrb, 0.0, NEG_BIAS)
            return carry

        lax.fori_loop(0, nkb, write_block, 0)

    @pl.when(any_tie != 0)
    def _():
        need = jnp.where(tie_row, top_k - cnt_hi, big).astype(F32)
        ur = lax.broadcasted_iota(I32, (LANE, LANE), 0)
        uc = lax.broadcasted_iota(I32, (LANE, LANE), 1)
        upper = (ur <= uc).astype(BF16)

        def write_block(kb, run):
            k0 = pl.multiple_of(kb * tki, tki)
            for c in range(tki // LANE):
                blk = key_ref[:, pl.ds(k0 + c * LANE, LANE)]
                eq = blk == thrb
                pc = jnp.dot(jnp.where(eq, 1.0, 0.0).astype(BF16), upper, preferred_element_type=F32)
                keep_eq = jnp.where((run + pc) <= need, 0.0, NEG_BIAS)
                bias_ref[:, pl.ds(k0 + c * LANE, LANE)] = jnp.where(
                    blk > thrb, 0.0, jnp.where(eq, keep_eq, NEG_BIAS))
                run = run + pc[:, LANE - 1:LANE]
            return run

        lax.fori_loop(0, nkb, write_block, jnp.zeros((tq, 1), F32))

    def fill_block(kb, carry):
        k0 = pl.multiple_of(kb * tki, tki)
        bias_ref[:, pl.ds(k0, tki)] = jnp.full((tq, tki), NEG_BIAS, F32)
        return carry

    lax.fori_loop(nkb, seq // tki, fill_block, 0)


def _indexer_bias(qi, kit, wi, top_k, tq=256, ts=128, tki=512):
    bsz, seq, _ = qi.shape
    tki = min(tki, seq)
    tkc = 2 * tki if (seq // tki) % 2 == 0 else tki
    return pl.pallas_call(
        functools.partial(_indexer_kernel, tq=tq, ts=ts, tki=tki, tkc=tkc, top_k=top_k, seq=seq),
        grid=(bsz, seq // tq),
        in_specs=[pl.BlockSpec((None, tq, IDX_HEADS * IDX_DIM), lambda b, i: (b, i, 0)),
                  pl.BlockSpec((None, IDX_DIM, seq), lambda b, i: (b, 0, 0)),
                  pl.BlockSpec((None, tq, IDX_HEADS), lambda b, i: (b, i, 0))],
        out_specs=pl.BlockSpec((None, tq, seq), lambda b, i: (b, i, 0)),
        out_shape=jax.ShapeDtypeStruct((bsz, seq, seq), F32),
        scratch_shapes=[pltpu.VMEM((tq, seq), I32)],
        compiler_params=_cparams(("parallel", "arbitrary")),
        name="indexer_select",
    )(qi, kit, wi)


def _attn_kernel(qidx_ref, kidx_ref, q_ref, kt_ref, v_ref, b_ref, o_ref, m_ref, l_ref, acc_ref,
                 *, nheads, hd, tq, tk):
    p = pl.program_id(1)
    qb = qidx_ref[p]
    kb = kidx_ref[p]
    kb_last = ((qb + 1) * tq - 1) // tk

    @pl.when(kb == 0)
    def _():
        m_ref[...] = jnp.full(m_ref.shape, M_INIT, F32)
        l_ref[...] = jnp.zeros(l_ref.shape, F32)
        acc_ref[...] = jnp.zeros(acc_ref.shape, F32)

    bias = b_ref[...]

    def logits(h):
        sl = slice(h * hd, (h + 1) * hd)
        return jnp.dot(q_ref[:, sl], kt_ref[sl, :], preferred_element_type=F32)

    s_next = logits(0)
    for h in range(nheads):
        sl = slice(h * hd, (h + 1) * hd)
        s = s_next + bias
        if h + 1 < nheads:
            s_next = logits(h + 1)
        m_old = m_ref[h]
        m_new = jnp.maximum(m_old, jnp.max(s, axis=1, keepdims=True))
        alpha = jnp.exp(m_old - m_new)
        pr = jnp.exp(s - pltpu.repeat(m_new, tk // LANE, axis=1))
        l_ref[h] = alpha * l_ref[h] + jnp.sum(pr, axis=1, keepdims=True)
        m_ref[h] = m_new
        acc_ref[:, sl] = alpha * acc_ref[:, sl] + jnp.dot(pr.astype(BF16), v_ref[:, sl],
                                                          preferred_element_type=F32)

    @pl.when(kb == kb_last)
    def _():
        for h in range(nheads):
            sl = slice(h * hd, (h + 1) * hd)
            o_ref[:, sl] = (acc_ref[:, sl] / l_ref[h]).astype(o_ref.dtype)


def _masked_attention(q, kt, v, bias, tq=256, tk=512):
    bsz, seq, width = q.shape
    nheads = width // A_HEAD_DIM
    nq = seq // tq
    tk = min(tk, seq)
    pairs = [(i, j) for i in range(nq) for j in range(((i + 1) * tq - 1) // tk + 1)]
    qidx = jnp.asarray([pq for pq, _ in pairs], I32)
    kidx = jnp.asarray([pk for _, pk in pairs], I32)
    grid_spec = pltpu.PrefetchScalarGridSpec(
        num_scalar_prefetch=2,
        grid=(bsz, len(pairs)),
        in_specs=[pl.BlockSpec((None, tq, width), lambda b, p, qi, ki: (b, qi[p], 0)),
                  pl.BlockSpec((None, width, tk), lambda b, p, qi, ki: (b, 0, ki[p])),
                  pl.BlockSpec((None, tk, width), lambda b, p, qi, ki: (b, ki[p], 0)),
                  pl.BlockSpec((None, tq, tk), lambda b, p, qi, ki: (b, qi[p], ki[p]))],
        out_specs=pl.BlockSpec((None, tq, width), lambda b, p, qi, ki: (b, qi[p], 0)),
        scratch_shapes=[pltpu.VMEM((nheads, tq, LANE), F32),
                        pltpu.VMEM((nheads, tq, LANE), F32),
                        pltpu.VMEM((tq, width), F32)],
    )
    return pl.pallas_call(
        functools.partial(_attn_kernel, nheads=nheads, hd=A_HEAD_DIM, tq=tq, tk=tk),
        grid_spec=grid_spec,
        out_shape=jax.ShapeDtypeStruct((bsz, seq, width), BF16),
        compiler_params=_cparams(("parallel", "arbitrary")),
        name="masked_attention",
    )(qidx, kidx, q, kt, v, bias)


def _indexer_t_kernel(ki_ref, qit_ref, wit_ref, bias_ref, key_ref, *, tq, tks, tkc, top_k, seq):
    qb = pl.program_id(1)
    nks = ((qb + 1) * tq + tks - 1) // tks
    nkc = (nks * tks + tkc - 1) // tkc
    q_chunk = (qb * tq + lax.broadcasted_iota(I32, (1, tq), 1)) // CHUNK
    w = wit_ref[...]

    def score_block(kb, rmax):
        k0 = pl.multiple_of(kb * tks, tks)
        kblk = ki_ref[pl.ds(k0, tks), :]
        acc = jnp.zeros((tks, tq), F32)
        for h in range(IDX_HEADS):
            x = jnp.dot(kblk, qit_ref[h * IDX_DIM:(h + 1) * IDX_DIM, :], preferred_element_type=F32)
            acc = acc + jnp.maximum(x, 0.0) * w[h:h + 1, :]
        k_chunk = (k0 + lax.broadcasted_iota(I32, (tks, 1), 0)) // CHUNK
        bits = lax.bitcast_convert_type(acc, I32)
        key = bits ^ ((bits >> 31) & jnp.int32(0x7FFFFFFF))
        key = jnp.where(k_chunk <= q_chunk, key, jnp.int32(INT_MIN))
        key_ref[pl.ds(k0, tks), :] = key
        return jnp.maximum(rmax, jnp.max(key.reshape(tks // 8, 8, tq), axis=0))

    rmax = lax.fori_loop(0, nks, score_block, jnp.full((8, tq), INT_MIN, I32))
    colmax = jnp.max(rmax, axis=0, keepdims=True)

    def pad_block(kb, carry):
        k0 = pl.multiple_of(kb * tks, tks)
        key_ref[pl.ds(k0, tks), :] = jnp.full((tks, tq), INT_MIN, I32)
        return carry

    lax.fori_loop(nks, nkc * (tkc // tks), pad_block, 0)

    def count_ge(cand):
        def count_block(kb, cnt):
            k0 = pl.multiple_of(kb * tkc, tkc)
            ind = jnp.where(key_ref[pl.ds(k0, tkc), :] >= cand, 1, 0)
            return cnt + jnp.sum(ind.reshape(tkc // 8, 8, tq), axis=0)

        cnt = lax.fori_loop(0, nkc, count_block, jnp.zeros((8, tq), I32))
        return jnp.sum(cnt, axis=0, keepdims=True)

    def probe_step(state, mid):
        lo, hi, cnt_lo, cnt_hi = state
        cnt = count_ge(mid)
        ge = cnt >= top_k
        new_hi = jnp.where(cnt == top_k, mid + 1, jnp.where(ge, hi, mid))
        return (jnp.where(ge, mid, lo), new_hi, jnp.where(ge, cnt, cnt_lo), jnp.where(ge, cnt_hi, cnt))

    def midpoint(state):
        lo, hi = state[0], state[1]
        return (lo >> 1) + (hi >> 1) + (lo & hi & 1)

    def n_active(state):
        return jnp.max(((state[0] + 1) < state[1]).astype(I32))

    big = jnp.int32(1 << 30)
    state = (jnp.full((1, tq), INT_MIN, I32), colmax + 1, jnp.full((1, tq), big, I32), jnp.zeros((1, tq), I32))
    first = jnp.where(colmax > jnp.int32(INT_MIN + (1 << 25)), colmax - jnp.int32(3 << 23),
                      jnp.int32(INT_MIN + 1))
    state = probe_step(state, first)

    def bisect_body(carry):
        it, _, state = carry
        state = probe_step(state, midpoint(state))
        state = probe_step(state, midpoint(state))
        return it + 2, n_active(state), state

    _, _, state = lax.while_loop(lambda c: (c[1] > 0) & (c[0] < 40), bisect_body,
                                 (jnp.int32(0), n_active(state), state))
    lo, _, cnt_lo, cnt_hi = state

    thr = jnp.maximum(lo, jnp.int32(INT_MIN + 1))
    tie_q = (cnt_lo > top_k) & (lo > jnp.int32(INT_MIN))
    any_tie = jnp.max(tie_q.astype(I32))

    @pl.when(any_tie == 0)
    def _():
        def write_block(kb, carry):
            k0 = pl.multiple_of(kb * tks, tks)
            bias_ref[pl.ds(k0, tks), :] = jnp.where(key_ref[pl.ds(k0, tks), :] >= thr, 0.0, NEG_BIAS)
            return carry

        lax.fori_loop(0, nks, write_block, 0)

    @pl.when(any_tie != 0)
    def _():
        need = jnp.where(tie_q, top_k - cnt_hi, big).astype(F32)
        lr = lax.broadcasted_iota(I32, (LANE, LANE), 0)
        lc = lax.broadcasted_iota(I32, (LANE, LANE), 1)
        lower = (lr >= lc).astype(BF16)

        def write_block(kb, run):
            k0 = pl.multiple_of(kb * LANE, LANE)
            blk = key_ref[pl.ds(k0, LANE), :]
            eq = blk == thr
            pc = jnp.dot(lower, jnp.where(eq, 1.0, 0.0).astype(BF16), preferred_element_type=F32)
            keep_eq = jnp.where((run + pc) <= need, 0.0, NEG_BIAS)
            bias_ref[pl.ds(k0, LANE), :] = jnp.where(blk > thr, 0.0, jnp.where(eq, keep_eq, NEG_BIAS))
            return run + pc[LANE - 1:LANE, :]

        lax.fori_loop(0, nks * (tks // LANE), write_block, jnp.zeros((1, tq), F32))

    def fill_block(kb, carry):
        k0 = pl.multiple_of(kb * tks, tks)
        bias_ref[pl.ds(k0, tks), :] = jnp.full((tks, tq), NEG_BIAS, F32)
        return carry

    lax.fori_loop(nks, seq // tks, fill_block, 0)


def _indexer_bias_t(ki, qit, wit, top_k, tq=256, tks=256, tkc=512):
    bsz, seq, _ = ki.shape
    tq = min(tq, seq)
    return pl.pallas_call(
        functools.partial(_indexer_t_kernel, tq=tq, tks=tks, tkc=tkc, top_k=top_k, seq=seq),
        grid=(bsz, seq // tq),
        in_specs=[pl.BlockSpec((None, seq, IDX_DIM), lambda b, i: (b, 0, 0)),
                  pl.BlockSpec((None, IDX_HEADS * IDX_DIM, tq), lambda b, i: (b, 0, i)),
                  pl.BlockSpec((None, IDX_HEADS, tq), lambda b, i: (b, 0, i))],
        out_specs=pl.BlockSpec((None, seq, tq), lambda b, i: (b, 0, i)),
        out_shape=jax.ShapeDtypeStruct((bsz, seq, seq), F32),
        scratch_shapes=[pltpu.VMEM((seq, tq), I32)],
        compiler_params=_cparams(("parallel", "arbitrary")),
        name="indexer_select",
    )(ki, qit, wit)


def _attn_t_kernel(qidx_ref, kidx_ref, qt_ref, k_ref, vt_ref, b_ref, o_ref, m_ref, l_ref, acc_ref,
                   *, nheads, hd, tq, tk):
    p = pl.program_id(1)
    qb = qidx_ref[p]
    kb = kidx_ref[p]
    kb_last = ((qb + 1) * tq - 1) // tk

    @pl.when(kb == 0)
    def _():
        m_ref[...] = jnp.full(m_ref.shape, M_INIT, F32)
        l_ref[...] = jnp.zeros(l_ref.shape, F32)
        acc_ref[...] = jnp.zeros(acc_ref.shape, F32)

    bias = b_ref[...]
    hrows = [slice(h * hd, (h + 1) * hd) for h in range(nheads)]
    s = [jnp.dot(k_ref[:, r], qt_ref[r, :], preferred_element_type=F32) + bias for r in hrows]
    m_old = [m_ref[h] for h in range(nheads)]
    m_new = [jnp.maximum(m_old[h], jnp.max(s[h], axis=0, keepdims=True)) for h in range(nheads)]
    alpha = [jnp.exp2(m_old[h] - m_new[h]) for h in range(nheads)]
    pr = [jnp.exp2(s[h] - m_new[h]) for h in range(nheads)]
    pv = [jnp.dot(vt_ref[hrows[h], :], pr[h].astype(BF16), preferred_element_type=F32) for h in range(nheads)]
    for h in range(nheads):
        l_ref[h] = alpha[h] * l_ref[h] + jnp.sum(pr[h], axis=0, keepdims=True)
        m_ref[h] = m_new[h]
        acc_ref[hrows[h], :] = alpha[h] * acc_ref[hrows[h], :] + pv[h]

    @pl.when(kb == kb_last)
    def _():
        for h in range(nheads):
            rows = slice(h * hd, (h + 1) * hd)
            o_ref[rows, :] = (acc_ref[rows, :] / l_ref[h]).astype(o_ref.dtype)


def _masked_attention_t(qt, k, vt, bias_t, tq=256, tk=512):
    bsz, width, seq = qt.shape
    nheads = width // A_HEAD_DIM
    tq = min(tq, seq)
    tk = min(tk, seq)
    nq = seq // tq
    pairs = [(i, j) for i in range(nq) for j in range(((i + 1) * tq - 1) // tk + 1)]
    qidx = jnp.asarray([pq for pq, _ in pairs], I32)
    kidx = jnp.asarray([pk for _, pk in pairs], I32)
    grid_spec = pltpu.PrefetchScalarGridSpec(
        num_scalar_prefetch=2,
        grid=(bsz, len(pairs)),
        in_specs=[pl.BlockSpec((None, width, tq), lambda b, p, qi, ki: (b, 0, qi[p])),
                  pl.BlockSpec((None, tk, width), lambda b, p, qi, ki: (b, ki[p], 0)),
                  pl.BlockSpec((None, width, tk), lambda b, p, qi, ki: (b, 0, ki[p])),
                  pl.BlockSpec((None, tk, tq), lambda b, p, qi, ki: (b, ki[p], qi[p]))],
        out_specs=pl.BlockSpec((None, width, tq), lambda b, p, qi, ki: (b, 0, qi[p])),
        scratch_shapes=[pltpu.VMEM((nheads, 1, tq), F32),
                        pltpu.VMEM((nheads, 1, tq), F32),
                        pltpu.VMEM((width, tq), F32)],
    )
    return pl.pallas_call(
        functools.partial(_attn_t_kernel, nheads=nheads, hd=A_HEAD_DIM, tq=tq, tk=tk),
        grid_spec=grid_spec,
        out_shape=jax.ShapeDtypeStruct((bsz, width, seq), BF16),
        compiler_params=_cparams(("parallel", "arbitrary")),
        name="masked_attention",
    )(qidx, kidx, qt, k, vt, bias_t)


def _retention_kernel(qk_ref, v_ref, g_ref, o_ref, state_ref, *, tc, dk, dv):
    @pl.when(pl.program_id(1) == 0)
    def _():
        state_ref[...] = jnp.zeros(state_ref.shape, F32)

    row = lax.broadcasted_iota(I32, (tc, tc), 0)
    col = lax.broadcasted_iota(I32, (tc, tc), 1)
    diff = (row - col).astype(F32)
    pos = lax.broadcasted_iota(I32, (tc, 1), 0).astype(F32)
    for h in range(R_HEADS):
        log_g = math.log1p(-(2.0 ** (-5.0 - h)))
        q = qk_ref[:, h * dk:(h + 1) * dk]
        k = qk_ref[:, (R_HEADS + h) * dk:(R_HEADS + h + 1) * dk]
        v = v_ref[:, h * dv:(h + 1) * dv]
        decay = jnp.where(diff >= 0, jnp.exp(jnp.maximum(diff, 0.0) * log_g), 0.0)
        s = lax.dot_general(q, k, (((1,), (1,)), ((), ())), preferred_element_type=F32) * decay
        intra = jnp.dot(s.astype(BF16), v, preferred_element_type=F32)
        xi = jnp.exp((pos + 1.0) * log_g)
        zeta = jnp.exp((tc - 1.0 - pos) * log_g)
        state = state_ref[h]
        cross = jnp.dot((q.astype(F32) * xi).astype(BF16), state.astype(BF16), preferred_element_type=F32)
        kz = (k.astype(F32) * zeta).astype(BF16)
        kv = lax.dot_general(kz, v, (((0,), (0,)), ((), ())), preferred_element_type=F32)
        state_ref[h] = state * math.exp(tc * log_g) + kv
        ret = intra + cross
        mu = jnp.mean(ret, axis=1, keepdims=True)
        d = ret - mu
        var = jnp.mean(d * d, axis=1, keepdims=True)
        gate = g_ref[:, h * dv:(h + 1) * dv]
        gate = gate / (1.0 + jnp.exp(-gate))
        o_ref[:, h * dv:(h + 1) * dv] = (d * lax.rsqrt(var + 1e-5) * gate).astype(o_ref.dtype)


def _retention(bqk, bv, bg, tc=256):
    bsz, seq, w2 = bqk.shape
    dk = w2 // (2 * R_HEADS)
    dv = bv.shape[2] // R_HEADS
    tc = min(tc, seq)
    return pl.pallas_call(
        functools.partial(_retention_kernel, tc=tc, dk=dk, dv=dv),
        grid=(bsz, seq // tc),
        in_specs=[pl.BlockSpec((None, tc, w2), lambda b, c: (b, c, 0)),
                  pl.BlockSpec((None, tc, R_HEADS * dv), lambda b, c: (b, c, 0)),
                  pl.BlockSpec((None, tc, R_HEADS * dv), lambda b, c: (b, c, 0))],
        out_specs=pl.BlockSpec((None, tc, R_HEADS * dv), lambda b, c: (b, c, 0)),
        out_shape=jax.ShapeDtypeStruct((bsz, seq, R_HEADS * dv), BF16),
        scratch_shapes=[pltpu.VMEM((R_HEADS, dk, dv), F32)],
        compiler_params=_cparams(("parallel", "arbitrary")),
        name="retention",
    )(bqk, bv, bg)


def _head_sum_matrix(width, hd):
    r = jnp.arange(width)
    return (r[:, None] // hd == r[None, :] // hd).astype(F32)


def _sigmoid(x):
    return 1.0 / (1.0 + jnp.exp(-x))


def _rwkv_prep_kernel(*refs, cw, has_vres):
    if has_vres:
        (c_ref, mu_ref, vec_ref, wlb_ref, alb_ref, glb_ref, hs_ref, vlb_ref, vfirst_ref,
         r_o, lw_o, k_o, v_o, al_o, be_o, g_o, bo_o, carry_ref) = refs
    else:
        (c_ref, mu_ref, vec_ref, wlb_ref, alb_ref, glb_ref, hs_ref,
         r_o, lw_o, k_o, v_o, al_o, be_o, g_o, bo_o, carry_ref) = refs
    tr = c_ref.shape[0]

    @pl.when(pl.program_id(1) == 0)
    def _():
        carry_ref[...] = jnp.zeros(carry_ref.shape, F32)

    c = c_ref[...]
    row = lax.broadcasted_iota(I32, (tr, 1), 0)
    prev = jnp.where(row == 0, carry_ref[0:1, :], pltpu.roll(c, 1, axis=0))
    carry_ref[0:1, :] = c[tr - 1:tr, :]
    cs = c + (prev - c) * mu_ref[...]

    r = cs[:, 0:cw]
    k = cs[:, cw:2 * cw]
    v = cs[:, 2 * cw:3 * cw]
    o = 3 * cw
    wl = cs[:, o:o + LANE]
    al = cs[:, o + LANE:o + 2 * LANE]
    gl = cs[:, o + 2 * LANE:o + 2 * LANE + GATE_LORA]
    w0, a0, k_k, k_a, r_k, v0 = (vec_ref[i:i + 1, :] for i in range(6))

    z = -(w0 + jnp.dot(jnp.tanh(wl).astype(BF16), wlb_ref[...], preferred_element_type=F32))
    softplus = jnp.maximum(z, 0.0) + jnp.log(1.0 + jnp.exp(-jnp.abs(z)))
    lw = -jnp.exp(-softplus - 0.5)
    a = _sigmoid(a0 + jnp.dot(al.astype(BF16), alb_ref[...], preferred_element_type=F32))
    g = jnp.dot(_sigmoid(gl).astype(BF16), glb_ref[...], preferred_element_type=F32)
    if has_vres:
        vr = cs[:, o + 2 * LANE + GATE_LORA:o + 2 * LANE + GATE_LORA + vlb_ref.shape[0]]
        mix = _sigmoid(v0 + jnp.dot(vr.astype(BF16), vlb_ref[...], preferred_element_type=F32))
        v = v + (vfirst_ref[...] - v) * mix
    hs = hs_ref[...]
    kk = k * k_k
    ss = jnp.dot(kk * kk, hs, precision=HIGHEST, preferred_element_type=F32)
    kk = kk / jnp.maximum(jnp.sqrt(ss), 1e-12)
    kh = k * (1.0 + (a - 1.0) * k_a)
    rk = jnp.dot(r * kh * r_k, hs, precision=HIGHEST, preferred_element_type=F32)

    r_o[...] = r
    lw_o[...] = lw
    k_o[...] = kh
    v_o[...] = v
    al_o[...] = -kk
    be_o[...] = kk * a
    g_o[...] = g
    bo_o[...] = rk * v


def _rwkv_prep(cproj, mu, vecs, wlb, alb, glb, vlb, v_first, cw, tr=256):
    bsz, seq, wc = cproj.shape
    tr = min(tr, seq)
    has_vres = vlb is not None
    hs = _head_sum_matrix(cw, W_HEAD_DIM)
    full = lambda a: pl.BlockSpec(a.shape, lambda b, t: (0,) * a.ndim)
    tok = pl.BlockSpec((None, tr, cw), lambda b, t: (b, t, 0))
    ins = [cproj, mu, vecs, wlb, alb, glb, hs]
    in_specs = [pl.BlockSpec((None, tr, wc), lambda b, t: (b, t, 0)),
                full(mu), full(vecs), full(wlb), full(alb), full(glb), full(hs)]
    if has_vres:
        ins += [vlb, v_first]
        in_specs += [full(vlb), tok]
    out = jax.ShapeDtypeStruct((bsz, seq, cw), F32)
    return pl.pallas_call(
        functools.partial(_rwkv_prep_kernel, cw=cw, has_vres=has_vres),
        grid=(bsz, seq // tr),
        in_specs=in_specs,
        out_specs=[tok] * 8,
        out_shape=[out] * 8,
        scratch_shapes=[pltpu.VMEM((8, wc), F32)],
        compiler_params=_cparams(("parallel", "arbitrary")),
        name="rwkv_prep",
    )(*ins)


def _dot_hi(a, b):
    return jnp.dot(a, b, precision=HIGHEST, preferred_element_type=F32)


def _bdot(a, b):
    return jnp.dot(a.astype(BF16), b.astype(BF16), preferred_element_type=F32)


def _bdot_tn(a, b):
    return lax.dot_general(a.astype(BF16), b.astype(BF16), (((0,), (0,)), ((), ())),
                           preferred_element_type=F32)


def _rwkv_chunk_kernel(r_ref, lw_ref, k_ref, v_ref, al_ref, be_ref, g_o, y0_o, m_o, z0_o, *, tc, npairs):
    row = lax.broadcasted_iota(I32, (tc, tc), 0)
    col = lax.broadcasted_iota(I32, (tc, tc), 1)
    incl = row >= col
    strict = row > col
    tri = incl.astype(BF16)
    eye = (row == col).astype(F32)
    lane = lax.broadcasted_iota(I32, (1, LANE), 1)
    mh0 = (lane // W_HEAD_DIM) == 0
    prow = lax.broadcasted_iota(I32, (LANE, LANE), 0)
    pcol = lax.broadcasted_iota(I32, (LANE, LANE), 1)
    same_head = (prow // W_HEAD_DIM) == (pcol // W_HEAD_DIM)

    lw_all = lw_ref[...]
    hi = lw_all.astype(BF16)
    rem = lw_all - hi.astype(F32)
    mid = rem.astype(BF16)
    lo = (rem - mid.astype(F32)).astype(BF16)
    cum_all = (jnp.dot(tri, hi, preferred_element_type=F32) + jnp.dot(tri, mid, preferred_element_type=F32)
               + jnp.dot(tri, lo, preferred_element_type=F32))

    pairs = range(npairs)
    heads = [(p, h) for p in pairs for h in range(2)]
    sls = [slice(p * LANE, (p + 1) * LANE) for p in pairs]
    cum = [cum_all[:, sl] for sl in sls]
    tot = [c[tc - 1:tc, :] for c in cum]
    p_inv = [jnp.exp(-cum[p]) for p in pairs]
    p_end = [jnp.exp(tot[p] - cum[p]) for p in pairs]
    at = [al_ref[:, sls[p]] * jnp.exp(cum[p] - lw_all[:, sls[p]]) for p in pairs]
    rt = [r_ref[:, sls[p]] * jnp.exp(cum[p]) for p in pairs]
    bh = [be_ref[:, sls[p]] * p_end[p] for p in pairs]
    khat = [k_ref[:, sls[p]] * p_end[p] for p in pairs]
    v = [v_ref[:, sl] for sl in sls]
    rhs = [jnp.concatenate([be_ref[:, sls[p]] * p_inv[p], k_ref[:, sls[p]] * p_inv[p]], axis=0).astype(BF16)
           for p in pairs]

    a_ab, a_ak, qcat = {}, {}, {}
    for p, h in heads:
        mh = (lane // W_HEAD_DIM) == h
        lhs = jnp.concatenate([jnp.where(mh, at[p], 0.0), jnp.where(mh, rt[p], 0.0)], axis=0).astype(BF16)
        x = lax.dot_general(lhs, rhs[p], (((1,), (1,)), ((), ())), preferred_element_type=F32)
        a_ab[p, h] = jnp.where(strict, x[:tc, :tc], 0.0)
        a_ak[p, h] = jnp.where(strict, x[:tc, tc:], 0.0)
        qcat[p, h] = jnp.concatenate([jnp.where(incl, x[tc:, tc:], 0.0),
                                      jnp.where(incl, x[tc:, :tc], 0.0)], axis=1)

    t_inv = {ph: eye + a_ab[ph] for ph in heads}
    pw = {ph: _bdot(a_ab[ph], a_ab[ph]) for ph in heads}
    akv = {(p, h): _bdot(a_ak[p, h], v[p]) for p, h in heads}
    n = 2
    while n < tc:
        for ph in heads:
            res = _bdot(jnp.concatenate([pw[ph], t_inv[ph]], axis=0), pw[ph])
            t_inv[ph] = t_inv[ph] + res[tc:]
            pw[ph] = res[:tc]
        n *= 2
    tw = {(p, h): _bdot(t_inv[p, h], jnp.concatenate([at[p], akv[p, h]], axis=1)) for p, h in heads}

    zero = jnp.zeros((tc, LANE), F32)
    w = [jnp.where(mh0, tw[p, 0][:, :LANE], tw[p, 1][:, :LANE]) for p in pairs]
    u0 = [jnp.where(mh0, tw[p, 0][:, LANE:], tw[p, 1][:, LANE:]) for p in pairs]
    vu = [jnp.concatenate([jnp.concatenate([v[p], zero], axis=1),
                           jnp.concatenate([u0[p], w[p]], axis=1)], axis=0) for p in pairs]
    yg = {(p, h): _bdot(qcat[p, h], vu[p]) for p, h in heads}
    m_mat = [_bdot_tn(bh[p], w[p]) for p in pairs]
    z0 = [_bdot_tn(jnp.concatenate([bh[p], khat[p]], axis=0), jnp.concatenate([u0[p], v[p]], axis=0))
          for p in pairs]
    for p in pairs:
        y0_o[:, sls[p]] = jnp.where(mh0, yg[p, 0][:, :LANE], yg[p, 1][:, :LANE])
        g_o[:, sls[p]] = rt[p] + jnp.where(mh0, yg[p, 0][:, LANE:], yg[p, 1][:, LANE:])
        m_o[p] = jnp.where(same_head, m_mat[p], 0.0) + jnp.where(
            prow == pcol, jnp.broadcast_to(jnp.exp(tot[p]), (LANE, LANE)), 0.0)
        z0_o[p] = jnp.where(same_head, z0[p], 0.0)


def _rwkv_chunk_ops(r, lw, kh, v, alpha, beta, tc=CHUNK):
    bsz, seq, cw = r.shape
    npairs = cw // LANE
    nc = seq // tc
    tok = pl.BlockSpec((None, tc, cw), lambda b, c: (b, c, 0))
    mat = pl.BlockSpec((None, None, npairs, LANE, LANE), lambda b, c: (b, c, 0, 0, 0))
    tok_shape = jax.ShapeDtypeStruct((bsz, seq, cw), F32)
    mat_shape = jax.ShapeDtypeStruct((bsz, nc, npairs, LANE, LANE), F32)
    return pl.pallas_call(
        functools.partial(_rwkv_chunk_kernel, tc=tc, npairs=npairs),
        grid=(bsz, nc),
        in_specs=[tok] * 6,
        out_specs=[tok, tok, mat, mat],
        out_shape=[tok_shape, tok_shape, mat_shape, mat_shape],
        compiler_params=_cparams(("parallel", "parallel")),
        name="rwkv_chunk_ops",
    )(r, lw, kh, v, alpha, beta)


def _rwkv_scan_kernel(g_ref, y0_ref, m_ref, z0_ref, bo_ref, gate_ref, ln_ref, hs_ref, o_ref, state_ref,
                      *, tc, nch, npairs):
    @pl.when(pl.program_id(1) == 0)
    def _():
        state_ref[...] = jnp.zeros(state_ref.shape, F32)

    hs = hs_ref[...]
    inv_n = 1.0 / W_HEAD_DIM
    sls = [slice(p * LANE, (p + 1) * LANE) for p in range(npairs)]
    st = [state_ref[p] for p in range(npairs)]
    for ch in range(nch):
        rows = slice(ch * tc, (ch + 1) * tc)
        y = [_dot_hi(g_ref[rows, sls[p]], st[p]) + y0_ref[rows, sls[p]] for p in range(npairs)]
        st = [_dot_hi(m_ref[ch, p], st[p]) + z0_ref[ch, p] for p in range(npairs)]
        mu = [_dot_hi(y[p], hs) * inv_n for p in range(npairs)]
        d = [y[p] - mu[p] for p in range(npairs)]
        var = [_dot_hi(d[p] * d[p], hs) * inv_n for p in range(npairs)]
        for p in range(npairs):
            yn = d[p] * lax.rsqrt(var[p] + LNX_EPS) * ln_ref[0:1, sls[p]] + ln_ref[1:2, sls[p]]
            o_ref[rows, sls[p]] = ((yn + bo_ref[rows, sls[p]]) * gate_ref[rows, sls[p]]).astype(o_ref.dtype)
    for p in range(npairs):
        state_ref[p] = st[p]


def _rwkv_scan(g, y0, m, z0, bonus, gate, ln, tc=CHUNK, nch=4):
    bsz, seq, cw = g.shape
    npairs = cw // LANE
    nch = min(nch, seq // tc)
    hs = _head_sum_matrix(LANE, W_HEAD_DIM)
    tok = pl.BlockSpec((None, tc * nch, cw), lambda b, c: (b, c, 0))
    mat = pl.BlockSpec((None, nch, npairs, LANE, LANE), lambda b, c: (b, c, 0, 0, 0))
    return pl.pallas_call(
        functools.partial(_rwkv_scan_kernel, tc=tc, nch=nch, npairs=npairs),
        grid=(bsz, seq // (tc * nch)),
        in_specs=[tok, tok, mat, mat, tok, tok,
                  pl.BlockSpec(ln.shape, lambda b, c: (0, 0)),
                  pl.BlockSpec(hs.shape, lambda b, c: (0, 0))],
        out_specs=tok,
        out_shape=jax.ShapeDtypeStruct((bsz, seq, cw), BF16),
        scratch_shapes=[pltpu.VMEM((npairs, LANE, LANE), F32)],
        compiler_params=_cparams(("parallel", "arbitrary")),
        name="rwkv_scan",
    )(g, y0, m, z0, bonus, gate, ln, hs)


def _out_proj_kernel(x_ref, a_ref, b_ref, c_ref, w_ref, o_ref, *, wa, wb):
    acc = jnp.dot(a_ref[...], w_ref[0:wa, :], preferred_element_type=F32)
    acc = acc + jnp.dot(b_ref[...], w_ref[wa:wa + wb, :], preferred_element_type=F32)
    acc = acc + jnp.dot(c_ref[...], w_ref[wa + wb:, :], preferred_element_type=F32)
    o_ref[...] = x_ref[...] + acc


def _out_proj(x2d, oa, ob, oc, w, tm=512, tn=1024):
    m, d = x2d.shape
    wa, wb, wc = oa.shape[1], ob.shape[1], oc.shape[1]
    tn = min(tn, d)
    return pl.pallas_call(
        functools.partial(_out_proj_kernel, wa=wa, wb=wb),
        grid=(m // tm, d // tn),
        in_specs=[pl.BlockSpec((tm, tn), lambda i, j: (i, j)),
                  pl.BlockSpec((tm, wa), lambda i, j: (i, 0)),
                  pl.BlockSpec((tm, wb), lambda i, j: (i, 0)),
                  pl.BlockSpec((tm, wc), lambda i, j: (i, 0)),
                  pl.BlockSpec((wa + wb + wc, tn), lambda i, j: (0, j))],
        out_specs=pl.BlockSpec((tm, tn), lambda i, j: (i, j)),
        out_shape=jax.ShapeDtypeStruct((m, d), F32),
        compiler_params=_cparams(("parallel", "arbitrary")),
        name="out_proj",
    )(x2d, oa, ob, oc, w)


def _gate_up_kernel(h_ref, wg_ref, wu_ref, o_ref):
    h = h_ref[...]
    gate = jnp.dot(h, wg_ref[...], preferred_element_type=F32)
    up = jnp.dot(h, wu_ref[...], preferred_element_type=F32)
    o_ref[...] = (gate / (1.0 + jnp.exp(-gate)) * up).astype(o_ref.dtype)


def _gate_up(h, w_gate_up, tm=512, tf_cap=512):
    m, d = h.shape
    dff = w_gate_up.shape[1] // 2
    tf = _pick_tile(dff, tf_cap)
    nf = dff // tf
    return pl.pallas_call(
        _gate_up_kernel,
        grid=(m // tm, nf),
        in_specs=[pl.BlockSpec((tm, d), lambda i, j: (i, 0)),
                  pl.BlockSpec((d, tf), lambda i, j: (0, j)),
                  pl.BlockSpec((d, tf), lambda i, j: (0, j + nf))],
        out_specs=pl.BlockSpec((tm, tf), lambda i, j: (i, j)),
        out_shape=jax.ShapeDtypeStruct((m, dff), BF16),
        compiler_params=_cparams(("parallel", "arbitrary")),
        name="ffn_gate_up",
    )(h, w_gate_up, w_gate_up)


def _down_kernel(x_ref, a_ref, w_ref, o_ref):
    o_ref[...] = x_ref[...] + jnp.dot(a_ref[...], w_ref[...], preferred_element_type=F32)


def _down_proj(x2d, act, w, tm=512, tn=512):
    m, d = x2d.shape
    dff = act.shape[1]
    tn = min(tn, d)
    return pl.pallas_call(
        _down_kernel,
        grid=(m // tm, d // tn),
        in_specs=[pl.BlockSpec((tm, tn), lambda i, j: (i, j)),
                  pl.BlockSpec((tm, dff), lambda i, j: (i, 0)),
                  pl.BlockSpec((dff, tn), lambda i, j: (0, j))],
        out_specs=pl.BlockSpec((tm, tn), lambda i, j: (i, j)),
        out_shape=jax.ShapeDtypeStruct((m, d), F32),
        compiler_params=_cparams(("parallel", "arbitrary")),
        name="ffn_down",
    )(x2d, act, w)


def _pad_cols(w, width):
    return jnp.pad(w, ((0, 0), (0, width - w.shape[1])))


def _pad_rows(w, height):
    return jnp.pad(w, ((0, height - w.shape[0]), (0, 0)))


def _pad_vec(v, width):
    return jnp.pad(v, (0, width - v.shape[0]))


def kernel(x, norm_mix_g, w_in, w_in_vres, rwkv_mu, rwkv_mu_vres, rwkv_w0, rwkv_w_lora_b, rwkv_a0,
           rwkv_a_lora_b, rwkv_v0, rwkv_v_lora_b, rwkv_g_lora_b, rwkv_k_k, rwkv_k_a, rwkv_r_k,
           rwkv_lnx_g, rwkv_lnx_b, w_out, norm_ffn_g, w_gate_up, w_down, final_norm_g):
    bsz, seq, d_model = x.shape
    depth = w_in.shape[0]
    m = bsz * seq
    d_mix = w_out.shape[1]
    a_w = d_mix // 2
    qi_w = IDX_HEADS * IDX_DIM
    b_v_w = d_mix // 4
    b_qk_w = b_v_w // 2
    c_w = d_mix // 4
    r_qk_dim = b_qk_w // R_HEADS
    top_k = min(TOPK_MAX, seq // 4)
    vres_pad = 2 * LANE

    o_q, o_k, o_v = 0, a_w, 2 * a_w
    o_qi = 3 * a_w
    o_ki = o_qi + qi_w
    o_wi = o_ki + IDX_DIM
    o_bq = o_wi + IDX_HEADS
    o_bk = o_bq + b_qk_w
    o_bv = o_bk + b_qk_w
    o_bg = o_bv + b_v_w
    o_c = o_bg + b_v_w
    o_wl = o_c + 3 * c_w
    o_al = o_wl + DECAY_LORA
    o_gl = o_al + AAA_LORA
    n_in = o_gl + GATE_LORA

    a_rot = A_HEAD_DIM // ROPE_FRAC
    tq_ = _rope_tables(seq, A_HEAD_DIM, a_rot, ROPE_THETA, scale=A_HEAD_DIM ** -0.5 * math.log2(math.e))
    tk_ = _rope_tables(seq, A_HEAD_DIM, a_rot, ROPE_THETA)
    tab_qk = tuple(jnp.stack([a, b]) for a, b in zip(tq_, tk_))
    i_rot = IDX_DIM // ROPE_FRAC
    tab_qi = tuple(t[None] for t in _rope_tables(seq, IDX_DIM, i_rot, ROPE_THETA))
    lane = jnp.arange(LANE)
    kiwi_pass = jnp.where(lane < IDX_DIM, 1.0, IDX_W_SCALE).astype(F32)
    c_kw, s1_kw, s2_kw = _rope_tables(seq, LANE, i_rot, ROPE_THETA, pass_scale=kiwi_pass)
    tab_kiwi = (c_kw[None], s1_kw[None], s2_kw[None])
    tbq = _rope_tables(seq, r_qk_dim, r_qk_dim, R_THETA)
    tbk = _rope_tables(seq, r_qk_dim, r_qk_dim, R_THETA, scale=r_qk_dim ** -0.5)
    tab_bqk = tuple(jnp.stack([a, b]) for a, b in zip(tbq, tbk))

    x2d = x.reshape(m, d_model)
    v_first = None
    for l in range(depth):
        wl_ = w_in[l]
        cols = lambda o, n: wl_[:, o:o + n]
        w_qk = wl_[:, o_q:o_v].astype(BF16)
        w_v = jnp.concatenate([cols(o_v, a_w), cols(o_bv, b_v_w)], axis=1).astype(BF16)
        w_qi = cols(o_qi, qi_w).astype(BF16)
        w_kiwi = _pad_cols(cols(o_ki, IDX_DIM + IDX_HEADS), LANE).astype(BF16)
        w_bqk = cols(o_bq, 2 * b_qk_w).astype(BF16)
        w_bg = cols(o_bg, b_v_w).astype(BF16)
        vres_w = (w_in_vres[l - 1] if l > 0 else jnp.zeros((d_model, MV_LORA), F32))
        w_c = jnp.concatenate([cols(o_c, 3 * c_w), _pad_cols(cols(o_wl, DECAY_LORA), LANE),
                               _pad_cols(cols(o_al, AAA_LORA), LANE), cols(o_gl, GATE_LORA),
                               _pad_cols(vres_w, vres_pad)], axis=1).astype(BF16)
        mu = rwkv_mu[l]
        mu_vres = rwkv_mu_vres[l - 1] if l > 0 else jnp.zeros((MV_LORA,), F32)
        mu_c = jnp.concatenate([mu[:3 * c_w], _pad_vec(mu[3 * c_w:3 * c_w + DECAY_LORA], LANE),
                                _pad_vec(mu[3 * c_w + DECAY_LORA:3 * c_w + DECAY_LORA + AAA_LORA], LANE),
                                mu[3 * c_w + DECAY_LORA + AAA_LORA:], _pad_vec(mu_vres, vres_pad)])[None, :]

        h = _rmsnorm(x2d, norm_mix_g[l], BF16)
        qk = _mm_rope(h, w_qk, tab_qk, a_rot // 2, seq, a_w, BF16).reshape(bsz, seq, 2 * a_w)
        vv = _mm_plain(h, w_v, BF16, tn_cap=512).reshape(bsz, seq, a_w + b_v_w)
        qi = _mm_rope(h, w_qi, tab_qi, i_rot // 2, seq, qi_w, BF16).reshape(bsz, seq, qi_w)
        kiwi = _mm_rope(h, w_kiwi, tab_kiwi, i_rot // 2, seq, LANE, F32).reshape(bsz, seq, LANE)
        bqk = _mm_rope(h, w_bqk, tab_bqk, r_qk_dim // 2, seq, b_qk_w, BF16).reshape(bsz, seq, 2 * b_qk_w)
        bg = _mm_plain(h, w_bg, F32).reshape(bsz, seq, b_v_w)
        cproj = _mm_plain(h, w_c, F32, tn_cap=768).reshape(bsz, seq, w_c.shape[1])

        ki = kiwi[:, :, :IDX_DIM].astype(BF16)
        wit = jnp.swapaxes(kiwi[:, :, IDX_DIM:IDX_DIM + IDX_HEADS], 1, 2)
        bias_t = _indexer_bias_t(ki, jnp.swapaxes(qi, 1, 2), wit, top_k)
        qt = jnp.swapaxes(qk[:, :, :a_w], 1, 2)
        vt = jnp.swapaxes(vv[:, :, :a_w], 1, 2)
        out_a = jnp.swapaxes(_masked_attention_t(qt, qk[:, :, a_w:], vt, bias_t), 1, 2)

        out_b = _retention(bqk, vv[:, :, a_w:], bg)

        vecs = jnp.stack([rwkv_w0[l], rwkv_a0[l], rwkv_k_k[l], rwkv_k_a[l], rwkv_r_k[l].reshape(-1),
                          rwkv_v0[l - 1] if l > 0 else jnp.zeros((c_w,), F32),
                          jnp.zeros((c_w,), F32), jnp.zeros((c_w,), F32)])
        wlb = _pad_rows(rwkv_w_lora_b[l], LANE).astype(BF16)
        alb = _pad_rows(rwkv_a_lora_b[l], LANE).astype(BF16)
        glb = rwkv_g_lora_b[l].astype(BF16)
        vlb = _pad_rows(rwkv_v_lora_b[l - 1], vres_pad).astype(BF16) if l > 0 else None
        r_, lw_, kh_, v_, al_, be_, g_, bo_ = _rwkv_prep(cproj, mu_c, vecs, wlb, alb, glb, vlb, v_first, c_w)
        if l == 0:
            v_first = v_
        gm, y0, mm, z0 = _rwkv_chunk_ops(r_, lw_, kh_, v_, al_, be_)
        ln = jnp.stack([rwkv_lnx_g[l], rwkv_lnx_b[l]] + [jnp.zeros((c_w,), F32)] * 6)
        out_c = _rwkv_scan(gm, y0, mm, z0, bo_, g_, ln)

        x2d = _out_proj(x2d, out_a.reshape(m, a_w), out_b.reshape(m, b_v_w), out_c.reshape(m, c_w),
                        w_out[l].astype(BF16))

        h = _rmsnorm(x2d, norm_ffn_g[l], BF16)
        act = _gate_up(h, w_gate_up[l].astype(BF16))
        x2d = _down_proj(x2d, act, w_down[l].astype(BF16))

    return _rmsnorm(x2d, final_norm_g, F32).reshape(bsz, seq, d_model)
```

```python
import functools
import math

import jax
import jax.numpy as jnp
from jax import lax
from jax.experimental import pallas as pl
from jax.experimental.pallas import tpu as pltpu

F32 = jnp.float32
BF16 = jnp.bfloat16
I32 = jnp.int32
HIGHEST = lax.Precision.HIGHEST

CHUNK = 64
A_HEAD_DIM = 128
IDX_HEADS = 16
IDX_DIM = 64
IDX_W_SCALE = (IDX_HEADS * IDX_DIM) ** -0.5
TOPK_MAX = 256
ROPE_THETA = 500000.0
ROPE_FRAC = 4
R_HEADS = 4
R_THETA = 10000.0
W_HEAD_DIM = 64
DECAY_LORA = 96
AAA_LORA = 96
MV_LORA = 64
GATE_LORA = 256
LNX_EPS = 64e-5
RMS_EPS = 1e-5

LANE = 128
VMEM_LIMIT = 56 * 1024 * 1024
NEG_BIAS = -1e30
M_INIT = -1e20
INT_MIN = -2147483648
COUNT_CHAINS = 8


def _cparams(sem):
    return pltpu.CompilerParams(dimension_semantics=sem, vmem_limit_bytes=VMEM_LIMIT)


def _pick_tile(n, cap):
    best = LANE
    t = LANE
    while t <= min(n, cap):
        if n % t == 0:
            best = t
        t += LANE
    return best


def _rmsnorm_kernel(x_ref, g_ref, o_ref):
    x = x_ref[...]
    ms = jnp.mean(x * x, axis=-1, keepdims=True)
    o_ref[...] = (x * lax.rsqrt(ms + RMS_EPS) * g_ref[...]).astype(o_ref.dtype)


def _rmsnorm(x2d, g, out_dtype, tm=512):
    m, d = x2d.shape
    return pl.pallas_call(
        _rmsnorm_kernel,
        grid=(m // tm,),
        in_specs=[pl.BlockSpec((tm, d), lambda i: (i, 0)),
                  pl.BlockSpec((1, d), lambda i: (0, 0))],
        out_specs=pl.BlockSpec((tm, d), lambda i: (i, 0)),
        out_shape=jax.ShapeDtypeStruct((m, d), out_dtype),
        compiler_params=_cparams(("parallel",)),
        name="rmsnorm",
    )(x2d, g.reshape(1, d).astype(F32))


def _mm_plain_kernel(a_ref, w_ref, o_ref):
    o_ref[...] = jnp.dot(a_ref[...], w_ref[...], preferred_element_type=F32).astype(o_ref.dtype)


def _mm_plain(a, w, out_dtype, tm=1024, tn_cap=1024):
    m, k = a.shape
    n = w.shape[1]
    tn = _pick_tile(n, tn_cap)
    return pl.pallas_call(
        _mm_plain_kernel,
        grid=(m // tm, n // tn),
        in_specs=[pl.BlockSpec((tm, k), lambda i, j: (i, 0)),
                  pl.BlockSpec((k, tn), lambda i, j: (0, j))],
        out_specs=pl.BlockSpec((tm, tn), lambda i, j: (i, j)),
        out_shape=jax.ShapeDtypeStruct((m, n), out_dtype),
        compiler_params=_cparams(("parallel", "arbitrary")),
        name="proj_plain",
    )(a, w)


def _mm_rope_kernel(a_ref, w_ref, c_ref, s1_ref, s2_ref, o_ref, *, half):
    acc = jnp.dot(a_ref[...], w_ref[...], preferred_element_type=F32)
    c, s1, s2 = c_ref[...], s1_ref[...], s2_ref[...]
    for blk in range(acc.shape[1] // LANE):
        x = acc[:, blk * LANE:(blk + 1) * LANE]
        up = pltpu.roll(x, LANE - half, axis=1)
        dn = pltpu.roll(x, half, axis=1)
        o_ref[:, blk * LANE:(blk + 1) * LANE] = (x * c + up * s1 + dn * s2).astype(o_ref.dtype)


def _mm_rope(a, w, tables, half, seq, tn, out_dtype, tm=1024):
    m, k = a.shape
    n = w.shape[1]
    tpb = seq // tm
    tab_spec = pl.BlockSpec((None, tm, LANE), lambda i, j: (j, i % tpb, 0))
    return pl.pallas_call(
        functools.partial(_mm_rope_kernel, half=half),
        grid=(m // tm, n // tn),
        in_specs=[pl.BlockSpec((tm, k), lambda i, j: (i, 0)),
                  pl.BlockSpec((k, tn), lambda i, j: (0, j)),
                  tab_spec, tab_spec, tab_spec],
        out_specs=pl.BlockSpec((tm, tn), lambda i, j: (i, j)),
        out_shape=jax.ShapeDtypeStruct((m, n), out_dtype),
        compiler_params=_cparams(("parallel", "arbitrary")),
        name="proj_rope",
    )(a, w, *tables)


def _rope_tables(seq, group, rot_dim, theta, scale=1.0, pass_scale=None):
    half = rot_dim // 2
    freqs = jnp.power(F32(theta), -jnp.arange(half, dtype=F32) / half)
    ang = jnp.arange(seq, dtype=F32)[:, None] * freqs[None, :]
    cos, sin = jnp.cos(ang), jnp.sin(ang)
    lane = jnp.arange(LANE) % group
    idx = lane % half
    cosl, sinl = cos[:, idx], sin[:, idx]
    passv = jnp.ones((LANE,), F32) if pass_scale is None else pass_scale
    c = jnp.where(lane < rot_dim, cosl, passv[None, :])
    s1 = jnp.where(lane < half, -sinl, 0.0)
    s2 = jnp.where((lane >= half) & (lane < rot_dim), sinl, 0.0)
    return c * scale, s1 * scale, s2 * scale


def _indexer_kernel(qi_ref, kit_ref, wi_ref, bias_ref, key_ref, *, tq, ts, tki, tkc, top_k, seq):
    qb = pl.program_id(1)
    nkb = ((qb + 1) * tq + tki - 1) // tki
    q_chunk = (qb * tq + lax.broadcasted_iota(I32, (tq, 1), 0)) // CHUNK
    w = wi_ref[...]

    def score_block(kb, rmax):
        k0 = pl.multiple_of(kb * tki, tki)
        kblk = kit_ref[:, pl.ds(k0, tki)]
        k_chunk = (k0 + lax.broadcasted_iota(I32, (1, tki), 1)) // CHUNK
        new_max = []
        for rh in range(tq // ts):
            rows = slice(rh * ts, (rh + 1) * ts)
            acc = jnp.zeros((ts, tki), F32)
            for h in range(IDX_HEADS):
                x = jnp.dot(qi_ref[rows, h * IDX_DIM:(h + 1) * IDX_DIM], kblk, preferred_element_type=F32)
                acc = acc + jnp.maximum(x, 0.0) * w[rows, h:h + 1]
            bits = lax.bitcast_convert_type(acc, I32)
            key = bits ^ ((bits >> 31) & jnp.int32(0x7FFFFFFF))
            key = jnp.where(k_chunk <= q_chunk[rows], key, jnp.int32(INT_MIN))
            key_ref[rows, pl.ds(k0, tki)] = key
            r = rmax[rows]
            for c in range(tki // LANE):
                r = jnp.maximum(r, key[:, c * LANE:(c + 1) * LANE])
            new_max.append(r)
        return jnp.concatenate(new_max, axis=0)

    rmax = lax.fori_loop(0, nkb, score_block, jnp.full((tq, LANE), INT_MIN, I32))
    rowmax = jnp.max(rmax, axis=1, keepdims=True)

    nkc = (nkb * tki + tkc - 1) // tkc

    def pad_block(kb, carry):
        k0 = pl.multiple_of(kb * tki, tki)
        key_ref[:, pl.ds(k0, tki)] = jnp.full((tq, tki), INT_MIN, I32)
        return carry

    lax.fori_loop(nkb, nkc * (tkc // tki), pad_block, 0)

    def count_ge(cand):
        totals = []
        for rh in range(tq // ts):
            rows = slice(rh * ts, (rh + 1) * ts)
            candb = jnp.broadcast_to(cand[rows], (ts, LANE))

            def count_block(kb, cnt, rows=rows, candb=candb):
                k0 = pl.multiple_of(kb * tkc, tkc)
                for c in range(tkc // LANE):
                    blk = key_ref[rows, pl.ds(k0 + c * LANE, LANE)]
                    cnt = cnt + jnp.where(blk >= candb, 1, 0)
                return cnt

            cnt = lax.fori_loop(0, nkc, count_block, jnp.zeros((ts, LANE), I32))
            totals.append(jnp.sum(cnt, axis=1, keepdims=True))
        return jnp.concatenate(totals, axis=0)

    probe = jnp.where(rowmax > jnp.int32(INT_MIN + (1 << 25)), rowmax - jnp.int32(3 << 23),
                      jnp.int32(INT_MIN + 1))

    def bisect_cond(state):
        it, n_active = state[0], state[1]
        return (n_active > 0) & (it < 40)

    def bisect_body(state):
        it, _, lo, hi, cnt_lo, cnt_hi = state
        mid = (lo >> 1) + (hi >> 1) + (lo & hi & 1)
        mid = jnp.where(it == 0, probe, mid)
        cnt = count_ge(mid)
        ge = cnt >= top_k
        exact = cnt == top_k
        new_lo = jnp.where(ge, mid, lo)
        new_hi = jnp.where(exact, mid + 1, jnp.where(ge, hi, mid))
        cnt_lo = jnp.where(ge, cnt, cnt_lo)
        cnt_hi = jnp.where(ge, cnt_hi, cnt)
        active = (new_lo + 1) < new_hi
        n_active = jnp.max(active.astype(I32))
        return it + 1, n_active, new_lo, new_hi, cnt_lo, cnt_hi

    big = jnp.int32(1 << 30)
    init = (jnp.int32(0), jnp.int32(1), jnp.full((tq, 1), INT_MIN, I32), rowmax + 1,
            jnp.full((tq, 1), big, I32), jnp.zeros((tq, 1), I32))
    _, _, lo, _, cnt_lo, cnt_hi = lax.while_loop(bisect_cond, bisect_body, init)

    thr = jnp.maximum(lo, jnp.int32(INT_MIN + 1))
    thrb = jnp.broadcast_to(thr, (tq, LANE))
    tie_row = (cnt_lo > top_k) & (lo > jnp.int32(INT_MIN))
    any_tie = jnp.max(tie_row.astype(I32))

    @pl.when(any_tie == 0)
    def _():
        def write_block(kb, carry):
            k0 = pl.multiple_of(kb * tki, tki)
            for c in range(tki // LANE):
                blk = key_ref[:, pl.ds(k0 + c * LANE, LANE)]
                bias_ref[:, pl.ds(k0 + c * LANE, LANE)] = jnp.where(blk >= thrb, 0.0, NEG_BIAS)
            return carry

        lax.fori_loop(0, nkb, write_block, 0)

    @pl.when(any_tie != 0)
    def _():
        need = jnp.where(tie_row, top_k - cnt_hi, big).astype(F32)
        ur = lax.broadcasted_iota(I32, (LANE, LANE), 0)
        uc = lax.broadcasted_iota(I32, (LANE, LANE), 1)
        upper = (ur <= uc).astype(BF16)

        def write_block(kb, run):
            k0 = pl.multiple_of(kb * tki, tki)
            for c in range(tki // LANE):
                blk = key_ref[:, pl.ds(k0 + c * LANE, LANE)]
                eq = blk == thrb
                pc = jnp.dot(jnp.where(eq, 1.0, 0.0).astype(BF16), upper, preferred_element_type=F32)
                keep_eq = jnp.where((run + pc) <= need, 0.0, NEG_BIAS)
                bias_ref[:, pl.ds(k0 + c * LANE, LANE)] = jnp.where(
                    blk > thrb, 0.0, jnp.where(eq, keep_eq, NEG_BIAS))
                run = run + pc[:, LANE - 1:LANE]
            return run

        lax.fori_loop(0, nkb, write_block, jnp.zeros((tq, 1), F32))

    def fill_block(kb, carry):
        k0 = pl.multiple_of(kb * tki, tki)
        bias_ref[:, pl.ds(k0, tki)] = jnp.full((tq, tki), NEG_BIAS, F32)
        return carry

    lax.fori_loop(nkb, seq // tki, fill_block, 0)


def _indexer_bias(qi, kit, wi, top_k, tq=256, ts=128, tki=512):
    bsz, seq, _ = qi.shape
    tki = min(tki, seq)
    tkc = 2 * tki if (seq // tki) % 2 == 0 else tki
    return pl.pallas_call(
        functools.partial(_indexer_kernel, tq=tq, ts=ts, tki=tki, tkc=tkc, top_k=top_k, seq=seq),
        grid=(bsz, seq // tq),
        in_specs=[pl.BlockSpec((None, tq, IDX_HEADS * IDX_DIM), lambda b, i: (b, i, 0)),
                  pl.BlockSpec((None, IDX_DIM, seq), lambda b, i: (b, 0, 0)),
                  pl.BlockSpec((None, tq, IDX_HEADS), lambda b, i: (b, i, 0))],
        out_specs=pl.BlockSpec((None, tq, seq), lambda b, i: (b, i, 0)),
        out_shape=jax.ShapeDtypeStruct((bsz, seq, seq), F32),
        scratch_shapes=[pltpu.VMEM((tq, seq), I32)],
        compiler_params=_cparams(("parallel", "arbitrary")),
        name="indexer_select",
    )(qi, kit, wi)


def _attn_kernel(qidx_ref, kidx_ref, q_ref, kt_ref, v_ref, b_ref, o_ref, m_ref, l_ref, acc_ref,
                 *, nheads, hd, tq, tk):
    p = pl.program_id(1)
    qb = qidx_ref[p]
    kb = kidx_ref[p]
    kb_last = ((qb + 1) * tq - 1) // tk

    @pl.when(kb == 0)
    def _():
        m_ref[...] = jnp.full(m_ref.shape, M_INIT, F32)
        l_ref[...] = jnp.zeros(l_ref.shape, F32)
        acc_ref[...] = jnp.zeros(acc_ref.shape, F32)

    bias = b_ref[...]

    def logits(h):
        sl = slice(h * hd, (h + 1) * hd)
        return jnp.dot(q_ref[:, sl], kt_ref[sl, :], preferred_element_type=F32)

    s_next = logits(0)
    for h in range(nheads):
        sl = slice(h * hd, (h + 1) * hd)
        s = s_next + bias
        if h + 1 < nheads:
            s_next = logits(h + 1)
        m_old = m_ref[h]
        m_new = jnp.maximum(m_old, jnp.max(s, axis=1, keepdims=True))
        alpha = jnp.exp(m_old - m_new)
        pr = jnp.exp(s - pltpu.repeat(m_new, tk // LANE, axis=1))
        l_ref[h] = alpha * l_ref[h] + jnp.sum(pr, axis=1, keepdims=True)
        m_ref[h] = m_new
        acc_ref[:, sl] = alpha * acc_ref[:, sl] + jnp.dot(pr.astype(BF16), v_ref[:, sl],
                                                          preferred_element_type=F32)

    @pl.when(kb == kb_last)
    def _():
        for h in range(nheads):
            sl = slice(h * hd, (h + 1) * hd)
            o_ref[:, sl] = (acc_ref[:, sl] / l_ref[h]).astype(o_ref.dtype)


def _masked_attention(q, kt, v, bias, tq=256, tk=512):
    bsz, seq, width = q.shape
    nheads = width // A_HEAD_DIM
    nq = seq // tq
    tk = min(tk, seq)
    pairs = [(i, j) for i in range(nq) for j in range(((i + 1) * tq - 1) // tk + 1)]
    qidx = jnp.asarray([pq for pq, _ in pairs], I32)
    kidx = jnp.asarray([pk for _, pk in pairs], I32)
    grid_spec = pltpu.PrefetchScalarGridSpec(
        num_scalar_prefetch=2,
        grid=(bsz, len(pairs)),
        in_specs=[pl.BlockSpec((None, tq, width), lambda b, p, qi, ki: (b, qi[p], 0)),
                  pl.BlockSpec((None, width, tk), lambda b, p, qi, ki: (b, 0, ki[p])),
                  pl.BlockSpec((None, tk, width), lambda b, p, qi, ki: (b, ki[p], 0)),
                  pl.BlockSpec((None, tq, tk), lambda b, p, qi, ki: (b, qi[p], ki[p]))],
        out_specs=pl.BlockSpec((None, tq, width), lambda b, p, qi, ki: (b, qi[p], 0)),
        scratch_shapes=[pltpu.VMEM((nheads, tq, LANE), F32),
                        pltpu.VMEM((nheads, tq, LANE), F32),
                        pltpu.VMEM((tq, width), F32)],
    )
    return pl.pallas_call(
        functools.partial(_attn_kernel, nheads=nheads, hd=A_HEAD_DIM, tq=tq, tk=tk),
        grid_spec=grid_spec,
        out_shape=jax.ShapeDtypeStruct((bsz, seq, width), BF16),
        compiler_params=_cparams(("parallel", "arbitrary")),
        name="masked_attention",
    )(qidx, kidx, q, kt, v, bias)


def _indexer_t_kernel(ki_ref, qit_ref, wit_ref, bias_ref, key_ref, *, tq, tks, tkc, top_k, seq):
    qb = pl.program_id(1)
    nks = ((qb + 1) * tq + tks - 1) // tks
    nkc = (nks * tks + tkc - 1) // tkc
    q_chunk = (qb * tq + lax.broadcasted_iota(I32, (1, tq), 1)) // CHUNK
    w = wit_ref[...]

    def score_block(kb, rmax):
        k0 = pl.multiple_of(kb * tks, tks)
        kblk = ki_ref[pl.ds(k0, tks), :]
        acc = jnp.zeros((tks, tq), F32)
        for h in range(IDX_HEADS):
            x = jnp.dot(kblk, qit_ref[h * IDX_DIM:(h + 1) * IDX_DIM, :], preferred_element_type=F32)
            acc = acc + jnp.maximum(x, 0.0) * w[h:h + 1, :]
        k_chunk = (k0 + lax.broadcasted_iota(I32, (tks, 1), 0)) // CHUNK
        bits = lax.bitcast_convert_type(acc, I32)
        key = bits ^ ((bits >> 31) & jnp.int32(0x7FFFFFFF))
        key = jnp.where(k_chunk <= q_chunk, key, jnp.int32(INT_MIN))
        key_ref[pl.ds(k0, tks), :] = key
        return jnp.maximum(rmax, jnp.max(key.reshape(tks // 8, 8, tq), axis=0))

    rmax = lax.fori_loop(0, nks, score_block, jnp.full((8, tq), INT_MIN, I32))
    colmax = jnp.max(rmax, axis=0, keepdims=True)

    def pad_block(kb, carry):
        k0 = pl.multiple_of(kb * tks, tks)
        key_ref[pl.ds(k0, tks), :] = jnp.full((tks, tq), INT_MIN, I32)
        return carry

    lax.fori_loop(nks, nkc * (tkc // tks), pad_block, 0)

    def count_ge(cand):
        def count_block(kb, cnt):
            k0 = pl.multiple_of(kb * tkc, tkc)
            part = tkc // COUNT_CHAINS
            sums = []
            for g in range(COUNT_CHAINS):
                ind = jnp.where(key_ref[pl.ds(k0 + g * part, part), :] >= cand, 1, 0)
                sums.append(jnp.sum(ind.reshape(part // 8, 8, tq), axis=0))
            while len(sums) > 1:
                sums = [a + b for a, b in zip(sums[0::2], sums[1::2])]
            return cnt + sums[0]

        cnt = lax.fori_loop(0, nkc, count_block, jnp.zeros((8, tq), I32))
        return jnp.sum(cnt, axis=0, keepdims=True)

    def probe_step(state, mid):
        lo, hi, cnt_lo, cnt_hi = state
        cnt = count_ge(mid)
        ge = cnt >= top_k
        new_hi = jnp.where(cnt == top_k, mid + 1, jnp.where(ge, hi, mid))
        return (jnp.where(ge, mid, lo), new_hi, jnp.where(ge, cnt, cnt_lo), jnp.where(ge, cnt_hi, cnt))

    def midpoint(state):
        lo, hi = state[0], state[1]
        return (lo >> 1) + (hi >> 1) + (lo & hi & 1)

    def n_active(state):
        return jnp.max(((state[0] + 1) < state[1]).astype(I32))

    big = jnp.int32(1 << 30)
    state = (jnp.full((1, tq), INT_MIN, I32), colmax + 1, jnp.full((1, tq), big, I32), jnp.zeros((1, tq), I32))
    first = jnp.where(colmax > jnp.int32(INT_MIN + (1 << 25)), colmax - jnp.int32(3 << 23),
                      jnp.int32(INT_MIN + 1))
    state = probe_step(state, first)

    def bisect_body(carry):
        it, _, state = carry
        state = probe_step(state, midpoint(state))
        state = probe_step(state, midpoint(state))
        return it + 2, n_active(state), state

    _, _, state = lax.while_loop(lambda c: (c[1] > 0) & (c[0] < 40), bisect_body,
                                 (jnp.int32(0), n_active(state), state))
    lo, _, cnt_lo, cnt_hi = state

    thr = jnp.maximum(lo, jnp.int32(INT_MIN + 1))
    tie_q = (cnt_lo > top_k) & (lo > jnp.int32(INT_MIN))
    any_tie = jnp.max(tie_q.astype(I32))

    @pl.when(any_tie == 0)
    def _():
        def write_block(kb, carry):
            k0 = pl.multiple_of(kb * tks, tks)
            bias_ref[pl.ds(k0, tks), :] = jnp.where(key_ref[pl.ds(k0, tks), :] >= thr, 0.0, NEG_BIAS)
            return carry

        lax.fori_loop(0, nks, write_block, 0)

    @pl.when(any_tie != 0)
    def _():
        need = jnp.where(tie_q, top_k - cnt_hi, big).astype(F32)
        lr = lax.broadcasted_iota(I32, (LANE, LANE), 0)
        lc = lax.broadcasted_iota(I32, (LANE, LANE), 1)
        lower = (lr >= lc).astype(BF16)

        def write_block(kb, run):
            k0 = pl.multiple_of(kb * LANE, LANE)
            blk = key_ref[pl.ds(k0, LANE), :]
            eq = blk == thr
            pc = jnp.dot(lower, jnp.where(eq, 1.0, 0.0).astype(BF16), preferred_element_type=F32)
            keep_eq = jnp.where((run + pc) <= need, 0.0, NEG_BIAS)
            bias_ref[pl.ds(k0, LANE), :] = jnp.where(blk > thr, 0.0, jnp.where(eq, keep_eq, NEG_BIAS))
            return run + pc[LANE - 1:LANE, :]

        lax.fori_loop(0, nks * (tks // LANE), write_block, jnp.zeros((1, tq), F32))

    def fill_block(kb, carry):
        k0 = pl.multiple_of(kb * tks, tks)
        bias_ref[pl.ds(k0, tks), :] = jnp.full((tks, tq), NEG_BIAS, F32)
        return carry

    lax.fori_loop(nks, seq // tks, fill_block, 0)


def _indexer_bias_t(ki, qit, wit, top_k, tq=256, tks=256, tkc=1024):
    bsz, seq, _ = ki.shape
    tq = min(tq, seq)
    return pl.pallas_call(
        functools.partial(_indexer_t_kernel, tq=tq, tks=tks, tkc=tkc, top_k=top_k, seq=seq),
        grid=(bsz, seq // tq),
        in_specs=[pl.BlockSpec((None, seq, IDX_DIM), lambda b, i: (b, 0, 0)),
                  pl.BlockSpec((None, IDX_HEADS * IDX_DIM, tq), lambda b, i: (b, 0, i)),
                  pl.BlockSpec((None, IDX_HEADS, tq), lambda b, i: (b, 0, i))],
        out_specs=pl.BlockSpec((None, seq, tq), lambda b, i: (b, 0, i)),
        out_shape=jax.ShapeDtypeStruct((bsz, seq, seq), F32),
        scratch_shapes=[pltpu.VMEM((seq, tq), I32)],
        compiler_params=_cparams(("parallel", "arbitrary")),
        name="indexer_select",
    )(ki, qit, wit)


def _attn_t_kernel(qidx_ref, kidx_ref, qt_ref, k_ref, vt_ref, b_ref, o_ref, m_ref, acc_ref,
                   *, nheads, hd, hda, tq, tk):
    p = pl.program_id(1)
    qb = qidx_ref[p]
    kb = kidx_ref[p]
    kb_last = ((qb + 1) * tq - 1) // tk

    @pl.when(kb == 0)
    def _():
        m_ref[...] = jnp.full(m_ref.shape, M_INIT, F32)
        acc_ref[...] = jnp.zeros(acc_ref.shape, F32)

    bias = b_ref[...]
    hrows = [slice(h * hd, (h + 1) * hd) for h in range(nheads)]
    arows = [slice(h * hda, (h + 1) * hda) for h in range(nheads)]
    s = [jnp.dot(k_ref[:, r], qt_ref[r, :], preferred_element_type=F32) + bias for r in hrows]
    m_old = [m_ref[h] for h in range(nheads)]
    m_new = [jnp.maximum(m_old[h], jnp.max(s[h], axis=0, keepdims=True)) for h in range(nheads)]
    alpha = [jnp.exp2(m_old[h] - m_new[h]) for h in range(nheads)]
    pr = [jnp.exp2(s[h] - m_new[h]).astype(BF16) for h in range(nheads)]
    pv = [jnp.dot(vt_ref[arows[h], :], pr[h], preferred_element_type=F32) for h in range(nheads)]
    for h in range(nheads):
        m_ref[h] = m_new[h]
        acc_ref[arows[h], :] = alpha[h] * acc_ref[arows[h], :] + pv[h]

    @pl.when(kb == kb_last)
    def _():
        for h in range(nheads):
            a0 = h * hda
            o_ref[hrows[h], :] = (acc_ref[a0:a0 + hd, :] / acc_ref[a0 + hd:a0 + hd + 1, :]).astype(o_ref.dtype)


def _masked_attention_t(qt, k, vt, bias_t, tq=512, tk=1024):
    bsz, width, seq = qt.shape
    nheads = width // A_HEAD_DIM
    ones_rows = 8
    hda = A_HEAD_DIM + ones_rows
    vt = jnp.concatenate([vt.reshape(bsz, nheads, A_HEAD_DIM, seq),
                          jnp.ones((bsz, nheads, ones_rows, seq), vt.dtype)], axis=2).reshape(bsz, nheads * hda, seq)
    tq = min(tq, seq)
    tk = min(tk, seq)
    nq = seq // tq
    pairs = [(i, j) for i in range(nq) for j in range(((i + 1) * tq - 1) // tk + 1)]
    qidx = jnp.asarray([pq for pq, _ in pairs], I32)
    kidx = jnp.asarray([pk for _, pk in pairs], I32)
    grid_spec = pltpu.PrefetchScalarGridSpec(
        num_scalar_prefetch=2,
        grid=(bsz, len(pairs)),
        in_specs=[pl.BlockSpec((None, width, tq), lambda b, p, qi, ki: (b, 0, qi[p])),
                  pl.BlockSpec((None, tk, width), lambda b, p, qi, ki: (b, ki[p], 0)),
                  pl.BlockSpec((None, nheads * hda, tk), lambda b, p, qi, ki: (b, 0, ki[p])),
                  pl.BlockSpec((None, tk, tq), lambda b, p, qi, ki: (b, ki[p], qi[p]))],
        out_specs=pl.BlockSpec((None, width, tq), lambda b, p, qi, ki: (b, 0, qi[p])),
        scratch_shapes=[pltpu.VMEM((nheads, 1, tq), F32),
                        pltpu.VMEM((nheads * hda, tq), F32)],
    )
    return pl.pallas_call(
        functools.partial(_attn_t_kernel, nheads=nheads, hd=A_HEAD_DIM, hda=hda, tq=tq, tk=tk),
        grid_spec=grid_spec,
        out_shape=jax.ShapeDtypeStruct((bsz, width, seq), BF16),
        compiler_params=_cparams(("parallel", "arbitrary")),
        name="masked_attention",
    )(qidx, kidx, qt, k, vt, bias_t)


def _retention_kernel(qk_ref, v_ref, g_ref, o_ref, state_ref, *, tc, dk, dv):
    @pl.when(pl.program_id(1) == 0)
    def _():
        state_ref[...] = jnp.zeros(state_ref.shape, F32)

    row = lax.broadcasted_iota(I32, (tc, tc), 0)
    col = lax.broadcasted_iota(I32, (tc, tc), 1)
    diff = (row - col).astype(F32)
    pos = lax.broadcasted_iota(I32, (tc, 1), 0).astype(F32)
    for h in range(R_HEADS):
        log_g = math.log1p(-(2.0 ** (-5.0 - h)))
        q = qk_ref[:, h * dk:(h + 1) * dk]
        k = qk_ref[:, (R_HEADS + h) * dk:(R_HEADS + h + 1) * dk]
        v = v_ref[:, h * dv:(h + 1) * dv]
        decay = jnp.where(diff >= 0, jnp.exp(jnp.maximum(diff, 0.0) * log_g), 0.0)
        s = lax.dot_general(q, k, (((1,), (1,)), ((), ())), preferred_element_type=F32) * decay
        intra = jnp.dot(s.astype(BF16), v, preferred_element_type=F32)
        xi = jnp.exp((pos + 1.0) * log_g)
        zeta = jnp.exp((tc - 1.0 - pos) * log_g)
        state = state_ref[h]
        cross = jnp.dot((q.astype(F32) * xi).astype(BF16), state.astype(BF16), preferred_element_type=F32)
        kz = (k.astype(F32) * zeta).astype(BF16)
        kv = lax.dot_general(kz, v, (((0,), (0,)), ((), ())), preferred_element_type=F32)
        state_ref[h] = state * math.exp(tc * log_g) + kv
        ret = intra + cross
        mu = jnp.mean(ret, axis=1, keepdims=True)
        d = ret - mu
        var = jnp.mean(d * d, axis=1, keepdims=True)
        gate = g_ref[:, h * dv:(h + 1) * dv]
        gate = gate / (1.0 + jnp.exp(-gate))
        o_ref[:, h * dv:(h + 1) * dv] = (d * lax.rsqrt(var + 1e-5) * gate).astype(o_ref.dtype)


def _retention(bqk, bv, bg, tc=256):
    bsz, seq, w2 = bqk.shape
    dk = w2 // (2 * R_HEADS)
    dv = bv.shape[2] // R_HEADS
    tc = min(tc, seq)
    return pl.pallas_call(
        functools.partial(_retention_kernel, tc=tc, dk=dk, dv=dv),
        grid=(bsz, seq // tc),
        in_specs=[pl.BlockSpec((None, tc, w2), lambda b, c: (b, c, 0)),
                  pl.BlockSpec((None, tc, R_HEADS * dv), lambda b, c: (b, c, 0)),
                  pl.BlockSpec((None, tc, R_HEADS * dv), lambda b, c: (b, c, 0))],
        out_specs=pl.BlockSpec((None, tc, R_HEADS * dv), lambda b, c: (b, c, 0)),
        out_shape=jax.ShapeDtypeStruct((bsz, seq, R_HEADS * dv), BF16),
        scratch_shapes=[pltpu.VMEM((R_HEADS, dk, dv), F32)],
        compiler_params=_cparams(("parallel", "arbitrary")),
        name="retention",
    )(bqk, bv, bg)


def _head_sum_matrix(width, hd):
    r = jnp.arange(width)
    return (r[:, None] // hd == r[None, :] // hd).astype(F32)


def _sigmoid(x):
    return 1.0 / (1.0 + jnp.exp(-x))


def _rwkv_prep_kernel(*refs, cw, has_vres):
    if has_vres:
        (c_ref, mu_ref, vec_ref, wlb_ref, alb_ref, glb_ref, hs_ref, vlb_ref, vfirst_ref,
         r_o, lw_o, k_o, v_o, al_o, be_o, g_o, bo_o, carry_ref) = refs
    else:
        (c_ref, mu_ref, vec_ref, wlb_ref, alb_ref, glb_ref, hs_ref,
         r_o, lw_o, k_o, v_o, al_o, be_o, g_o, bo_o, carry_ref) = refs
    tr = c_ref.shape[0]

    @pl.when(pl.program_id(1) == 0)
    def _():
        carry_ref[...] = jnp.zeros(carry_ref.shape, F32)

    c = c_ref[...]
    row = lax.broadcasted_iota(I32, (tr, 1), 0)
    prev = jnp.where(row == 0, carry_ref[0:1, :], pltpu.roll(c, 1, axis=0))
    carry_ref[0:1, :] = c[tr - 1:tr, :]
    cs = c + (prev - c) * mu_ref[...]

    r = cs[:, 0:cw]
    k = cs[:, cw:2 * cw]
    v = cs[:, 2 * cw:3 * cw]
    o = 3 * cw
    wl = cs[:, o:o + LANE]
    al = cs[:, o + LANE:o + 2 * LANE]
    gl = cs[:, o + 2 * LANE:o + 2 * LANE + GATE_LORA]
    w0, a0, k_k, k_a, r_k, v0 = (vec_ref[i:i + 1, :] for i in range(6))

    z = -(w0 + jnp.dot(jnp.tanh(wl).astype(BF16), wlb_ref[...], preferred_element_type=F32))
    softplus = jnp.maximum(z, 0.0) + jnp.log(1.0 + jnp.exp(-jnp.abs(z)))
    lw = -jnp.exp(-softplus - 0.5)
    a = _sigmoid(a0 + jnp.dot(al.astype(BF16), alb_ref[...], preferred_element_type=F32))
    g = jnp.dot(_sigmoid(gl).astype(BF16), glb_ref[...], preferred_element_type=F32)
    if has_vres:
        vr = cs[:, o + 2 * LANE + GATE_LORA:o + 2 * LANE + GATE_LORA + vlb_ref.shape[0]]
        mix = _sigmoid(v0 + jnp.dot(vr.astype(BF16), vlb_ref[...], preferred_element_type=F32))
        v = v + (vfirst_ref[...] - v) * mix
    hs = hs_ref[...]
    kk = k * k_k
    ss = jnp.dot(kk * kk, hs, precision=HIGHEST, preferred_element_type=F32)
    kk = kk / jnp.maximum(jnp.sqrt(ss), 1e-12)
    kh = k * (1.0 + (a - 1.0) * k_a)
    rk = jnp.dot(r * kh * r_k, hs, precision=HIGHEST, preferred_element_type=F32)

    r_o[...] = r
    lw_o[...] = lw
    k_o[...] = kh
    v_o[...] = v
    al_o[...] = -kk
    be_o[...] = kk * a
    g_o[...] = g
    bo_o[...] = rk * v


def _rwkv_prep(cproj, mu, vecs, wlb, alb, glb, vlb, v_first, cw, tr=256):
    bsz, seq, wc = cproj.shape
    tr = min(tr, seq)
    has_vres = vlb is not None
    hs = _head_sum_matrix(cw, W_HEAD_DIM)
    full = lambda a: pl.BlockSpec(a.shape, lambda b, t: (0,) * a.ndim)
    tok = pl.BlockSpec((None, tr, cw), lambda b, t: (b, t, 0))
    ins = [cproj, mu, vecs, wlb, alb, glb, hs]
    in_specs = [pl.BlockSpec((None, tr, wc), lambda b, t: (b, t, 0)),
                full(mu), full(vecs), full(wlb), full(alb), full(glb), full(hs)]
    if has_vres:
        ins += [vlb, v_first]
        in_specs += [full(vlb), tok]
    out = jax.ShapeDtypeStruct((bsz, seq, cw), F32)
    return pl.pallas_call(
        functools.partial(_rwkv_prep_kernel, cw=cw, has_vres=has_vres),
        grid=(bsz, seq // tr),
        in_specs=in_specs,
        out_specs=[tok] * 8,
        out_shape=[out] * 8,
        scratch_shapes=[pltpu.VMEM((8, wc), F32)],
        compiler_params=_cparams(("parallel", "arbitrary")),
        name="rwkv_prep",
    )(*ins)


def _dot_hi(a, b):
    return jnp.dot(a, b, precision=HIGHEST, preferred_element_type=F32)


def _bdot(a, b):
    return jnp.dot(a.astype(BF16), b.astype(BF16), preferred_element_type=F32)


def _bdot_tn(a, b):
    return lax.dot_general(a.astype(BF16), b.astype(BF16), (((0,), (0,)), ((), ())),
                           preferred_element_type=F32)


def _rwkv_chunk_kernel(r_ref, lw_ref, k_ref, v_ref, al_ref, be_ref, g_o, y0_o, m_o, z0_o, *, tc, npairs):
    row = lax.broadcasted_iota(I32, (tc, tc), 0)
    col = lax.broadcasted_iota(I32, (tc, tc), 1)
    incl = row >= col
    strict = row > col
    tri = incl.astype(BF16)
    eye = (row == col).astype(F32)
    lane = lax.broadcasted_iota(I32, (1, LANE), 1)
    mh0 = (lane // W_HEAD_DIM) == 0
    prow = lax.broadcasted_iota(I32, (LANE, LANE), 0)
    pcol = lax.broadcasted_iota(I32, (LANE, LANE), 1)
    same_head = (prow // W_HEAD_DIM) == (pcol // W_HEAD_DIM)

    lw_all = lw_ref[...]
    hi = lw_all.astype(BF16)
    rem = lw_all - hi.astype(F32)
    mid = rem.astype(BF16)
    lo = (rem - mid.astype(F32)).astype(BF16)
    cum_all = (jnp.dot(tri, hi, preferred_element_type=F32) + jnp.dot(tri, mid, preferred_element_type=F32)
               + jnp.dot(tri, lo, preferred_element_type=F32))

    pairs = range(npairs)
    heads = [(p, h) for p in pairs for h in range(2)]
    sls = [slice(p * LANE, (p + 1) * LANE) for p in pairs]
    cum = [cum_all[:, sl] for sl in sls]
    tot = [c[tc - 1:tc, :] for c in cum]
    p_inv = [jnp.exp(-cum[p]) for p in pairs]
    p_end = [jnp.exp(tot[p] - cum[p]) for p in pairs]
    at = [al_ref[:, sls[p]] * jnp.exp(cum[p] - lw_all[:, sls[p]]) for p in pairs]
    rt = [r_ref[:, sls[p]] * jnp.exp(cum[p]) for p in pairs]
    bh = [be_ref[:, sls[p]] * p_end[p] for p in pairs]
    khat = [k_ref[:, sls[p]] * p_end[p] for p in pairs]
    v = [v_ref[:, sl] for sl in sls]
    rhs = [jnp.concatenate([be_ref[:, sls[p]] * p_inv[p], k_ref[:, sls[p]] * p_inv[p]], axis=0).astype(BF16)
           for p in pairs]

    a_ab, a_ak, qcat = {}, {}, {}
    for p, h in heads:
        mh = (lane // W_HEAD_DIM) == h
        lhs = jnp.concatenate([jnp.where(mh, at[p], 0.0), jnp.where(mh, rt[p], 0.0)], axis=0).astype(BF16)
        x = lax.dot_general(lhs, rhs[p], (((1,), (1,)), ((), ())), preferred_element_type=F32)
        a_ab[p, h] = jnp.where(strict, x[:tc, :tc], 0.0)
        a_ak[p, h] = jnp.where(strict, x[:tc, tc:], 0.0)
        qcat[p, h] = jnp.concatenate([jnp.where(incl, x[tc:, tc:], 0.0),
                                      jnp.where(incl, x[tc:, :tc], 0.0)], axis=1)

    t_inv = {ph: eye + a_ab[ph] for ph in heads}
    pw = {ph: _bdot(a_ab[ph], a_ab[ph]) for ph in heads}
    akv = {(p, h): _bdot(a_ak[p, h], v[p]) for p, h in heads}
    n = 2
    while n < tc:
        for ph in heads:
            res = _bdot(jnp.concatenate([pw[ph], t_inv[ph]], axis=0), pw[ph])
            t_inv[ph] = t_inv[ph] + res[tc:]
            pw[ph] = res[:tc]
        n *= 2
    tw = {(p, h): _bdot(t_inv[p, h], jnp.concatenate([at[p], akv[p, h]], axis=1)) for p, h in heads}

    zero = jnp.zeros((tc, LANE), F32)
    w = [jnp.where(mh0, tw[p, 0][:, :LANE], tw[p, 1][:, :LANE]) for p in pairs]
    u0 = [jnp.where(mh0, tw[p, 0][:, LANE:], tw[p, 1][:, LANE:]) for p in pairs]
    vu = [jnp.concatenate([jnp.concatenate([v[p], zero], axis=1),
                           jnp.concatenate([u0[p], w[p]], axis=1)], axis=0) for p in pairs]
    yg = {(p, h): _bdot(qcat[p, h], vu[p]) for p, h in heads}
    m_mat = [_bdot_tn(bh[p], w[p]) for p in pairs]
    z0 = [_bdot_tn(jnp.concatenate([bh[p], khat[p]], axis=0), jnp.concatenate([u0[p], v[p]], axis=0))
          for p in pairs]
    for p in pairs:
        y0_o[:, sls[p]] = jnp.where(mh0, yg[p, 0][:, :LANE], yg[p, 1][:, :LANE])
        g_o[:, sls[p]] = rt[p] + jnp.where(mh0, yg[p, 0][:, LANE:], yg[p, 1][:, LANE:])
        m_o[p] = jnp.where(same_head, m_mat[p], 0.0) + jnp.where(
            prow == pcol, jnp.broadcast_to(jnp.exp(tot[p]), (LANE, LANE)), 0.0)
        z0_o[p] = jnp.where(same_head, z0[p], 0.0)


def _rwkv_chunk_ops(r, lw, kh, v, alpha, beta, tc=CHUNK):
    bsz, seq, cw = r.shape
    npairs = cw // LANE
    nc = seq // tc
    tok = pl.BlockSpec((None, tc, cw), lambda b, c: (b, c, 0))
    mat = pl.BlockSpec((None, None, npairs, LANE, LANE), lambda b, c: (b, c, 0, 0, 0))
    tok_shape = jax.ShapeDtypeStruct((bsz, seq, cw), F32)
    mat_shape = jax.ShapeDtypeStruct((bsz, nc, npairs, LANE, LANE), F32)
    return pl.pallas_call(
        functools.partial(_rwkv_chunk_kernel, tc=tc, npairs=npairs),
        grid=(bsz, nc),
        in_specs=[tok] * 6,
        out_specs=[tok, tok, mat, mat],
        out_shape=[tok_shape, tok_shape, mat_shape, mat_shape],
        compiler_params=_cparams(("parallel", "parallel")),
        name="rwkv_chunk_ops",
    )(r, lw, kh, v, alpha, beta)


def _rwkv_scan_kernel(g_ref, y0_ref, m_ref, z0_ref, bo_ref, gate_ref, ln_ref, hs_ref, o_ref, state_ref,
                      *, tc, nch, npairs):
    @pl.when(pl.program_id(1) == 0)
    def _():
        state_ref[...] = jnp.zeros(state_ref.shape, F32)

    hs = hs_ref[...]
    inv_n = 1.0 / W_HEAD_DIM
    sls = [slice(p * LANE, (p + 1) * LANE) for p in range(npairs)]
    st = [state_ref[p] for p in range(npairs)]
    for ch in range(nch):
        rows = slice(ch * tc, (ch + 1) * tc)
        y = [_dot_hi(g_ref[rows, sls[p]], st[p]) + y0_ref[rows, sls[p]] for p in range(npairs)]
        st = [_dot_hi(m_ref[ch, p], st[p]) + z0_ref[ch, p] for p in range(npairs)]
        mu = [_dot_hi(y[p], hs) * inv_n for p in range(npairs)]
        d = [y[p] - mu[p] for p in range(npairs)]
        var = [_dot_hi(d[p] * d[p], hs) * inv_n for p in range(npairs)]
        for p in range(npairs):
            yn = d[p] * lax.rsqrt(var[p] + LNX_EPS) * ln_ref[0:1, sls[p]] + ln_ref[1:2, sls[p]]
            o_ref[rows, sls[p]] = ((yn + bo_ref[rows, sls[p]]) * gate_ref[rows, sls[p]]).astype(o_ref.dtype)
    for p in range(npairs):
        state_ref[p] = st[p]


def _rwkv_scan(g, y0, m, z0, bonus, gate, ln, tc=CHUNK, nch=4):
    bsz, seq, cw = g.shape
    npairs = cw // LANE
    nch = min(nch, seq // tc)
    hs = _head_sum_matrix(LANE, W_HEAD_DIM)
    tok = pl.BlockSpec((None, tc * nch, cw), lambda b, c: (b, c, 0))
    mat = pl.BlockSpec((None, nch, npairs, LANE, LANE), lambda b, c: (b, c, 0, 0, 0))
    return pl.pallas_call(
        functools.partial(_rwkv_scan_kernel, tc=tc, nch=nch, npairs=npairs),
        grid=(bsz, seq // (tc * nch)),
        in_specs=[tok, tok, mat, mat, tok, tok,
                  pl.BlockSpec(ln.shape, lambda b, c: (0, 0)),
                  pl.BlockSpec(hs.shape, lambda b, c: (0, 0))],
        out_specs=tok,
        out_shape=jax.ShapeDtypeStruct((bsz, seq, cw), BF16),
        scratch_shapes=[pltpu.VMEM((npairs, LANE, LANE), F32)],
        compiler_params=_cparams(("parallel", "arbitrary")),
        name="rwkv_scan",
    )(g, y0, m, z0, bonus, gate, ln, hs)


def _out_proj_kernel(x_ref, a_ref, b_ref, c_ref, w_ref, o_ref, *, wa, wb):
    acc = jnp.dot(a_ref[...], w_ref[0:wa, :], preferred_element_type=F32)
    acc = acc + jnp.dot(b_ref[...], w_ref[wa:wa + wb, :], preferred_element_type=F32)
    acc = acc + jnp.dot(c_ref[...], w_ref[wa + wb:, :], preferred_element_type=F32)
    o_ref[...] = x_ref[...] + acc


def _out_proj(x2d, oa, ob, oc, w, tm=1024, tn=1024):
    m, d = x2d.shape
    wa, wb, wc = oa.shape[1], ob.shape[1], oc.shape[1]
    tn = min(tn, d)
    return pl.pallas_call(
        functools.partial(_out_proj_kernel, wa=wa, wb=wb),
        grid=(m // tm, d // tn),
        in_specs=[pl.BlockSpec((tm, tn), lambda i, j: (i, j)),
                  pl.BlockSpec((tm, wa), lambda i, j: (i, 0)),
                  pl.BlockSpec((tm, wb), lambda i, j: (i, 0)),
                  pl.BlockSpec((tm, wc), lambda i, j: (i, 0)),
                  pl.BlockSpec((wa + wb + wc, tn), lambda i, j: (0, j))],
        out_specs=pl.BlockSpec((tm, tn), lambda i, j: (i, j)),
        out_shape=jax.ShapeDtypeStruct((m, d), F32),
        compiler_params=_cparams(("parallel", "arbitrary")),
        name="out_proj",
    )(x2d, oa, ob, oc, w)


def _gate_up_kernel(h_ref, wg_ref, wu_ref, o_ref):
    h = h_ref[...]
    gate = jnp.dot(h, wg_ref[...], preferred_element_type=F32)
    up = jnp.dot(h, wu_ref[...], preferred_element_type=F32)
    o_ref[...] = (gate / (1.0 + jnp.exp(-gate)) * up).astype(o_ref.dtype)


def _gate_up(h, w_gate_up, tm=1024, tf_cap=512):
    m, d = h.shape
    dff = w_gate_up.shape[1] // 2
    tf = _pick_tile(dff, tf_cap)
    nf = dff // tf
    return pl.pallas_call(
        _gate_up_kernel,
        grid=(m // tm, nf),
        in_specs=[pl.BlockSpec((tm, d), lambda i, j: (i, 0)),
                  pl.BlockSpec((d, tf), lambda i, j: (0, j)),
                  pl.BlockSpec((d, tf), lambda i, j: (0, j + nf))],
        out_specs=pl.BlockSpec((tm, tf), lambda i, j: (i, j)),
        out_shape=jax.ShapeDtypeStruct((m, dff), BF16),
        compiler_params=_cparams(("parallel", "arbitrary")),
        name="ffn_gate_up",
    )(h, w_gate_up, w_gate_up)


def _down_kernel(x_ref, a_ref, w_ref, o_ref):
    o_ref[...] = x_ref[...] + jnp.dot(a_ref[...], w_ref[...], preferred_element_type=F32)


def _down_proj(x2d, act, w, tm=1024, tn=512):
    m, d = x2d.shape
    dff = act.shape[1]
    tn = min(tn, d)
    return pl.pallas_call(
        _down_kernel,
        grid=(m // tm, d // tn),
        in_specs=[pl.BlockSpec((tm, tn), lambda i, j: (i, j)),
                  pl.BlockSpec((tm, dff), lambda i, j: (i, 0)),
                  pl.BlockSpec((dff, tn), lambda i, j: (0, j))],
        out_specs=pl.BlockSpec((tm, tn), lambda i, j: (i, j)),
        out_shape=jax.ShapeDtypeStruct((m, d), F32),
        compiler_params=_cparams(("parallel", "arbitrary")),
        name="ffn_down",
    )(x2d, act, w)


def _pad_cols(w, width):
    return jnp.pad(w, ((0, 0), (0, width - w.shape[1])))


def _pad_rows(w, height):
    return jnp.pad(w, ((0, height - w.shape[0]), (0, 0)))


def _pad_vec(v, width):
    return jnp.pad(v, (0, width - v.shape[0]))


def kernel(x, norm_mix_g, w_in, w_in_vres, rwkv_mu, rwkv_mu_vres, rwkv_w0, rwkv_w_lora_b, rwkv_a0,
           rwkv_a_lora_b, rwkv_v0, rwkv_v_lora_b, rwkv_g_lora_b, rwkv_k_k, rwkv_k_a, rwkv_r_k,
           rwkv_lnx_g, rwkv_lnx_b, w_out, norm_ffn_g, w_gate_up, w_down, final_norm_g):
    bsz, seq, d_model = x.shape
    depth = w_in.shape[0]
    m = bsz * seq
    d_mix = w_out.shape[1]
    a_w = d_mix // 2
    qi_w = IDX_HEADS * IDX_DIM
    b_v_w = d_mix // 4
    b_qk_w = b_v_w // 2
    c_w = d_mix // 4
    r_qk_dim = b_qk_w // R_HEADS
    top_k = min(TOPK_MAX, seq // 4)
    vres_pad = 2 * LANE

    o_q, o_k, o_v = 0, a_w, 2 * a_w
    o_qi = 3 * a_w
    o_ki = o_qi + qi_w
    o_wi = o_ki + IDX_DIM
    o_bq = o_wi + IDX_HEADS
    o_bk = o_bq + b_qk_w
    o_bv = o_bk + b_qk_w
    o_bg = o_bv + b_v_w
    o_c = o_bg + b_v_w
    o_wl = o_c + 3 * c_w
    o_al = o_wl + DECAY_LORA
    o_gl = o_al + AAA_LORA
    n_in = o_gl + GATE_LORA

    a_rot = A_HEAD_DIM // ROPE_FRAC
    tq_ = _rope_tables(seq, A_HEAD_DIM, a_rot, ROPE_THETA, scale=A_HEAD_DIM ** -0.5 * math.log2(math.e))
    tk_ = _rope_tables(seq, A_HEAD_DIM, a_rot, ROPE_THETA)
    tab_qk = tuple(jnp.stack([a, b]) for a, b in zip(tq_, tk_))
    i_rot = IDX_DIM // ROPE_FRAC
    tab_qi = tuple(t[None] for t in _rope_tables(seq, IDX_DIM, i_rot, ROPE_THETA))
    lane = jnp.arange(LANE)
    kiwi_pass = jnp.where(lane < IDX_DIM, 1.0, IDX_W_SCALE).astype(F32)
    c_kw, s1_kw, s2_kw = _rope_tables(seq, LANE, i_rot, ROPE_THETA, pass_scale=kiwi_pass)
    tab_kiwi = (c_kw[None], s1_kw[None], s2_kw[None])
    tbq = _rope_tables(seq, r_qk_dim, r_qk_dim, R_THETA)
    tbk = _rope_tables(seq, r_qk_dim, r_qk_dim, R_THETA, scale=r_qk_dim ** -0.5)
    tab_bqk = tuple(jnp.stack([a, b]) for a, b in zip(tbq, tbk))

    x2d = x.reshape(m, d_model)
    v_first = None
    for l in range(depth):
        wl_ = w_in[l]
        cols = lambda o, n: wl_[:, o:o + n]
        w_qk = wl_[:, o_q:o_v].astype(BF16)
        w_v = jnp.concatenate([cols(o_v, a_w), cols(o_bv, b_v_w)], axis=1).astype(BF16)
        w_qi = cols(o_qi, qi_w).astype(BF16)
        w_kiwi = _pad_cols(cols(o_ki, IDX_DIM + IDX_HEADS), LANE).astype(BF16)
        w_bqk = cols(o_bq, 2 * b_qk_w).astype(BF16)
        w_bg = cols(o_bg, b_v_w).astype(BF16)
        vres_w = (w_in_vres[l - 1] if l > 0 else jnp.zeros((d_model, MV_LORA), F32))
        w_c = jnp.concatenate([cols(o_c, 3 * c_w), _pad_cols(cols(o_wl, DECAY_LORA), LANE),
                               _pad_cols(cols(o_al, AAA_LORA), LANE), cols(o_gl, GATE_LORA),
                               _pad_cols(vres_w, vres_pad)], axis=1).astype(BF16)
        mu = rwkv_mu[l]
        mu_vres = rwkv_mu_vres[l - 1] if l > 0 else jnp.zeros((MV_LORA,), F32)
        mu_c = jnp.concatenate([mu[:3 * c_w], _pad_vec(mu[3 * c_w:3 * c_w + DECAY_LORA], LANE),
                                _pad_vec(mu[3 * c_w + DECAY_LORA:3 * c_w + DECAY_LORA + AAA_LORA], LANE),
                                mu[3 * c_w + DECAY_LORA + AAA_LORA:], _pad_vec(mu_vres, vres_pad)])[None, :]

        h = _rmsnorm(x2d, norm_mix_g[l], BF16)
        qk = _mm_rope(h, w_qk, tab_qk, a_rot // 2, seq, a_w, BF16).reshape(bsz, seq, 2 * a_w)
        vv = _mm_plain(h, w_v, BF16, tn_cap=512).reshape(bsz, seq, a_w + b_v_w)
        qi = _mm_rope(h, w_qi, tab_qi, i_rot // 2, seq, qi_w, BF16).reshape(bsz, seq, qi_w)
        kiwi = _mm_rope(h, w_kiwi, tab_kiwi, i_rot // 2, seq, LANE, F32).reshape(bsz, seq, LANE)
        bqk = _mm_rope(h, w_bqk, tab_bqk, r_qk_dim // 2, seq, b_qk_w, BF16).reshape(bsz, seq, 2 * b_qk_w)
        bg = _mm_plain(h, w_bg, F32).reshape(bsz, seq, b_v_w)
        cproj = _mm_plain(h, w_c, F32, tn_cap=768).reshape(bsz, seq, w_c.shape[1])

        ki = kiwi[:, :, :IDX_DIM].astype(BF16)
        wit = jnp.swapaxes(kiwi[:, :, IDX_DIM:IDX_DIM + IDX_HEADS], 1, 2)
        bias_t = _indexer_bias_t(ki, jnp.swapaxes(qi, 1, 2), wit, top_k)
        qt = jnp.swapaxes(qk[:, :, :a_w], 1, 2)
        vt = jnp.swapaxes(vv[:, :, :a_w], 1, 2)
        out_a = jnp.swapaxes(_masked_attention_t(qt, qk[:, :, a_w:], vt, bias_t), 1, 2)

        out_b = _retention(bqk, vv[:, :, a_w:], bg)

        vecs = jnp.stack([rwkv_w0[l], rwkv_a0[l], rwkv_k_k[l], rwkv_k_a[l], rwkv_r_k[l].reshape(-1),
                          rwkv_v0[l - 1] if l > 0 else jnp.zeros((c_w,), F32),
                          jnp.zeros((c_w,), F32), jnp.zeros((c_w,), F32)])
        wlb = _pad_rows(rwkv_w_lora_b[l], LANE).astype(BF16)
        alb = _pad_rows(rwkv_a_lora_b[l], LANE).astype(BF16)
        glb = rwkv_g_lora_b[l].astype(BF16)
        vlb = _pad_rows(rwkv_v_lora_b[l - 1], vres_pad).astype(BF16) if l > 0 else None
        r_, lw_, kh_, v_, al_, be_, g_, bo_ = _rwkv_prep(cproj, mu_c, vecs, wlb, alb, glb, vlb, v_first, c_w)
        if l == 0:
            v_first = v_
        gm, y0, mm, z0 = _rwkv_chunk_ops(r_, lw_, kh_, v_, al_, be_)
        ln = jnp.stack([rwkv_lnx_g[l], rwkv_lnx_b[l]] + [jnp.zeros((c_w,), F32)] * 6)
        out_c = _rwkv_scan(gm, y0, mm, z0, bo_, g_, ln)

        x2d = _out_proj(x2d, out_a.reshape(m, a_w), out_b.reshape(m, b_v_w), out_c.reshape(m, c_w),
                        w_out[l].astype(BF16))

        h = _rmsnorm(x2d, norm_ffn_g[l], BF16)
        act = _gate_up(h, w_gate_up[l].astype(BF16))
        x2d = _down_proj(x2d, act, w_down[l].astype(BF16))

    return _rmsnorm(x2d, final_norm_g, F32).reshape(bsz, seq, d_model)
```

```python
import functools
import math

import jax
import jax.numpy as jnp
from jax import lax
from jax.experimental import pallas as pl
from jax.experimental.pallas import tpu as pltpu

F32 = jnp.float32
BF16 = jnp.bfloat16
I32 = jnp.int32
HIGHEST = lax.Precision.HIGHEST

CHUNK = 64
A_HEAD_DIM = 128
IDX_HEADS = 16
IDX_DIM = 64
IDX_W_SCALE = (IDX_HEADS * IDX_DIM) ** -0.5
TOPK_MAX = 256
ROPE_THETA = 500000.0
ROPE_FRAC = 4
R_HEADS = 4
R_THETA = 10000.0
W_HEAD_DIM = 64
DECAY_LORA = 96
AAA_LORA = 96
MV_LORA = 64
GATE_LORA = 256
LNX_EPS = 64e-5
RMS_EPS = 1e-5

LANE = 128
VMEM_LIMIT = 56 * 1024 * 1024
NEG_BIAS = -1e30
M_INIT = -1e20
INT_MIN = -2147483648
COUNT_CHAINS = 8
SCORE_GROUP = 4
ONES_ROWS = 8


def _cparams(sem):
    return pltpu.CompilerParams(dimension_semantics=sem, vmem_limit_bytes=VMEM_LIMIT)


def _pick_tile(n, cap):
    best = LANE
    t = LANE
    while t <= min(n, cap):
        if n % t == 0:
            best = t
        t += LANE
    return best


def _rmsnorm_kernel(x_ref, g_ref, o_ref):
    x = x_ref[...]
    ms = jnp.mean(x * x, axis=-1, keepdims=True)
    o_ref[...] = (x * lax.rsqrt(ms + RMS_EPS) * g_ref[...]).astype(o_ref.dtype)


def _rmsnorm(x2d, g, out_dtype, tm=512):
    m, d = x2d.shape
    return pl.pallas_call(
        _rmsnorm_kernel,
        grid=(m // tm,),
        in_specs=[pl.BlockSpec((tm, d), lambda i: (i, 0)),
                  pl.BlockSpec((1, d), lambda i: (0, 0))],
        out_specs=pl.BlockSpec((tm, d), lambda i: (i, 0)),
        out_shape=jax.ShapeDtypeStruct((m, d), out_dtype),
        compiler_params=_cparams(("parallel",)),
        name="rmsnorm",
    )(x2d, g.reshape(1, d).astype(F32))


def _store_cols(o_ref, blk, x, transpose, pad_rows):
    if transpose:
        r0 = blk * (LANE + pad_rows)
        o_ref[r0:r0 + LANE, :] = x.T.astype(o_ref.dtype)
        if pad_rows:
            o_ref[r0 + LANE:r0 + LANE + pad_rows, :] = jnp.ones((pad_rows, x.shape[0]), o_ref.dtype)
    else:
        o_ref[:, blk * LANE:(blk + 1) * LANE] = x.astype(o_ref.dtype)


def _proj_out_spec(m, n, tm, tn, seq, out_dtype, transpose, pad_rows):
    if not transpose:
        return pl.BlockSpec((tm, tn), lambda i, j: (i, j)), jax.ShapeDtypeStruct((m, n), out_dtype)
    tpb = seq // tm
    rows = lambda cols: cols // LANE * (LANE + pad_rows)
    return (pl.BlockSpec((None, rows(tn), tm), lambda i, j: (i // tpb, j, i % tpb)),
            jax.ShapeDtypeStruct((m // seq, rows(n), seq), out_dtype))


def _mm_plain_kernel(a_ref, w_ref, o_ref, *, transpose, pad_rows):
    acc = jnp.dot(a_ref[...], w_ref[...], preferred_element_type=F32)
    if not transpose:
        o_ref[...] = acc.astype(o_ref.dtype)
        return
    for blk in range(acc.shape[1] // LANE):
        _store_cols(o_ref, blk, acc[:, blk * LANE:(blk + 1) * LANE], transpose, pad_rows)


def _mm_plain(a, w, out_dtype, seq, tm=1024, tn_cap=1024, transpose=False, pad_rows=0):
    m, k = a.shape
    n = w.shape[1]
    tn = _pick_tile(n, tn_cap)
    out_spec, out_shape = _proj_out_spec(m, n, tm, tn, seq, out_dtype, transpose, pad_rows)
    return pl.pallas_call(
        functools.partial(_mm_plain_kernel, transpose=transpose, pad_rows=pad_rows),
        grid=(m // tm, n // tn),
        in_specs=[pl.BlockSpec((tm, k), lambda i, j: (i, 0)),
                  pl.BlockSpec((k, tn), lambda i, j: (0, j))],
        out_specs=out_spec,
        out_shape=out_shape,
        compiler_params=_cparams(("parallel", "arbitrary")),
        name="proj_plain",
    )(a, w)


def _mm_rope_kernel(a_ref, w_ref, c_ref, s1_ref, s2_ref, o_ref, *, half, transpose):
    acc = jnp.dot(a_ref[...], w_ref[...], preferred_element_type=F32)
    c, s1, s2 = c_ref[...], s1_ref[...], s2_ref[...]
    for blk in range(acc.shape[1] // LANE):
        x = acc[:, blk * LANE:(blk + 1) * LANE]
        up = pltpu.roll(x, LANE - half, axis=1)
        dn = pltpu.roll(x, half, axis=1)
        _store_cols(o_ref, blk, x * c + up * s1 + dn * s2, transpose, 0)


def _mm_rope(a, w, tables, half, seq, tn, out_dtype, tm=1024, transpose=False):
    m, k = a.shape
    n = w.shape[1]
    tpb = seq // tm
    tab_spec = pl.BlockSpec((None, tm, LANE), lambda i, j: (j, i % tpb, 0))
    out_spec, out_shape = _proj_out_spec(m, n, tm, tn, seq, out_dtype, transpose, 0)
    return pl.pallas_call(
        functools.partial(_mm_rope_kernel, half=half, transpose=transpose),
        grid=(m // tm, n // tn),
        in_specs=[pl.BlockSpec((tm, k), lambda i, j: (i, 0)),
                  pl.BlockSpec((k, tn), lambda i, j: (0, j)),
                  tab_spec, tab_spec, tab_spec],
        out_specs=out_spec,
        out_shape=out_shape,
        compiler_params=_cparams(("parallel", "arbitrary")),
        name="proj_rope",
    )(a, w, *tables)


def _rope_tables(seq, group, rot_dim, theta, scale=1.0, pass_scale=None):
    half = rot_dim // 2
    freqs = jnp.power(F32(theta), -jnp.arange(half, dtype=F32) / half)
    ang = jnp.arange(seq, dtype=F32)[:, None] * freqs[None, :]
    cos, sin = jnp.cos(ang), jnp.sin(ang)
    lane = jnp.arange(LANE) % group
    idx = lane % half
    cosl, sinl = cos[:, idx], sin[:, idx]
    passv = jnp.ones((LANE,), F32) if pass_scale is None else pass_scale
    c = jnp.where(lane < rot_dim, cosl, passv[None, :])
    s1 = jnp.where(lane < half, -sinl, 0.0)
    s2 = jnp.where((lane >= half) & (lane < rot_dim), sinl, 0.0)
    return c * scale, s1 * scale, s2 * scale


def _indexer_kernel(qi_ref, kit_ref, wi_ref, bias_ref, key_ref, *, tq, ts, tki, tkc, top_k, seq):
    qb = pl.program_id(1)
    nkb = ((qb + 1) * tq + tki - 1) // tki
    q_chunk = (qb * tq + lax.broadcasted_iota(I32, (tq, 1), 0)) // CHUNK
    w = wi_ref[...]

    def score_block(kb, rmax):
        k0 = pl.multiple_of(kb * tki, tki)
        kblk = kit_ref[:, pl.ds(k0, tki)]
        k_chunk = (k0 + lax.broadcasted_iota(I32, (1, tki), 1)) // CHUNK
        new_max = []
        for rh in range(tq // ts):
            rows = slice(rh * ts, (rh + 1) * ts)
            acc = jnp.zeros((ts, tki), F32)
            for h in range(IDX_HEADS):
                x = jnp.dot(qi_ref[rows, h * IDX_DIM:(h + 1) * IDX_DIM], kblk, preferred_element_type=F32)
                acc = acc + jnp.maximum(x, 0.0) * w[rows, h:h + 1]
            bits = lax.bitcast_convert_type(acc, I32)
            key = bits ^ ((bits >> 31) & jnp.int32(0x7FFFFFFF))
            key = jnp.where(k_chunk <= q_chunk[rows], key, jnp.int32(INT_MIN))
            key_ref[rows, pl.ds(k0, tki)] = key
            r = rmax[rows]
            for c in range(tki // LANE):
                r = jnp.maximum(r, key[:, c * LANE:(c + 1) * LANE])
            new_max.append(r)
        return jnp.concatenate(new_max, axis=0)

    rmax = lax.fori_loop(0, nkb, score_block, jnp.full((tq, LANE), INT_MIN, I32))
    rowmax = jnp.max(rmax, axis=1, keepdims=True)

    nkc = (nkb * tki + tkc - 1) // tkc

    def pad_block(kb, carry):
        k0 = pl.multiple_of(kb * tki, tki)
        key_ref[:, pl.ds(k0, tki)] = jnp.full((tq, tki), INT_MIN, I32)
        return carry

    lax.fori_loop(nkb, nkc * (tkc // tki), pad_block, 0)

    def count_ge(cand):
        totals = []
        for rh in range(tq // ts):
            rows = slice(rh * ts, (rh + 1) * ts)
            candb = jnp.broadcast_to(cand[rows], (ts, LANE))

            def count_block(kb, cnt, rows=rows, candb=candb):
                k0 = pl.multiple_of(kb * tkc, tkc)
                for c in range(tkc // LANE):
                    blk = key_ref[rows, pl.ds(k0 + c * LANE, LANE)]
                    cnt = cnt + jnp.where(blk >= candb, 1, 0)
                return cnt

            cnt = lax.fori_loop(0, nkc, count_block, jnp.zeros((ts, LANE), I32))
            totals.append(jnp.sum(cnt, axis=1, keepdims=True))
        return jnp.concatenate(totals, axis=0)

    probe = jnp.where(rowmax > jnp.int32(INT_MIN + (1 << 25)), rowmax - jnp.int32(3 << 23),
                      jnp.int32(INT_MIN + 1))

    def bisect_cond(state):
        it, n_active = state[0], state[1]
        return (n_active > 0) & (it < 40)

    def bisect_body(state):
        it, _, lo, hi, cnt_lo, cnt_hi = state
        mid = (lo >> 1) + (hi >> 1) + (lo & hi & 1)
        mid = jnp.where(it == 0, probe, mid)
        cnt = count_ge(mid)
        ge = cnt >= top_k
        exact = cnt == top_k
        new_lo = jnp.where(ge, mid, lo)
        new_hi = jnp.where(exact, mid + 1, jnp.where(ge, hi, mid))
        cnt_lo = jnp.where(ge, cnt, cnt_lo)
        cnt_hi = jnp.where(ge, cnt_hi, cnt)
        active = (new_lo + 1) < new_hi
        n_active = jnp.max(active.astype(I32))
        return it + 1, n_active, new_lo, new_hi, cnt_lo, cnt_hi

    big = jnp.int32(1 << 30)
    init = (jnp.int32(0), jnp.int32(1), jnp.full((tq, 1), INT_MIN, I32), rowmax + 1,
            jnp.full((tq, 1), big, I32), jnp.zeros((tq, 1), I32))
    _, _, lo, _, cnt_lo, cnt_hi = lax.while_loop(bisect_cond, bisect_body, init)

    thr = jnp.maximum(lo, jnp.int32(INT_MIN + 1))
    thrb = jnp.broadcast_to(thr, (tq, LANE))
    tie_row = (cnt_lo > top_k) & (lo > jnp.int32(INT_MIN))
    any_tie = jnp.max(tie_row.astype(I32))

    @pl.when(any_tie == 0)
    def _():
        def write_block(kb, carry):
            k0 = pl.multiple_of(kb * tki, tki)
            for c in range(tki // LANE):
                blk = key_ref[:, pl.ds(k0 + c * LANE, LANE)]
                bias_ref[:, pl.ds(k0 + c * LANE, LANE)] = jnp.where(blk >= thrb, 0.0, NEG_BIAS)
            return carry

        lax.fori_loop(0, nkb, write_block, 0)

    @pl.when(any_tie != 0)
    def _():
        need = jnp.where(tie_row, top_k - cnt_hi, big).astype(F32)
        ur = lax.broadcasted_iota(I32, (LANE, LANE), 0)
        uc = lax.broadcasted_iota(I32, (LANE, LANE), 1)
        upper = (ur <= uc).astype(BF16)

        def write_block(kb, run):
            k0 = pl.multiple_of(kb * tki, tki)
            for c in range(tki // LANE):
                blk = key_ref[:, pl.ds(k0 + c * LANE, LANE)]
                eq = blk == thrb
                pc = jnp.dot(jnp.where(eq, 1.0, 0.0).astype(BF16), upper, preferred_element_type=F32)
                keep_eq = jnp.where((run + pc) <= need, 0.0, NEG_BIAS)
                bias_ref[:, pl.ds(k0 + c * LANE, LANE)] = jnp.where(
                    blk > thrb, 0.0, jnp.where(eq, keep_eq, NEG_BIAS))
                run = run + pc[:, LANE - 1:LANE]
            return run

        lax.fori_loop(0, nkb, write_block, jnp.zeros((tq, 1), F32))

    def fill_block(kb, carry):
        k0 = pl.multiple_of(kb * tki, tki)
        bias_ref[:, pl.ds(k0, tki)] = jnp.full((tq, tki), NEG_BIAS, F32)
        return carry

    lax.fori_loop(nkb, seq // tki, fill_block, 0)


def _indexer_bias(qi, kit, wi, top_k, tq=256, ts=128, tki=512):
    bsz, seq, _ = qi.shape
    tki = min(tki, seq)
    tkc = 2 * tki if (seq // tki) % 2 == 0 else tki
    return pl.pallas_call(
        functools.partial(_indexer_kernel, tq=tq, ts=ts, tki=tki, tkc=tkc, top_k=top_k, seq=seq),
        grid=(bsz, seq // tq),
        in_specs=[pl.BlockSpec((None, tq, IDX_HEADS * IDX_DIM), lambda b, i: (b, i, 0)),
                  pl.BlockSpec((None, IDX_DIM, seq), lambda b, i: (b, 0, 0)),
                  pl.BlockSpec((None, tq, IDX_HEADS), lambda b, i: (b, i, 0))],
        out_specs=pl.BlockSpec((None, tq, seq), lambda b, i: (b, i, 0)),
        out_shape=jax.ShapeDtypeStruct((bsz, seq, seq), F32),
        scratch_shapes=[pltpu.VMEM((tq, seq), I32)],
        compiler_params=_cparams(("parallel", "arbitrary")),
        name="indexer_select",
    )(qi, kit, wi)


def _attn_kernel(qidx_ref, kidx_ref, q_ref, kt_ref, v_ref, b_ref, o_ref, m_ref, l_ref, acc_ref,
                 *, nheads, hd, tq, tk):
    p = pl.program_id(1)
    qb = qidx_ref[p]
    kb = kidx_ref[p]
    kb_last = ((qb + 1) * tq - 1) // tk

    @pl.when(kb == 0)
    def _():
        m_ref[...] = jnp.full(m_ref.shape, M_INIT, F32)
        l_ref[...] = jnp.zeros(l_ref.shape, F32)
        acc_ref[...] = jnp.zeros(acc_ref.shape, F32)

    bias = b_ref[...]

    def logits(h):
        sl = slice(h * hd, (h + 1) * hd)
        return jnp.dot(q_ref[:, sl], kt_ref[sl, :], preferred_element_type=F32)

    s_next = logits(0)
    for h in range(nheads):
        sl = slice(h * hd, (h + 1) * hd)
        s = s_next + bias
        if h + 1 < nheads:
            s_next = logits(h + 1)
        m_old = m_ref[h]
        m_new = jnp.maximum(m_old, jnp.max(s, axis=1, keepdims=True))
        alpha = jnp.exp(m_old - m_new)
        pr = jnp.exp(s - pltpu.repeat(m_new, tk // LANE, axis=1))
        l_ref[h] = alpha * l_ref[h] + jnp.sum(pr, axis=1, keepdims=True)
        m_ref[h] = m_new
        acc_ref[:, sl] = alpha * acc_ref[:, sl] + jnp.dot(pr.astype(BF16), v_ref[:, sl],
                                                          preferred_element_type=F32)

    @pl.when(kb == kb_last)
    def _():
        for h in range(nheads):
            sl = slice(h * hd, (h + 1) * hd)
            o_ref[:, sl] = (acc_ref[:, sl] / l_ref[h]).astype(o_ref.dtype)


def _masked_attention(q, kt, v, bias, tq=256, tk=512):
    bsz, seq, width = q.shape
    nheads = width // A_HEAD_DIM
    nq = seq // tq
    tk = min(tk, seq)
    pairs = [(i, j) for i in range(nq) for j in range(((i + 1) * tq - 1) // tk + 1)]
    qidx = jnp.asarray([pq for pq, _ in pairs], I32)
    kidx = jnp.asarray([pk for _, pk in pairs], I32)
    grid_spec = pltpu.PrefetchScalarGridSpec(
        num_scalar_prefetch=2,
        grid=(bsz, len(pairs)),
        in_specs=[pl.BlockSpec((None, tq, width), lambda b, p, qi, ki: (b, qi[p], 0)),
                  pl.BlockSpec((None, width, tk), lambda b, p, qi, ki: (b, 0, ki[p])),
                  pl.BlockSpec((None, tk, width), lambda b, p, qi, ki: (b, ki[p], 0)),
                  pl.BlockSpec((None, tq, tk), lambda b, p, qi, ki: (b, qi[p], ki[p]))],
        out_specs=pl.BlockSpec((None, tq, width), lambda b, p, qi, ki: (b, qi[p], 0)),
        scratch_shapes=[pltpu.VMEM((nheads, tq, LANE), F32),
                        pltpu.VMEM((nheads, tq, LANE), F32),
                        pltpu.VMEM((tq, width), F32)],
    )
    return pl.pallas_call(
        functools.partial(_attn_kernel, nheads=nheads, hd=A_HEAD_DIM, tq=tq, tk=tk),
        grid_spec=grid_spec,
        out_shape=jax.ShapeDtypeStruct((bsz, seq, width), BF16),
        compiler_params=_cparams(("parallel", "arbitrary")),
        name="masked_attention",
    )(qidx, kidx, q, kt, v, bias)


def _indexer_t_kernel(ki_ref, qit_ref, wit_ref, bias_ref, key_ref, *, tq, tks, tkc, top_k, seq):
    qb = pl.program_id(1)
    nks = ((qb + 1) * tq + tks - 1) // tks
    nkc = (nks * tks + tkc - 1) // tkc
    q_chunk = (qb * tq + lax.broadcasted_iota(I32, (1, tq), 1)) // CHUNK
    w = wit_ref[...]

    def score_block(kb, rmax):
        k0 = pl.multiple_of(kb * tks, tks)
        kblk = ki_ref[pl.ds(k0, tks), :]
        acc = jnp.zeros((tks, tq), F32)
        for h0 in range(0, IDX_HEADS, SCORE_GROUP):
            xs = [jnp.dot(kblk, qit_ref[h * IDX_DIM:(h + 1) * IDX_DIM, :], preferred_element_type=F32)
                  for h in range(h0, h0 + SCORE_GROUP)]
            for i, x in enumerate(xs):
                acc = acc + jnp.maximum(x, 0.0) * w[h0 + i:h0 + i + 1, :]
        k_chunk = (k0 + lax.broadcasted_iota(I32, (tks, 1), 0)) // CHUNK
        bits = lax.bitcast_convert_type(acc, I32)
        key = bits ^ ((bits >> 31) & jnp.int32(0x7FFFFFFF))
        key = jnp.where(k_chunk <= q_chunk, key, jnp.int32(INT_MIN))
        key_ref[pl.ds(k0, tks), :] = key
        return jnp.maximum(rmax, jnp.max(key.reshape(tks // 8, 8, tq), axis=0))

    rmax = lax.fori_loop(0, nks, score_block, jnp.full((8, tq), INT_MIN, I32))
    colmax = jnp.max(rmax, axis=0, keepdims=True)

    def pad_block(kb, carry):
        k0 = pl.multiple_of(kb * tks, tks)
        key_ref[pl.ds(k0, tks), :] = jnp.full((tks, tq), INT_MIN, I32)
        return carry

    lax.fori_loop(nks, nkc * (tkc // tks), pad_block, 0)

    def count_ge(cand):
        def count_block(kb, cnt):
            k0 = pl.multiple_of(kb * tkc, tkc)
            part = tkc // COUNT_CHAINS
            sums = []
            for g in range(COUNT_CHAINS):
                ind = jnp.where(key_ref[pl.ds(k0 + g * part, part), :] >= cand, 1, 0)
                sums.append(jnp.sum(ind.reshape(part // 8, 8, tq), axis=0))
            while len(sums) > 1:
                sums = [a + b for a, b in zip(sums[0::2], sums[1::2])]
            return cnt + sums[0]

        cnt = lax.fori_loop(0, nkc, count_block, jnp.zeros((8, tq), I32))
        return jnp.sum(cnt, axis=0, keepdims=True)

    def probe_step(state, mid):
        lo, hi, cnt_lo, cnt_hi = state
        cnt = count_ge(mid)
        ge = cnt >= top_k
        new_hi = jnp.where(cnt == top_k, mid + 1, jnp.where(ge, hi, mid))
        return (jnp.where(ge, mid, lo), new_hi, jnp.where(ge, cnt, cnt_lo), jnp.where(ge, cnt_hi, cnt))

    def midpoint(state):
        lo, hi = state[0], state[1]
        return (lo >> 1) + (hi >> 1) + (lo & hi & 1)

    def n_active(state):
        return jnp.max(((state[0] + 1) < state[1]).astype(I32))

    big = jnp.int32(1 << 30)
    state = (jnp.full((1, tq), INT_MIN, I32), colmax + 1, jnp.full((1, tq), big, I32), jnp.zeros((1, tq), I32))
    first = jnp.where(colmax > jnp.int32(INT_MIN + (1 << 25)), colmax - jnp.int32(3 << 23),
                      jnp.int32(INT_MIN + 1))
    state = probe_step(state, first)

    def bisect_body(carry):
        it, _, state = carry
        state = probe_step(state, midpoint(state))
        state = probe_step(state, midpoint(state))
        return it + 2, n_active(state), state

    _, _, state = lax.while_loop(lambda c: (c[1] > 0) & (c[0] < 40), bisect_body,
                                 (jnp.int32(0), n_active(state), state))
    lo, _, cnt_lo, cnt_hi = state

    thr = jnp.maximum(lo, jnp.int32(INT_MIN + 1))
    tie_q = (cnt_lo > top_k) & (lo > jnp.int32(INT_MIN))
    any_tie = jnp.max(tie_q.astype(I32))

    @pl.when(any_tie == 0)
    def _():
        def write_block(kb, carry):
            k0 = pl.multiple_of(kb * tks, tks)
            bias_ref[pl.ds(k0, tks), :] = jnp.where(key_ref[pl.ds(k0, tks), :] >= thr, 0.0, NEG_BIAS)
            return carry

        lax.fori_loop(0, nks, write_block, 0)

    @pl.when(any_tie != 0)
    def _():
        need = jnp.where(tie_q, top_k - cnt_hi, big).astype(F32)
        lr = lax.broadcasted_iota(I32, (LANE, LANE), 0)
        lc = lax.broadcasted_iota(I32, (LANE, LANE), 1)
        lower = (lr >= lc).astype(BF16)

        def write_block(kb, run):
            k0 = pl.multiple_of(kb * LANE, LANE)
            blk = key_ref[pl.ds(k0, LANE), :]
            eq = blk == thr
            pc = jnp.dot(lower, jnp.where(eq, 1.0, 0.0).astype(BF16), preferred_element_type=F32)
            keep_eq = jnp.where((run + pc) <= need, 0.0, NEG_BIAS)
            bias_ref[pl.ds(k0, LANE), :] = jnp.where(blk > thr, 0.0, jnp.where(eq, keep_eq, NEG_BIAS))
            return run + pc[LANE - 1:LANE, :]

        lax.fori_loop(0, nks * (tks // LANE), write_block, jnp.zeros((1, tq), F32))

    def fill_block(kb, carry):
        k0 = pl.multiple_of(kb * tks, tks)
        bias_ref[pl.ds(k0, tks), :] = jnp.full((tks, tq), NEG_BIAS, F32)
        return carry

    lax.fori_loop(nks, seq // tks, fill_block, 0)


def _indexer_bias_t(ki, qit, wit, top_k, tq=256, tks=256, tkc=1024):
    bsz, seq, _ = ki.shape
    tq = min(tq, seq)
    return pl.pallas_call(
        functools.partial(_indexer_t_kernel, tq=tq, tks=tks, tkc=tkc, top_k=top_k, seq=seq),
        grid=(bsz, seq // tq),
        in_specs=[pl.BlockSpec((None, seq, IDX_DIM), lambda b, i: (b, 0, 0)),
                  pl.BlockSpec((None, IDX_HEADS * IDX_DIM, tq), lambda b, i: (b, 0, i)),
                  pl.BlockSpec((None, IDX_HEADS, tq), lambda b, i: (b, 0, i))],
        out_specs=pl.BlockSpec((None, seq, tq), lambda b, i: (b, 0, i)),
        out_shape=jax.ShapeDtypeStruct((bsz, seq, seq), F32),
        scratch_shapes=[pltpu.VMEM((seq, tq), I32)],
        compiler_params=_cparams(("parallel", "arbitrary")),
        name="indexer_select",
    )(ki, qit, wit)


def _attn_t_kernel(qidx_ref, kidx_ref, qt_ref, k_ref, vt_ref, b_ref, o_ref, m_ref, acc_ref,
                   *, nheads, hd, hda, tq, tk):
    p = pl.program_id(1)
    qb = qidx_ref[p]
    kb = kidx_ref[p]
    kb_last = ((qb + 1) * tq - 1) // tk

    @pl.when(kb == 0)
    def _():
        m_ref[...] = jnp.full(m_ref.shape, M_INIT, F32)
        acc_ref[...] = jnp.zeros(acc_ref.shape, F32)

    bias = b_ref[...]
    hrows = [slice(h * hd, (h + 1) * hd) for h in range(nheads)]
    arows = [slice(h * hda, (h + 1) * hda) for h in range(nheads)]
    s = [jnp.dot(k_ref[:, r], qt_ref[r, :], preferred_element_type=F32) + bias for r in hrows]
    m_old = [m_ref[h] for h in range(nheads)]
    m_new = [jnp.maximum(m_old[h], jnp.max(s[h], axis=0, keepdims=True)) for h in range(nheads)]
    alpha = [jnp.exp2(m_old[h] - m_new[h]) for h in range(nheads)]
    pr = [jnp.exp2(s[h] - m_new[h]).astype(BF16) for h in range(nheads)]
    pv = [jnp.dot(vt_ref[arows[h], :], pr[h], preferred_element_type=F32) for h in range(nheads)]
    for h in range(nheads):
        m_ref[h] = m_new[h]
        acc_ref[arows[h], :] = alpha[h] * acc_ref[arows[h], :] + pv[h]

    @pl.when(kb == kb_last)
    def _():
        for h in range(nheads):
            a0 = h * hda
            out_t = acc_ref[a0:a0 + hd, :] / acc_ref[a0 + hd:a0 + hd + 1, :]
            o_ref[:, hrows[h]] = out_t.T.astype(o_ref.dtype)


def _masked_attention_t(qt, k, vt, bias_t, tq=512, tk=1024):
    bsz, width, seq = qt.shape
    nheads = width // A_HEAD_DIM
    hda = A_HEAD_DIM + ONES_ROWS
    tq = min(tq, seq)
    tk = min(tk, seq)
    nq = seq // tq
    pairs = [(i, j) for i in range(nq) for j in range(((i + 1) * tq - 1) // tk + 1)]
    qidx = jnp.asarray([pq for pq, _ in pairs], I32)
    kidx = jnp.asarray([pk for _, pk in pairs], I32)
    grid_spec = pltpu.PrefetchScalarGridSpec(
        num_scalar_prefetch=2,
        grid=(bsz, len(pairs)),
        in_specs=[pl.BlockSpec((None, width, tq), lambda b, p, qi, ki: (b, 0, qi[p])),
                  pl.BlockSpec((None, tk, width), lambda b, p, qi, ki: (b, ki[p], 0)),
                  pl.BlockSpec((None, nheads * hda, tk), lambda b, p, qi, ki: (b, 0, ki[p])),
                  pl.BlockSpec((None, tk, tq), lambda b, p, qi, ki: (b, ki[p], qi[p]))],
        out_specs=pl.BlockSpec((None, tq, width), lambda b, p, qi, ki: (b, qi[p], 0)),
        scratch_shapes=[pltpu.VMEM((nheads, 1, tq), F32),
                        pltpu.VMEM((nheads * hda, tq), F32)],
    )
    return pl.pallas_call(
        functools.partial(_attn_t_kernel, nheads=nheads, hd=A_HEAD_DIM, hda=hda, tq=tq, tk=tk),
        grid_spec=grid_spec,
        out_shape=jax.ShapeDtypeStruct((bsz, seq, width), BF16),
        compiler_params=_cparams(("parallel", "arbitrary")),
        name="masked_attention",
    )(qidx, kidx, qt, k, vt, bias_t)


def _retention_kernel(qk_ref, v_ref, g_ref, o_ref, state_ref, *, tc, dk, dv):
    @pl.when(pl.program_id(1) == 0)
    def _():
        state_ref[...] = jnp.zeros(state_ref.shape, F32)

    row = lax.broadcasted_iota(I32, (tc, tc), 0)
    col = lax.broadcasted_iota(I32, (tc, tc), 1)
    diff = (row - col).astype(F32)
    pos = lax.broadcasted_iota(I32, (tc, 1), 0).astype(F32)
    for h in range(R_HEADS):
        log_g = math.log1p(-(2.0 ** (-5.0 - h)))
        q = qk_ref[:, h * dk:(h + 1) * dk]
        k = qk_ref[:, (R_HEADS + h) * dk:(R_HEADS + h + 1) * dk]
        v = v_ref[:, h * dv:(h + 1) * dv]
        decay = jnp.where(diff >= 0, jnp.exp(jnp.maximum(diff, 0.0) * log_g), 0.0)
        s = lax.dot_general(q, k, (((1,), (1,)), ((), ())), preferred_element_type=F32) * decay
        intra = jnp.dot(s.astype(BF16), v, preferred_element_type=F32)
        xi = jnp.exp((pos + 1.0) * log_g)
        zeta = jnp.exp((tc - 1.0 - pos) * log_g)
        state = state_ref[h]
        cross = jnp.dot((q.astype(F32) * xi).astype(BF16), state.astype(BF16), preferred_element_type=F32)
        kz = (k.astype(F32) * zeta).astype(BF16)
        kv = lax.dot_general(kz, v, (((0,), (0,)), ((), ())), preferred_element_type=F32)
        state_ref[h] = state * math.exp(tc * log_g) + kv
        ret = intra + cross
        mu = jnp.mean(ret, axis=1, keepdims=True)
        d = ret - mu
        var = jnp.mean(d * d, axis=1, keepdims=True)
        gate = g_ref[:, h * dv:(h + 1) * dv]
        gate = gate / (1.0 + jnp.exp(-gate))
        o_ref[:, h * dv:(h + 1) * dv] = (d * lax.rsqrt(var + 1e-5) * gate).astype(o_ref.dtype)


def _retention(bqk, bv, bg, tc=256):
    bsz, seq, w2 = bqk.shape
    dk = w2 // (2 * R_HEADS)
    dv = bv.shape[2] // R_HEADS
    tc = min(tc, seq)
    return pl.pallas_call(
        functools.partial(_retention_kernel, tc=tc, dk=dk, dv=dv),
        grid=(bsz, seq // tc),
        in_specs=[pl.BlockSpec((None, tc, w2), lambda b, c: (b, c, 0)),
                  pl.BlockSpec((None, tc, R_HEADS * dv), lambda b, c: (b, c, 0)),
                  pl.BlockSpec((None, tc, R_HEADS * dv), lambda b, c: (b, c, 0))],
        out_specs=pl.BlockSpec((None, tc, R_HEADS * dv), lambda b, c: (b, c, 0)),
        out_shape=jax.ShapeDtypeStruct((bsz, seq, R_HEADS * dv), BF16),
        scratch_shapes=[pltpu.VMEM((R_HEADS, dk, dv), F32)],
        compiler_params=_cparams(("parallel", "arbitrary")),
        name="retention",
    )(bqk, bv, bg)


def _head_sum_matrix(width, hd):
    r = jnp.arange(width)
    return (r[:, None] // hd == r[None, :] // hd).astype(F32)


def _sigmoid(x):
    return 1.0 / (1.0 + jnp.exp(-x))


def _rwkv_prep_kernel(*refs, cw, has_vres):
    if has_vres:
        (c_ref, mu_ref, vec_ref, wlb_ref, alb_ref, glb_ref, hs_ref, vlb_ref, vfirst_ref,
         r_o, lw_o, k_o, v_o, al_o, be_o, g_o, bo_o, carry_ref) = refs
    else:
        (c_ref, mu_ref, vec_ref, wlb_ref, alb_ref, glb_ref, hs_ref,
         r_o, lw_o, k_o, v_o, al_o, be_o, g_o, bo_o, carry_ref) = refs
    tr = c_ref.shape[0]

    @pl.when(pl.program_id(1) == 0)
    def _():
        carry_ref[...] = jnp.zeros(carry_ref.shape, F32)

    c = c_ref[...]
    row = lax.broadcasted_iota(I32, (tr, 1), 0)
    prev = jnp.where(row == 0, carry_ref[0:1, :], pltpu.roll(c, 1, axis=0))
    carry_ref[0:1, :] = c[tr - 1:tr, :]
    cs = c + (prev - c) * mu_ref[...]

    r = cs[:, 0:cw]
    k = cs[:, cw:2 * cw]
    v = cs[:, 2 * cw:3 * cw]
    o = 3 * cw
    wl = cs[:, o:o + LANE]
    al = cs[:, o + LANE:o + 2 * LANE]
    gl = cs[:, o + 2 * LANE:o + 2 * LANE + GATE_LORA]
    w0, a0, k_k, k_a, r_k, v0 = (vec_ref[i:i + 1, :] for i in range(6))

    z = -(w0 + jnp.dot(jnp.tanh(wl).astype(BF16), wlb_ref[...], preferred_element_type=F32))
    softplus = jnp.maximum(z, 0.0) + jnp.log(1.0 + jnp.exp(-jnp.abs(z)))
    lw = -jnp.exp(-softplus - 0.5)
    a = _sigmoid(a0 + jnp.dot(al.astype(BF16), alb_ref[...], preferred_element_type=F32))
    g = jnp.dot(_sigmoid(gl).astype(BF16), glb_ref[...], preferred_element_type=F32)
    if has_vres:
        vr = cs[:, o + 2 * LANE + GATE_LORA:o + 2 * LANE + GATE_LORA + vlb_ref.shape[0]]
        mix = _sigmoid(v0 + jnp.dot(vr.astype(BF16), vlb_ref[...], preferred_element_type=F32))
        v = v + (vfirst_ref[...] - v) * mix
    hs = hs_ref[...]
    kk = k * k_k
    ss = jnp.dot(kk * kk, hs, precision=HIGHEST, preferred_element_type=F32)
    kk = kk / jnp.maximum(jnp.sqrt(ss), 1e-12)
    kh = k * (1.0 + (a - 1.0) * k_a)
    rk = jnp.dot(r * kh * r_k, hs, precision=HIGHEST, preferred_element_type=F32)

    r_o[...] = r
    lw_o[...] = lw
    k_o[...] = kh
    v_o[...] = v
    al_o[...] = -kk
    be_o[...] = kk * a
    g_o[...] = g
    bo_o[...] = rk * v


def _rwkv_prep(cproj, mu, vecs, wlb, alb, glb, vlb, v_first, cw, tr=256):
    bsz, seq, wc = cproj.shape
    tr = min(tr, seq)
    has_vres = vlb is not None
    hs = _head_sum_matrix(cw, W_HEAD_DIM)
    full = lambda a: pl.BlockSpec(a.shape, lambda b, t: (0,) * a.ndim)
    tok = pl.BlockSpec((None, tr, cw), lambda b, t: (b, t, 0))
    ins = [cproj, mu, vecs, wlb, alb, glb, hs]
    in_specs = [pl.BlockSpec((None, tr, wc), lambda b, t: (b, t, 0)),
                full(mu), full(vecs), full(wlb), full(alb), full(glb), full(hs)]
    if has_vres:
        ins += [vlb, v_first]
        in_specs += [full(vlb), tok]
    out = jax.ShapeDtypeStruct((bsz, seq, cw), F32)
    return pl.pallas_call(
        functools.partial(_rwkv_prep_kernel, cw=cw, has_vres=has_vres),
        grid=(bsz, seq // tr),
        in_specs=in_specs,
        out_specs=[tok] * 8,
        out_shape=[out] * 8,
        scratch_shapes=[pltpu.VMEM((8, wc), F32)],
        compiler_params=_cparams(("parallel", "arbitrary")),
        name="rwkv_prep",
    )(*ins)


def _dot_hi(a, b):
    return jnp.dot(a, b, precision=HIGHEST, preferred_element_type=F32)


def _bdot(a, b):
    return jnp.dot(a.astype(BF16), b.astype(BF16), preferred_element_type=F32)


def _bdot_tn(a, b):
    return lax.dot_general(a.astype(BF16), b.astype(BF16), (((0,), (0,)), ((), ())),
                           preferred_element_type=F32)


def _rwkv_chunk_kernel(r_ref, lw_ref, k_ref, v_ref, al_ref, be_ref, g_o, y0_o, m_o, z0_o, *, tc, npairs):
    row = lax.broadcasted_iota(I32, (tc, tc), 0)
    col = lax.broadcasted_iota(I32, (tc, tc), 1)
    incl = row >= col
    strict = row > col
    tri = incl.astype(BF16)
    eye = (row == col).astype(F32)
    lane = lax.broadcasted_iota(I32, (1, LANE), 1)
    mh0 = (lane // W_HEAD_DIM) == 0
    prow = lax.broadcasted_iota(I32, (LANE, LANE), 0)
    pcol = lax.broadcasted_iota(I32, (LANE, LANE), 1)
    same_head = (prow // W_HEAD_DIM) == (pcol // W_HEAD_DIM)

    lw_all = lw_ref[...]
    hi = lw_all.astype(BF16)
    rem = lw_all - hi.astype(F32)
    mid = rem.astype(BF16)
    lo = (rem - mid.astype(F32)).astype(BF16)
    cum_all = (jnp.dot(tri, hi, preferred_element_type=F32) + jnp.dot(tri, mid, preferred_element_type=F32)
               + jnp.dot(tri, lo, preferred_element_type=F32))

    pairs = range(npairs)
    heads = [(p, h) for p in pairs for h in range(2)]
    sls = [slice(p * LANE, (p + 1) * LANE) for p in pairs]
    cum = [cum_all[:, sl] for sl in sls]
    tot = [c[tc - 1:tc, :] for c in cum]
    p_inv = [jnp.exp(-cum[p]) for p in pairs]
    p_end = [jnp.exp(tot[p] - cum[p]) for p in pairs]
    at = [al_ref[:, sls[p]] * jnp.exp(cum[p] - lw_all[:, sls[p]]) for p in pairs]
    rt = [r_ref[:, sls[p]] * jnp.exp(cum[p]) for p in pairs]
    bh = [be_ref[:, sls[p]] * p_end[p] for p in pairs]
    khat = [k_ref[:, sls[p]] * p_end[p] for p in pairs]
    v = [v_ref[:, sl] for sl in sls]
    rhs = [jnp.concatenate([be_ref[:, sls[p]] * p_inv[p], k_ref[:, sls[p]] * p_inv[p]], axis=0).astype(BF16)
           for p in pairs]

    a_ab, a_ak, qcat = {}, {}, {}
    for p, h in heads:
        mh = (lane // W_HEAD_DIM) == h
        lhs = jnp.concatenate([jnp.where(mh, at[p], 0.0), jnp.where(mh, rt[p], 0.0)], axis=0).astype(BF16)
        x = lax.dot_general(lhs, rhs[p], (((1,), (1,)), ((), ())), preferred_element_type=F32)
        a_ab[p, h] = jnp.where(strict, x[:tc, :tc], 0.0)
        a_ak[p, h] = jnp.where(strict, x[:tc, tc:], 0.0)
        qcat[p, h] = jnp.concatenate([jnp.where(incl, x[tc:, tc:], 0.0),
                                      jnp.where(incl, x[tc:, :tc], 0.0)], axis=1)

    t_inv = {ph: eye + a_ab[ph] for ph in heads}
    pw = {ph: _bdot(a_ab[ph], a_ab[ph]) for ph in heads}
    akv = {(p, h): _bdot(a_ak[p, h], v[p]) for p, h in heads}
    n = 2
    while n < tc:
        for ph in heads:
            res = _bdot(jnp.concatenate([pw[ph], t_inv[ph]], axis=0), pw[ph])
            t_inv[ph] = t_inv[ph] + res[tc:]
            pw[ph] = res[:tc]
        n *= 2
    tw = {(p, h): _bdot(t_inv[p, h], jnp.concatenate([at[p], akv[p, h]], axis=1)) for p, h in heads}

    zero = jnp.zeros((tc, LANE), F32)
    w = [jnp.where(mh0, tw[p, 0][:, :LANE], tw[p, 1][:, :LANE]) for p in pairs]
    u0 = [jnp.where(mh0, tw[p, 0][:, LANE:], tw[p, 1][:, LANE:]) for p in pairs]
    vu = [jnp.concatenate([jnp.concatenate([v[p], zero], axis=1),
                           jnp.concatenate([u0[p], w[p]], axis=1)], axis=0) for p in pairs]
    yg = {(p, h): _bdot(qcat[p, h], vu[p]) for p, h in heads}
    m_mat = [_bdot_tn(bh[p], w[p]) for p in pairs]
    z0 = [_bdot_tn(jnp.concatenate([bh[p], khat[p]], axis=0), jnp.concatenate([u0[p], v[p]], axis=0))
          for p in pairs]
    for p in pairs:
        y0_o[:, sls[p]] = jnp.where(mh0, yg[p, 0][:, :LANE], yg[p, 1][:, :LANE])
        g_o[:, sls[p]] = rt[p] + jnp.where(mh0, yg[p, 0][:, LANE:], yg[p, 1][:, LANE:])
        m_o[p] = jnp.where(same_head, m_mat[p], 0.0) + jnp.where(
            prow == pcol, jnp.broadcast_to(jnp.exp(tot[p]), (LANE, LANE)), 0.0)
        z0_o[p] = jnp.where(same_head, z0[p], 0.0)


def _rwkv_chunk_ops(r, lw, kh, v, alpha, beta, tc=CHUNK):
    bsz, seq, cw = r.shape
    npairs = cw // LANE
    nc = seq // tc
    tok = pl.BlockSpec((None, tc, cw), lambda b, c: (b, c, 0))
    mat = pl.BlockSpec((None, None, npairs, LANE, LANE), lambda b, c: (b, c, 0, 0, 0))
    tok_shape = jax.ShapeDtypeStruct((bsz, seq, cw), F32)
    mat_shape = jax.ShapeDtypeStruct((bsz, nc, npairs, LANE, LANE), F32)
    return pl.pallas_call(
        functools.partial(_rwkv_chunk_kernel, tc=tc, npairs=npairs),
        grid=(bsz, nc),
        in_specs=[tok] * 6,
        out_specs=[tok, tok, mat, mat],
        out_shape=[tok_shape, tok_shape, mat_shape, mat_shape],
        compiler_params=_cparams(("parallel", "parallel")),
        name="rwkv_chunk_ops",
    )(r, lw, kh, v, alpha, beta)


def _rwkv_scan_kernel(g_ref, y0_ref, m_ref, z0_ref, bo_ref, gate_ref, ln_ref, hs_ref, o_ref, state_ref,
                      *, tc, nch, npairs):
    @pl.when(pl.program_id(1) == 0)
    def _():
        state_ref[...] = jnp.zeros(state_ref.shape, F32)

    hs = hs_ref[...]
    inv_n = 1.0 / W_HEAD_DIM
    sls = [slice(p * LANE, (p + 1) * LANE) for p in range(npairs)]
    st = [state_ref[p] for p in range(npairs)]
    for ch in range(nch):
        rows = slice(ch * tc, (ch + 1) * tc)
        y = [_dot_hi(g_ref[rows, sls[p]], st[p]) + y0_ref[rows, sls[p]] for p in range(npairs)]
        st = [_dot_hi(m_ref[ch, p], st[p]) + z0_ref[ch, p] for p in range(npairs)]
        mu = [_dot_hi(y[p], hs) * inv_n for p in range(npairs)]
        d = [y[p] - mu[p] for p in range(npairs)]
        var = [_dot_hi(d[p] * d[p], hs) * inv_n for p in range(npairs)]
        for p in range(npairs):
            yn = d[p] * lax.rsqrt(var[p] + LNX_EPS) * ln_ref[0:1, sls[p]] + ln_ref[1:2, sls[p]]
            o_ref[rows, sls[p]] = ((yn + bo_ref[rows, sls[p]]) * gate_ref[rows, sls[p]]).astype(o_ref.dtype)
    for p in range(npairs):
        state_ref[p] = st[p]


def _rwkv_scan(g, y0, m, z0, bonus, gate, ln, tc=CHUNK, nch=4):
    bsz, seq, cw = g.shape
    npairs = cw // LANE
    nch = min(nch, seq // tc)
    hs = _head_sum_matrix(LANE, W_HEAD_DIM)
    tok = pl.BlockSpec((None, tc * nch, cw), lambda b, c: (b, c, 0))
    mat = pl.BlockSpec((None, nch, npairs, LANE, LANE), lambda b, c: (b, c, 0, 0, 0))
    return pl.pallas_call(
        functools.partial(_rwkv_scan_kernel, tc=tc, nch=nch, npairs=npairs),
        grid=(bsz, seq // (tc * nch)),
        in_specs=[tok, tok, mat, mat, tok, tok,
                  pl.BlockSpec(ln.shape, lambda b, c: (0, 0)),
                  pl.BlockSpec(hs.shape, lambda b, c: (0, 0))],
        out_specs=tok,
        out_shape=jax.ShapeDtypeStruct((bsz, seq, cw), BF16),
        scratch_shapes=[pltpu.VMEM((npairs, LANE, LANE), F32)],
        compiler_params=_cparams(("parallel", "arbitrary")),
        name="rwkv_scan",
    )(g, y0, m, z0, bonus, gate, ln, hs)


def _out_proj_kernel(x_ref, a_ref, b_ref, c_ref, w_ref, o_ref, *, wa, wb):
    acc = jnp.dot(a_ref[...], w_ref[0:wa, :], preferred_element_type=F32)
    acc = acc + jnp.dot(b_ref[...], w_ref[wa:wa + wb, :], preferred_element_type=F32)
    acc = acc + jnp.dot(c_ref[...], w_ref[wa + wb:, :], preferred_element_type=F32)
    o_ref[...] = x_ref[...] + acc


def _out_proj(x2d, oa, ob, oc, w, tm=1024, tn=1024):
    m, d = x2d.shape
    wa, wb, wc = oa.shape[1], ob.shape[1], oc.shape[1]
    tn = min(tn, d)
    return pl.pallas_call(
        functools.partial(_out_proj_kernel, wa=wa, wb=wb),
        grid=(m // tm, d // tn),
        in_specs=[pl.BlockSpec((tm, tn), lambda i, j: (i, j)),
                  pl.BlockSpec((tm, wa), lambda i, j: (i, 0)),
                  pl.BlockSpec((tm, wb), lambda i, j: (i, 0)),
                  pl.BlockSpec((tm, wc), lambda i, j: (i, 0)),
                  pl.BlockSpec((wa + wb + wc, tn), lambda i, j: (0, j))],
        out_specs=pl.BlockSpec((tm, tn), lambda i, j: (i, j)),
        out_shape=jax.ShapeDtypeStruct((m, d), F32),
        compiler_params=_cparams(("parallel", "arbitrary")),
        name="out_proj",
    )(x2d, oa, ob, oc, w)


def _gate_up_kernel(h_ref, wg_ref, wu_ref, o_ref):
    h = h_ref[...]
    gate = jnp.dot(h, wg_ref[...], preferred_element_type=F32)
    up = jnp.dot(h, wu_ref[...], preferred_element_type=F32)
    o_ref[...] = (gate / (1.0 + jnp.exp(-gate)) * up).astype(o_ref.dtype)


def _gate_up(h, w_gate_up, tm=1024, tf_cap=512):
    m, d = h.shape
    dff = w_gate_up.shape[1] // 2
    tf = _pick_tile(dff, tf_cap)
    nf = dff // tf
    return pl.pallas_call(
        _gate_up_kernel,
        grid=(m // tm, nf),
        in_specs=[pl.BlockSpec((tm, d), lambda i, j: (i, 0)),
                  pl.BlockSpec((d, tf), lambda i, j: (0, j)),
                  pl.BlockSpec((d, tf), lambda i, j: (0, j + nf))],
        out_specs=pl.BlockSpec((tm, tf), lambda i, j: (i, j)),
        out_shape=jax.ShapeDtypeStruct((m, dff), BF16),
        compiler_params=_cparams(("parallel", "arbitrary")),
        name="ffn_gate_up",
    )(h, w_gate_up, w_gate_up)


def _down_kernel(x_ref, a_ref, w_ref, o_ref):
    o_ref[...] = x_ref[...] + jnp.dot(a_ref[...], w_ref[...], preferred_element_type=F32)


def _down_proj(x2d, act, w, tm=1024, tn=512):
    m, d = x2d.shape
    dff = act.shape[1]
    tn = min(tn, d)
    return pl.pallas_call(
        _down_kernel,
        grid=(m // tm, d // tn),
        in_specs=[pl.BlockSpec((tm, tn), lambda i, j: (i, j)),
                  pl.BlockSpec((tm, dff), lambda i, j: (i, 0)),
                  pl.BlockSpec((dff, tn), lambda i, j: (0, j))],
        out_specs=pl.BlockSpec((tm, tn), lambda i, j: (i, j)),
        out_shape=jax.ShapeDtypeStruct((m, d), F32),
        compiler_params=_cparams(("parallel", "arbitrary")),
        name="ffn_down",
    )(x2d, act, w)


def _pad_cols(w, width):
    return jnp.pad(w, ((0, 0), (0, width - w.shape[1])))


def _pad_rows(w, height):
    return jnp.pad(w, ((0, height - w.shape[0]), (0, 0)))


def _pad_vec(v, width):
    return jnp.pad(v, (0, width - v.shape[0]))


def kernel(x, norm_mix_g, w_in, w_in_vres, rwkv_mu, rwkv_mu_vres, rwkv_w0, rwkv_w_lora_b, rwkv_a0,
           rwkv_a_lora_b, rwkv_v0, rwkv_v_lora_b, rwkv_g_lora_b, rwkv_k_k, rwkv_k_a, rwkv_r_k,
           rwkv_lnx_g, rwkv_lnx_b, w_out, norm_ffn_g, w_gate_up, w_down, final_norm_g):
    bsz, seq, d_model = x.shape
    depth = w_in.shape[0]
    m = bsz * seq
    d_mix = w_out.shape[1]
    a_w = d_mix // 2
    qi_w = IDX_HEADS * IDX_DIM
    b_v_w = d_mix // 4
    b_qk_w = b_v_w // 2
    c_w = d_mix // 4
    r_qk_dim = b_qk_w // R_HEADS
    top_k = min(TOPK_MAX, seq // 4)
    vres_pad = 2 * LANE

    o_q, o_k, o_v = 0, a_w, 2 * a_w
    o_qi = 3 * a_w
    o_ki = o_qi + qi_w
    o_wi = o_ki + IDX_DIM
    o_bq = o_wi + IDX_HEADS
    o_bk = o_bq + b_qk_w
    o_bv = o_bk + b_qk_w
    o_bg = o_bv + b_v_w
    o_c = o_bg + b_v_w
    o_wl = o_c + 3 * c_w
    o_al = o_wl + DECAY_LORA
    o_gl = o_al + AAA_LORA
    n_in = o_gl + GATE_LORA

    a_rot = A_HEAD_DIM // ROPE_FRAC
    tq_ = _rope_tables(seq, A_HEAD_DIM, a_rot, ROPE_THETA, scale=A_HEAD_DIM ** -0.5 * math.log2(math.e))
    tk_ = _rope_tables(seq, A_HEAD_DIM, a_rot, ROPE_THETA)
    tab_q = tuple(t[None] for t in tq_)
    tab_k = tuple(t[None] for t in tk_)
    i_rot = IDX_DIM // ROPE_FRAC
    tab_qi = tuple(t[None] for t in _rope_tables(seq, IDX_DIM, i_rot, ROPE_THETA))
    lane = jnp.arange(LANE)
    kiwi_pass = jnp.where(lane < IDX_DIM, 1.0, IDX_W_SCALE).astype(F32)
    c_kw, s1_kw, s2_kw = _rope_tables(seq, LANE, i_rot, ROPE_THETA, pass_scale=kiwi_pass)
    tab_kiwi = (c_kw[None], s1_kw[None], s2_kw[None])
    tbq = _rope_tables(seq, r_qk_dim, r_qk_dim, R_THETA)
    tbk = _rope_tables(seq, r_qk_dim, r_qk_dim, R_THETA, scale=r_qk_dim ** -0.5)
    tab_bqk = tuple(jnp.stack([a, b]) for a, b in zip(tbq, tbk))

    x2d = x.reshape(m, d_model)
    v_first = None
    for l in range(depth):
        wl_ = w_in[l]
        cols = lambda o, n: wl_[:, o:o + n]
        w_q = cols(o_q, a_w).astype(BF16)
        w_k = cols(o_k, a_w).astype(BF16)
        w_v = cols(o_v, a_w).astype(BF16)
        w_bv = cols(o_bv, b_v_w).astype(BF16)
        w_qi = cols(o_qi, qi_w).astype(BF16)
        w_kiwi = _pad_cols(cols(o_ki, IDX_DIM + IDX_HEADS), LANE).astype(BF16)
        w_bqk = cols(o_bq, 2 * b_qk_w).astype(BF16)
        w_bg = cols(o_bg, b_v_w).astype(BF16)
        vres_w = (w_in_vres[l - 1] if l > 0 else jnp.zeros((d_model, MV_LORA), F32))
        w_c = jnp.concatenate([cols(o_c, 3 * c_w), _pad_cols(cols(o_wl, DECAY_LORA), LANE),
                               _pad_cols(cols(o_al, AAA_LORA), LANE), cols(o_gl, GATE_LORA),
                               _pad_cols(vres_w, vres_pad)], axis=1).astype(BF16)
        mu = rwkv_mu[l]
        mu_vres = rwkv_mu_vres[l - 1] if l > 0 else jnp.zeros((MV_LORA,), F32)
        mu_c = jnp.concatenate([mu[:3 * c_w], _pad_vec(mu[3 * c_w:3 * c_w + DECAY_LORA], LANE),
                                _pad_vec(mu[3 * c_w + DECAY_LORA:3 * c_w + DECAY_LORA + AAA_LORA], LANE),
                                mu[3 * c_w + DECAY_LORA + AAA_LORA:], _pad_vec(mu_vres, vres_pad)])[None, :]

        h = _rmsnorm(x2d, norm_mix_g[l], BF16)
        qt = _mm_rope(h, w_q, tab_q, a_rot // 2, seq, a_w, BF16, transpose=True)
        ak = _mm_rope(h, w_k, tab_k, a_rot // 2, seq, a_w, BF16).reshape(bsz, seq, a_w)
        vt = _mm_plain(h, w_v, BF16, seq, transpose=True, pad_rows=ONES_ROWS)
        bv = _mm_plain(h, w_bv, BF16, seq, tn_cap=512).reshape(bsz, seq, b_v_w)
        qit = _mm_rope(h, w_qi, tab_qi, i_rot // 2, seq, qi_w, BF16, transpose=True)
        kiwi = _mm_rope(h, w_kiwi, tab_kiwi, i_rot // 2, seq, LANE, F32).reshape(bsz, seq, LANE)
        bqk = _mm_rope(h, w_bqk, tab_bqk, r_qk_dim // 2, seq, b_qk_w, BF16).reshape(bsz, seq, 2 * b_qk_w)
        bg = _mm_plain(h, w_bg, F32, seq).reshape(bsz, seq, b_v_w)
        cproj = _mm_plain(h, w_c, F32, seq, tn_cap=768).reshape(bsz, seq, w_c.shape[1])

        ki = kiwi[:, :, :IDX_DIM].astype(BF16)
        wit = jnp.swapaxes(kiwi[:, :, IDX_DIM:IDX_DIM + IDX_HEADS], 1, 2)
        bias_t = _indexer_bias_t(ki, qit, wit, top_k)
        out_a = _masked_attention_t(qt, ak, vt, bias_t)

        out_b = _retention(bqk, bv, bg)

        vecs = jnp.stack([rwkv_w0[l], rwkv_a0[l], rwkv_k_k[l], rwkv_k_a[l], rwkv_r_k[l].reshape(-1),
                          rwkv_v0[l - 1] if l > 0 else jnp.zeros((c_w,), F32),
                          jnp.zeros((c_w,), F32), jnp.zeros((c_w,), F32)])
        wlb = _pad_rows(rwkv_w_lora_b[l], LANE).astype(BF16)
        alb = _pad_rows(rwkv_a_lora_b[l], LANE).astype(BF16)
        glb = rwkv_g_lora_b[l].astype(BF16)
        vlb = _pad_rows(rwkv_v_lora_b[l - 1], vres_pad).astype(BF16) if l > 0 else None
        r_, lw_, kh_, v_, al_, be_, g_, bo_ = _rwkv_prep(cproj, mu_c, vecs, wlb, alb, glb, vlb, v_first, c_w)
        if l == 0:
            v_first = v_
        gm, y0, mm, z0 = _rwkv_chunk_ops(r_, lw_, kh_, v_, al_, be_)
        ln = jnp.stack([rwkv_lnx_g[l], rwkv_lnx_b[l]] + [jnp.zeros((c_w,), F32)] * 6)
        out_c = _rwkv_scan(gm, y0, mm, z0, bo_, g_, ln)

        x2d = _out_proj(x2d, out_a.reshape(m, a_w), out_b.reshape(m, b_v_w), out_c.reshape(m, c_w),
                        w_out[l].astype(BF16))

        h = _rmsnorm(x2d, norm_ffn_g[l], BF16)
        act = _gate_up(h, w_gate_up[l].astype(BF16))
        x2d = _down_proj(x2d, act, w_down[l].astype(BF16))

    return _rmsnorm(x2d, final_norm_g, F32).reshape(bsz, seq, d_model)
```

```python
import functools
import math

import jax
import jax.numpy as jnp
from jax import lax
from jax.experimental import pallas as pl
from jax.experimental.pallas import tpu as pltpu

F32 = jnp.float32
BF16 = jnp.bfloat16
I32 = jnp.int32
HIGHEST = lax.Precision.HIGHEST

CHUNK = 64
A_HEAD_DIM = 128
IDX_HEADS = 16
IDX_DIM = 64
IDX_W_SCALE = (IDX_HEADS * IDX_DIM) ** -0.5
TOPK_MAX = 256
ROPE_THETA = 500000.0
ROPE_FRAC = 4
R_HEADS = 4
R_THETA = 10000.0
W_HEAD_DIM = 64
DECAY_LORA = 96
AAA_LORA = 96
MV_LORA = 64
GATE_LORA = 256
LNX_EPS = 64e-5
RMS_EPS = 1e-5

LANE = 128
VMEM_LIMIT = 56 * 1024 * 1024
NEG_BIAS = -1e30
M_INIT = -1e20
INT_MIN = -2147483648
COUNT_CHAINS = 8
SCORE_GROUP = 4
ONES_ROWS = 8


def _cparams(sem):
    return pltpu.CompilerParams(dimension_semantics=sem, vmem_limit_bytes=VMEM_LIMIT)


def _pick_tile(n, cap):
    best = LANE
    t = LANE
    while t <= min(n, cap):
        if n % t == 0:
            best = t
        t += LANE
    return best


def _rmsnorm_kernel(x_ref, g_ref, o_ref):
    x = x_ref[...]
    ms = jnp.mean(x * x, axis=-1, keepdims=True)
    o_ref[...] = (x * lax.rsqrt(ms + RMS_EPS) * g_ref[...]).astype(o_ref.dtype)


def _rmsnorm(x2d, g, out_dtype, tm=512):
    m, d = x2d.shape
    return pl.pallas_call(
        _rmsnorm_kernel,
        grid=(m // tm,),
        in_specs=[pl.BlockSpec((tm, d), lambda i: (i, 0)),
                  pl.BlockSpec((1, d), lambda i: (0, 0))],
        out_specs=pl.BlockSpec((tm, d), lambda i: (i, 0)),
        out_shape=jax.ShapeDtypeStruct((m, d), out_dtype),
        compiler_params=_cparams(("parallel",)),
        name="rmsnorm",
    )(x2d, g.reshape(1, d).astype(F32))


def _store_cols(o_ref, blk, x, transpose, pad_rows):
    if transpose:
        r0 = blk * (LANE + pad_rows)
        o_ref[r0:r0 + LANE, :] = x.T.astype(o_ref.dtype)
        if pad_rows:
            o_ref[r0 + LANE:r0 + LANE + pad_rows, :] = jnp.ones((pad_rows, x.shape[0]), o_ref.dtype)
    else:
        o_ref[:, blk * LANE:(blk + 1) * LANE] = x.astype(o_ref.dtype)


def _proj_out_spec(m, n, tm, tn, seq, out_dtype, transpose, pad_rows):
    if not transpose:
        return pl.BlockSpec((tm, tn), lambda i, j: (i, j)), jax.ShapeDtypeStruct((m, n), out_dtype)
    tpb = seq // tm
    rows = lambda cols: cols // LANE * (LANE + pad_rows)
    return (pl.BlockSpec((None, rows(tn), tm), lambda i, j: (i // tpb, j, i % tpb)),
            jax.ShapeDtypeStruct((m // seq, rows(n), seq), out_dtype))


def _mm_plain_kernel(a_ref, w_ref, o_ref, *, transpose, pad_rows):
    acc = jnp.dot(a_ref[...], w_ref[...], preferred_element_type=F32)
    if not transpose:
        o_ref[...] = acc.astype(o_ref.dtype)
        return
    for blk in range(acc.shape[1] // LANE):
        _store_cols(o_ref, blk, acc[:, blk * LANE:(blk + 1) * LANE], transpose, pad_rows)


def _mm_plain(a, w, out_dtype, seq, tm=1024, tn_cap=1024, transpose=False, pad_rows=0):
    m, k = a.shape
    n = w.shape[1]
    tn = _pick_tile(n, tn_cap)
    out_spec, out_shape = _proj_out_spec(m, n, tm, tn, seq, out_dtype, transpose, pad_rows)
    return pl.pallas_call(
        functools.partial(_mm_plain_kernel, transpose=transpose, pad_rows=pad_rows),
        grid=(m // tm, n // tn),
        in_specs=[pl.BlockSpec((tm, k), lambda i, j: (i, 0)),
                  pl.BlockSpec((k, tn), lambda i, j: (0, j))],
        out_specs=out_spec,
        out_shape=out_shape,
        compiler_params=_cparams(("parallel", "arbitrary")),
        name="proj_plain",
    )(a, w)


def _mm_rope_kernel(a_ref, w_ref, c_ref, s1_ref, s2_ref, o_ref, *, half, transpose):
    acc = jnp.dot(a_ref[...], w_ref[...], preferred_element_type=F32)
    c, s1, s2 = c_ref[...], s1_ref[...], s2_ref[...]
    for blk in range(acc.shape[1] // LANE):
        x = acc[:, blk * LANE:(blk + 1) * LANE]
        up = pltpu.roll(x, LANE - half, axis=1)
        dn = pltpu.roll(x, half, axis=1)
        _store_cols(o_ref, blk, x * c + up * s1 + dn * s2, transpose, 0)


def _mm_rope(a, w, tables, half, seq, tn, out_dtype, tm=1024, transpose=False):
    m, k = a.shape
    n = w.shape[1]
    tpb = seq // tm
    tab_spec = pl.BlockSpec((None, tm, LANE), lambda i, j: (j, i % tpb, 0))
    out_spec, out_shape = _proj_out_spec(m, n, tm, tn, seq, out_dtype, transpose, 0)
    return pl.pallas_call(
        functools.partial(_mm_rope_kernel, half=half, transpose=transpose),
        grid=(m // tm, n // tn),
        in_specs=[pl.BlockSpec((tm, k), lambda i, j: (i, 0)),
                  pl.BlockSpec((k, tn), lambda i, j: (0, j)),
                  tab_spec, tab_spec, tab_spec],
        out_specs=out_spec,
        out_shape=out_shape,
        compiler_params=_cparams(("parallel", "arbitrary")),
        name="proj_rope",
    )(a, w, *tables)


def _rope_tables(seq, group, rot_dim, theta, scale=1.0, pass_scale=None):
    half = rot_dim // 2
    freqs = jnp.power(F32(theta), -jnp.arange(half, dtype=F32) / half)
    ang = jnp.arange(seq, dtype=F32)[:, None] * freqs[None, :]
    cos, sin = jnp.cos(ang), jnp.sin(ang)
    lane = jnp.arange(LANE) % group
    idx = lane % half
    cosl, sinl = cos[:, idx], sin[:, idx]
    passv = jnp.ones((LANE,), F32) if pass_scale is None else pass_scale
    c = jnp.where(lane < rot_dim, cosl, passv[None, :])
    s1 = jnp.where(lane < half, -sinl, 0.0)
    s2 = jnp.where((lane >= half) & (lane < rot_dim), sinl, 0.0)
    return c * scale, s1 * scale, s2 * scale


def _indexer_kernel(qi_ref, kit_ref, wi_ref, bias_ref, key_ref, *, tq, ts, tki, tkc, top_k, seq):
    qb = pl.program_id(1)
    nkb = ((qb + 1) * tq + tki - 1) // tki
    q_chunk = (qb * tq + lax.broadcasted_iota(I32, (tq, 1), 0)) // CHUNK
    w = wi_ref[...]

    def score_block(kb, rmax):
        k0 = pl.multiple_of(kb * tki, tki)
        kblk = kit_ref[:, pl.ds(k0, tki)]
        k_chunk = (k0 + lax.broadcasted_iota(I32, (1, tki), 1)) // CHUNK
        new_max = []
        for rh in range(tq // ts):
            rows = slice(rh * ts, (rh + 1) * ts)
            acc = jnp.zeros((ts, tki), F32)
            for h in range(IDX_HEADS):
                x = jnp.dot(qi_ref[rows, h * IDX_DIM:(h + 1) * IDX_DIM], kblk, preferred_element_type=F32)
                acc = acc + jnp.maximum(x, 0.0) * w[rows, h:h + 1]
            bits = lax.bitcast_convert_type(acc, I32)
            key = bits ^ ((bits >> 31) & jnp.int32(0x7FFFFFFF))
            key = jnp.where(k_chunk <= q_chunk[rows], key, jnp.int32(INT_MIN))
            key_ref[rows, pl.ds(k0, tki)] = key
            r = rmax[rows]
            for c in range(tki // LANE):
                r = jnp.maximum(r, key[:, c * LANE:(c + 1) * LANE])
            new_max.append(r)
        return jnp.concatenate(new_max, axis=0)

    rmax = lax.fori_loop(0, nkb, score_block, jnp.full((tq, LANE), INT_MIN, I32))
    rowmax = jnp.max(rmax, axis=1, keepdims=True)

    nkc = (nkb * tki + tkc - 1) // tkc

    def pad_block(kb, carry):
        k0 = pl.multiple_of(kb * tki, tki)
        key_ref[:, pl.ds(k0, tki)] = jnp.full((tq, tki), INT_MIN, I32)
        return carry

    lax.fori_loop(nkb, nkc * (tkc // tki), pad_block, 0)

    def count_ge(cand):
        totals = []
        for rh in range(tq // ts):
            rows = slice(rh * ts, (rh + 1) * ts)
            candb = jnp.broadcast_to(cand[rows], (ts, LANE))

            def count_block(kb, cnt, rows=rows, candb=candb):
                k0 = pl.multiple_of(kb * tkc, tkc)
                for c in range(tkc // LANE):
                    blk = key_ref[rows, pl.ds(k0 + c * LANE, LANE)]
                    cnt = cnt + jnp.where(blk >= candb, 1, 0)
                return cnt

            cnt = lax.fori_loop(0, nkc, count_block, jnp.zeros((ts, LANE), I32))
            totals.append(jnp.sum(cnt, axis=1, keepdims=True))
        return jnp.concatenate(totals, axis=0)

    probe = jnp.where(rowmax > jnp.int32(INT_MIN + (1 << 25)), rowmax - jnp.int32(3 << 23),
                      jnp.int32(INT_MIN + 1))

    def bisect_cond(state):
        it, n_active = state[0], state[1]
        return (n_active > 0) & (it < 40)

    def bisect_body(state):
        it, _, lo, hi, cnt_lo, cnt_hi = state
        mid = (lo >> 1) + (hi >> 1) + (lo & hi & 1)
        mid = jnp.where(it == 0, probe, mid)
        cnt = count_ge(mid)
        ge = cnt >= top_k
        exact = cnt == top_k
        new_lo = jnp.where(ge, mid, lo)
        new_hi = jnp.where(exact, mid + 1, jnp.where(ge, hi, mid))
        cnt_lo = jnp.where(ge, cnt, cnt_lo)
        cnt_hi = jnp.where(ge, cnt_hi, cnt)
        active = (new_lo + 1) < new_hi
        n_active = jnp.max(active.astype(I32))
        return it + 1, n_active, new_lo, new_hi, cnt_lo, cnt_hi

    big = jnp.int32(1 << 30)
    init = (jnp.int32(0), jnp.int32(1), jnp.full((tq, 1), INT_MIN, I32), rowmax + 1,
            jnp.full((tq, 1), big, I32), jnp.zeros((tq, 1), I32))
    _, _, lo, _, cnt_lo, cnt_hi = lax.while_loop(bisect_cond, bisect_body, init)

    thr = jnp.maximum(lo, jnp.int32(INT_MIN + 1))
    thrb = jnp.broadcast_to(thr, (tq, LANE))
    tie_row = (cnt_lo > top_k) & (lo > jnp.int32(INT_MIN))
    any_tie = jnp.max(tie_row.astype(I32))

    @pl.when(any_tie == 0)
    def _():
        def write_block(kb, carry):
            k0 = pl.multiple_of(kb * tki, tki)
            for c in range(tki // LANE):
                blk = key_ref[:, pl.ds(k0 + c * LANE, LANE)]
                bias_ref[:, pl.ds(k0 + c * LANE, LANE)] = jnp.where(blk >= thrb, 0.0, NEG_BIAS)
            return carry

        lax.fori_loop(0, nkb, write_block, 0)

    @pl.when(any_tie != 0)
    def _():
        need = jnp.where(tie_row, top_k - cnt_hi, big).astype(F32)
        ur = lax.broadcasted_iota(I32, (LANE, LANE), 0)
        uc = lax.broadcasted_iota(I32, (LANE, LANE), 1)
        upper = (ur <= uc).astype(BF16)

        def write_block(kb, run):
            k0 = pl.multiple_of(kb * tki, tki)
            for c in range(tki // LANE):
                blk = key_ref[:, pl.ds(k0 + c * LANE, LANE)]
                eq = blk == thrb
                pc = jnp.dot(jnp.where(eq, 1.0, 0.0).astype(BF16), upper, preferred_element_type=F32)
                keep_eq = jnp.where((run + pc) <= need, 0.0, NEG_BIAS)
                bias_ref[:, pl.ds(k0 + c * LANE, LANE)] = jnp.where(
                    blk > thrb, 0.0, jnp.where(eq, keep_eq, NEG_BIAS))
                run = run + pc[:, LANE - 1:LANE]
            return run

        lax.fori_loop(0, nkb, write_block, jnp.zeros((tq, 1), F32))

    def fill_block(kb, carry):
        k0 = pl.multiple_of(kb * tki, tki)
        bias_ref[:, pl.ds(k0, tki)] = jnp.full((tq, tki), NEG_BIAS, F32)
        return carry

    lax.fori_loop(nkb, seq // tki, fill_block, 0)


def _indexer_bias(qi, kit, wi, top_k, tq=256, ts=128, tki=512):
    bsz, seq, _ = qi.shape
    tki = min(tki, seq)
    tkc = 2 * tki if (seq // tki) % 2 == 0 else tki
    return pl.pallas_call(
        functools.partial(_indexer_kernel, tq=tq, ts=ts, tki=tki, tkc=tkc, top_k=top_k, seq=seq),
        grid=(bsz, seq // tq),
        in_specs=[pl.BlockSpec((None, tq, IDX_HEADS * IDX_DIM), lambda b, i: (b, i, 0)),
                  pl.BlockSpec((None, IDX_DIM, seq), lambda b, i: (b, 0, 0)),
                  pl.BlockSpec((None, tq, IDX_HEADS), lambda b, i: (b, i, 0))],
        out_specs=pl.BlockSpec((None, tq, seq), lambda b, i: (b, i, 0)),
        out_shape=jax.ShapeDtypeStruct((bsz, seq, seq), F32),
        scratch_shapes=[pltpu.VMEM((tq, seq), I32)],
        compiler_params=_cparams(("parallel", "arbitrary")),
        name="indexer_select",
    )(qi, kit, wi)


def _attn_kernel(qidx_ref, kidx_ref, q_ref, kt_ref, v_ref, b_ref, o_ref, m_ref, l_ref, acc_ref,
                 *, nheads, hd, tq, tk):
    p = pl.program_id(1)
    qb = qidx_ref[p]
    kb = kidx_ref[p]
    kb_last = ((qb + 1) * tq - 1) // tk

    @pl.when(kb == 0)
    def _():
        m_ref[...] = jnp.full(m_ref.shape, M_INIT, F32)
        l_ref[...] = jnp.zeros(l_ref.shape, F32)
        acc_ref[...] = jnp.zeros(acc_ref.shape, F32)

    bias = b_ref[...]

    def logits(h):
        sl = slice(h * hd, (h + 1) * hd)
        return jnp.dot(q_ref[:, sl], kt_ref[sl, :], preferred_element_type=F32)

    s_next = logits(0)
    for h in range(nheads):
        sl = slice(h * hd, (h + 1) * hd)
        s = s_next + bias
        if h + 1 < nheads:
            s_next = logits(h + 1)
        m_old = m_ref[h]
        m_new = jnp.maximum(m_old, jnp.max(s, axis=1, keepdims=True))
        alpha = jnp.exp(m_old - m_new)
        pr = jnp.exp(s - pltpu.repeat(m_new, tk // LANE, axis=1))
        l_ref[h] = alpha * l_ref[h] + jnp.sum(pr, axis=1, keepdims=True)
        m_ref[h] = m_new
        acc_ref[:, sl] = alpha * acc_ref[:, sl] + jnp.dot(pr.astype(BF16), v_ref[:, sl],
                                                          preferred_element_type=F32)

    @pl.when(kb == kb_last)
    def _():
        for h in range(nheads):
            sl = slice(h * hd, (h + 1) * hd)
            o_ref[:, sl] = (acc_ref[:, sl] / l_ref[h]).astype(o_ref.dtype)


def _masked_attention(q, kt, v, bias, tq=256, tk=512):
    bsz, seq, width = q.shape
    nheads = width // A_HEAD_DIM
    nq = seq // tq
    tk = min(tk, seq)
    pairs = [(i, j) for i in range(nq) for j in range(((i + 1) * tq - 1) // tk + 1)]
    qidx = jnp.asarray([pq for pq, _ in pairs], I32)
    kidx = jnp.asarray([pk for _, pk in pairs], I32)
    grid_spec = pltpu.PrefetchScalarGridSpec(
        num_scalar_prefetch=2,
        grid=(bsz, len(pairs)),
        in_specs=[pl.BlockSpec((None, tq, width), lambda b, p, qi, ki: (b, qi[p], 0)),
                  pl.BlockSpec((None, width, tk), lambda b, p, qi, ki: (b, 0, ki[p])),
                  pl.BlockSpec((None, tk, width), lambda b, p, qi, ki: (b, ki[p], 0)),
                  pl.BlockSpec((None, tq, tk), lambda b, p, qi, ki: (b, qi[p], ki[p]))],
        out_specs=pl.BlockSpec((None, tq, width), lambda b, p, qi, ki: (b, qi[p], 0)),
        scratch_shapes=[pltpu.VMEM((nheads, tq, LANE), F32),
                        pltpu.VMEM((nheads, tq, LANE), F32),
                        pltpu.VMEM((tq, width), F32)],
    )
    return pl.pallas_call(
        functools.partial(_attn_kernel, nheads=nheads, hd=A_HEAD_DIM, tq=tq, tk=tk),
        grid_spec=grid_spec,
        out_shape=jax.ShapeDtypeStruct((bsz, seq, width), BF16),
        compiler_params=_cparams(("parallel", "arbitrary")),
        name="masked_attention",
    )(qidx, kidx, q, kt, v, bias)


def _indexer_t_kernel(ki_ref, qit_ref, wit_ref, bias_ref, key_ref, *, tq, tks, tkc, top_k, seq):
    qb = pl.program_id(1)
    nks = ((qb + 1) * tq + tks - 1) // tks
    nkc = (nks * tks + tkc - 1) // tkc
    q_chunk = (qb * tq + lax.broadcasted_iota(I32, (1, tq), 1)) // CHUNK
    w = wit_ref[...]

    def score_block(kb, rmax):
        k0 = pl.multiple_of(kb * tks, tks)
        kblk = ki_ref[pl.ds(k0, tks), :]
        acc = jnp.zeros((tks, tq), F32)
        for h0 in range(0, IDX_HEADS, SCORE_GROUP):
            xs = [jnp.dot(kblk, qit_ref[h * IDX_DIM:(h + 1) * IDX_DIM, :], preferred_element_type=F32)
                  for h in range(h0, h0 + SCORE_GROUP)]
            for i, x in enumerate(xs):
                acc = acc + jnp.maximum(x, 0.0) * w[h0 + i:h0 + i + 1, :]
        k_chunk = (k0 + lax.broadcasted_iota(I32, (tks, 1), 0)) // CHUNK
        bits = lax.bitcast_convert_type(acc, I32)
        key = bits ^ ((bits >> 31) & jnp.int32(0x7FFFFFFF))
        key = jnp.where(k_chunk <= q_chunk, key, jnp.int32(INT_MIN))
        key_ref[pl.ds(k0, tks), :] = key
        return jnp.maximum(rmax, jnp.max(key.reshape(tks // 8, 8, tq), axis=0))

    rmax = lax.fori_loop(0, nks, score_block, jnp.full((8, tq), INT_MIN, I32))
    colmax = jnp.max(rmax, axis=0, keepdims=True)

    def pad_block(kb, carry):
        k0 = pl.multiple_of(kb * tks, tks)
        key_ref[pl.ds(k0, tks), :] = jnp.full((tks, tq), INT_MIN, I32)
        return carry

    lax.fori_loop(nks, nkc * (tkc // tks), pad_block, 0)

    def count_ge(cand):
        def count_block(kb, cnt):
            k0 = pl.multiple_of(kb * tkc, tkc)
            part = tkc // COUNT_CHAINS
            sums = []
            for g in range(COUNT_CHAINS):
                ind = jnp.where(key_ref[pl.ds(k0 + g * part, part), :] >= cand, 1, 0)
                sums.append(jnp.sum(ind.reshape(part // 8, 8, tq), axis=0))
            while len(sums) > 1:
                sums = [a + b for a, b in zip(sums[0::2], sums[1::2])]
            return cnt + sums[0]

        cnt = lax.fori_loop(0, nkc, count_block, jnp.zeros((8, tq), I32))
        return jnp.sum(cnt, axis=0, keepdims=True)

    def probe_step(state, mid):
        lo, hi, cnt_lo, cnt_hi = state
        cnt = count_ge(mid)
        ge = cnt >= top_k
        new_hi = jnp.where(cnt == top_k, mid + 1, jnp.where(ge, hi, mid))
        return (jnp.where(ge, mid, lo), new_hi, jnp.where(ge, cnt, cnt_lo), jnp.where(ge, cnt_hi, cnt))

    def midpoint(state):
        lo, hi = state[0], state[1]
        return (lo >> 1) + (hi >> 1) + (lo & hi & 1)

    def n_active(state):
        return jnp.max(((state[0] + 1) < state[1]).astype(I32))

    big = jnp.int32(1 << 30)
    state = (jnp.full((1, tq), INT_MIN, I32), colmax + 1, jnp.full((1, tq), big, I32), jnp.zeros((1, tq), I32))
    first = jnp.where(colmax > jnp.int32(INT_MIN + (1 << 25)), colmax - jnp.int32(3 << 23),
                      jnp.int32(INT_MIN + 1))
    state = probe_step(state, first)

    def bisect_body(carry):
        it, _, state = carry
        state = probe_step(state, midpoint(state))
        state = probe_step(state, midpoint(state))
        return it + 2, n_active(state), state

    _, _, state = lax.while_loop(lambda c: (c[1] > 0) & (c[0] < 40), bisect_body,
                                 (jnp.int32(0), n_active(state), state))
    lo, _, cnt_lo, cnt_hi = state

    thr = jnp.maximum(lo, jnp.int32(INT_MIN + 1))
    tie_q = (cnt_lo > top_k) & (lo > jnp.int32(INT_MIN))
    any_tie = jnp.max(tie_q.astype(I32))

    @pl.when(any_tie == 0)
    def _():
        def write_block(kb, carry):
            k0 = pl.multiple_of(kb * tks, tks)
            bias_ref[pl.ds(k0, tks), :] = jnp.where(key_ref[pl.ds(k0, tks), :] >= thr, 0.0, NEG_BIAS)
            return carry

        lax.fori_loop(0, nks, write_block, 0)

    @pl.when(any_tie != 0)
    def _():
        need = jnp.where(tie_q, top_k - cnt_hi, big).astype(F32)
        lr = lax.broadcasted_iota(I32, (LANE, LANE), 0)
        lc = lax.broadcasted_iota(I32, (LANE, LANE), 1)
        lower = (lr >= lc).astype(BF16)

        def write_block(kb, run):
            k0 = pl.multiple_of(kb * LANE, LANE)
            blk = key_ref[pl.ds(k0, LANE), :]
            eq = blk == thr
            pc = jnp.dot(lower, jnp.where(eq, 1.0, 0.0).astype(BF16), preferred_element_type=F32)
            keep_eq = jnp.where((run + pc) <= need, 0.0, NEG_BIAS)
            bias_ref[pl.ds(k0, LANE), :] = jnp.where(blk > thr, 0.0, jnp.where(eq, keep_eq, NEG_BIAS))
            return run + pc[LANE - 1:LANE, :]

        lax.fori_loop(0, nks * (tks // LANE), write_block, jnp.zeros((1, tq), F32))

    def fill_block(kb, carry):
        k0 = pl.multiple_of(kb * tks, tks)
        bias_ref[pl.ds(k0, tks), :] = jnp.full((tks, tq), NEG_BIAS, F32)
        return carry

    lax.fori_loop(nks, seq // tks, fill_block, 0)


def _indexer_bias_t(ki, qit, wit, top_k, tq=256, tks=256, tkc=1024):
    bsz, seq, _ = ki.shape
    tq = min(tq, seq)
    return pl.pallas_call(
        functools.partial(_indexer_t_kernel, tq=tq, tks=tks, tkc=tkc, top_k=top_k, seq=seq),
        grid=(bsz, seq // tq),
        in_specs=[pl.BlockSpec((None, seq, IDX_DIM), lambda b, i: (b, 0, 0)),
                  pl.BlockSpec((None, IDX_HEADS * IDX_DIM, tq), lambda b, i: (b, 0, i)),
                  pl.BlockSpec((None, IDX_HEADS, tq), lambda b, i: (b, 0, i))],
        out_specs=pl.BlockSpec((None, seq, tq), lambda b, i: (b, 0, i)),
        out_shape=jax.ShapeDtypeStruct((bsz, seq, seq), F32),
        scratch_shapes=[pltpu.VMEM((seq, tq), I32)],
        compiler_params=_cparams(("parallel", "arbitrary")),
        name="indexer_select",
    )(ki, qit, wit)


def _attn_t_kernel(qidx_ref, kidx_ref, qt_ref, k_ref, vt_ref, b_ref, o_ref, m_ref, acc_ref,
                   *, nheads, hd, hda, tq, tk):
    p = pl.program_id(1)
    qb = qidx_ref[p]
    kb = kidx_ref[p]
    kb_last = ((qb + 1) * tq - 1) // tk

    @pl.when(kb == 0)
    def _():
        m_ref[...] = jnp.full(m_ref.shape, M_INIT, F32)
        acc_ref[...] = jnp.zeros(acc_ref.shape, F32)

    bias = b_ref[...]
    hrows = [slice(h * hd, (h + 1) * hd) for h in range(nheads)]
    arows = [slice(h * hda, (h + 1) * hda) for h in range(nheads)]
    s = [jnp.dot(k_ref[:, r], qt_ref[r, :], preferred_element_type=F32) + bias for r in hrows]
    m_old = [m_ref[h] for h in range(nheads)]
    m_new = [jnp.maximum(m_old[h], jnp.max(s[h], axis=0, keepdims=True)) for h in range(nheads)]
    alpha = [jnp.exp2(m_old[h] - m_new[h]) for h in range(nheads)]
    pr = [jnp.exp2(s[h] - m_new[h]).astype(BF16) for h in range(nheads)]
    pv = [jnp.dot(vt_ref[arows[h], :], pr[h], preferred_element_type=F32) for h in range(nheads)]
    for h in range(nheads):
        m_ref[h] = m_new[h]
        acc_ref[arows[h], :] = alpha[h] * acc_ref[arows[h], :] + pv[h]

    @pl.when(kb == kb_last)
    def _():
        for h in range(nheads):
            a0 = h * hda
            out_t = acc_ref[a0:a0 + hd, :] / acc_ref[a0 + hd:a0 + hd + 1, :]
            o_ref[:, hrows[h]] = out_t.T.astype(o_ref.dtype)


def _masked_attention_t(qt, k, vt, bias_t, tq=512, tk=1024):
    bsz, width, seq = qt.shape
    nheads = width // A_HEAD_DIM
    hda = A_HEAD_DIM + ONES_ROWS
    tq = min(tq, seq)
    tk = min(tk, seq)
    nq = seq // tq
    pairs = [(i, j) for i in range(nq) for j in range(((i + 1) * tq - 1) // tk + 1)]
    qidx = jnp.asarray([pq for pq, _ in pairs], I32)
    kidx = jnp.asarray([pk for _, pk in pairs], I32)
    grid_spec = pltpu.PrefetchScalarGridSpec(
        num_scalar_prefetch=2,
        grid=(bsz, len(pairs)),
        in_specs=[pl.BlockSpec((None, width, tq), lambda b, p, qi, ki: (b, 0, qi[p])),
                  pl.BlockSpec((None, tk, width), lambda b, p, qi, ki: (b, ki[p], 0)),
                  pl.BlockSpec((None, nheads * hda, tk), lambda b, p, qi, ki: (b, 0, ki[p])),
                  pl.BlockSpec((None, tk, tq), lambda b, p, qi, ki: (b, ki[p], qi[p]))],
        out_specs=pl.BlockSpec((None, tq, width), lambda b, p, qi, ki: (b, qi[p], 0)),
        scratch_shapes=[pltpu.VMEM((nheads, 1, tq), F32),
                        pltpu.VMEM((nheads * hda, tq), F32)],
    )
    return pl.pallas_call(
        functools.partial(_attn_t_kernel, nheads=nheads, hd=A_HEAD_DIM, hda=hda, tq=tq, tk=tk),
        grid_spec=grid_spec,
        out_shape=jax.ShapeDtypeStruct((bsz, seq, width), BF16),
        compiler_params=_cparams(("parallel", "arbitrary")),
        name="masked_attention",
    )(qidx, kidx, qt, k, vt, bias_t)


def _retention_kernel(qk_ref, v_ref, g_ref, o_ref, state_ref, *, tc, dk, dv):
    @pl.when(pl.program_id(1) == 0)
    def _():
        state_ref[...] = jnp.zeros(state_ref.shape, F32)

    row = lax.broadcasted_iota(I32, (tc, tc), 0)
    col = lax.broadcasted_iota(I32, (tc, tc), 1)
    diff = (row - col).astype(F32)
    pos = lax.broadcasted_iota(I32, (tc, 1), 0).astype(F32)
    for h in range(R_HEADS):
        log_g = math.log1p(-(2.0 ** (-5.0 - h)))
        q = qk_ref[:, h * dk:(h + 1) * dk]
        k = qk_ref[:, (R_HEADS + h) * dk:(R_HEADS + h + 1) * dk]
        v = v_ref[:, h * dv:(h + 1) * dv]
        decay = jnp.where(diff >= 0, jnp.exp(jnp.maximum(diff, 0.0) * log_g), 0.0)
        s = lax.dot_general(q, k, (((1,), (1,)), ((), ())), preferred_element_type=F32) * decay
        intra = jnp.dot(s.astype(BF16), v, preferred_element_type=F32)
        xi = jnp.exp((pos + 1.0) * log_g)
        zeta = jnp.exp((tc - 1.0 - pos) * log_g)
        state = state_ref[h]
        cross = jnp.dot((q.astype(F32) * xi).astype(BF16), state.astype(BF16), preferred_element_type=F32)
        kz = (k.astype(F32) * zeta).astype(BF16)
        kv = lax.dot_general(kz, v, (((0,), (0,)), ((), ())), preferred_element_type=F32)
        state_ref[h] = state * math.exp(tc * log_g) + kv
        ret = intra + cross
        mu = jnp.mean(ret, axis=1, keepdims=True)
        d = ret - mu
        var = jnp.mean(d * d, axis=1, keepdims=True)
        gate = g_ref[:, h * dv:(h + 1) * dv]
        gate = gate / (1.0 + jnp.exp(-gate))
        o_ref[:, h * dv:(h + 1) * dv] = (d * lax.rsqrt(var + 1e-5) * gate).astype(o_ref.dtype)


def _retention(bqk, bv, bg, tc=256):
    bsz, seq, w2 = bqk.shape
    dk = w2 // (2 * R_HEADS)
    dv = bv.shape[2] // R_HEADS
    tc = min(tc, seq)
    return pl.pallas_call(
        functools.partial(_retention_kernel, tc=tc, dk=dk, dv=dv),
        grid=(bsz, seq // tc),
        in_specs=[pl.BlockSpec((None, tc, w2), lambda b, c: (b, c, 0)),
                  pl.BlockSpec((None, tc, R_HEADS * dv), lambda b, c: (b, c, 0)),
                  pl.BlockSpec((None, tc, R_HEADS * dv), lambda b, c: (b, c, 0))],
        out_specs=pl.BlockSpec((None, tc, R_HEADS * dv), lambda b, c: (b, c, 0)),
        out_shape=jax.ShapeDtypeStruct((bsz, seq, R_HEADS * dv), BF16),
        scratch_shapes=[pltpu.VMEM((R_HEADS, dk, dv), F32)],
        compiler_params=_cparams(("parallel", "arbitrary")),
        name="retention",
    )(bqk, bv, bg)


def _head_sum_matrix(width, hd):
    r = jnp.arange(width)
    return (r[:, None] // hd == r[None, :] // hd).astype(BF16)


def _sigmoid(x):
    return 1.0 / (1.0 + jnp.exp(-x))


def _rwkv_prep_kernel(*refs, cw, has_vres):
    if has_vres:
        (c_ref, mu_ref, vec_ref, wlb_ref, alb_ref, glb_ref, hs_ref, vlb_ref, vfirst_ref,
         r_o, lw_o, k_o, v_o, al_o, be_o, g_o, bo_o, carry_ref) = refs
    else:
        (c_ref, mu_ref, vec_ref, wlb_ref, alb_ref, glb_ref, hs_ref,
         r_o, lw_o, k_o, v_o, al_o, be_o, g_o, bo_o, carry_ref) = refs
    tr = c_ref.shape[0]

    @pl.when(pl.program_id(1) == 0)
    def _():
        carry_ref[...] = jnp.zeros(carry_ref.shape, F32)

    c = c_ref[...]
    row = lax.broadcasted_iota(I32, (tr, 1), 0)
    prev = jnp.where(row == 0, carry_ref[0:1, :], pltpu.roll(c, 1, axis=0))
    carry_ref[0:1, :] = c[tr - 1:tr, :]
    cs = c + (prev - c) * mu_ref[...]

    r = cs[:, 0:cw]
    k = cs[:, cw:2 * cw]
    v = cs[:, 2 * cw:3 * cw]
    o = 3 * cw
    wl = cs[:, o:o + LANE]
    al = cs[:, o + LANE:o + 2 * LANE]
    gl = cs[:, o + 2 * LANE:o + 2 * LANE + GATE_LORA]
    w0, a0, k_k, k_a, r_k, v0 = (vec_ref[i:i + 1, :] for i in range(6))

    z = -(w0 + jnp.dot(jnp.tanh(wl).astype(BF16), wlb_ref[...], preferred_element_type=F32))
    softplus = jnp.maximum(z, 0.0) + jnp.log(1.0 + jnp.exp(-jnp.abs(z)))
    lw = -jnp.exp(-softplus - 0.5)
    a = _sigmoid(a0 + jnp.dot(al.astype(BF16), alb_ref[...], preferred_element_type=F32))
    g = jnp.dot(_sigmoid(gl).astype(BF16), glb_ref[...], preferred_element_type=F32)
    if has_vres:
        vr = cs[:, o + 2 * LANE + GATE_LORA:o + 2 * LANE + GATE_LORA + vlb_ref.shape[0]]
        mix = _sigmoid(v0 + jnp.dot(vr.astype(BF16), vlb_ref[...], preferred_element_type=F32))
        v = v + (vfirst_ref[...] - v) * mix
    hs = hs_ref[...]
    kk = k * k_k
    ss = _dot_sel(kk * kk, hs)
    kk = kk / jnp.maximum(jnp.sqrt(ss), 1e-12)
    kh = k * (1.0 + (a - 1.0) * k_a)
    rk = _dot_sel(r * kh * r_k, hs)

    r_o[...] = r
    lw_o[...] = lw
    k_o[...] = kh
    v_o[...] = v
    al_o[...] = -kk
    be_o[...] = kk * a
    g_o[...] = g
    bo_o[...] = rk * v


def _rwkv_prep(cproj, mu, vecs, wlb, alb, glb, vlb, v_first, cw, tr=256):
    bsz, seq, wc = cproj.shape
    tr = min(tr, seq)
    has_vres = vlb is not None
    hs = _head_sum_matrix(cw, W_HEAD_DIM)
    full = lambda a: pl.BlockSpec(a.shape, lambda b, t: (0,) * a.ndim)
    tok = pl.BlockSpec((None, tr, cw), lambda b, t: (b, t, 0))
    ins = [cproj, mu, vecs, wlb, alb, glb, hs]
    in_specs = [pl.BlockSpec((None, tr, wc), lambda b, t: (b, t, 0)),
                full(mu), full(vecs), full(wlb), full(alb), full(glb), full(hs)]
    if has_vres:
        ins += [vlb, v_first]
        in_specs += [full(vlb), tok]
    out = jax.ShapeDtypeStruct((bsz, seq, cw), F32)
    return pl.pallas_call(
        functools.partial(_rwkv_prep_kernel, cw=cw, has_vres=has_vres),
        grid=(bsz, seq // tr),
        in_specs=in_specs,
        out_specs=[tok] * 8,
        out_shape=[out] * 8,
        scratch_shapes=[pltpu.VMEM((8, wc), F32)],
        compiler_params=_cparams(("parallel", "arbitrary")),
        name="rwkv_prep",
    )(*ins)


def _split_bf16(x):
    hi = x.astype(BF16)
    return hi, (x - hi.astype(F32)).astype(BF16)


def _dot_x3(a, b):
    ah, al = _split_bf16(a)
    bh, bl = _split_bf16(b)
    return (jnp.dot(ah, bh, preferred_element_type=F32) + jnp.dot(ah, bl, preferred_element_type=F32)
            + jnp.dot(al, bh, preferred_element_type=F32))


def _dot_sel(a, sel):
    ah, al = _split_bf16(a)
    return jnp.dot(ah, sel, preferred_element_type=F32) + jnp.dot(al, sel, preferred_element_type=F32)


def _bdot(a, b):
    return jnp.dot(a.astype(BF16), b.astype(BF16), preferred_element_type=F32)


def _bdot_tn(a, b):
    return lax.dot_general(a.astype(BF16), b.astype(BF16), (((0,), (0,)), ((), ())),
                           preferred_element_type=F32)


def _rwkv_chunk_kernel(r_ref, lw_ref, k_ref, v_ref, al_ref, be_ref, g_o, y0_o, m_o, z0_o, *, tc, nck, npairs):
    row = lax.broadcasted_iota(I32, (tc, tc), 0)
    col = lax.broadcasted_iota(I32, (tc, tc), 1)
    incl = row >= col
    strict = row > col
    tri = incl.astype(BF16)
    eye = (row == col).astype(F32)
    lane = lax.broadcasted_iota(I32, (1, LANE), 1)
    mh0 = (lane // W_HEAD_DIM) == 0
    prow = lax.broadcasted_iota(I32, (LANE, LANE), 0)
    pcol = lax.broadcasted_iota(I32, (LANE, LANE), 1)
    same_head = (prow // W_HEAD_DIM) == (pcol // W_HEAD_DIM)

    lw_all = lw_ref[...]
    hi = lw_all.astype(BF16)
    rem = lw_all - hi.astype(F32)
    mid = rem.astype(BF16)
    lo = (rem - mid.astype(F32)).astype(BF16)
    cum_all = jnp.concatenate(
        [sum(jnp.dot(tri, part[c * tc:(c + 1) * tc], preferred_element_type=F32) for part in (hi, mid, lo))
         for c in range(nck)], axis=0)

    pairs = range(nck * npairs)
    heads = [(p, h) for p in pairs for h in range(2)]
    sls = [slice((p % npairs) * LANE, (p % npairs + 1) * LANE) for p in pairs]
    rws = [slice((p // npairs) * tc, (p // npairs + 1) * tc) for p in pairs]
    cum = [cum_all[rws[p], sls[p]] for p in pairs]
    tot = [c[tc - 1:tc, :] for c in cum]
    p_inv = [jnp.exp(-cum[p]) for p in pairs]
    p_end = [jnp.exp(tot[p] - cum[p]) for p in pairs]
    at = [al_ref[rws[p], sls[p]] * jnp.exp(cum[p] - lw_all[rws[p], sls[p]]) for p in pairs]
    rt = [r_ref[rws[p], sls[p]] * jnp.exp(cum[p]) for p in pairs]
    bh = [be_ref[rws[p], sls[p]] * p_end[p] for p in pairs]
    khat = [k_ref[rws[p], sls[p]] * p_end[p] for p in pairs]
    v = [v_ref[rws[p], sls[p]] for p in pairs]
    rhs = [jnp.concatenate([be_ref[rws[p], sls[p]] * p_inv[p], k_ref[rws[p], sls[p]] * p_inv[p]],
                           axis=0).astype(BF16) for p in pairs]

    a_ab, a_ak, qcat = {}, {}, {}
    for p, h in heads:
        mh = (lane // W_HEAD_DIM) == h
        lhs = jnp.concatenate([jnp.where(mh, at[p], 0.0), jnp.where(mh, rt[p], 0.0)], axis=0).astype(BF16)
        x = lax.dot_general(lhs, rhs[p], (((1,), (1,)), ((), ())), preferred_element_type=F32)
        a_ab[p, h] = jnp.where(strict, x[:tc, :tc], 0.0)
        a_ak[p, h] = jnp.where(strict, x[:tc, tc:], 0.0)
        qcat[p, h] = jnp.concatenate([jnp.where(incl, x[tc:, tc:], 0.0),
                                      jnp.where(incl, x[tc:, :tc], 0.0)], axis=1)

    t_inv = {ph: eye + a_ab[ph] for ph in heads}
    pw = {ph: _bdot(a_ab[ph], a_ab[ph]) for ph in heads}
    akv = {(p, h): _bdot(a_ak[p, h], v[p]) for p, h in heads}
    n = 2
    while n < tc:
        for ph in heads:
            res = _bdot(jnp.concatenate([pw[ph], t_inv[ph]], axis=0), pw[ph])
            t_inv[ph] = t_inv[ph] + res[tc:]
            pw[ph] = res[:tc]
        n *= 2
    tw = {(p, h): _bdot(t_inv[p, h], jnp.concatenate([at[p], akv[p, h]], axis=1)) for p, h in heads}

    zero = jnp.zeros((tc, LANE), F32)
    w = [jnp.where(mh0, tw[p, 0][:, :LANE], tw[p, 1][:, :LANE]) for p in pairs]
    u0 = [jnp.where(mh0, tw[p, 0][:, LANE:], tw[p, 1][:, LANE:]) for p in pairs]
    vu = [jnp.concatenate([jnp.concatenate([v[p], zero], axis=1),
                           jnp.concatenate([u0[p], w[p]], axis=1)], axis=0) for p in pairs]
    yg = {(p, h): _bdot(qcat[p, h], vu[p]) for p, h in heads}
    m_mat = [_bdot_tn(bh[p], w[p]) for p in pairs]
    z0 = [_bdot_tn(jnp.concatenate([bh[p], khat[p]], axis=0), jnp.concatenate([u0[p], v[p]], axis=0))
          for p in pairs]
    for p in pairs:
        y0_o[rws[p], sls[p]] = jnp.where(mh0, yg[p, 0][:, :LANE], yg[p, 1][:, :LANE])
        g_o[rws[p], sls[p]] = rt[p] + jnp.where(mh0, yg[p, 0][:, LANE:], yg[p, 1][:, LANE:])
        m_o[p // npairs, p % npairs] = jnp.where(same_head, m_mat[p], 0.0) + jnp.where(
            prow == pcol, jnp.broadcast_to(jnp.exp(tot[p]), (LANE, LANE)), 0.0)
        z0_o[p // npairs, p % npairs] = jnp.where(same_head, z0[p], 0.0)


def _rwkv_chunk_ops(r, lw, kh, v, alpha, beta, tc=CHUNK, nck=4):
    bsz, seq, cw = r.shape
    npairs = cw // LANE
    nc = seq // tc
    tok = pl.BlockSpec((None, nck * tc, cw), lambda b, c: (b, c, 0))
    mat = pl.BlockSpec((None, nck, npairs, LANE, LANE), lambda b, c: (b, c, 0, 0, 0))
    tok_shape = jax.ShapeDtypeStruct((bsz, seq, cw), F32)
    mat_shape = jax.ShapeDtypeStruct((bsz, nc, npairs, LANE, LANE), F32)
    return pl.pallas_call(
        functools.partial(_rwkv_chunk_kernel, tc=tc, nck=nck, npairs=npairs),
        grid=(bsz, nc // nck),
        in_specs=[tok] * 6,
        out_specs=[tok, tok, mat, mat],
        out_shape=[tok_shape, tok_shape, mat_shape, mat_shape],
        compiler_params=_cparams(("parallel", "parallel")),
        name="rwkv_chunk_ops",
    )(r, lw, kh, v, alpha, beta)


def _rwkv_scan_kernel(g_ref, y0_ref, m_ref, z0_ref, bo_ref, gate_ref, ln_ref, hs_ref, o_ref, state_ref,
                      *, tc, nch, npairs):
    @pl.when(pl.program_id(1) == 0)
    def _():
        state_ref[...] = jnp.zeros(state_ref.shape, F32)

    hs = hs_ref[...]
    inv_n = 1.0 / W_HEAD_DIM
    sls = [slice(p * LANE, (p + 1) * LANE) for p in range(npairs)]
    st = [state_ref[p] for p in range(npairs)]
    for ch in range(nch):
        rows = slice(ch * tc, (ch + 1) * tc)
        gm = [_dot_x3(jnp.concatenate([g_ref[rows, sls[p]], m_ref[ch, p]], axis=0), st[p]) for p in range(npairs)]
        y = [gm[p][:tc] + y0_ref[rows, sls[p]] for p in range(npairs)]
        st = [gm[p][tc:] + z0_ref[ch, p] for p in range(npairs)]
        mu = [_dot_sel(y[p], hs) * inv_n for p in range(npairs)]
        d = [y[p] - mu[p] for p in range(npairs)]
        var = [_dot_sel(d[p] * d[p], hs) * inv_n for p in range(npairs)]
        for p in range(npairs):
            yn = d[p] * lax.rsqrt(var[p] + LNX_EPS) * ln_ref[0:1, sls[p]] + ln_ref[1:2, sls[p]]
            o_ref[rows, sls[p]] = ((yn + bo_ref[rows, sls[p]]) * gate_ref[rows, sls[p]]).astype(o_ref.dtype)
    for p in range(npairs):
        state_ref[p] = st[p]


def _rwkv_scan(g, y0, m, z0, bonus, gate, ln, tc=CHUNK, nch=4):
    bsz, seq, cw = g.shape
    npairs = cw // LANE
    nch = min(nch, seq // tc)
    hs = _head_sum_matrix(LANE, W_HEAD_DIM)
    tok = pl.BlockSpec((None, tc * nch, cw), lambda b, c: (b, c, 0))
    mat = pl.BlockSpec((None, nch, npairs, LANE, LANE), lambda b, c: (b, c, 0, 0, 0))
    return pl.pallas_call(
        functools.partial(_rwkv_scan_kernel, tc=tc, nch=nch, npairs=npairs),
        grid=(bsz, seq // (tc * nch)),
        in_specs=[tok, tok, mat, mat, tok, tok,
                  pl.BlockSpec(ln.shape, lambda b, c: (0, 0)),
                  pl.BlockSpec(hs.shape, lambda b, c: (0, 0))],
        out_specs=tok,
        out_shape=jax.ShapeDtypeStruct((bsz, seq, cw), BF16),
        scratch_shapes=[pltpu.VMEM((npairs, LANE, LANE), F32)],
        compiler_params=_cparams(("parallel", "arbitrary")),
        name="rwkv_scan",
    )(g, y0, m, z0, bonus, gate, ln, hs)


def _out_proj_kernel(x_ref, a_ref, b_ref, c_ref, w_ref, o_ref, *, wa, wb):
    acc = jnp.dot(a_ref[...], w_ref[0:wa, :], preferred_element_type=F32)
    acc = acc + jnp.dot(b_ref[...], w_ref[wa:wa + wb, :], preferred_element_type=F32)
    acc = acc + jnp.dot(c_ref[...], w_ref[wa + wb:, :], preferred_element_type=F32)
    o_ref[...] = x_ref[...] + acc


def _out_proj(x2d, oa, ob, oc, w, tm=1024, tn=1024):
    m, d = x2d.shape
    wa, wb, wc = oa.shape[1], ob.shape[1], oc.shape[1]
    tn = min(tn, d)
    return pl.pallas_call(
        functools.partial(_out_proj_kernel, wa=wa, wb=wb),
        grid=(m // tm, d // tn),
        in_specs=[pl.BlockSpec((tm, tn), lambda i, j: (i, j)),
                  pl.BlockSpec((tm, wa), lambda i, j: (i, 0)),
                  pl.BlockSpec((tm, wb), lambda i, j: (i, 0)),
                  pl.BlockSpec((tm, wc), lambda i, j: (i, 0)),
                  pl.BlockSpec((wa + wb + wc, tn), lambda i, j: (0, j))],
        out_specs=pl.BlockSpec((tm, tn), lambda i, j: (i, j)),
        out_shape=jax.ShapeDtypeStruct((m, d), F32),
        compiler_params=_cparams(("parallel", "arbitrary")),
        name="out_proj",
    )(x2d, oa, ob, oc, w)


def _gate_up_kernel(h_ref, wg_ref, wu_ref, o_ref):
    h = h_ref[...]
    gate = jnp.dot(h, wg_ref[...], preferred_element_type=F32)
    up = jnp.dot(h, wu_ref[...], preferred_element_type=F32)
    o_ref[...] = (gate / (1.0 + jnp.exp(-gate)) * up).astype(o_ref.dtype)


def _gate_up(h, w_gate_up, tm=1024, tf_cap=512):
    m, d = h.shape
    dff = w_gate_up.shape[1] // 2
    tf = _pick_tile(dff, tf_cap)
    nf = dff // tf
    return pl.pallas_call(
        _gate_up_kernel,
        grid=(m // tm, nf),
        in_specs=[pl.BlockSpec((tm, d), lambda i, j: (i, 0)),
                  pl.BlockSpec((d, tf), lambda i, j: (0, j)),
                  pl.BlockSpec((d, tf), lambda i, j: (0, j + nf))],
        out_specs=pl.BlockSpec((tm, tf), lambda i, j: (i, j)),
        out_shape=jax.ShapeDtypeStruct((m, dff), BF16),
        compiler_params=_cparams(("parallel", "arbitrary")),
        name="ffn_gate_up",
    )(h, w_gate_up, w_gate_up)


def _down_kernel(x_ref, a_ref, w_ref, o_ref):
    o_ref[...] = x_ref[...] + jnp.dot(a_ref[...], w_ref[...], preferred_element_type=F32)


def _down_proj(x2d, act, w, tm=1024, tn=512):
    m, d = x2d.shape
    dff = act.shape[1]
    tn = min(tn, d)
    return pl.pallas_call(
        _down_kernel,
        grid=(m // tm, d // tn),
        in_specs=[pl.BlockSpec((tm, tn), lambda i, j: (i, j)),
                  pl.BlockSpec((tm, dff), lambda i, j: (i, 0)),
                  pl.BlockSpec((dff, tn), lambda i, j: (0, j))],
        out_specs=pl.BlockSpec((tm, tn), lambda i, j: (i, j)),
        out_shape=jax.ShapeDtypeStruct((m, d), F32),
        compiler_params=_cparams(("parallel", "arbitrary")),
        name="ffn_down",
    )(x2d, act, w)


def _pad_cols(w, width):
    return jnp.pad(w, ((0, 0), (0, width - w.shape[1])))


def _pad_rows(w, height):
    return jnp.pad(w, ((0, height - w.shape[0]), (0, 0)))


def _pad_vec(v, width):
    return jnp.pad(v, (0, width - v.shape[0]))


def kernel(x, norm_mix_g, w_in, w_in_vres, rwkv_mu, rwkv_mu_vres, rwkv_w0, rwkv_w_lora_b, rwkv_a0,
           rwkv_a_lora_b, rwkv_v0, rwkv_v_lora_b, rwkv_g_lora_b, rwkv_k_k, rwkv_k_a, rwkv_r_k,
           rwkv_lnx_g, rwkv_lnx_b, w_out, norm_ffn_g, w_gate_up, w_down, final_norm_g):
    bsz, seq, d_model = x.shape
    depth = w_in.shape[0]
    m = bsz * seq
    d_mix = w_out.shape[1]
    a_w = d_mix // 2
    qi_w = IDX_HEADS * IDX_DIM
    b_v_w = d_mix // 4
    b_qk_w = b_v_w // 2
    c_w = d_mix // 4
    r_qk_dim = b_qk_w // R_HEADS
    top_k = min(TOPK_MAX, seq // 4)
    vres_pad = 2 * LANE

    o_q, o_k, o_v = 0, a_w, 2 * a_w
    o_qi = 3 * a_w
    o_ki = o_qi + qi_w
    o_wi = o_ki + IDX_DIM
    o_bq = o_wi + IDX_HEADS
    o_bk = o_bq + b_qk_w
    o_bv = o_bk + b_qk_w
    o_bg = o_bv + b_v_w
    o_c = o_bg + b_v_w
    o_wl = o_c + 3 * c_w
    o_al = o_wl + DECAY_LORA
    o_gl = o_al + AAA_LORA
    n_in = o_gl + GATE_LORA

    a_rot = A_HEAD_DIM // ROPE_FRAC
    tq_ = _rope_tables(seq, A_HEAD_DIM, a_rot, ROPE_THETA, scale=A_HEAD_DIM ** -0.5 * math.log2(math.e))
    tk_ = _rope_tables(seq, A_HEAD_DIM, a_rot, ROPE_THETA)
    tab_q = tuple(t[None] for t in tq_)
    tab_k = tuple(t[None] for t in tk_)
    i_rot = IDX_DIM // ROPE_FRAC
    tab_qi = tuple(t[None] for t in _rope_tables(seq, IDX_DIM, i_rot, ROPE_THETA))
    lane = jnp.arange(LANE)
    kiwi_pass = jnp.where(lane < IDX_DIM, 1.0, IDX_W_SCALE).astype(F32)
    c_kw, s1_kw, s2_kw = _rope_tables(seq, LANE, i_rot, ROPE_THETA, pass_scale=kiwi_pass)
    tab_kiwi = (c_kw[None], s1_kw[None], s2_kw[None])
    tbq = _rope_tables(seq, r_qk_dim, r_qk_dim, R_THETA)
    tbk = _rope_tables(seq, r_qk_dim, r_qk_dim, R_THETA, scale=r_qk_dim ** -0.5)
    tab_bqk = tuple(jnp.stack([a, b]) for a, b in zip(tbq, tbk))

    x2d = x.reshape(m, d_model)
    v_first = None
    for l in range(depth):
        wl_ = w_in[l]
        cols = lambda o, n: wl_[:, o:o + n]
        w_q = cols(o_q, a_w).astype(BF16)
        w_k = cols(o_k, a_w).astype(BF16)
        w_v = cols(o_v, a_w).astype(BF16)
        w_bv = cols(o_bv, b_v_w).astype(BF16)
        w_qi = cols(o_qi, qi_w).astype(BF16)
        w_kiwi = _pad_cols(cols(o_ki, IDX_DIM + IDX_HEADS), LANE).astype(BF16)
        w_bqk = cols(o_bq, 2 * b_qk_w).astype(BF16)
        w_bg = cols(o_bg, b_v_w).astype(BF16)
        vres_w = (w_in_vres[l - 1] if l > 0 else jnp.zeros((d_model, MV_LORA), F32))
        w_c = jnp.concatenate([cols(o_c, 3 * c_w), _pad_cols(cols(o_wl, DECAY_LORA), LANE),
                               _pad_cols(cols(o_al, AAA_LORA), LANE), cols(o_gl, GATE_LORA),
                               _pad_cols(vres_w, vres_pad)], axis=1).astype(BF16)
        mu = rwkv_mu[l]
        mu_vres = rwkv_mu_vres[l - 1] if l > 0 else jnp.zeros((MV_LORA,), F32)
        mu_c = jnp.concatenate([mu[:3 * c_w], _pad_vec(mu[3 * c_w:3 * c_w + DECAY_LORA], LANE),
                                _pad_vec(mu[3 * c_w + DECAY_LORA:3 * c_w + DECAY_LORA + AAA_LORA], LANE),
                                mu[3 * c_w + DECAY_LORA + AAA_LORA:], _pad_vec(mu_vres, vres_pad)])[None, :]

        h = _rmsnorm(x2d, norm_mix_g[l], BF16)
        qt = _mm_rope(h, w_q, tab_q, a_rot // 2, seq, a_w, BF16, transpose=True)
        ak = _mm_rope(h, w_k, tab_k, a_rot // 2, seq, a_w, BF16).reshape(bsz, seq, a_w)
        vt = _mm_plain(h, w_v, BF16, seq, transpose=True, pad_rows=ONES_ROWS)
        bv = _mm_plain(h, w_bv, BF16, seq, tn_cap=512).reshape(bsz, seq, b_v_w)
        qit = _mm_rope(h, w_qi, tab_qi, i_rot // 2, seq, qi_w, BF16, transpose=True)
        kiwi = _mm_rope(h, w_kiwi, tab_kiwi, i_rot // 2, seq, LANE, F32).reshape(bsz, seq, LANE)
        bqk = _mm_rope(h, w_bqk, tab_bqk, r_qk_dim // 2, seq, b_qk_w, BF16).reshape(bsz, seq, 2 * b_qk_w)
        bg = _mm_plain(h, w_bg, F32, seq).reshape(bsz, seq, b_v_w)
        cproj = _mm_plain(h, w_c, F32, seq, tn_cap=768).reshape(bsz, seq, w_c.shape[1])

        ki = kiwi[:, :, :IDX_DIM].astype(BF16)
        wit = jnp.swapaxes(kiwi[:, :, IDX_DIM:IDX_DIM + IDX_HEADS], 1, 2)
        bias_t = _indexer_bias_t(ki, qit, wit, top_k)
        out_a = _masked_attention_t(qt, ak, vt, bias_t)

        out_b = _retention(bqk, bv, bg)

        vecs = jnp.stack([rwkv_w0[l], rwkv_a0[l], rwkv_k_k[l], rwkv_k_a[l], rwkv_r_k[l].reshape(-1),
                          rwkv_v0[l - 1] if l > 0 else jnp.zeros((c_w,), F32),
                          jnp.zeros((c_w,), F32), jnp.zeros((c_w,), F32)])
        wlb = _pad_rows(rwkv_w_lora_b[l], LANE).astype(BF16)
        alb = _pad_rows(rwkv_a_lora_b[l], LANE).astype(BF16)
        glb = rwkv_g_lora_b[l].astype(BF16)
        vlb = _pad_rows(rwkv_v_lora_b[l - 1], vres_pad).astype(BF16) if l > 0 else None
        r_, lw_, kh_, v_, al_, be_, g_, bo_ = _rwkv_prep(cproj, mu_c, vecs, wlb, alb, glb, vlb, v_first, c_w)
        if l == 0:
            v_first = v_
        gm, y0, mm, z0 = _rwkv_chunk_ops(r_, lw_, kh_, v_, al_, be_)
        ln = jnp.stack([rwkv_lnx_g[l], rwkv_lnx_b[l]] + [jnp.zeros((c_w,), F32)] * 6)
        out_c = _rwkv_scan(gm, y0, mm, z0, bo_, g_, ln)

        x2d = _out_proj(x2d, out_a.reshape(m, a_w), out_b.reshape(m, b_v_w), out_c.reshape(m, c_w),
                        w_out[l].astype(BF16))

        h = _rmsnorm(x2d, norm_ffn_g[l], BF16)
        act = _gate_up(h, w_gate_up[l].astype(BF16))
        x2d = _down_proj(x2d, act, w_down[l].astype(BF16))

    return _rmsnorm(x2d, final_norm_g, F32).reshape(bsz, seq, d_model)
```

```python
import functools
import math

import jax
import jax.numpy as jnp
from jax import lax
from jax.experimental import pallas as pl
from jax.experimental.pallas import tpu as pltpu

F32 = jnp.float32
BF16 = jnp.bfloat16
I32 = jnp.int32
HIGHEST = lax.Precision.HIGHEST

CHUNK = 64
A_HEAD_DIM = 128
IDX_HEADS = 16
IDX_DIM = 64
IDX_W_SCALE = (IDX_HEADS * IDX_DIM) ** -0.5
TOPK_MAX = 256
ROPE_THETA = 500000.0
ROPE_FRAC = 4
R_HEADS = 4
R_THETA = 10000.0
W_HEAD_DIM = 64
DECAY_LORA = 96
AAA_LORA = 96
MV_LORA = 64
GATE_LORA = 256
LNX_EPS = 64e-5
RMS_EPS = 1e-5

LANE = 128
VMEM_LIMIT = 56 * 1024 * 1024
NEG_BIAS = -1e30
M_INIT = -1e20
INT_MIN = -2147483648
COUNT_CHAINS = 8
SCORE_GROUP = 4
ONES_ROWS = 8


def _cparams(sem):
    return pltpu.CompilerParams(dimension_semantics=sem, vmem_limit_bytes=VMEM_LIMIT)


def _pick_tile(n, cap):
    best = LANE
    t = LANE
    while t <= min(n, cap):
        if n % t == 0:
            best = t
        t += LANE
    return best


def _rmsnorm_kernel(x_ref, g_ref, o_ref):
    x = x_ref[...]
    ms = jnp.mean(x * x, axis=-1, keepdims=True)
    o_ref[...] = (x * lax.rsqrt(ms + RMS_EPS) * g_ref[...]).astype(o_ref.dtype)


def _rmsnorm(x2d, g, out_dtype, tm=512):
    m, d = x2d.shape
    return pl.pallas_call(
        _rmsnorm_kernel,
        grid=(m // tm,),
        in_specs=[pl.BlockSpec((tm, d), lambda i: (i, 0)),
                  pl.BlockSpec((1, d), lambda i: (0, 0))],
        out_specs=pl.BlockSpec((tm, d), lambda i: (i, 0)),
        out_shape=jax.ShapeDtypeStruct((m, d), out_dtype),
        compiler_params=_cparams(("parallel",)),
        name="rmsnorm",
    )(x2d, g.reshape(1, d).astype(F32))


def _store_cols(o_ref, blk, x, transpose, pad_rows):
    if transpose:
        r0 = blk * (LANE + pad_rows)
        o_ref[r0:r0 + LANE, :] = x.T.astype(o_ref.dtype)
        if pad_rows:
            o_ref[r0 + LANE:r0 + LANE + pad_rows, :] = jnp.ones((pad_rows, x.shape[0]), o_ref.dtype)
    else:
        o_ref[:, blk * LANE:(blk + 1) * LANE] = x.astype(o_ref.dtype)


def _proj_out_spec(m, n, tm, tn, seq, out_dtype, transpose, pad_rows):
    if not transpose:
        return pl.BlockSpec((tm, tn), lambda i, j: (i, j)), jax.ShapeDtypeStruct((m, n), out_dtype)
    tpb = seq // tm
    rows = lambda cols: cols // LANE * (LANE + pad_rows)
    return (pl.BlockSpec((None, rows(tn), tm), lambda i, j: (i // tpb, j, i % tpb)),
            jax.ShapeDtypeStruct((m // seq, rows(n), seq), out_dtype))


def _mm_plain_kernel(a_ref, w_ref, o_ref, *, transpose, pad_rows):
    acc = jnp.dot(a_ref[...], w_ref[...], preferred_element_type=F32)
    if not transpose:
        o_ref[...] = acc.astype(o_ref.dtype)
        return
    for blk in range(acc.shape[1] // LANE):
        _store_cols(o_ref, blk, acc[:, blk * LANE:(blk + 1) * LANE], transpose, pad_rows)


def _mm_plain(a, w, out_dtype, seq, tm=1024, tn_cap=1024, transpose=False, pad_rows=0):
    m, k = a.shape
    n = w.shape[1]
    tn = _pick_tile(n, tn_cap)
    out_spec, out_shape = _proj_out_spec(m, n, tm, tn, seq, out_dtype, transpose, pad_rows)
    return pl.pallas_call(
        functools.partial(_mm_plain_kernel, transpose=transpose, pad_rows=pad_rows),
        grid=(m // tm, n // tn),
        in_specs=[pl.BlockSpec((tm, k), lambda i, j: (i, 0)),
                  pl.BlockSpec((k, tn), lambda i, j: (0, j))],
        out_specs=out_spec,
        out_shape=out_shape,
        compiler_params=_cparams(("parallel", "arbitrary")),
        name="proj_plain",
    )(a, w)


def _mm_rope_kernel(a_ref, w_ref, c_ref, s1_ref, s2_ref, o_ref, *, half, transpose):
    acc = jnp.dot(a_ref[...], w_ref[...], preferred_element_type=F32)
    c, s1, s2 = c_ref[...], s1_ref[...], s2_ref[...]
    for blk in range(acc.shape[1] // LANE):
        x = acc[:, blk * LANE:(blk + 1) * LANE]
        up = pltpu.roll(x, LANE - half, axis=1)
        dn = pltpu.roll(x, half, axis=1)
        _store_cols(o_ref, blk, x * c + up * s1 + dn * s2, transpose, 0)


def _mm_rope(a, w, tables, half, seq, tn, out_dtype, tm=1024, transpose=False):
    m, k = a.shape
    n = w.shape[1]
    tpb = seq // tm
    tab_spec = pl.BlockSpec((None, tm, LANE), lambda i, j: (j, i % tpb, 0))
    out_spec, out_shape = _proj_out_spec(m, n, tm, tn, seq, out_dtype, transpose, 0)
    return pl.pallas_call(
        functools.partial(_mm_rope_kernel, half=half, transpose=transpose),
        grid=(m // tm, n // tn),
        in_specs=[pl.BlockSpec((tm, k), lambda i, j: (i, 0)),
                  pl.BlockSpec((k, tn), lambda i, j: (0, j)),
                  tab_spec, tab_spec, tab_spec],
        out_specs=out_spec,
        out_shape=out_shape,
        compiler_params=_cparams(("parallel", "arbitrary")),
        name="proj_rope",
    )(a, w, *tables)


def _rope_tables(seq, group, rot_dim, theta, scale=1.0, pass_scale=None):
    half = rot_dim // 2
    freqs = jnp.power(F32(theta), -jnp.arange(half, dtype=F32) / half)
    ang = jnp.arange(seq, dtype=F32)[:, None] * freqs[None, :]
    cos, sin = jnp.cos(ang), jnp.sin(ang)
    lane = jnp.arange(LANE) % group
    idx = lane % half
    cosl, sinl = cos[:, idx], sin[:, idx]
    passv = jnp.ones((LANE,), F32) if pass_scale is None else pass_scale
    c = jnp.where(lane < rot_dim, cosl, passv[None, :])
    s1 = jnp.where(lane < half, -sinl, 0.0)
    s2 = jnp.where((lane >= half) & (lane < rot_dim), sinl, 0.0)
    return c * scale, s1 * scale, s2 * scale


def _indexer_kernel(qi_ref, kit_ref, wi_ref, bias_ref, key_ref, *, tq, ts, tki, tkc, top_k, seq):
    qb = pl.program_id(1)
    nkb = ((qb + 1) * tq + tki - 1) // tki
    q_chunk = (qb * tq + lax.broadcasted_iota(I32, (tq, 1), 0)) // CHUNK
    w = wi_ref[...]

    def score_block(kb, rmax):
        k0 = pl.multiple_of(kb * tki, tki)
        kblk = kit_ref[:, pl.ds(k0, tki)]
        k_chunk = (k0 + lax.broadcasted_iota(I32, (1, tki), 1)) // CHUNK
        new_max = []
        for rh in range(tq // ts):
            rows = slice(rh * ts, (rh + 1) * ts)
            acc = jnp.zeros((ts, tki), F32)
            for h in range(IDX_HEADS):
                x = jnp.dot(qi_ref[rows, h * IDX_DIM:(h + 1) * IDX_DIM], kblk, preferred_element_type=F32)
                acc = acc + jnp.maximum(x, 0.0) * w[rows, h:h + 1]
            bits = lax.bitcast_convert_type(acc, I32)
            key = bits ^ ((bits >> 31) & jnp.int32(0x7FFFFFFF))
            key = jnp.where(k_chunk <= q_chunk[rows], key, jnp.int32(INT_MIN))
            key_ref[rows, pl.ds(k0, tki)] = key
            r = rmax[rows]
            for c in range(tki // LANE):
                r = jnp.maximum(r, key[:, c * LANE:(c + 1) * LANE])
            new_max.append(r)
        return jnp.concatenate(new_max, axis=0)

    rmax = lax.fori_loop(0, nkb, score_block, jnp.full((tq, LANE), INT_MIN, I32))
    rowmax = jnp.max(rmax, axis=1, keepdims=True)

    nkc = (nkb * tki + tkc - 1) // tkc

    def pad_block(kb, carry):
        k0 = pl.multiple_of(kb * tki, tki)
        key_ref[:, pl.ds(k0, tki)] = jnp.full((tq, tki), INT_MIN, I32)
        return carry

    lax.fori_loop(nkb, nkc * (tkc // tki), pad_block, 0)

    def count_ge(cand):
        totals = []
        for rh in range(tq // ts):
            rows = slice(rh * ts, (rh + 1) * ts)
            candb = jnp.broadcast_to(cand[rows], (ts, LANE))

            def count_block(kb, cnt, rows=rows, candb=candb):
                k0 = pl.multiple_of(kb * tkc, tkc)
                for c in range(tkc // LANE):
                    blk = key_ref[rows, pl.ds(k0 + c * LANE, LANE)]
                    cnt = cnt + jnp.where(blk >= candb, 1, 0)
                return cnt

            cnt = lax.fori_loop(0, nkc, count_block, jnp.zeros((ts, LANE), I32))
            totals.append(jnp.sum(cnt, axis=1, keepdims=True))
        return jnp.concatenate(totals, axis=0)

    probe = jnp.where(rowmax > jnp.int32(INT_MIN + (1 << 25)), rowmax - jnp.int32(3 << 23),
                      jnp.int32(INT_MIN + 1))

    def bisect_cond(state):
        it, n_active = state[0], state[1]
        return (n_active > 0) & (it < 40)

    def bisect_body(state):
        it, _, lo, hi, cnt_lo, cnt_hi = state
        mid = (lo >> 1) + (hi >> 1) + (lo & hi & 1)
        mid = jnp.where(it == 0, probe, mid)
        cnt = count_ge(mid)
        ge = cnt >= top_k
        exact = cnt == top_k
        new_lo = jnp.where(ge, mid, lo)
        new_hi = jnp.where(exact, mid + 1, jnp.where(ge, hi, mid))
        cnt_lo = jnp.where(ge, cnt, cnt_lo)
        cnt_hi = jnp.where(ge, cnt_hi, cnt)
        active = (new_lo + 1) < new_hi
        n_active = jnp.max(active.astype(I32))
        return it + 1, n_active, new_lo, new_hi, cnt_lo, cnt_hi

    big = jnp.int32(1 << 30)
    init = (jnp.int32(0), jnp.int32(1), jnp.full((tq, 1), INT_MIN, I32), rowmax + 1,
            jnp.full((tq, 1), big, I32), jnp.zeros((tq, 1), I32))
    _, _, lo, _, cnt_lo, cnt_hi = lax.while_loop(bisect_cond, bisect_body, init)

    thr = jnp.maximum(lo, jnp.int32(INT_MIN + 1))
    thrb = jnp.broadcast_to(thr, (tq, LANE))
    tie_row = (cnt_lo > top_k) & (lo > jnp.int32(INT_MIN))
    any_tie = jnp.max(tie_row.astype(I32))

    @pl.when(any_tie == 0)
    def _():
        def write_block(kb, carry):
            k0 = pl.multiple_of(kb * tki, tki)
            for c in range(tki // LANE):
                blk = key_ref[:, pl.ds(k0 + c * LANE, LANE)]
                bias_ref[:, pl.ds(k0 + c * LANE, LANE)] = jnp.where(blk >= thrb, 0.0, NEG_BIAS)
            return carry

        lax.fori_loop(0, nkb, write_block, 0)

    @pl.when(any_tie != 0)
    def _():
        need = jnp.where(tie_row, top_k - cnt_hi, big).astype(F32)
        ur = lax.broadcasted_iota(I32, (LANE, LANE), 0)
        uc = lax.broadcasted_iota(I32, (LANE, LANE), 1)
        upper = (ur <= uc).astype(BF16)

        def write_block(kb, run):
            k0 = pl.multiple_of(kb * tki, tki)
            for c in range(tki // LANE):
                blk = key_ref[:, pl.ds(k0 + c * LANE, LANE)]
                eq = blk == thrb
                pc = jnp.dot(jnp.where(eq, 1.0, 0.0).astype(BF16), upper, preferred_element_type=F32)
                keep_eq = jnp.where((run + pc) <= need, 0.0, NEG_BIAS)
                bias_ref[:, pl.ds(k0 + c * LANE, LANE)] = jnp.where(
                    blk > thrb, 0.0, jnp.where(eq, keep_eq, NEG_BIAS))
                run = run + pc[:, LANE - 1:LANE]
            return run

        lax.fori_loop(0, nkb, write_block, jnp.zeros((tq, 1), F32))

    def fill_block(kb, carry):
        k0 = pl.multiple_of(kb * tki, tki)
        bias_ref[:, pl.ds(k0, tki)] = jnp.full((tq, tki), NEG_BIAS, F32)
        return carry

    lax.fori_loop(nkb, seq // tki, fill_block, 0)


def _indexer_bias(qi, kit, wi, top_k, tq=256, ts=128, tki=512):
    bsz, seq, _ = qi.shape
    tki = min(tki, seq)
    tkc = 2 * tki if (seq // tki) % 2 == 0 else tki
    return pl.pallas_call(
        functools.partial(_indexer_kernel, tq=tq, ts=ts, tki=tki, tkc=tkc, top_k=top_k, seq=seq),
        grid=(bsz, seq // tq),
        in_specs=[pl.BlockSpec((None, tq, IDX_HEADS * IDX_DIM), lambda b, i: (b, i, 0)),
                  pl.BlockSpec((None, IDX_DIM, seq), lambda b, i: (b, 0, 0)),
                  pl.BlockSpec((None, tq, IDX_HEADS), lambda b, i: (b, i, 0))],
        out_specs=pl.BlockSpec((None, tq, seq), lambda b, i: (b, i, 0)),
        out_shape=jax.ShapeDtypeStruct((bsz, seq, seq), F32),
        scratch_shapes=[pltpu.VMEM((tq, seq), I32)],
        compiler_params=_cparams(("parallel", "arbitrary")),
        name="indexer_select",
    )(qi, kit, wi)


def _attn_kernel(qidx_ref, kidx_ref, q_ref, kt_ref, v_ref, b_ref, o_ref, m_ref, l_ref, acc_ref,
                 *, nheads, hd, tq, tk):
    p = pl.program_id(1)
    qb = qidx_ref[p]
    kb = kidx_ref[p]
    kb_last = ((qb + 1) * tq - 1) // tk

    @pl.when(kb == 0)
    def _():
        m_ref[...] = jnp.full(m_ref.shape, M_INIT, F32)
        l_ref[...] = jnp.zeros(l_ref.shape, F32)
        acc_ref[...] = jnp.zeros(acc_ref.shape, F32)

    bias = b_ref[...]

    def logits(h):
        sl = slice(h * hd, (h + 1) * hd)
        return jnp.dot(q_ref[:, sl], kt_ref[sl, :], preferred_element_type=F32)

    s_next = logits(0)
    for h in range(nheads):
        sl = slice(h * hd, (h + 1) * hd)
        s = s_next + bias
        if h + 1 < nheads:
            s_next = logits(h + 1)
        m_old = m_ref[h]
        m_new = jnp.maximum(m_old, jnp.max(s, axis=1, keepdims=True))
        alpha = jnp.exp(m_old - m_new)
        pr = jnp.exp(s - pltpu.repeat(m_new, tk // LANE, axis=1))
        l_ref[h] = alpha * l_ref[h] + jnp.sum(pr, axis=1, keepdims=True)
        m_ref[h] = m_new
        acc_ref[:, sl] = alpha * acc_ref[:, sl] + jnp.dot(pr.astype(BF16), v_ref[:, sl],
                                                          preferred_element_type=F32)

    @pl.when(kb == kb_last)
    def _():
        for h in range(nheads):
            sl = slice(h * hd, (h + 1) * hd)
            o_ref[:, sl] = (acc_ref[:, sl] / l_ref[h]).astype(o_ref.dtype)


def _masked_attention(q, kt, v, bias, tq=256, tk=512):
    bsz, seq, width = q.shape
    nheads = width // A_HEAD_DIM
    nq = seq // tq
    tk = min(tk, seq)
    pairs = [(i, j) for i in range(nq) for j in range(((i + 1) * tq - 1) // tk + 1)]
    qidx = jnp.asarray([pq for pq, _ in pairs], I32)
    kidx = jnp.asarray([pk for _, pk in pairs], I32)
    grid_spec = pltpu.PrefetchScalarGridSpec(
        num_scalar_prefetch=2,
        grid=(bsz, len(pairs)),
        in_specs=[pl.BlockSpec((None, tq, width), lambda b, p, qi, ki: (b, qi[p], 0)),
                  pl.BlockSpec((None, width, tk), lambda b, p, qi, ki: (b, 0, ki[p])),
                  pl.BlockSpec((None, tk, width), lambda b, p, qi, ki: (b, ki[p], 0)),
                  pl.BlockSpec((None, tq, tk), lambda b, p, qi, ki: (b, qi[p], ki[p]))],
        out_specs=pl.BlockSpec((None, tq, width), lambda b, p, qi, ki: (b, qi[p], 0)),
        scratch_shapes=[pltpu.VMEM((nheads, tq, LANE), F32),
                        pltpu.VMEM((nheads, tq, LANE), F32),
                        pltpu.VMEM((tq, width), F32)],
    )
    return pl.pallas_call(
        functools.partial(_attn_kernel, nheads=nheads, hd=A_HEAD_DIM, tq=tq, tk=tk),
        grid_spec=grid_spec,
        out_shape=jax.ShapeDtypeStruct((bsz, seq, width), BF16),
        compiler_params=_cparams(("parallel", "arbitrary")),
        name="masked_attention",
    )(qidx, kidx, q, kt, v, bias)


def _indexer_t_kernel(ki_ref, qit_ref, wit_ref, bias_ref, key_ref, k16_ref, *, tq, tks, tkc, top_k, seq):
    qb = pl.program_id(1)
    nks = ((qb + 1) * tq + tks - 1) // tks
    nkc = (nks * tks + tkc - 1) // tkc
    q_chunk = (qb * tq + lax.broadcasted_iota(I32, (1, tq), 1)) // CHUNK
    w = wit_ref[...]

    def score_block(kb, rmax):
        k0 = pl.multiple_of(kb * tks, tks)
        kblk = ki_ref[pl.ds(k0, tks), :]
        acc = jnp.zeros((tks, tq), F32)
        for h0 in range(0, IDX_HEADS, SCORE_GROUP):
            xs = [jnp.dot(kblk, qit_ref[h * IDX_DIM:(h + 1) * IDX_DIM, :], preferred_element_type=F32)
                  for h in range(h0, h0 + SCORE_GROUP)]
            for i, x in enumerate(xs):
                acc = acc + jnp.maximum(x, 0.0) * w[h0 + i:h0 + i + 1, :]
        k_chunk = (k0 + lax.broadcasted_iota(I32, (tks, 1), 0)) // CHUNK
        bits = lax.bitcast_convert_type(acc, I32)
        key = bits ^ ((bits >> 31) & jnp.int32(0x7FFFFFFF))
        key = jnp.where(k_chunk <= q_chunk, key, jnp.int32(INT_MIN))
        key_ref[pl.ds(k0, tks), :] = key
        k16_ref[pl.ds(k0, tks), :] = (key >> 16).astype(jnp.int16)
        return jnp.maximum(rmax, jnp.max(key.reshape(tks // 8, 8, tq), axis=0))

    rmax = lax.fori_loop(0, nks, score_block, jnp.full((8, tq), INT_MIN, I32))
    colmax = jnp.max(rmax, axis=0, keepdims=True)

    def pad_block(kb, carry):
        k0 = pl.multiple_of(kb * tks, tks)
        key_ref[pl.ds(k0, tks), :] = jnp.full((tks, tq), INT_MIN, I32)
        k16_ref[pl.ds(k0, tks), :] = jnp.full((tks, tq), -32768, jnp.int16)
        return carry

    lax.fori_loop(nks, nkc * (tkc // tks), pad_block, 0)

    def count_ge(cand):
        def count_block(kb, cnt):
            k0 = pl.multiple_of(kb * tkc, tkc)
            part = tkc // COUNT_CHAINS
            sums = []
            for g in range(COUNT_CHAINS):
                ind = jnp.where(key_ref[pl.ds(k0 + g * part, part), :] >= cand, 1, 0)
                sums.append(jnp.sum(ind.reshape(part // 8, 8, tq), axis=0))
            while len(sums) > 1:
                sums = [a + b for a, b in zip(sums[0::2], sums[1::2])]
            return cnt + sums[0]

        cnt = lax.fori_loop(0, nkc, count_block, jnp.zeros((8, tq), I32))
        return jnp.sum(cnt, axis=0, keepdims=True)

    def count_ge16(cand):
        c16 = cand.astype(jnp.int16)
        one, zero = jnp.int16(1), jnp.int16(0)

        def count_block(kb, cnt):
            k0 = pl.multiple_of(kb * tkc, tkc)
            part = tkc // COUNT_CHAINS
            sums = []
            for g in range(COUNT_CHAINS):
                ind = jnp.where(k16_ref[pl.ds(k0 + g * part, part), :] >= c16, one, zero)
                acc = ind[0:16]
                for i in range(1, part // 16):
                    acc = acc + ind[i * 16:(i + 1) * 16]
                sums.append(acc)
            while len(sums) > 1:
                sums = [a + b for a, b in zip(sums[0::2], sums[1::2])]
            return cnt + sums[0]

        cnt = lax.fori_loop(0, nkc, count_block, jnp.zeros((16, tq), jnp.int16))
        return jnp.sum(cnt.astype(I32), axis=0, keepdims=True)

    def probe_step(count_fn, state, mid):
        lo, hi, cnt_lo, cnt_hi = state
        cnt = count_fn(mid)
        ge = cnt >= top_k
        new_hi = jnp.where(cnt == top_k, mid + 1, jnp.where(ge, hi, mid))
        return (jnp.where(ge, mid, lo), new_hi, jnp.where(ge, cnt, cnt_lo), jnp.where(ge, cnt_hi, cnt))

    def midpoint(state):
        lo, hi = state[0], state[1]
        return (lo >> 1) + (hi >> 1) + (lo & hi & 1)

    def n_active(state):
        return jnp.max(((state[0] + 1) < state[1]).astype(I32))

    def bisect(count_fn, state):
        def body(carry):
            it, _, st = carry
            st = probe_step(count_fn, st, midpoint(st))
            st = probe_step(count_fn, st, midpoint(st))
            return it + 2, n_active(st), st

        return lax.while_loop(lambda c: (c[1] > 0) & (c[0] < 40), body, (jnp.int32(0), n_active(state), state))[2]

    big = jnp.int32(1 << 30)
    top16 = colmax >> 16
    state = (jnp.full((1, tq), -32768, I32), top16 + 1, jnp.full((1, tq), big, I32), jnp.zeros((1, tq), I32))
    first = jnp.where(top16 > jnp.int32(-32768 + 512), top16 - jnp.int32(3 << 7), jnp.int32(-32767))
    state = bisect(count_ge16, probe_step(count_ge16, state, first))
    lo16, hi16, cnt_lo, cnt_hi = state
    lo = lo16 << 16
    hi = jnp.where(cnt_lo == top_k, lo + 1, jnp.where(hi16 > 32767, jnp.int32(2147483647), hi16 << 16))
    lo, _, cnt_lo, cnt_hi = bisect(count_ge, (lo, hi, cnt_lo, cnt_hi))

    thr = jnp.maximum(lo, jnp.int32(INT_MIN + 1))
    tie_q = (cnt_lo > top_k) & (lo > jnp.int32(INT_MIN))
    any_tie = jnp.max(tie_q.astype(I32))

    @pl.when(any_tie == 0)
    def _():
        def write_block(kb, carry):
            k0 = pl.multiple_of(kb * tks, tks)
            bias_ref[pl.ds(k0, tks), :] = jnp.where(key_ref[pl.ds(k0, tks), :] >= thr, 0.0, NEG_BIAS)
            return carry

        lax.fori_loop(0, nks, write_block, 0)

    @pl.when(any_tie != 0)
    def _():
        need = jnp.where(tie_q, top_k - cnt_hi, big).astype(F32)
        lr = lax.broadcasted_iota(I32, (LANE, LANE), 0)
        lc = lax.broadcasted_iota(I32, (LANE, LANE), 1)
        lower = (lr >= lc).astype(BF16)

        def write_block(kb, run):
            k0 = pl.multiple_of(kb * LANE, LANE)
            blk = key_ref[pl.ds(k0, LANE), :]
            eq = blk == thr
            pc = jnp.dot(lower, jnp.where(eq, 1.0, 0.0).astype(BF16), preferred_element_type=F32)
            keep_eq = jnp.where((run + pc) <= need, 0.0, NEG_BIAS)
            bias_ref[pl.ds(k0, LANE), :] = jnp.where(blk > thr, 0.0, jnp.where(eq, keep_eq, NEG_BIAS))
            return run + pc[LANE - 1:LANE, :]

        lax.fori_loop(0, nks * (tks // LANE), write_block, jnp.zeros((1, tq), F32))

    def fill_block(kb, carry):
        k0 = pl.multiple_of(kb * tks, tks)
        bias_ref[pl.ds(k0, tks), :] = jnp.full((tks, tq), NEG_BIAS, F32)
        return carry

    lax.fori_loop(nks, seq // tks, fill_block, 0)


def _indexer_bias_t(ki, qit, wit, top_k, tq=256, tks=256, tkc=1024):
    bsz, seq, _ = ki.shape
    tq = min(tq, seq)
    return pl.pallas_call(
        functools.partial(_indexer_t_kernel, tq=tq, tks=tks, tkc=tkc, top_k=top_k, seq=seq),
        grid=(bsz, seq // tq),
        in_specs=[pl.BlockSpec((None, seq, IDX_DIM), lambda b, i: (b, 0, 0)),
                  pl.BlockSpec((None, IDX_HEADS * IDX_DIM, tq), lambda b, i: (b, 0, i)),
                  pl.BlockSpec((None, IDX_HEADS, tq), lambda b, i: (b, 0, i))],
        out_specs=pl.BlockSpec((None, seq, tq), lambda b, i: (b, 0, i)),
        out_shape=jax.ShapeDtypeStruct((bsz, seq, seq), F32),
        scratch_shapes=[pltpu.VMEM((seq, tq), I32), pltpu.VMEM((seq, tq), jnp.int16)],
        compiler_params=_cparams(("parallel", "arbitrary")),
        name="indexer_select",
    )(ki, qit, wit)


def _attn_t_kernel(qidx_ref, kidx_ref, qt_ref, k_ref, vt_ref, b_ref, o_ref, m_ref, acc_ref,
                   *, nheads, hd, hda, tq, tk):
    p = pl.program_id(1)
    qb = qidx_ref[p]
    kb = kidx_ref[p]
    kb_last = ((qb + 1) * tq - 1) // tk

    @pl.when(kb == 0)
    def _():
        m_ref[...] = jnp.full(m_ref.shape, M_INIT, F32)
        acc_ref[...] = jnp.zeros(acc_ref.shape, F32)

    bias = b_ref[...]
    hrows = [slice(h * hd, (h + 1) * hd) for h in range(nheads)]
    arows = [slice(h * hda, (h + 1) * hda) for h in range(nheads)]
    s = [jnp.dot(k_ref[:, r], qt_ref[r, :], preferred_element_type=F32) + bias for r in hrows]
    m_old = [m_ref[h] for h in range(nheads)]
    m_new = [jnp.maximum(m_old[h], jnp.max(s[h], axis=0, keepdims=True)) for h in range(nheads)]
    alpha = [jnp.exp2(m_old[h] - m_new[h]) for h in range(nheads)]
    pr = [jnp.exp2(s[h] - m_new[h]).astype(BF16) for h in range(nheads)]
    pv = [jnp.dot(vt_ref[arows[h], :], pr[h], preferred_element_type=F32) for h in range(nheads)]
    for h in range(nheads):
        m_ref[h] = m_new[h]
        acc_ref[arows[h], :] = alpha[h] * acc_ref[arows[h], :] + pv[h]

    @pl.when(kb == kb_last)
    def _():
        for h in range(nheads):
            a0 = h * hda
            out_t = acc_ref[a0:a0 + hd, :] / acc_ref[a0 + hd:a0 + hd + 1, :]
            o_ref[:, hrows[h]] = out_t.T.astype(o_ref.dtype)


def _masked_attention_t(qt, k, vt, bias_t, tq=512, tk=1024):
    bsz, width, seq = qt.shape
    nheads = width // A_HEAD_DIM
    hda = A_HEAD_DIM + ONES_ROWS
    tq = min(tq, seq)
    tk = min(tk, seq)
    nq = seq // tq
    pairs = [(i, j) for i in range(nq) for j in range(((i + 1) * tq - 1) // tk + 1)]
    qidx = jnp.asarray([pq for pq, _ in pairs], I32)
    kidx = jnp.asarray([pk for _, pk in pairs], I32)
    grid_spec = pltpu.PrefetchScalarGridSpec(
        num_scalar_prefetch=2,
        grid=(bsz, len(pairs)),
        in_specs=[pl.BlockSpec((None, width, tq), lambda b, p, qi, ki: (b, 0, qi[p])),
                  pl.BlockSpec((None, tk, width), lambda b, p, qi, ki: (b, ki[p], 0)),
                  pl.BlockSpec((None, nheads * hda, tk), lambda b, p, qi, ki: (b, 0, ki[p])),
                  pl.BlockSpec((None, tk, tq), lambda b, p, qi, ki: (b, ki[p], qi[p]))],
        out_specs=pl.BlockSpec((None, tq, width), lambda b, p, qi, ki: (b, qi[p], 0)),
        scratch_shapes=[pltpu.VMEM((nheads, 1, tq), F32),
                        pltpu.VMEM((nheads * hda, tq), F32)],
    )
    return pl.pallas_call(
        functools.partial(_attn_t_kernel, nheads=nheads, hd=A_HEAD_DIM, hda=hda, tq=tq, tk=tk),
        grid_spec=grid_spec,
        out_shape=jax.ShapeDtypeStruct((bsz, seq, width), BF16),
        compiler_params=_cparams(("parallel", "arbitrary")),
        name="masked_attention",
    )(qidx, kidx, qt, k, vt, bias_t)


def _retention_kernel(qk_ref, v_ref, g_ref, o_ref, state_ref, *, tc, dk, dv):
    @pl.when(pl.program_id(1) == 0)
    def _():
        state_ref[...] = jnp.zeros(state_ref.shape, F32)

    row = lax.broadcasted_iota(I32, (tc, tc), 0)
    col = lax.broadcasted_iota(I32, (tc, tc), 1)
    diff = (row - col).astype(F32)
    pos = lax.broadcasted_iota(I32, (tc, 1), 0).astype(F32)
    for h in range(R_HEADS):
        log_g = math.log1p(-(2.0 ** (-5.0 - h)))
        q = qk_ref[:, h * dk:(h + 1) * dk]
        k = qk_ref[:, (R_HEADS + h) * dk:(R_HEADS + h + 1) * dk]
        v = v_ref[:, h * dv:(h + 1) * dv]
        decay = jnp.where(diff >= 0, jnp.exp(jnp.maximum(diff, 0.0) * log_g), 0.0)
        s = lax.dot_general(q, k, (((1,), (1,)), ((), ())), preferred_element_type=F32) * decay
        intra = jnp.dot(s.astype(BF16), v, preferred_element_type=F32)
        xi = jnp.exp((pos + 1.0) * log_g)
        zeta = jnp.exp((tc - 1.0 - pos) * log_g)
        state = state_ref[h]
        cross = jnp.dot((q.astype(F32) * xi).astype(BF16), state.astype(BF16), preferred_element_type=F32)
        kz = (k.astype(F32) * zeta).astype(BF16)
        kv = lax.dot_general(kz, v, (((0,), (0,)), ((), ())), preferred_element_type=F32)
        state_ref[h] = state * math.exp(tc * log_g) + kv
        ret = intra + cross
        mu = jnp.mean(ret, axis=1, keepdims=True)
        d = ret - mu
        var = jnp.mean(d * d, axis=1, keepdims=True)
        gate = g_ref[:, h * dv:(h + 1) * dv]
        gate = gate / (1.0 + jnp.exp(-gate))
        o_ref[:, h * dv:(h + 1) * dv] = (d * lax.rsqrt(var + 1e-5) * gate).astype(o_ref.dtype)


def _retention(bqk, bv, bg, tc=256):
    bsz, seq, w2 = bqk.shape
    dk = w2 // (2 * R_HEADS)
    dv = bv.shape[2] // R_HEADS
    tc = min(tc, seq)
    return pl.pallas_call(
        functools.partial(_retention_kernel, tc=tc, dk=dk, dv=dv),
        grid=(bsz, seq // tc),
        in_specs=[pl.BlockSpec((None, tc, w2), lambda b, c: (b, c, 0)),
                  pl.BlockSpec((None, tc, R_HEADS * dv), lambda b, c: (b, c, 0)),
                  pl.BlockSpec((None, tc, R_HEADS * dv), lambda b, c: (b, c, 0))],
        out_specs=pl.BlockSpec((None, tc, R_HEADS * dv), lambda b, c: (b, c, 0)),
        out_shape=jax.ShapeDtypeStruct((bsz, seq, R_HEADS * dv), BF16),
        scratch_shapes=[pltpu.VMEM((R_HEADS, dk, dv), F32)],
        compiler_params=_cparams(("parallel", "arbitrary")),
        name="retention",
    )(bqk, bv, bg)


def _head_sum_matrix(width, hd):
    r = jnp.arange(width)
    return (r[:, None] // hd == r[None, :] // hd).astype(BF16)


def _sigmoid(x):
    return 1.0 / (1.0 + jnp.exp(-x))


def _rwkv_prep_kernel(*refs, cw, has_vres):
    if has_vres:
        (c_ref, mu_ref, vec_ref, wlb_ref, alb_ref, glb_ref, hs_ref, vlb_ref, vfirst_ref,
         r_o, lw_o, k_o, v_o, al_o, be_o, g_o, bo_o, carry_ref) = refs
    else:
        (c_ref, mu_ref, vec_ref, wlb_ref, alb_ref, glb_ref, hs_ref,
         r_o, lw_o, k_o, v_o, al_o, be_o, g_o, bo_o, carry_ref) = refs
    tr = c_ref.shape[0]

    @pl.when(pl.program_id(1) == 0)
    def _():
        carry_ref[...] = jnp.zeros(carry_ref.shape, F32)

    c = c_ref[...]
    row = lax.broadcasted_iota(I32, (tr, 1), 0)
    prev = jnp.where(row == 0, carry_ref[0:1, :], pltpu.roll(c, 1, axis=0))
    carry_ref[0:1, :] = c[tr - 1:tr, :]
    cs = c + (prev - c) * mu_ref[...]

    r = cs[:, 0:cw]
    k = cs[:, cw:2 * cw]
    v = cs[:, 2 * cw:3 * cw]
    o = 3 * cw
    wl = cs[:, o:o + LANE]
    al = cs[:, o + LANE:o + 2 * LANE]
    gl = cs[:, o + 2 * LANE:o + 2 * LANE + GATE_LORA]
    w0, a0, k_k, k_a, r_k, v0 = (vec_ref[i:i + 1, :] for i in range(6))

    z = -(w0 + jnp.dot(jnp.tanh(wl).astype(BF16), wlb_ref[...], preferred_element_type=F32))
    softplus = jnp.maximum(z, 0.0) + jnp.log(1.0 + jnp.exp(-jnp.abs(z)))
    lw = -jnp.exp(-softplus - 0.5)
    a = _sigmoid(a0 + jnp.dot(al.astype(BF16), alb_ref[...], preferred_element_type=F32))
    g = jnp.dot(_sigmoid(gl).astype(BF16), glb_ref[...], preferred_element_type=F32)
    if has_vres:
        vr = cs[:, o + 2 * LANE + GATE_LORA:o + 2 * LANE + GATE_LORA + vlb_ref.shape[0]]
        mix = _sigmoid(v0 + jnp.dot(vr.astype(BF16), vlb_ref[...], preferred_element_type=F32))
        v = v + (vfirst_ref[...] - v) * mix
    hs = hs_ref[...]
    kk = k * k_k
    ss = _dot_sel(kk * kk, hs)
    kk = kk / jnp.maximum(jnp.sqrt(ss), 1e-12)
    kh = k * (1.0 + (a - 1.0) * k_a)
    rk = _dot_sel(r * kh * r_k, hs)

    r_o[...] = r
    lw_o[...] = lw
    k_o[...] = kh
    v_o[...] = v
    al_o[...] = -kk
    be_o[...] = kk * a
    g_o[...] = g
    bo_o[...] = rk * v


def _rwkv_prep(cproj, mu, vecs, wlb, alb, glb, vlb, v_first, cw, tr=256):
    bsz, seq, wc = cproj.shape
    tr = min(tr, seq)
    has_vres = vlb is not None
    hs = _head_sum_matrix(cw, W_HEAD_DIM)
    full = lambda a: pl.BlockSpec(a.shape, lambda b, t: (0,) * a.ndim)
    tok = pl.BlockSpec((None, tr, cw), lambda b, t: (b, t, 0))
    ins = [cproj, mu, vecs, wlb, alb, glb, hs]
    in_specs = [pl.BlockSpec((None, tr, wc), lambda b, t: (b, t, 0)),
                full(mu), full(vecs), full(wlb), full(alb), full(glb), full(hs)]
    if has_vres:
        ins += [vlb, v_first]
        in_specs += [full(vlb), tok]
    out = jax.ShapeDtypeStruct((bsz, seq, cw), F32)
    return pl.pallas_call(
        functools.partial(_rwkv_prep_kernel, cw=cw, has_vres=has_vres),
        grid=(bsz, seq // tr),
        in_specs=in_specs,
        out_specs=[tok] * 8,
        out_shape=[out] * 8,
        scratch_shapes=[pltpu.VMEM((8, wc), F32)],
        compiler_params=_cparams(("parallel", "arbitrary")),
        name="rwkv_prep",
    )(*ins)


def _split_bf16(x):
    hi = x.astype(BF16)
    return hi, (x - hi.astype(F32)).astype(BF16)


def _dot_x3(a, b):
    ah, al = _split_bf16(a)
    bh, bl = _split_bf16(b)
    return (jnp.dot(ah, bh, preferred_element_type=F32) + jnp.dot(ah, bl, preferred_element_type=F32)
            + jnp.dot(al, bh, preferred_element_type=F32))


def _dot_sel(a, sel):
    ah, al = _split_bf16(a)
    return jnp.dot(ah, sel, preferred_element_type=F32) + jnp.dot(al, sel, preferred_element_type=F32)


def _bdot(a, b):
    return jnp.dot(a.astype(BF16), b.astype(BF16), preferred_element_type=F32)


def _bdot_tn(a, b):
    return lax.dot_general(a.astype(BF16), b.astype(BF16), (((0,), (0,)), ((), ())),
                           preferred_element_type=F32)


def _rwkv_chunk_kernel(r_ref, lw_ref, k_ref, v_ref, al_ref, be_ref, g_o, y0_o, m_o, z0_o, *, tc, nck, npairs):
    row = lax.broadcasted_iota(I32, (tc, tc), 0)
    col = lax.broadcasted_iota(I32, (tc, tc), 1)
    incl = row >= col
    strict = row > col
    tri = incl.astype(BF16)
    eye = (row == col).astype(F32)
    lane = lax.broadcasted_iota(I32, (1, LANE), 1)
    mh0 = (lane // W_HEAD_DIM) == 0
    prow = lax.broadcasted_iota(I32, (LANE, LANE), 0)
    pcol = lax.broadcasted_iota(I32, (LANE, LANE), 1)
    same_head = (prow // W_HEAD_DIM) == (pcol // W_HEAD_DIM)

    lw_all = lw_ref[...]
    hi = lw_all.astype(BF16)
    rem = lw_all - hi.astype(F32)
    mid = rem.astype(BF16)
    lo = (rem - mid.astype(F32)).astype(BF16)
    cum_all = jnp.concatenate(
        [sum(jnp.dot(tri, part[c * tc:(c + 1) * tc], preferred_element_type=F32) for part in (hi, mid, lo))
         for c in range(nck)], axis=0)

    pairs = range(nck * npairs)
    heads = [(p, h) for p in pairs for h in range(2)]
    sls = [slice((p % npairs) * LANE, (p % npairs + 1) * LANE) for p in pairs]
    rws = [slice((p // npairs) * tc, (p // npairs + 1) * tc) for p in pairs]
    cum = [cum_all[rws[p], sls[p]] for p in pairs]
    tot = [c[tc - 1:tc, :] for c in cum]
    p_inv = [jnp.exp(-cum[p]) for p in pairs]
    p_end = [jnp.exp(tot[p] - cum[p]) for p in pairs]
    at = [al_ref[rws[p], sls[p]] * jnp.exp(cum[p] - lw_all[rws[p], sls[p]]) for p in pairs]
    rt = [r_ref[rws[p], sls[p]] * jnp.exp(cum[p]) for p in pairs]
    bh = [be_ref[rws[p], sls[p]] * p_end[p] for p in pairs]
    khat = [k_ref[rws[p], sls[p]] * p_end[p] for p in pairs]
    v = [v_ref[rws[p], sls[p]] for p in pairs]
    rhs = [jnp.concatenate([be_ref[rws[p], sls[p]] * p_inv[p], k_ref[rws[p], sls[p]] * p_inv[p]],
                           axis=0).astype(BF16) for p in pairs]

    a_ab, a_ak, qcat = {}, {}, {}
    for p, h in heads:
        mh = (lane // W_HEAD_DIM) == h
        lhs = jnp.concatenate([jnp.where(mh, at[p], 0.0), jnp.where(mh, rt[p], 0.0)], axis=0).astype(BF16)
        x = lax.dot_general(lhs, rhs[p], (((1,), (1,)), ((), ())), preferred_element_type=F32)
        a_ab[p, h] = jnp.where(strict, x[:tc, :tc], 0.0)
        a_ak[p, h] = jnp.where(strict, x[:tc, tc:], 0.0)
        qcat[p, h] = jnp.concatenate([jnp.where(incl, x[tc:, tc:], 0.0),
                                      jnp.where(incl, x[tc:, :tc], 0.0)], axis=1)

    t_inv = {ph: eye + a_ab[ph] for ph in heads}
    pw = {ph: _bdot(a_ab[ph], a_ab[ph]) for ph in heads}
    akv = {(p, h): _bdot(a_ak[p, h], v[p]) for p, h in heads}
    n = 2
    while n < tc:
        for ph in heads:
            res = _bdot(jnp.concatenate([pw[ph], t_inv[ph]], axis=0), pw[ph])
            t_inv[ph] = t_inv[ph] + res[tc:]
            pw[ph] = res[:tc]
        n *= 2
    tw = {(p, h): _bdot(t_inv[p, h], jnp.concatenate([at[p], akv[p, h]], axis=1)) for p, h in heads}

    zero = jnp.zeros((tc, LANE), F32)
    w = [jnp.where(mh0, tw[p, 0][:, :LANE], tw[p, 1][:, :LANE]) for p in pairs]
    u0 = [jnp.where(mh0, tw[p, 0][:, LANE:], tw[p, 1][:, LANE:]) for p in pairs]
    vu = [jnp.concatenate([jnp.concatenate([v[p], zero], axis=1),
                           jnp.concatenate([u0[p], w[p]], axis=1)], axis=0) for p in pairs]
    yg = {(p, h): _bdot(qcat[p, h], vu[p]) for p, h in heads}
    m_mat = [_bdot_tn(bh[p], w[p]) for p in pairs]
    z0 = [_bdot_tn(jnp.concatenate([bh[p], khat[p]], axis=0), jnp.concatenate([u0[p], v[p]], axis=0))
          for p in pairs]
    for p in pairs:
        y0_o[rws[p], sls[p]] = jnp.where(mh0, yg[p, 0][:, :LANE], yg[p, 1][:, :LANE])
        g_o[rws[p], sls[p]] = rt[p] + jnp.where(mh0, yg[p, 0][:, LANE:], yg[p, 1][:, LANE:])
        m_o[p // npairs, p % npairs] = jnp.where(same_head, m_mat[p], 0.0) + jnp.where(
            prow == pcol, jnp.broadcast_to(jnp.exp(tot[p]), (LANE, LANE)), 0.0)
        z0_o[p // npairs, p % npairs] = jnp.where(same_head, z0[p], 0.0)


def _rwkv_chunk_ops(r, lw, kh, v, alpha, beta, tc=CHUNK, nck=4):
    bsz, seq, cw = r.shape
    npairs = cw // LANE
    nc = seq // tc
    tok = pl.BlockSpec((None, nck * tc, cw), lambda b, c: (b, c, 0))
    mat = pl.BlockSpec((None, nck, npairs, LANE, LANE), lambda b, c: (b, c, 0, 0, 0))
    tok_shape = jax.ShapeDtypeStruct((bsz, seq, cw), F32)
    mat_shape = jax.ShapeDtypeStruct((bsz, nc, npairs, LANE, LANE), F32)
    return pl.pallas_call(
        functools.partial(_rwkv_chunk_kernel, tc=tc, nck=nck, npairs=npairs),
        grid=(bsz, nc // nck),
        in_specs=[tok] * 6,
        out_specs=[tok, tok, mat, mat],
        out_shape=[tok_shape, tok_shape, mat_shape, mat_shape],
        compiler_params=_cparams(("parallel", "parallel")),
        name="rwkv_chunk_ops",
    )(r, lw, kh, v, alpha, beta)


def _rwkv_scan_kernel(g_ref, y0_ref, m_ref, z0_ref, bo_ref, gate_ref, ln_ref, hs_ref, o_ref, state_ref,
                      *, tc, nch, npairs):
    @pl.when(pl.program_id(1) == 0)
    def _():
        state_ref[...] = jnp.zeros(state_ref.shape, F32)

    hs = hs_ref[...]
    inv_n = 1.0 / W_HEAD_DIM
    sls = [slice(p * LANE, (p + 1) * LANE) for p in range(npairs)]
    st = [state_ref[p] for p in range(npairs)]
    for ch in range(nch):
        rows = slice(ch * tc, (ch + 1) * tc)
        gm = [_dot_x3(jnp.concatenate([g_ref[rows, sls[p]], m_ref[ch, p]], axis=0), st[p]) for p in range(npairs)]
        y = [gm[p][:tc] + y0_ref[rows, sls[p]] for p in range(npairs)]
        st = [gm[p][tc:] + z0_ref[ch, p] for p in range(npairs)]
        mu = [_dot_sel(y[p], hs) * inv_n for p in range(npairs)]
        d = [y[p] - mu[p] for p in range(npairs)]
        var = [_dot_sel(d[p] * d[p], hs) * inv_n for p in range(npairs)]
        for p in range(npairs):
            yn = d[p] * lax.rsqrt(var[p] + LNX_EPS) * ln_ref[0:1, sls[p]] + ln_ref[1:2, sls[p]]
            o_ref[rows, sls[p]] = ((yn + bo_ref[rows, sls[p]]) * gate_ref[rows, sls[p]]).astype(o_ref.dtype)
    for p in range(npairs):
        state_ref[p] = st[p]


def _rwkv_scan(g, y0, m, z0, bonus, gate, ln, tc=CHUNK, nch=4):
    bsz, seq, cw = g.shape
    npairs = cw // LANE
    nch = min(nch, seq // tc)
    hs = _head_sum_matrix(LANE, W_HEAD_DIM)
    tok = pl.BlockSpec((None, tc * nch, cw), lambda b, c: (b, c, 0))
    mat = pl.BlockSpec((None, nch, npairs, LANE, LANE), lambda b, c: (b, c, 0, 0, 0))
    return pl.pallas_call(
        functools.partial(_rwkv_scan_kernel, tc=tc, nch=nch, npairs=npairs),
        grid=(bsz, seq // (tc * nch)),
        in_specs=[tok, tok, mat, mat, tok, tok,
                  pl.BlockSpec(ln.shape, lambda b, c: (0, 0)),
                  pl.BlockSpec(hs.shape, lambda b, c: (0, 0))],
        out_specs=tok,
        out_shape=jax.ShapeDtypeStruct((bsz, seq, cw), BF16),
        scratch_shapes=[pltpu.VMEM((npairs, LANE, LANE), F32)],
        compiler_params=_cparams(("parallel", "arbitrary")),
        name="rwkv_scan",
    )(g, y0, m, z0, bonus, gate, ln, hs)


def _out_proj_kernel(x_ref, a_ref, b_ref, c_ref, w_ref, o_ref, *, wa, wb):
    acc = jnp.dot(a_ref[...], w_ref[0:wa, :], preferred_element_type=F32)
    acc = acc + jnp.dot(b_ref[...], w_ref[wa:wa + wb, :], preferred_element_type=F32)
    acc = acc + jnp.dot(c_ref[...], w_ref[wa + wb:, :], preferred_element_type=F32)
    o_ref[...] = x_ref[...] + acc


def _out_proj(x2d, oa, ob, oc, w, tm=1024, tn=1024):
    m, d = x2d.shape
    wa, wb, wc = oa.shape[1], ob.shape[1], oc.shape[1]
    tn = min(tn, d)
    return pl.pallas_call(
        functools.partial(_out_proj_kernel, wa=wa, wb=wb),
        grid=(m // tm, d // tn),
        in_specs=[pl.BlockSpec((tm, tn), lambda i, j: (i, j)),
                  pl.BlockSpec((tm, wa), lambda i, j: (i, 0)),
                  pl.BlockSpec((tm, wb), lambda i, j: (i, 0)),
                  pl.BlockSpec((tm, wc), lambda i, j: (i, 0)),
                  pl.BlockSpec((wa + wb + wc, tn), lambda i, j: (0, j))],
        out_specs=pl.BlockSpec((tm, tn), lambda i, j: (i, j)),
        out_shape=jax.ShapeDtypeStruct((m, d), F32),
        compiler_params=_cparams(("parallel", "arbitrary")),
        name="out_proj",
    )(x2d, oa, ob, oc, w)


def _gate_up_kernel(h_ref, wg_ref, wu_ref, o_ref):
    h = h_ref[...]
    gate = jnp.dot(h, wg_ref[...], preferred_element_type=F32)
    up = jnp.dot(h, wu_ref[...], preferred_element_type=F32)
    o_ref[...] = (gate / (1.0 + jnp.exp(-gate)) * up).astype(o_ref.dtype)


def _gate_up(h, w_gate_up, tm=1024, tf_cap=512):
    m, d = h.shape
    dff = w_gate_up.shape[1] // 2
    tf = _pick_tile(dff, tf_cap)
    nf = dff // tf
    return pl.pallas_call(
        _gate_up_kernel,
        grid=(m // tm, nf),
        in_specs=[pl.BlockSpec((tm, d), lambda i, j: (i, 0)),
                  pl.BlockSpec((d, tf), lambda i, j: (0, j)),
                  pl.BlockSpec((d, tf), lambda i, j: (0, j + nf))],
        out_specs=pl.BlockSpec((tm, tf), lambda i, j: (i, j)),
        out_shape=jax.ShapeDtypeStruct((m, dff), BF16),
        compiler_params=_cparams(("parallel", "arbitrary")),
        name="ffn_gate_up",
    )(h, w_gate_up, w_gate_up)


def _down_kernel(x_ref, a_ref, w_ref, o_ref):
    o_ref[...] = x_ref[...] + jnp.dot(a_ref[...], w_ref[...], preferred_element_type=F32)


def _down_proj(x2d, act, w, tm=1024, tn=512):
    m, d = x2d.shape
    dff = act.shape[1]
    tn = min(tn, d)
    return pl.pallas_call(
        _down_kernel,
        grid=(m // tm, d // tn),
        in_specs=[pl.BlockSpec((tm, tn), lambda i, j: (i, j)),
                  pl.BlockSpec((tm, dff), lambda i, j: (i, 0)),
                  pl.BlockSpec((dff, tn), lambda i, j: (0, j))],
        out_specs=pl.BlockSpec((tm, tn), lambda i, j: (i, j)),
        out_shape=jax.ShapeDtypeStruct((m, d), F32),
        compiler_params=_cparams(("parallel", "arbitrary")),
        name="ffn_down",
    )(x2d, act, w)


def _pad_cols(w, width):
    return jnp.pad(w, ((0, 0), (0, width - w.shape[1])))


def _pad_rows(w, height):
    return jnp.pad(w, ((0, height - w.shape[0]), (0, 0)))


def _pad_vec(v, width):
    return jnp.pad(v, (0, width - v.shape[0]))


def kernel(x, norm_mix_g, w_in, w_in_vres, rwkv_mu, rwkv_mu_vres, rwkv_w0, rwkv_w_lora_b, rwkv_a0,
           rwkv_a_lora_b, rwkv_v0, rwkv_v_lora_b, rwkv_g_lora_b, rwkv_k_k, rwkv_k_a, rwkv_r_k,
           rwkv_lnx_g, rwkv_lnx_b, w_out, norm_ffn_g, w_gate_up, w_down, final_norm_g):
    bsz, seq, d_model = x.shape
    depth = w_in.shape[0]
    m = bsz * seq
    d_mix = w_out.shape[1]
    a_w = d_mix // 2
    qi_w = IDX_HEADS * IDX_DIM
    b_v_w = d_mix // 4
    b_qk_w = b_v_w // 2
    c_w = d_mix // 4
    r_qk_dim = b_qk_w // R_HEADS
    top_k = min(TOPK_MAX, seq // 4)
    vres_pad = 2 * LANE

    o_q, o_k, o_v = 0, a_w, 2 * a_w
    o_qi = 3 * a_w
    o_ki = o_qi + qi_w
    o_wi = o_ki + IDX_DIM
    o_bq = o_wi + IDX_HEADS
    o_bk = o_bq + b_qk_w
    o_bv = o_bk + b_qk_w
    o_bg = o_bv + b_v_w
    o_c = o_bg + b_v_w
    o_wl = o_c + 3 * c_w
    o_al = o_wl + DECAY_LORA
    o_gl = o_al + AAA_LORA
    n_in = o_gl + GATE_LORA

    a_rot = A_HEAD_DIM // ROPE_FRAC
    tq_ = _rope_tables(seq, A_HEAD_DIM, a_rot, ROPE_THETA, scale=A_HEAD_DIM ** -0.5 * math.log2(math.e))
    tk_ = _rope_tables(seq, A_HEAD_DIM, a_rot, ROPE_THETA)
    tab_q = tuple(t[None] for t in tq_)
    tab_k = tuple(t[None] for t in tk_)
    i_rot = IDX_DIM // ROPE_FRAC
    tab_qi = tuple(t[None] for t in _rope_tables(seq, IDX_DIM, i_rot, ROPE_THETA))
    lane = jnp.arange(LANE)
    kiwi_pass = jnp.where(lane < IDX_DIM, 1.0, IDX_W_SCALE).astype(F32)
    c_kw, s1_kw, s2_kw = _rope_tables(seq, LANE, i_rot, ROPE_THETA, pass_scale=kiwi_pass)
    tab_kiwi = (c_kw[None], s1_kw[None], s2_kw[None])
    tbq = _rope_tables(seq, r_qk_dim, r_qk_dim, R_THETA)
    tbk = _rope_tables(seq, r_qk_dim, r_qk_dim, R_THETA, scale=r_qk_dim ** -0.5)
    tab_bqk = tuple(jnp.stack([a, b]) for a, b in zip(tbq, tbk))

    x2d = x.reshape(m, d_model)
    v_first = None
    for l in range(depth):
        wl_ = w_in[l]
        cols = lambda o, n: wl_[:, o:o + n]
        w_q = cols(o_q, a_w).astype(BF16)
        w_k = cols(o_k, a_w).astype(BF16)
        w_v = cols(o_v, a_w).astype(BF16)
        w_bv = cols(o_bv, b_v_w).astype(BF16)
        w_qi = cols(o_qi, qi_w).astype(BF16)
        w_kiwi = _pad_cols(cols(o_ki, IDX_DIM + IDX_HEADS), LANE).astype(BF16)
        w_bqk = cols(o_bq, 2 * b_qk_w).astype(BF16)
        w_bg = cols(o_bg, b_v_w).astype(BF16)
        vres_w = (w_in_vres[l - 1] if l > 0 else jnp.zeros((d_model, MV_LORA), F32))
        w_c = jnp.concatenate([cols(o_c, 3 * c_w), _pad_cols(cols(o_wl, DECAY_LORA), LANE),
                               _pad_cols(cols(o_al, AAA_LORA), LANE), cols(o_gl, GATE_LORA),
                               _pad_cols(vres_w, vres_pad)], axis=1).astype(BF16)
        mu = rwkv_mu[l]
        mu_vres = rwkv_mu_vres[l - 1] if l > 0 else jnp.zeros((MV_LORA,), F32)
        mu_c = jnp.concatenate([mu[:3 * c_w], _pad_vec(mu[3 * c_w:3 * c_w + DECAY_LORA], LANE),
                                _pad_vec(mu[3 * c_w + DECAY_LORA:3 * c_w + DECAY_LORA + AAA_LORA], LANE),
                                mu[3 * c_w + DECAY_LORA + AAA_LORA:], _pad_vec(mu_vres, vres_pad)])[None, :]

        h = _rmsnorm(x2d, norm_mix_g[l], BF16)
        qt = _mm_rope(h, w_q, tab_q, a_rot // 2, seq, a_w, BF16, transpose=True)
        ak = _mm_rope(h, w_k, tab_k, a_rot // 2, seq, a_w, BF16).reshape(bsz, seq, a_w)
        vt = _mm_plain(h, w_v, BF16, seq, transpose=True, pad_rows=ONES_ROWS)
        bv = _mm_plain(h, w_bv, BF16, seq, tn_cap=512).reshape(bsz, seq, b_v_w)
        qit = _mm_rope(h, w_qi, tab_qi, i_rot // 2, seq, qi_w, BF16, transpose=True)
        kiwi = _mm_rope(h, w_kiwi, tab_kiwi, i_rot // 2, seq, LANE, F32).reshape(bsz, seq, LANE)
        bqk = _mm_rope(h, w_bqk, tab_bqk, r_qk_dim // 2, seq, b_qk_w, BF16).reshape(bsz, seq, 2 * b_qk_w)
        bg = _mm_plain(h, w_bg, F32, seq).reshape(bsz, seq, b_v_w)
        cproj = _mm_plain(h, w_c, F32, seq, tn_cap=768).reshape(bsz, seq, w_c.shape[1])

        ki = kiwi[:, :, :IDX_DIM].astype(BF16)
        wit = jnp.swapaxes(kiwi[:, :, IDX_DIM:IDX_DIM + IDX_HEADS], 1, 2)
        bias_t = _indexer_bias_t(ki, qit, wit, top_k)
        out_a = _masked_attention_t(qt, ak, vt, bias_t)

        out_b = _retention(bqk, bv, bg)

        vecs = jnp.stack([rwkv_w0[l], rwkv_a0[l], rwkv_k_k[l], rwkv_k_a[l], rwkv_r_k[l].reshape(-1),
                          rwkv_v0[l - 1] if l > 0 else jnp.zeros((c_w,), F32),
                          jnp.zeros((c_w,), F32), jnp.zeros((c_w,), F32)])
        wlb = _pad_rows(rwkv_w_lora_b[l], LANE).astype(BF16)
        alb = _pad_rows(rwkv_a_lora_b[l], LANE).astype(BF16)
        glb = rwkv_g_lora_b[l].astype(BF16)
        vlb = _pad_rows(rwkv_v_lora_b[l - 1], vres_pad).astype(BF16) if l > 0 else None
        r_, lw_, kh_, v_, al_, be_, g_, bo_ = _rwkv_prep(cproj, mu_c, vecs, wlb, alb, glb, vlb, v_first, c_w)
        if l == 0:
            v_first = v_
        gm, y0, mm, z0 = _rwkv_chunk_ops(r_, lw_, kh_, v_, al_, be_)
        ln = jnp.stack([rwkv_lnx_g[l], rwkv_lnx_b[l]] + [jnp.zeros((c_w,), F32)] * 6)
        out_c = _rwkv_scan(gm, y0, mm, z0, bo_, g_, ln)

        x2d = _out_proj(x2d, out_a.reshape(m, a_w), out_b.reshape(m, b_v_w), out_c.reshape(m, c_w),
                        w_out[l].astype(BF16))

        h = _rmsnorm(x2d, norm_ffn_g[l], BF16)
        act = _gate_up(h, w_gate_up[l].astype(BF16))
        x2d = _down_proj(x2d, act, w_down[l].astype(BF16))

    return _rmsnorm(x2d, final_norm_g, F32).reshape(bsz, seq, d_model)
```

```python
import functools
import math

import jax
import jax.numpy as jnp
from jax import lax
from jax.experimental import pallas as pl
from jax.experimental.pallas import tpu as pltpu

F32 = jnp.float32
BF16 = jnp.bfloat16
I32 = jnp.int32
HIGHEST = lax.Precision.HIGHEST

CHUNK = 64
A_HEAD_DIM = 128
IDX_HEADS = 16
IDX_DIM = 64
IDX_W_SCALE = (IDX_HEADS * IDX_DIM) ** -0.5
TOPK_MAX = 256
ROPE_THETA = 500000.0
ROPE_FRAC = 4
R_HEADS = 4
R_THETA = 10000.0
W_HEAD_DIM = 64
DECAY_LORA = 96
AAA_LORA = 96
MV_LORA = 64
GATE_LORA = 256
LNX_EPS = 64e-5
RMS_EPS = 1e-5

LANE = 128
VMEM_LIMIT = 56 * 1024 * 1024
NEG_BIAS = -1e30
M_INIT = -1e20
INT_MIN = -2147483648
COUNT_CHAINS = 8
SCORE_GROUP = 4
SCORE_UNROLL = 4
ONES_ROWS = 8


def _cparams(sem):
    return pltpu.CompilerParams(dimension_semantics=sem, vmem_limit_bytes=VMEM_LIMIT)


def _pick_tile(n, cap):
    best = LANE
    t = LANE
    while t <= min(n, cap):
        if n % t == 0:
            best = t
        t += LANE
    return best


def _rmsnorm_kernel(x_ref, g_ref, o_ref):
    x = x_ref[...]
    ms = jnp.mean(x * x, axis=-1, keepdims=True)
    o_ref[...] = (x * lax.rsqrt(ms + RMS_EPS) * g_ref[...]).astype(o_ref.dtype)


def _rmsnorm(x2d, g, out_dtype, tm=512):
    m, d = x2d.shape
    return pl.pallas_call(
        _rmsnorm_kernel,
        grid=(m // tm,),
        in_specs=[pl.BlockSpec((tm, d), lambda i: (i, 0)),
                  pl.BlockSpec((1, d), lambda i: (0, 0))],
        out_specs=pl.BlockSpec((tm, d), lambda i: (i, 0)),
        out_shape=jax.ShapeDtypeStruct((m, d), out_dtype),
        compiler_params=_cparams(("parallel",)),
        name="rmsnorm",
    )(x2d, g.reshape(1, d).astype(F32))


def _store_cols(o_ref, blk, x, transpose, pad_rows):
    if transpose:
        r0 = blk * (LANE + pad_rows)
        o_ref[r0:r0 + LANE, :] = x.T.astype(o_ref.dtype)
        if pad_rows:
            o_ref[r0 + LANE:r0 + LANE + pad_rows, :] = jnp.ones((pad_rows, x.shape[0]), o_ref.dtype)
    else:
        o_ref[:, blk * LANE:(blk + 1) * LANE] = x.astype(o_ref.dtype)


def _proj_out_spec(m, n, tm, tn, seq, out_dtype, transpose, pad_rows):
    if not transpose:
        return pl.BlockSpec((tm, tn), lambda i, j: (i, j)), jax.ShapeDtypeStruct((m, n), out_dtype)
    tpb = seq // tm
    rows = lambda cols: cols // LANE * (LANE + pad_rows)
    return (pl.BlockSpec((None, rows(tn), tm), lambda i, j: (i // tpb, j, i % tpb)),
            jax.ShapeDtypeStruct((m // seq, rows(n), seq), out_dtype))


def _mm_plain_kernel(a_ref, w_ref, o_ref, *, transpose, pad_rows):
    acc = jnp.dot(a_ref[...], w_ref[...], preferred_element_type=F32)
    if not transpose:
        o_ref[...] = acc.astype(o_ref.dtype)
        return
    for blk in range(acc.shape[1] // LANE):
        _store_cols(o_ref, blk, acc[:, blk * LANE:(blk + 1) * LANE], transpose, pad_rows)


def _mm_plain(a, w, out_dtype, seq, tm=1024, tn_cap=1024, transpose=False, pad_rows=0):
    m, k = a.shape
    n = w.shape[1]
    tn = _pick_tile(n, tn_cap)
    out_spec, out_shape = _proj_out_spec(m, n, tm, tn, seq, out_dtype, transpose, pad_rows)
    return pl.pallas_call(
        functools.partial(_mm_plain_kernel, transpose=transpose, pad_rows=pad_rows),
        grid=(m // tm, n // tn),
        in_specs=[pl.BlockSpec((tm, k), lambda i, j: (i, 0)),
                  pl.BlockSpec((k, tn), lambda i, j: (0, j))],
        out_specs=out_spec,
        out_shape=out_shape,
        compiler_params=_cparams(("parallel", "arbitrary")),
        name="proj_plain",
    )(a, w)


def _mm_rope_kernel(a_ref, w_ref, c_ref, s1_ref, s2_ref, o_ref, *, half, transpose):
    acc = jnp.dot(a_ref[...], w_ref[...], preferred_element_type=F32)
    c, s1, s2 = c_ref[...], s1_ref[...], s2_ref[...]
    for blk in range(acc.shape[1] // LANE):
        x = acc[:, blk * LANE:(blk + 1) * LANE]
        up = pltpu.roll(x, LANE - half, axis=1)
        dn = pltpu.roll(x, half, axis=1)
        _store_cols(o_ref, blk, x * c + up * s1 + dn * s2, transpose, 0)


def _mm_rope(a, w, tables, half, seq, tn, out_dtype, tm=1024, transpose=False):
    m, k = a.shape
    n = w.shape[1]
    tpb = seq // tm
    tab_spec = pl.BlockSpec((None, tm, LANE), lambda i, j: (j, i % tpb, 0))
    out_spec, out_shape = _proj_out_spec(m, n, tm, tn, seq, out_dtype, transpose, 0)
    return pl.pallas_call(
        functools.partial(_mm_rope_kernel, half=half, transpose=transpose),
        grid=(m // tm, n // tn),
        in_specs=[pl.BlockSpec((tm, k), lambda i, j: (i, 0)),
                  pl.BlockSpec((k, tn), lambda i, j: (0, j)),
                  tab_spec, tab_spec, tab_spec],
        out_specs=out_spec,
        out_shape=out_shape,
        compiler_params=_cparams(("parallel", "arbitrary")),
        name="proj_rope",
    )(a, w, *tables)


def _rope_tables(seq, group, rot_dim, theta, scale=1.0, pass_scale=None):
    half = rot_dim // 2
    freqs = jnp.power(F32(theta), -jnp.arange(half, dtype=F32) / half)
    ang = jnp.arange(seq, dtype=F32)[:, None] * freqs[None, :]
    cos, sin = jnp.cos(ang), jnp.sin(ang)
    lane = jnp.arange(LANE) % group
    idx = lane % half
    cosl, sinl = cos[:, idx], sin[:, idx]
    passv = jnp.ones((LANE,), F32) if pass_scale is None else pass_scale
    c = jnp.where(lane < rot_dim, cosl, passv[None, :])
    s1 = jnp.where(lane < half, -sinl, 0.0)
    s2 = jnp.where((lane >= half) & (lane < rot_dim), sinl, 0.0)
    return c * scale, s1 * scale, s2 * scale


def _indexer_kernel(qi_ref, kit_ref, wi_ref, bias_ref, key_ref, *, tq, ts, tki, tkc, top_k, seq):
    qb = pl.program_id(1)
    nkb = ((qb + 1) * tq + tki - 1) // tki
    q_chunk = (qb * tq + lax.broadcasted_iota(I32, (tq, 1), 0)) // CHUNK
    w = wi_ref[...]

    def score_block(kb, rmax):
        k0 = pl.multiple_of(kb * tki, tki)
        kblk = kit_ref[:, pl.ds(k0, tki)]
        k_chunk = (k0 + lax.broadcasted_iota(I32, (1, tki), 1)) // CHUNK
        new_max = []
        for rh in range(tq // ts):
            rows = slice(rh * ts, (rh + 1) * ts)
            acc = jnp.zeros((ts, tki), F32)
            for h in range(IDX_HEADS):
                x = jnp.dot(qi_ref[rows, h * IDX_DIM:(h + 1) * IDX_DIM], kblk, preferred_element_type=F32)
                acc = acc + jnp.maximum(x, 0.0) * w[rows, h:h + 1]
            bits = lax.bitcast_convert_type(acc, I32)
            key = bits ^ ((bits >> 31) & jnp.int32(0x7FFFFFFF))
            key = jnp.where(k_chunk <= q_chunk[rows], key, jnp.int32(INT_MIN))
            key_ref[rows, pl.ds(k0, tki)] = key
            r = rmax[rows]
            for c in range(tki // LANE):
                r = jnp.maximum(r, key[:, c * LANE:(c + 1) * LANE])
            new_max.append(r)
        return jnp.concatenate(new_max, axis=0)

    rmax = lax.fori_loop(0, nkb, score_block, jnp.full((tq, LANE), INT_MIN, I32))
    rowmax = jnp.max(rmax, axis=1, keepdims=True)

    nkc = (nkb * tki + tkc - 1) // tkc

    def pad_block(kb, carry):
        k0 = pl.multiple_of(kb * tki, tki)
        key_ref[:, pl.ds(k0, tki)] = jnp.full((tq, tki), INT_MIN, I32)
        return carry

    lax.fori_loop(nkb, nkc * (tkc // tki), pad_block, 0)

    def count_ge(cand):
        totals = []
        for rh in range(tq // ts):
            rows = slice(rh * ts, (rh + 1) * ts)
            candb = jnp.broadcast_to(cand[rows], (ts, LANE))

            def count_block(kb, cnt, rows=rows, candb=candb):
                k0 = pl.multiple_of(kb * tkc, tkc)
                for c in range(tkc // LANE):
                    blk = key_ref[rows, pl.ds(k0 + c * LANE, LANE)]
                    cnt = cnt + jnp.where(blk >= candb, 1, 0)
                return cnt

            cnt = lax.fori_loop(0, nkc, count_block, jnp.zeros((ts, LANE), I32))
            totals.append(jnp.sum(cnt, axis=1, keepdims=True))
        return jnp.concatenate(totals, axis=0)

    probe = jnp.where(rowmax > jnp.int32(INT_MIN + (1 << 25)), rowmax - jnp.int32(3 << 23),
                      jnp.int32(INT_MIN + 1))

    def bisect_cond(state):
        it, n_active = state[0], state[1]
        return (n_active > 0) & (it < 40)

    def bisect_body(state):
        it, _, lo, hi, cnt_lo, cnt_hi = state
        mid = (lo >> 1) + (hi >> 1) + (lo & hi & 1)
        mid = jnp.where(it == 0, probe, mid)
        cnt = count_ge(mid)
        ge = cnt >= top_k
        exact = cnt == top_k
        new_lo = jnp.where(ge, mid, lo)
        new_hi = jnp.where(exact, mid + 1, jnp.where(ge, hi, mid))
        cnt_lo = jnp.where(ge, cnt, cnt_lo)
        cnt_hi = jnp.where(ge, cnt_hi, cnt)
        active = (new_lo + 1) < new_hi
        n_active = jnp.max(active.astype(I32))
        return it + 1, n_active, new_lo, new_hi, cnt_lo, cnt_hi

    big = jnp.int32(1 << 30)
    init = (jnp.int32(0), jnp.int32(1), jnp.full((tq, 1), INT_MIN, I32), rowmax + 1,
            jnp.full((tq, 1), big, I32), jnp.zeros((tq, 1), I32))
    _, _, lo, _, cnt_lo, cnt_hi = lax.while_loop(bisect_cond, bisect_body, init)

    thr = jnp.maximum(lo, jnp.int32(INT_MIN + 1))
    thrb = jnp.broadcast_to(thr, (tq, LANE))
    tie_row = (cnt_lo > top_k) & (lo > jnp.int32(INT_MIN))
    any_tie = jnp.max(tie_row.astype(I32))

    @pl.when(any_tie == 0)
    def _():
        def write_block(kb, carry):
            k0 = pl.multiple_of(kb * tki, tki)
            for c in range(tki // LANE):
                blk = key_ref[:, pl.ds(k0 + c * LANE, LANE)]
                bias_ref[:, pl.ds(k0 + c * LANE, LANE)] = jnp.where(blk >= thrb, 0.0, NEG_BIAS)
            return carry

        lax.fori_loop(0, nkb, write_block, 0)

    @pl.when(any_tie != 0)
    def _():
        need = jnp.where(tie_row, top_k - cnt_hi, big).astype(F32)
        ur = lax.broadcasted_iota(I32, (LANE, LANE), 0)
        uc = lax.broadcasted_iota(I32, (LANE, LANE), 1)
        upper = (ur <= uc).astype(BF16)

        def write_block(kb, run):
            k0 = pl.multiple_of(kb * tki, tki)
            for c in range(tki // LANE):
                blk = key_ref[:, pl.ds(k0 + c * LANE, LANE)]
                eq = blk == thrb
                pc = jnp.dot(jnp.where(eq, 1.0, 0.0).astype(BF16), upper, preferred_element_type=F32)
                keep_eq = jnp.where((run + pc) <= need, 0.0, NEG_BIAS)
                bias_ref[:, pl.ds(k0 + c * LANE, LANE)] = jnp.where(
                    blk > thrb, 0.0, jnp.where(eq, keep_eq, NEG_BIAS))
                run = run + pc[:, LANE - 1:LANE]
            return run

        lax.fori_loop(0, nkb, write_block, jnp.zeros((tq, 1), F32))

    def fill_block(kb, carry):
        k0 = pl.multiple_of(kb * tki, tki)
        bias_ref[:, pl.ds(k0, tki)] = jnp.full((tq, tki), NEG_BIAS, F32)
        return carry

    lax.fori_loop(nkb, seq // tki, fill_block, 0)


def _indexer_bias(qi, kit, wi, top_k, tq=256, ts=128, tki=512):
    bsz, seq, _ = qi.shape
    tki = min(tki, seq)
    tkc = 2 * tki if (seq // tki) % 2 == 0 else tki
    return pl.pallas_call(
        functools.partial(_indexer_kernel, tq=tq, ts=ts, tki=tki, tkc=tkc, top_k=top_k, seq=seq),
        grid=(bsz, seq // tq),
        in_specs=[pl.BlockSpec((None, tq, IDX_HEADS * IDX_DIM), lambda b, i: (b, i, 0)),
                  pl.BlockSpec((None, IDX_DIM, seq), lambda b, i: (b, 0, 0)),
                  pl.BlockSpec((None, tq, IDX_HEADS), lambda b, i: (b, i, 0))],
        out_specs=pl.BlockSpec((None, tq, seq), lambda b, i: (b, i, 0)),
        out_shape=jax.ShapeDtypeStruct((bsz, seq, seq), F32),
        scratch_shapes=[pltpu.VMEM((tq, seq), I32)],
        compiler_params=_cparams(("parallel", "arbitrary")),
        name="indexer_select",
    )(qi, kit, wi)


def _attn_kernel(qidx_ref, kidx_ref, q_ref, kt_ref, v_ref, b_ref, o_ref, m_ref, l_ref, acc_ref,
                 *, nheads, hd, tq, tk):
    p = pl.program_id(1)
    qb = qidx_ref[p]
    kb = kidx_ref[p]
    kb_last = ((qb + 1) * tq - 1) // tk

    @pl.when(kb == 0)
    def _():
        m_ref[...] = jnp.full(m_ref.shape, M_INIT, F32)
        l_ref[...] = jnp.zeros(l_ref.shape, F32)
        acc_ref[...] = jnp.zeros(acc_ref.shape, F32)

    bias = b_ref[...]

    def logits(h):
        sl = slice(h * hd, (h + 1) * hd)
        return jnp.dot(q_ref[:, sl], kt_ref[sl, :], preferred_element_type=F32)

    s_next = logits(0)
    for h in range(nheads):
        sl = slice(h * hd, (h + 1) * hd)
        s = s_next + bias
        if h + 1 < nheads:
            s_next = logits(h + 1)
        m_old = m_ref[h]
        m_new = jnp.maximum(m_old, jnp.max(s, axis=1, keepdims=True))
        alpha = jnp.exp(m_old - m_new)
        pr = jnp.exp(s - pltpu.repeat(m_new, tk // LANE, axis=1))
        l_ref[h] = alpha * l_ref[h] + jnp.sum(pr, axis=1, keepdims=True)
        m_ref[h] = m_new
        acc_ref[:, sl] = alpha * acc_ref[:, sl] + jnp.dot(pr.astype(BF16), v_ref[:, sl],
                                                          preferred_element_type=F32)

    @pl.when(kb == kb_last)
    def _():
        for h in range(nheads):
            sl = slice(h * hd, (h + 1) * hd)
            o_ref[:, sl] = (acc_ref[:, sl] / l_ref[h]).astype(o_ref.dtype)


def _masked_attention(q, kt, v, bias, tq=256, tk=512):
    bsz, seq, width = q.shape
    nheads = width // A_HEAD_DIM
    nq = seq // tq
    tk = min(tk, seq)
    pairs = [(i, j) for i in range(nq) for j in range(((i + 1) * tq - 1) // tk + 1)]
    qidx = jnp.asarray([pq for pq, _ in pairs], I32)
    kidx = jnp.asarray([pk for _, pk in pairs], I32)
    grid_spec = pltpu.PrefetchScalarGridSpec(
        num_scalar_prefetch=2,
        grid=(bsz, len(pairs)),
        in_specs=[pl.BlockSpec((None, tq, width), lambda b, p, qi, ki: (b, qi[p], 0)),
                  pl.BlockSpec((None, width, tk), lambda b, p, qi, ki: (b, 0, ki[p])),
                  pl.BlockSpec((None, tk, width), lambda b, p, qi, ki: (b, ki[p], 0)),
                  pl.BlockSpec((None, tq, tk), lambda b, p, qi, ki: (b, qi[p], ki[p]))],
        out_specs=pl.BlockSpec((None, tq, width), lambda b, p, qi, ki: (b, qi[p], 0)),
        scratch_shapes=[pltpu.VMEM((nheads, tq, LANE), F32),
                        pltpu.VMEM((nheads, tq, LANE), F32),
                        pltpu.VMEM((tq, width), F32)],
    )
    return pl.pallas_call(
        functools.partial(_attn_kernel, nheads=nheads, hd=A_HEAD_DIM, tq=tq, tk=tk),
        grid_spec=grid_spec,
        out_shape=jax.ShapeDtypeStruct((bsz, seq, width), BF16),
        compiler_params=_cparams(("parallel", "arbitrary")),
        name="masked_attention",
    )(qidx, kidx, q, kt, v, bias)


def _indexer_t_kernel(ki_ref, qit_ref, wit_ref, bias_ref, key_ref, k16_ref, *, tq, tks, tkc, top_k, seq):
    qb = pl.program_id(1)
    span = tks * SCORE_UNROLL
    nks = ((qb + 1) * tq + span - 1) // span * SCORE_UNROLL
    nkc = (nks * tks + tkc - 1) // tkc
    q_chunk = (qb * tq + lax.broadcasted_iota(I32, (1, tq), 1)) // CHUNK
    w = wit_ref[...]

    def score_step(kbu, rmax):
        for u in range(SCORE_UNROLL):
            rmax = score_block(kbu * SCORE_UNROLL + u, rmax)
        return rmax

    def score_block(kb, rmax):
        k0 = pl.multiple_of(kb * tks, tks)
        kblk = ki_ref[pl.ds(k0, tks), :]
        acc = jnp.zeros((tks, tq), F32)
        for h0 in range(0, IDX_HEADS, SCORE_GROUP):
            xs = [jnp.dot(kblk, qit_ref[h * IDX_DIM:(h + 1) * IDX_DIM, :], preferred_element_type=F32)
                  for h in range(h0, h0 + SCORE_GROUP)]
            for i, x in enumerate(xs):
                acc = acc + jnp.maximum(x, 0.0) * w[h0 + i:h0 + i + 1, :]
        k_chunk = (k0 + lax.broadcasted_iota(I32, (tks, 1), 0)) // CHUNK
        bits = lax.bitcast_convert_type(acc, I32)
        key = bits ^ ((bits >> 31) & jnp.int32(0x7FFFFFFF))
        key = jnp.where(k_chunk <= q_chunk, key, jnp.int32(INT_MIN))
        key_ref[pl.ds(k0, tks), :] = key
        k16_ref[pl.ds(k0, tks), :] = (key >> 16).astype(jnp.int16)
        return jnp.maximum(rmax, jnp.max(key.reshape(tks // 8, 8, tq), axis=0))

    rmax = lax.fori_loop(0, nks // SCORE_UNROLL, score_step, jnp.full((8, tq), INT_MIN, I32))
    colmax = jnp.max(rmax, axis=0, keepdims=True)

    def pad_block(kb, carry):
        k0 = pl.multiple_of(kb * tks, tks)
        key_ref[pl.ds(k0, tks), :] = jnp.full((tks, tq), INT_MIN, I32)
        k16_ref[pl.ds(k0, tks), :] = jnp.full((tks, tq), -32768, jnp.int16)
        return carry

    lax.fori_loop(nks, nkc * (tkc // tks), pad_block, 0)

    def count_ge(cand):
        def count_block(kb, cnt):
            k0 = pl.multiple_of(kb * tkc, tkc)
            part = tkc // COUNT_CHAINS
            sums = []
            for g in range(COUNT_CHAINS):
                ind = jnp.where(key_ref[pl.ds(k0 + g * part, part), :] >= cand, 1, 0)
                sums.append(jnp.sum(ind.reshape(part // 8, 8, tq), axis=0))
            while len(sums) > 1:
                sums = [a + b for a, b in zip(sums[0::2], sums[1::2])]
            return cnt + sums[0]

        cnt = lax.fori_loop(0, nkc, count_block, jnp.zeros((8, tq), I32))
        return jnp.sum(cnt, axis=0, keepdims=True)

    def count_ge16(cand):
        c16 = cand.astype(jnp.int16)
        one, zero = jnp.int16(1), jnp.int16(0)

        def count_block(kb, cnt):
            k0 = pl.multiple_of(kb * tkc, tkc)
            part = tkc // COUNT_CHAINS
            sums = []
            for g in range(COUNT_CHAINS):
                ind = jnp.where(k16_ref[pl.ds(k0 + g * part, part), :] >= c16, one, zero)
                acc = ind[0:16]
                for i in range(1, part // 16):
                    acc = acc + ind[i * 16:(i + 1) * 16]
                sums.append(acc)
            while len(sums) > 1:
                sums = [a + b for a, b in zip(sums[0::2], sums[1::2])]
            return cnt + sums[0]

        cnt = lax.fori_loop(0, nkc, count_block, jnp.zeros((16, tq), jnp.int16))
        return jnp.sum(cnt.astype(I32), axis=0, keepdims=True)

    def probe_step(count_fn, state, mid):
        lo, hi, cnt_lo, cnt_hi = state
        cnt = count_fn(mid)
        ge = cnt >= top_k
        new_hi = jnp.where(cnt == top_k, mid + 1, jnp.where(ge, hi, mid))
        return (jnp.where(ge, mid, lo), new_hi, jnp.where(ge, cnt, cnt_lo), jnp.where(ge, cnt_hi, cnt))

    def midpoint(state):
        lo, hi = state[0], state[1]
        return (lo >> 1) + (hi >> 1) + (lo & hi & 1)

    def n_active(state):
        return jnp.max(((state[0] + 1) < state[1]).astype(I32))

    def bisect(count_fn, state, floor=None):
        def next_mid(st, it):
            mid = midpoint(st)
            if floor is None:
                return mid
            step = jnp.int32(3 << 7) << jnp.minimum(it, 7)
            return jnp.where(st[0] == floor, jnp.maximum(st[1] - step, mid), mid)

        def body(carry):
            it, _, st = carry
            st = probe_step(count_fn, st, next_mid(st, it + 1))
            st = probe_step(count_fn, st, next_mid(st, it + 2))
            return it + 2, n_active(st), st

        return lax.while_loop(lambda c: (c[1] > 0) & (c[0] < 40), body, (jnp.int32(0), n_active(state), state))[2]

    big = jnp.int32(1 << 30)
    top16 = colmax >> 16
    state = (jnp.full((1, tq), -32768, I32), top16 + 1, jnp.full((1, tq), big, I32), jnp.zeros((1, tq), I32))
    first = jnp.where(top16 > jnp.int32(-32768 + 512), top16 - jnp.int32(3 << 7), jnp.int32(-32767))
    state = bisect(count_ge16, probe_step(count_ge16, state, first), floor=jnp.int32(-32768))
    lo16, hi16, cnt_lo, cnt_hi = state
    lo = lo16 << 16
    hi = jnp.where(cnt_lo == top_k, lo + 1, jnp.where(hi16 > 32767, jnp.int32(2147483647), hi16 << 16))
    lo, _, cnt_lo, cnt_hi = bisect(count_ge, (lo, hi, cnt_lo, cnt_hi))

    thr = jnp.maximum(lo, jnp.int32(INT_MIN + 1))
    tie_q = (cnt_lo > top_k) & (lo > jnp.int32(INT_MIN))
    any_tie = jnp.max(tie_q.astype(I32))

    @pl.when(any_tie == 0)
    def _():
        def write_block(kb, carry):
            k0 = pl.multiple_of(kb * tks, tks)
            bias_ref[pl.ds(k0, tks), :] = jnp.where(key_ref[pl.ds(k0, tks), :] >= thr, 0.0, NEG_BIAS)
            return carry

        lax.fori_loop(0, nks, write_block, 0)

    @pl.when(any_tie != 0)
    def _():
        need = jnp.where(tie_q, top_k - cnt_hi, big).astype(F32)
        lr = lax.broadcasted_iota(I32, (LANE, LANE), 0)
        lc = lax.broadcasted_iota(I32, (LANE, LANE), 1)
        lower = (lr >= lc).astype(BF16)

        def write_block(kb, run):
            k0 = pl.multiple_of(kb * LANE, LANE)
            blk = key_ref[pl.ds(k0, LANE), :]
            eq = blk == thr
            pc = jnp.dot(lower, jnp.where(eq, 1.0, 0.0).astype(BF16), preferred_element_type=F32)
            keep_eq = jnp.where((run + pc) <= need, 0.0, NEG_BIAS)
            bias_ref[pl.ds(k0, LANE), :] = jnp.where(blk > thr, 0.0, jnp.where(eq, keep_eq, NEG_BIAS))
            return run + pc[LANE - 1:LANE, :]

        lax.fori_loop(0, nks * (tks // LANE), write_block, jnp.zeros((1, tq), F32))

    def fill_block(kb, carry):
        k0 = pl.multiple_of(kb * tks, tks)
        bias_ref[pl.ds(k0, tks), :] = jnp.full((tks, tq), NEG_BIAS, F32)
        return carry

    lax.fori_loop(nks, seq // tks, fill_block, 0)


def _indexer_bias_t(ki, qit, wit, top_k, tq=256, tks=128, tkc=1024):
    bsz, seq, _ = ki.shape
    tq = min(tq, seq)
    return pl.pallas_call(
        functools.partial(_indexer_t_kernel, tq=tq, tks=tks, tkc=tkc, top_k=top_k, seq=seq),
        grid=(bsz, seq // tq),
        in_specs=[pl.BlockSpec((None, seq, IDX_DIM), lambda b, i: (b, 0, 0)),
                  pl.BlockSpec((None, IDX_HEADS * IDX_DIM, tq), lambda b, i: (b, 0, i)),
                  pl.BlockSpec((None, IDX_HEADS, tq), lambda b, i: (b, 0, i))],
        out_specs=pl.BlockSpec((None, seq, tq), lambda b, i: (b, 0, i)),
        out_shape=jax.ShapeDtypeStruct((bsz, seq, seq), F32),
        scratch_shapes=[pltpu.VMEM((seq, tq), I32), pltpu.VMEM((seq, tq), jnp.int16)],
        compiler_params=_cparams(("parallel", "arbitrary")),
        name="indexer_select",
    )(ki, qit, wit)


def _attn_t_kernel(qidx_ref, kidx_ref, qt_ref, k_ref, vt_ref, b_ref, o_ref, m_ref, acc_ref,
                   *, nheads, hd, hda, tq, tk):
    p = pl.program_id(1)
    qb = qidx_ref[p]
    kb = kidx_ref[p]
    kb_last = ((qb + 1) * tq - 1) // tk

    @pl.when(kb == 0)
    def _():
        m_ref[...] = jnp.full(m_ref.shape, M_INIT, F32)
        acc_ref[...] = jnp.zeros(acc_ref.shape, F32)

    bias = b_ref[...]
    hrows = [slice(h * hd, (h + 1) * hd) for h in range(nheads)]
    arows = [slice(h * hda, (h + 1) * hda) for h in range(nheads)]
    s = [jnp.dot(k_ref[:, r], qt_ref[r, :], preferred_element_type=F32) + bias for r in hrows]
    m_old = [m_ref[h] for h in range(nheads)]
    m_new = [jnp.maximum(m_old[h], jnp.max(s[h], axis=0, keepdims=True)) for h in range(nheads)]
    alpha = [jnp.exp2(m_old[h] - m_new[h]) for h in range(nheads)]
    pr = [jnp.exp2(s[h] - m_new[h]).astype(BF16) for h in range(nheads)]
    pv = [jnp.dot(vt_ref[arows[h], :], pr[h], preferred_element_type=F32) for h in range(nheads)]
    for h in range(nheads):
        m_ref[h] = m_new[h]
        acc_ref[arows[h], :] = alpha[h] * acc_ref[arows[h], :] + pv[h]

    @pl.when(kb == kb_last)
    def _():
        for h in range(nheads):
            a0 = h * hda
            out_t = acc_ref[a0:a0 + hd, :] / acc_ref[a0 + hd:a0 + hd + 1, :]
            o_ref[:, hrows[h]] = out_t.T.astype(o_ref.dtype)


def _masked_attention_t(qt, k, vt, bias_t, tq=512, tk=1024):
    bsz, width, seq = qt.shape
    nheads = width // A_HEAD_DIM
    hda = A_HEAD_DIM + ONES_ROWS
    tq = min(tq, seq)
    tk = min(tk, seq)
    nq = seq // tq
    pairs = [(i, j) for i in range(nq) for j in range(((i + 1) * tq - 1) // tk + 1)]
    qidx = jnp.asarray([pq for pq, _ in pairs], I32)
    kidx = jnp.asarray([pk for _, pk in pairs], I32)
    grid_spec = pltpu.PrefetchScalarGridSpec(
        num_scalar_prefetch=2,
        grid=(bsz, len(pairs)),
        in_specs=[pl.BlockSpec((None, width, tq), lambda b, p, qi, ki: (b, 0, qi[p])),
                  pl.BlockSpec((None, tk, width), lambda b, p, qi, ki: (b, ki[p], 0)),
                  pl.BlockSpec((None, nheads * hda, tk), lambda b, p, qi, ki: (b, 0, ki[p])),
                  pl.BlockSpec((None, tk, tq), lambda b, p, qi, ki: (b, ki[p], qi[p]))],
        out_specs=pl.BlockSpec((None, tq, width), lambda b, p, qi, ki: (b, qi[p], 0)),
        scratch_shapes=[pltpu.VMEM((nheads, 1, tq), F32),
                        pltpu.VMEM((nheads * hda, tq), F32)],
    )
    return pl.pallas_call(
        functools.partial(_attn_t_kernel, nheads=nheads, hd=A_HEAD_DIM, hda=hda, tq=tq, tk=tk),
        grid_spec=grid_spec,
        out_shape=jax.ShapeDtypeStruct((bsz, seq, width), BF16),
        compiler_params=_cparams(("parallel", "arbitrary")),
        name="masked_attention",
    )(qidx, kidx, qt, k, vt, bias_t)


def _retention_kernel(qk_ref, v_ref, g_ref, o_ref, state_ref, *, tc, dk, dv):
    @pl.when(pl.program_id(1) == 0)
    def _():
        state_ref[...] = jnp.zeros(state_ref.shape, F32)

    row = lax.broadcasted_iota(I32, (tc, tc), 0)
    col = lax.broadcasted_iota(I32, (tc, tc), 1)
    diff = (row - col).astype(F32)
    pos = lax.broadcasted_iota(I32, (tc, 1), 0).astype(F32)
    for h in range(R_HEADS):
        log_g = math.log1p(-(2.0 ** (-5.0 - h)))
        q = qk_ref[:, h * dk:(h + 1) * dk]
        k = qk_ref[:, (R_HEADS + h) * dk:(R_HEADS + h + 1) * dk]
        v = v_ref[:, h * dv:(h + 1) * dv]
        decay = jnp.where(diff >= 0, jnp.exp(jnp.maximum(diff, 0.0) * log_g), 0.0)
        s = lax.dot_general(q, k, (((1,), (1,)), ((), ())), preferred_element_type=F32) * decay
        intra = jnp.dot(s.astype(BF16), v, preferred_element_type=F32)
        xi = jnp.exp((pos + 1.0) * log_g)
        zeta = jnp.exp((tc - 1.0 - pos) * log_g)
        state = state_ref[h]
        cross = jnp.dot((q.astype(F32) * xi).astype(BF16), state.astype(BF16), preferred_element_type=F32)
        kz = (k.astype(F32) * zeta).astype(BF16)
        kv = lax.dot_general(kz, v, (((0,), (0,)), ((), ())), preferred_element_type=F32)
        state_ref[h] = state * math.exp(tc * log_g) + kv
        ret = intra + cross
        mu = jnp.mean(ret, axis=1, keepdims=True)
        d = ret - mu
        var = jnp.mean(d * d, axis=1, keepdims=True)
        gate = g_ref[:, h * dv:(h + 1) * dv]
        gate = gate / (1.0 + jnp.exp(-gate))
        o_ref[:, h * dv:(h + 1) * dv] = (d * lax.rsqrt(var + 1e-5) * gate).astype(o_ref.dtype)


def _retention(bqk, bv, bg, tc=256):
    bsz, seq, w2 = bqk.shape
    dk = w2 // (2 * R_HEADS)
    dv = bv.shape[2] // R_HEADS
    tc = min(tc, seq)
    return pl.pallas_call(
        functools.partial(_retention_kernel, tc=tc, dk=dk, dv=dv),
        grid=(bsz, seq // tc),
        in_specs=[pl.BlockSpec((None, tc, w2), lambda b, c: (b, c, 0)),
                  pl.BlockSpec((None, tc, R_HEADS * dv), lambda b, c: (b, c, 0)),
                  pl.BlockSpec((None, tc, R_HEADS * dv), lambda b, c: (b, c, 0))],
        out_specs=pl.BlockSpec((None, tc, R_HEADS * dv), lambda b, c: (b, c, 0)),
        out_shape=jax.ShapeDtypeStruct((bsz, seq, R_HEADS * dv), BF16),
        scratch_shapes=[pltpu.VMEM((R_HEADS, dk, dv), F32)],
        compiler_params=_cparams(("parallel", "arbitrary")),
        name="retention",
    )(bqk, bv, bg)


def _head_sum_matrix(width, hd):
    r = jnp.arange(width)
    return (r[:, None] // hd == r[None, :] // hd).astype(BF16)


def _sigmoid(x):
    return 1.0 / (1.0 + jnp.exp(-x))


def _rwkv_prep_kernel(*refs, cw, has_vres):
    if has_vres:
        (c_ref, mu_ref, vec_ref, wlb_ref, alb_ref, glb_ref, hs_ref, vlb_ref, vfirst_ref,
         r_o, lw_o, k_o, v_o, al_o, be_o, g_o, bo_o, carry_ref) = refs
    else:
        (c_ref, mu_ref, vec_ref, wlb_ref, alb_ref, glb_ref, hs_ref,
         r_o, lw_o, k_o, v_o, al_o, be_o, g_o, bo_o, carry_ref) = refs
    tr = c_ref.shape[0]

    @pl.when(pl.program_id(1) == 0)
    def _():
        carry_ref[...] = jnp.zeros(carry_ref.shape, F32)

    c = c_ref[...]
    row = lax.broadcasted_iota(I32, (tr, 1), 0)
    prev = jnp.where(row == 0, carry_ref[0:1, :], pltpu.roll(c, 1, axis=0))
    carry_ref[0:1, :] = c[tr - 1:tr, :]
    cs = c + (prev - c) * mu_ref[...]

    r = cs[:, 0:cw]
    k = cs[:, cw:2 * cw]
    v = cs[:, 2 * cw:3 * cw]
    o = 3 * cw
    wl = cs[:, o:o + LANE]
    al = cs[:, o + LANE:o + 2 * LANE]
    gl = cs[:, o + 2 * LANE:o + 2 * LANE + GATE_LORA]
    w0, a0, k_k, k_a, r_k, v0 = (vec_ref[i:i + 1, :] for i in range(6))

    z = -(w0 + jnp.dot(jnp.tanh(wl).astype(BF16), wlb_ref[...], preferred_element_type=F32))
    softplus = jnp.maximum(z, 0.0) + jnp.log(1.0 + jnp.exp(-jnp.abs(z)))
    lw = -jnp.exp(-softplus - 0.5)
    a = _sigmoid(a0 + jnp.dot(al.astype(BF16), alb_ref[...], preferred_element_type=F32))
    g = jnp.dot(_sigmoid(gl).astype(BF16), glb_ref[...], preferred_element_type=F32)
    if has_vres:
        vr = cs[:, o + 2 * LANE + GATE_LORA:o + 2 * LANE + GATE_LORA + vlb_ref.shape[0]]
        mix = _sigmoid(v0 + jnp.dot(vr.astype(BF16), vlb_ref[...], preferred_element_type=F32))
        v = v + (vfirst_ref[...] - v) * mix
    hs = hs_ref[...]
    kk = k * k_k
    ss = _dot_sel(kk * kk, hs)
    kk = kk / jnp.maximum(jnp.sqrt(ss), 1e-12)
    kh = k * (1.0 + (a - 1.0) * k_a)
    rk = _dot_sel(r * kh * r_k, hs)

    r_o[...] = r
    lw_o[...] = lw
    k_o[...] = kh
    v_o[...] = v
    al_o[...] = -kk
    be_o[...] = kk * a
    g_o[...] = g
    bo_o[...] = rk * v


def _rwkv_prep(cproj, mu, vecs, wlb, alb, glb, vlb, v_first, cw, tr=256):
    bsz, seq, wc = cproj.shape
    tr = min(tr, seq)
    has_vres = vlb is not None
    hs = _head_sum_matrix(cw, W_HEAD_DIM)
    full = lambda a: pl.BlockSpec(a.shape, lambda b, t: (0,) * a.ndim)
    tok = pl.BlockSpec((None, tr, cw), lambda b, t: (b, t, 0))
    ins = [cproj, mu, vecs, wlb, alb, glb, hs]
    in_specs = [pl.BlockSpec((None, tr, wc), lambda b, t: (b, t, 0)),
                full(mu), full(vecs), full(wlb), full(alb), full(glb), full(hs)]
    if has_vres:
        ins += [vlb, v_first]
        in_specs += [full(vlb), tok]
    out = jax.ShapeDtypeStruct((bsz, seq, cw), F32)
    return pl.pallas_call(
        functools.partial(_rwkv_prep_kernel, cw=cw, has_vres=has_vres),
        grid=(bsz, seq // tr),
        in_specs=in_specs,
        out_specs=[tok] * 8,
        out_shape=[out] * 8,
        scratch_shapes=[pltpu.VMEM((8, wc), F32)],
        compiler_params=_cparams(("parallel", "arbitrary")),
        name="rwkv_prep",
    )(*ins)


def _split_bf16(x):
    hi = x.astype(BF16)
    return hi, (x - hi.astype(F32)).astype(BF16)


def _dot_x3(a, b):
    ah, al = _split_bf16(a)
    bh, bl = _split_bf16(b)
    return (jnp.dot(ah, bh, preferred_element_type=F32) + jnp.dot(ah, bl, preferred_element_type=F32)
            + jnp.dot(al, bh, preferred_element_type=F32))


def _dot_sel(a, sel):
    ah, al = _split_bf16(a)
    return jnp.dot(ah, sel, preferred_element_type=F32) + jnp.dot(al, sel, preferred_element_type=F32)


def _bdot(a, b):
    return jnp.dot(a.astype(BF16), b.astype(BF16), preferred_element_type=F32)


def _bdot_tn(a, b):
    return lax.dot_general(a.astype(BF16), b.astype(BF16), (((0,), (0,)), ((), ())),
                           preferred_element_type=F32)


def _rwkv_chunk_kernel(r_ref, lw_ref, k_ref, v_ref, al_ref, be_ref, g_o, y0_o, m_o, z0_o, *, tc, nck, npairs):
    row = lax.broadcasted_iota(I32, (tc, tc), 0)
    col = lax.broadcasted_iota(I32, (tc, tc), 1)
    incl = row >= col
    strict = row > col
    tri = incl.astype(BF16)
    eye = (row == col).astype(F32)
    lane = lax.broadcasted_iota(I32, (1, LANE), 1)
    mh0 = (lane // W_HEAD_DIM) == 0
    prow = lax.broadcasted_iota(I32, (LANE, LANE), 0)
    pcol = lax.broadcasted_iota(I32, (LANE, LANE), 1)
    same_head = (prow // W_HEAD_DIM) == (pcol // W_HEAD_DIM)

    lw_all = lw_ref[...]
    hi = lw_all.astype(BF16)
    rem = lw_all - hi.astype(F32)
    mid = rem.astype(BF16)
    lo = (rem - mid.astype(F32)).astype(BF16)
    cum_all = jnp.concatenate(
        [sum(jnp.dot(tri, part[c * tc:(c + 1) * tc], preferred_element_type=F32) for part in (hi, mid, lo))
         for c in range(nck)], axis=0)

    pairs = range(nck * npairs)
    heads = [(p, h) for p in pairs for h in range(2)]
    sls = [slice((p % npairs) * LANE, (p % npairs + 1) * LANE) for p in pairs]
    rws = [slice((p // npairs) * tc, (p // npairs + 1) * tc) for p in pairs]
    cum = [cum_all[rws[p], sls[p]] for p in pairs]
    tot = [c[tc - 1:tc, :] for c in cum]
    p_inv = [jnp.exp(-cum[p]) for p in pairs]
    p_end = [jnp.exp(tot[p] - cum[p]) for p in pairs]
    at = [al_ref[rws[p], sls[p]] * jnp.exp(cum[p] - lw_all[rws[p], sls[p]]) for p in pairs]
    rt = [r_ref[rws[p], sls[p]] * jnp.exp(cum[p]) for p in pairs]
    bh = [be_ref[rws[p], sls[p]] * p_end[p] for p in pairs]
    khat = [k_ref[rws[p], sls[p]] * p_end[p] for p in pairs]
    v = [v_ref[rws[p], sls[p]] for p in pairs]
    rhs = [jnp.concatenate([be_ref[rws[p], sls[p]] * p_inv[p], k_ref[rws[p], sls[p]] * p_inv[p]],
                           axis=0).astype(BF16) for p in pairs]

    a_ab, a_ak, qcat = {}, {}, {}
    for p, h in heads:
        mh = (lane // W_HEAD_DIM) == h
        lhs = jnp.concatenate([jnp.where(mh, at[p], 0.0), jnp.where(mh, rt[p], 0.0)], axis=0).astype(BF16)
        x = lax.dot_general(lhs, rhs[p], (((1,), (1,)), ((), ())), preferred_element_type=F32)
        a_ab[p, h] = jnp.where(strict, x[:tc, :tc], 0.0)
        a_ak[p, h] = jnp.where(strict, x[:tc, tc:], 0.0)
        qcat[p, h] = jnp.concatenate([jnp.where(incl, x[tc:, tc:], 0.0),
                                      jnp.where(incl, x[tc:, :tc], 0.0)], axis=1)

    t_inv = {ph: eye + a_ab[ph] for ph in heads}
    pw = {ph: _bdot(a_ab[ph], a_ab[ph]) for ph in heads}
    akv = {(p, h): _bdot(a_ak[p, h], v[p]) for p, h in heads}
    n = 2
    while n < tc:
        for ph in heads:
            res = _bdot(jnp.concatenate([pw[ph], t_inv[ph]], axis=0), pw[ph])
            t_inv[ph] = t_inv[ph] + res[tc:]
            pw[ph] = res[:tc]
        n *= 2
    tw = {(p, h): _bdot(t_inv[p, h], jnp.concatenate([at[p], akv[p, h]], axis=1)) for p, h in heads}

    zero = jnp.zeros((tc, LANE), F32)
    w = [jnp.where(mh0, tw[p, 0][:, :LANE], tw[p, 1][:, :LANE]) for p in pairs]
    u0 = [jnp.where(mh0, tw[p, 0][:, LANE:], tw[p, 1][:, LANE:]) for p in pairs]
    vu = [jnp.concatenate([jnp.concatenate([v[p], zero], axis=1),
                           jnp.concatenate([u0[p], w[p]], axis=1)], axis=0) for p in pairs]
    yg = {(p, h): _bdot(qcat[p, h], vu[p]) for p, h in heads}
    m_mat = [_bdot_tn(bh[p], w[p]) for p in pairs]
    z0 = [_bdot_tn(jnp.concatenate([bh[p], khat[p]], axis=0), jnp.concatenate([u0[p], v[p]], axis=0))
          for p in pairs]
    for p in pairs:
        y0_o[rws[p], sls[p]] = jnp.where(mh0, yg[p, 0][:, :LANE], yg[p, 1][:, :LANE])
        g_o[rws[p], sls[p]] = rt[p] + jnp.where(mh0, yg[p, 0][:, LANE:], yg[p, 1][:, LANE:])
        m_o[p // npairs, p % npairs] = jnp.where(same_head, m_mat[p], 0.0) + jnp.where(
            prow == pcol, jnp.broadcast_to(jnp.exp(tot[p]), (LANE, LANE)), 0.0)
        z0_o[p // npairs, p % npairs] = jnp.where(same_head, z0[p], 0.0)


def _rwkv_chunk_ops(r, lw, kh, v, alpha, beta, tc=CHUNK, nck=4):
    bsz, seq, cw = r.shape
    npairs = cw // LANE
    nc = seq // tc
    tok = pl.BlockSpec((None, nck * tc, cw), lambda b, c: (b, c, 0))
    mat = pl.BlockSpec((None, nck, npairs, LANE, LANE), lambda b, c: (b, c, 0, 0, 0))
    tok_shape = jax.ShapeDtypeStruct((bsz, seq, cw), F32)
    mat_shape = jax.ShapeDtypeStruct((bsz, nc, npairs, LANE, LANE), F32)
    return pl.pallas_call(
        functools.partial(_rwkv_chunk_kernel, tc=tc, nck=nck, npairs=npairs),
        grid=(bsz, nc // nck),
        in_specs=[tok] * 6,
        out_specs=[tok, tok, mat, mat],
        out_shape=[tok_shape, tok_shape, mat_shape, mat_shape],
        compiler_params=_cparams(("parallel", "parallel")),
        name="rwkv_chunk_ops",
    )(r, lw, kh, v, alpha, beta)


def _rwkv_scan_kernel(g_ref, y0_ref, m_ref, z0_ref, bo_ref, gate_ref, ln_ref, hs_ref, o_ref, state_ref,
                      *, tc, nch, npairs):
    @pl.when(pl.program_id(1) == 0)
    def _():
        state_ref[...] = jnp.zeros(state_ref.shape, F32)

    hs = hs_ref[...]
    inv_n = 1.0 / W_HEAD_DIM
    sls = [slice(p * LANE, (p + 1) * LANE) for p in range(npairs)]
    st = [state_ref[p] for p in range(npairs)]
    for ch in range(nch):
        rows = slice(ch * tc, (ch + 1) * tc)
        gm = [_dot_x3(jnp.concatenate([g_ref[rows, sls[p]], m_ref[ch, p]], axis=0), st[p]) for p in range(npairs)]
        y = [gm[p][:tc] + y0_ref[rows, sls[p]] for p in range(npairs)]
        st = [gm[p][tc:] + z0_ref[ch, p] for p in range(npairs)]
        mu = [_dot_sel(y[p], hs) * inv_n for p in range(npairs)]
        d = [y[p] - mu[p] for p in range(npairs)]
        var = [_dot_sel(d[p] * d[p], hs) * inv_n for p in range(npairs)]
        for p in range(npairs):
            yn = d[p] * lax.rsqrt(var[p] + LNX_EPS) * ln_ref[0:1, sls[p]] + ln_ref[1:2, sls[p]]
            o_ref[rows, sls[p]] = ((yn + bo_ref[rows, sls[p]]) * gate_ref[rows, sls[p]]).astype(o_ref.dtype)
    for p in range(npairs):
        state_ref[p] = st[p]


def _rwkv_scan(g, y0, m, z0, bonus, gate, ln, tc=CHUNK, nch=4):
    bsz, seq, cw = g.shape
    npairs = cw // LANE
    nch = min(nch, seq // tc)
    hs = _head_sum_matrix(LANE, W_HEAD_DIM)
    tok = pl.BlockSpec((None, tc * nch, cw), lambda b, c: (b, c, 0))
    mat = pl.BlockSpec((None, nch, npairs, LANE, LANE), lambda b, c: (b, c, 0, 0, 0))
    return pl.pallas_call(
        functools.partial(_rwkv_scan_kernel, tc=tc, nch=nch, npairs=npairs),
        grid=(bsz, seq // (tc * nch)),
        in_specs=[tok, tok, mat, mat, tok, tok,
                  pl.BlockSpec(ln.shape, lambda b, c: (0, 0)),
                  pl.BlockSpec(hs.shape, lambda b, c: (0, 0))],
        out_specs=tok,
        out_shape=jax.ShapeDtypeStruct((bsz, seq, cw), BF16),
        scratch_shapes=[pltpu.VMEM((npairs, LANE, LANE), F32)],
        compiler_params=_cparams(("parallel", "arbitrary")),
        name="rwkv_scan",
    )(g, y0, m, z0, bonus, gate, ln, hs)


def _out_proj_kernel(x_ref, a_ref, b_ref, c_ref, w_ref, o_ref, *, wa, wb):
    acc = jnp.dot(a_ref[...], w_ref[0:wa, :], preferred_element_type=F32)
    acc = acc + jnp.dot(b_ref[...], w_ref[wa:wa + wb, :], preferred_element_type=F32)
    acc = acc + jnp.dot(c_ref[...], w_ref[wa + wb:, :], preferred_element_type=F32)
    o_ref[...] = x_ref[...] + acc


def _out_proj(x2d, oa, ob, oc, w, tm=1024, tn=1024):
    m, d = x2d.shape
    wa, wb, wc = oa.shape[1], ob.shape[1], oc.shape[1]
    tn = min(tn, d)
    return pl.pallas_call(
        functools.partial(_out_proj_kernel, wa=wa, wb=wb),
        grid=(m // tm, d // tn),
        in_specs=[pl.BlockSpec((tm, tn), lambda i, j: (i, j)),
                  pl.BlockSpec((tm, wa), lambda i, j: (i, 0)),
                  pl.BlockSpec((tm, wb), lambda i, j: (i, 0)),
                  pl.BlockSpec((tm, wc), lambda i, j: (i, 0)),
                  pl.BlockSpec((wa + wb + wc, tn), lambda i, j: (0, j))],
        out_specs=pl.BlockSpec((tm, tn), lambda i, j: (i, j)),
        out_shape=jax.ShapeDtypeStruct((m, d), F32),
        compiler_params=_cparams(("parallel", "arbitrary")),
        name="out_proj",
    )(x2d, oa, ob, oc, w)


def _gate_up_kernel(h_ref, wg_ref, wu_ref, o_ref):
    h = h_ref[...]
    gate = jnp.dot(h, wg_ref[...], preferred_element_type=F32)
    up = jnp.dot(h, wu_ref[...], preferred_element_type=F32)
    o_ref[...] = (gate / (1.0 + jnp.exp(-gate)) * up).astype(o_ref.dtype)


def _gate_up(h, w_gate_up, tm=1024, tf_cap=512):
    m, d = h.shape
    dff = w_gate_up.shape[1] // 2
    tf = _pick_tile(dff, tf_cap)
    nf = dff // tf
    return pl.pallas_call(
        _gate_up_kernel,
        grid=(m // tm, nf),
        in_specs=[pl.BlockSpec((tm, d), lambda i, j: (i, 0)),
                  pl.BlockSpec((d, tf), lambda i, j: (0, j)),
                  pl.BlockSpec((d, tf), lambda i, j: (0, j + nf))],
        out_specs=pl.BlockSpec((tm, tf), lambda i, j: (i, j)),
        out_shape=jax.ShapeDtypeStruct((m, dff), BF16),
        compiler_params=_cparams(("parallel", "arbitrary")),
        name="ffn_gate_up",
    )(h, w_gate_up, w_gate_up)


def _down_kernel(x_ref, a_ref, w_ref, o_ref):
    o_ref[...] = x_ref[...] + jnp.dot(a_ref[...], w_ref[...], preferred_element_type=F32)


def _down_proj(x2d, act, w, tm=1024, tn=512):
    m, d = x2d.shape
    dff = act.shape[1]
    tn = min(tn, d)
    return pl.pallas_call(
        _down_kernel,
        grid=(m // tm, d // tn),
        in_specs=[pl.BlockSpec((tm, tn), lambda i, j: (i, j)),
                  pl.BlockSpec((tm, dff), lambda i, j: (i, 0)),
                  pl.BlockSpec((dff, tn), lambda i, j: (0, j))],
        out_specs=pl.BlockSpec((tm, tn), lambda i, j: (i, j)),
        out_shape=jax.ShapeDtypeStruct((m, d), F32),
        compiler_params=_cparams(("parallel", "arbitrary")),
        name="ffn_down",
    )(x2d, act, w)


def _pad_cols(w, width):
    return jnp.pad(w, ((0, 0), (0, width - w.shape[1])))


def _pad_rows(w, height):
    return jnp.pad(w, ((0, height - w.shape[0]), (0, 0)))


def _pad_vec(v, width):
    return jnp.pad(v, (0, width - v.shape[0]))


def kernel(x, norm_mix_g, w_in, w_in_vres, rwkv_mu, rwkv_mu_vres, rwkv_w0, rwkv_w_lora_b, rwkv_a0,
           rwkv_a_lora_b, rwkv_v0, rwkv_v_lora_b, rwkv_g_lora_b, rwkv_k_k, rwkv_k_a, rwkv_r_k,
           rwkv_lnx_g, rwkv_lnx_b, w_out, norm_ffn_g, w_gate_up, w_down, final_norm_g):
    bsz, seq, d_model = x.shape
    depth = w_in.shape[0]
    m = bsz * seq
    d_mix = w_out.shape[1]
    a_w = d_mix // 2
    qi_w = IDX_HEADS * IDX_DIM
    b_v_w = d_mix // 4
    b_qk_w = b_v_w // 2
    c_w = d_mix // 4
    r_qk_dim = b_qk_w // R_HEADS
    top_k = min(TOPK_MAX, seq // 4)
    vres_pad = 2 * LANE

    o_q, o_k, o_v = 0, a_w, 2 * a_w
    o_qi = 3 * a_w
    o_ki = o_qi + qi_w
    o_wi = o_ki + IDX_DIM
    o_bq = o_wi + IDX_HEADS
    o_bk = o_bq + b_qk_w
    o_bv = o_bk + b_qk_w
    o_bg = o_bv + b_v_w
    o_c = o_bg + b_v_w
    o_wl = o_c + 3 * c_w
    o_al = o_wl + DECAY_LORA
    o_gl = o_al + AAA_LORA
    n_in = o_gl + GATE_LORA

    a_rot = A_HEAD_DIM // ROPE_FRAC
    tq_ = _rope_tables(seq, A_HEAD_DIM, a_rot, ROPE_THETA, scale=A_HEAD_DIM ** -0.5 * math.log2(math.e))
    tk_ = _rope_tables(seq, A_HEAD_DIM, a_rot, ROPE_THETA)
    tab_q = tuple(t[None] for t in tq_)
    tab_k = tuple(t[None] for t in tk_)
    i_rot = IDX_DIM // ROPE_FRAC
    tab_qi = tuple(t[None] for t in _rope_tables(seq, IDX_DIM, i_rot, ROPE_THETA))
    lane = jnp.arange(LANE)
    kiwi_pass = jnp.where(lane < IDX_DIM, 1.0, IDX_W_SCALE).astype(F32)
    c_kw, s1_kw, s2_kw = _rope_tables(seq, LANE, i_rot, ROPE_THETA, pass_scale=kiwi_pass)
    tab_kiwi = (c_kw[None], s1_kw[None], s2_kw[None])
    tbq = _rope_tables(seq, r_qk_dim, r_qk_dim, R_THETA)
    tbk = _rope_tables(seq, r_qk_dim, r_qk_dim, R_THETA, scale=r_qk_dim ** -0.5)
    tab_bqk = tuple(jnp.stack([a, b]) for a, b in zip(tbq, tbk))

    x2d = x.reshape(m, d_model)
    v_first = None
    for l in range(depth):
        wl_ = w_in[l]
        cols = lambda o, n: wl_[:, o:o + n]
        w_q = cols(o_q, a_w).astype(BF16)
        w_k = cols(o_k, a_w).astype(BF16)
        w_v = cols(o_v, a_w).astype(BF16)
        w_bv = cols(o_bv, b_v_w).astype(BF16)
        w_qi = cols(o_qi, qi_w).astype(BF16)
        w_kiwi = _pad_cols(cols(o_ki, IDX_DIM + IDX_HEADS), LANE).astype(BF16)
        w_bqk = cols(o_bq, 2 * b_qk_w).astype(BF16)
        w_bg = cols(o_bg, b_v_w).astype(BF16)
        vres_w = (w_in_vres[l - 1] if l > 0 else jnp.zeros((d_model, MV_LORA), F32))
        w_c = jnp.concatenate([cols(o_c, 3 * c_w), _pad_cols(cols(o_wl, DECAY_LORA), LANE),
                               _pad_cols(cols(o_al, AAA_LORA), LANE), cols(o_gl, GATE_LORA),
                               _pad_cols(vres_w, vres_pad)], axis=1).astype(BF16)
        mu = rwkv_mu[l]
        mu_vres = rwkv_mu_vres[l - 1] if l > 0 else jnp.zeros((MV_LORA,), F32)
        mu_c = jnp.concatenate([mu[:3 * c_w], _pad_vec(mu[3 * c_w:3 * c_w + DECAY_LORA], LANE),
                                _pad_vec(mu[3 * c_w + DECAY_LORA:3 * c_w + DECAY_LORA + AAA_LORA], LANE),
                                mu[3 * c_w + DECAY_LORA + AAA_LORA:], _pad_vec(mu_vres, vres_pad)])[None, :]

        h = _rmsnorm(x2d, norm_mix_g[l], BF16)
        qt = _mm_rope(h, w_q, tab_q, a_rot // 2, seq, a_w, BF16, transpose=True)
        ak = _mm_rope(h, w_k, tab_k, a_rot // 2, seq, a_w, BF16).reshape(bsz, seq, a_w)
        vt = _mm_plain(h, w_v, BF16, seq, transpose=True, pad_rows=ONES_ROWS)
        bv = _mm_plain(h, w_bv, BF16, seq, tn_cap=512).reshape(bsz, seq, b_v_w)
        qit = _mm_rope(h, w_qi, tab_qi, i_rot // 2, seq, qi_w, BF16, transpose=True)
        kiwi = _mm_rope(h, w_kiwi, tab_kiwi, i_rot // 2, seq, LANE, F32).reshape(bsz, seq, LANE)
        bqk = _mm_rope(h, w_bqk, tab_bqk, r_qk_dim // 2, seq, b_qk_w, BF16).reshape(bsz, seq, 2 * b_qk_w)
        bg = _mm_plain(h, w_bg, F32, seq).reshape(bsz, seq, b_v_w)
        cproj = _mm_plain(h, w_c, F32, seq, tn_cap=768).reshape(bsz, seq, w_c.shape[1])

        ki = kiwi[:, :, :IDX_DIM].astype(BF16)
        wit = jnp.swapaxes(kiwi[:, :, IDX_DIM:IDX_DIM + IDX_HEADS], 1, 2)
        bias_t = _indexer_bias_t(ki, qit, wit, top_k)
        out_a = _masked_attention_t(qt, ak, vt, bias_t)

        out_b = _retention(bqk, bv, bg)

        vecs = jnp.stack([rwkv_w0[l], rwkv_a0[l], rwkv_k_k[l], rwkv_k_a[l], rwkv_r_k[l].reshape(-1),
                          rwkv_v0[l - 1] if l > 0 else jnp.zeros((c_w,), F32),
                          jnp.zeros((c_w,), F32), jnp.zeros((c_w,), F32)])
        wlb = _pad_rows(rwkv_w_lora_b[l], LANE).astype(BF16)
        alb = _pad_rows(rwkv_a_lora_b[l], LANE).astype(BF16)
        glb = rwkv_g_lora_b[l].astype(BF16)
        vlb = _pad_rows(rwkv_v_lora_b[l - 1], vres_pad).astype(BF16) if l > 0 else None
        r_, lw_, kh_, v_, al_, be_, g_, bo_ = _rwkv_prep(cproj, mu_c, vecs, wlb, alb, glb, vlb, v_first, c_w)
        if l == 0:
            v_first = v_
        gm, y0, mm, z0 = _rwkv_chunk_ops(r_, lw_, kh_, v_, al_, be_)
        ln = jnp.stack([rwkv_lnx_g[l], rwkv_lnx_b[l]] + [jnp.zeros((c_w,), F32)] * 6)
        out_c = _rwkv_scan(gm, y0, mm, z0, bo_, g_, ln)

        x2d = _out_proj(x2d, out_a.reshape(m, a_w), out_b.reshape(m, b_v_w), out_c.reshape(m, c_w),
                        w_out[l].astype(BF16))

        h = _rmsnorm(x2d, norm_ffn_g[l], BF16)
        act = _gate_up(h, w_gate_up[l].astype(BF16))
        x2d = _down_proj(x2d, act, w_down[l].astype(BF16))

    return _rmsnorm(x2d, final_norm_g, F32).reshape(bsz, seq, d_model)
```

```python
import functools
import math

import jax
import jax.numpy as jnp
from jax import lax
from jax.experimental import pallas as pl
from jax.experimental.pallas import tpu as pltpu

F32 = jnp.float32
BF16 = jnp.bfloat16
I32 = jnp.int32
I16 = jnp.int16

CHUNK = 64
A_HEAD_DIM = 128
IDX_HEADS = 16
IDX_DIM = 64
IDX_W_SCALE = (IDX_HEADS * IDX_DIM) ** -0.5
TOPK_MAX = 256
ROPE_THETA = 500000.0
ROPE_FRAC = 4
R_HEADS = 4
R_THETA = 10000.0
W_HEAD_DIM = 64
DECAY_LORA = 96
AAA_LORA = 96
MV_LORA = 64
GATE_LORA = 256
LNX_EPS = 64e-5
RMS_EPS = 1e-5

LANE = 128
SUBLANE = 8
PACKED_ROWS = 16
VMEM_LIMIT = 56 * 1024 * 1024
TM = 1024
TM_NORM = 512
IDX_TQ = 256
IDX_TKS = 128
IDX_TKC = 1024
SCORE_UNROLL = 4
COUNT_CHAINS = 8
ATTN_TQ = 512
ATTN_TK = 1024
ONES_ROWS = SUBLANE
RET_CHUNK = 256
PREP_ROWS = 256
RWKV_CHUNKS = 4

NEG_BIAS = -1e30
M_INIT = -1e20
INT_MIN = -2147483648
INT_MAX = 2147483647
I16_MIN = -32768


def _cparams(sem):
    return pltpu.CompilerParams(dimension_semantics=sem, vmem_limit_bytes=VMEM_LIMIT)


def _pick_tile(n, cap):
    best = LANE
    t = LANE
    while t <= min(n, cap):
        if n % t == 0:
            best = t
        t += LANE
    return best


def _rmsnorm_kernel(x_ref, g_ref, o_ref):
    x = x_ref[...]
    ms = jnp.mean(x * x, axis=-1, keepdims=True)
    o_ref[...] = (x * lax.rsqrt(ms + RMS_EPS) * g_ref[...]).astype(o_ref.dtype)


def _rmsnorm(x2d, g, out_dtype, tm=TM_NORM):
    m, d = x2d.shape
    return pl.pallas_call(
        _rmsnorm_kernel,
        grid=(m // tm,),
        in_specs=[pl.BlockSpec((tm, d), lambda i: (i, 0)),
                  pl.BlockSpec((1, d), lambda i: (0, 0))],
        out_specs=pl.BlockSpec((tm, d), lambda i: (i, 0)),
        out_shape=jax.ShapeDtypeStruct((m, d), out_dtype),
        compiler_params=_cparams(("parallel",)),
        name="rmsnorm",
    )(x2d, g.reshape(1, d).astype(F32))


def _store_cols(o_ref, blk, x, transpose, pad_rows):
    if transpose:
        r0 = blk * (LANE + pad_rows)
        o_ref[r0:r0 + LANE, :] = x.T.astype(o_ref.dtype)
        if pad_rows:
            o_ref[r0 + LANE:r0 + LANE + pad_rows, :] = jnp.ones((pad_rows, x.shape[0]), o_ref.dtype)
    else:
        o_ref[:, blk * LANE:(blk + 1) * LANE] = x.astype(o_ref.dtype)


def _proj_out_spec(m, n, tm, tn, seq, out_dtype, transpose, pad_rows):
    if not transpose:
        return pl.BlockSpec((tm, tn), lambda i, j: (i, j)), jax.ShapeDtypeStruct((m, n), out_dtype)
    tpb = seq // tm
    rows = lambda cols: cols // LANE * (LANE + pad_rows)
    return (pl.BlockSpec((None, rows(tn), tm), lambda i, j: (i // tpb, j, i % tpb)),
            jax.ShapeDtypeStruct((m // seq, rows(n), seq), out_dtype))


def _mm_plain_kernel(a_ref, w_ref, o_ref, *, transpose, pad_rows):
    acc = jnp.dot(a_ref[...], w_ref[...], preferred_element_type=F32)
    if not transpose:
        o_ref[...] = acc.astype(o_ref.dtype)
        return
    for blk in range(acc.shape[1] // LANE):
        _store_cols(o_ref, blk, acc[:, blk * LANE:(blk + 1) * LANE], transpose, pad_rows)


def _mm_plain(a, w, out_dtype, seq, tm=TM, tn_cap=1024, transpose=False, pad_rows=0):
    m, k = a.shape
    n = w.shape[1]
    tm = min(tm, seq)
    tn = _pick_tile(n, tn_cap)
    out_spec, out_shape = _proj_out_spec(m, n, tm, tn, seq, out_dtype, transpose, pad_rows)
    return pl.pallas_call(
        functools.partial(_mm_plain_kernel, transpose=transpose, pad_rows=pad_rows),
        grid=(m // tm, n // tn),
        in_specs=[pl.BlockSpec((tm, k), lambda i, j: (i, 0)),
                  pl.BlockSpec((k, tn), lambda i, j: (0, j))],
        out_specs=out_spec,
        out_shape=out_shape,
        compiler_params=_cparams(("parallel", "arbitrary")),
        name="proj_plain",
    )(a, w)


def _mm_rope_kernel(a_ref, w_ref, c_ref, s1_ref, s2_ref, o_ref, *, half, transpose):
    acc = jnp.dot(a_ref[...], w_ref[...], preferred_element_type=F32)
    c, s1, s2 = c_ref[...], s1_ref[...], s2_ref[...]
    for blk in range(acc.shape[1] // LANE):
        x = acc[:, blk * LANE:(blk + 1) * LANE]
        up = pltpu.roll(x, LANE - half, axis=1)
        dn = pltpu.roll(x, half, axis=1)
        _store_cols(o_ref, blk, x * c + up * s1 + dn * s2, transpose, 0)


def _mm_rope(a, w, tables, half, seq, tn, out_dtype, tm=TM, transpose=False):
    m, k = a.shape
    n = w.shape[1]
    tm = min(tm, seq)
    tpb = seq // tm
    tab_spec = pl.BlockSpec((None, tm, LANE), lambda i, j: (j, i % tpb, 0))
    out_spec, out_shape = _proj_out_spec(m, n, tm, tn, seq, out_dtype, transpose, 0)
    return pl.pallas_call(
        functools.partial(_mm_rope_kernel, half=half, transpose=transpose),
        grid=(m // tm, n // tn),
        in_specs=[pl.BlockSpec((tm, k), lambda i, j: (i, 0)),
                  pl.BlockSpec((k, tn), lambda i, j: (0, j)),
                  tab_spec, tab_spec, tab_spec],
        out_specs=out_spec,
        out_shape=out_shape,
        compiler_params=_cparams(("parallel", "arbitrary")),
        name="proj_rope",
    )(a, w, *tables)


def _rope_tables(seq, group, rot_dim, theta, scale=1.0, pass_scale=None):
    half = rot_dim // 2
    freqs = jnp.power(F32(theta), -jnp.arange(half, dtype=F32) / half)
    ang = jnp.arange(seq, dtype=F32)[:, None] * freqs[None, :]
    cos, sin = jnp.cos(ang), jnp.sin(ang)
    lane = jnp.arange(LANE) % group
    idx = lane % half
    cosl, sinl = cos[:, idx], sin[:, idx]
    passv = jnp.ones((LANE,), F32) if pass_scale is None else pass_scale
    c = jnp.where(lane < rot_dim, cosl, passv[None, :])
    s1 = jnp.where(lane < half, -sinl, 0.0)
    s2 = jnp.where((lane >= half) & (lane < rot_dim), sinl, 0.0)
    return c * scale, s1 * scale, s2 * scale


def _indexer_kernel(ki_ref, qit_ref, wit_ref, bias_ref, key_ref, k16_ref, *, tq, tks, tkc, top_k, seq):
    qb = pl.program_id(1)
    span = tks * SCORE_UNROLL
    nks = ((qb + 1) * tq + span - 1) // span * SCORE_UNROLL
    nkc = (nks * tks + tkc - 1) // tkc
    q_chunk = (qb * tq + lax.broadcasted_iota(I32, (1, tq), 1)) // CHUNK
    w = wit_ref[...]

    def score_block(kb, rmax):
        k0 = pl.multiple_of(kb * tks, tks)
        kblk = ki_ref[pl.ds(k0, tks), :]
        acc = jnp.zeros((tks, tq), F32)
        for h in range(IDX_HEADS):
            x = jnp.dot(kblk, qit_ref[h * IDX_DIM:(h + 1) * IDX_DIM, :], preferred_element_type=F32)
            acc = acc + jnp.maximum(x, 0.0) * w[h:h + 1, :]
        k_chunk = (k0 + lax.broadcasted_iota(I32, (tks, 1), 0)) // CHUNK
        bits = lax.bitcast_convert_type(acc, I32)
        key = bits ^ ((bits >> 31) & jnp.int32(INT_MAX))
        key = jnp.where(k_chunk <= q_chunk, key, jnp.int32(INT_MIN))
        key_ref[pl.ds(k0, tks), :] = key
        k16_ref[pl.ds(k0, tks), :] = (key >> 16).astype(I16)
        return jnp.maximum(rmax, jnp.max(key.reshape(tks // SUBLANE, SUBLANE, tq), axis=0))

    def score_step(kbu, rmax):
        for u in range(SCORE_UNROLL):
            rmax = score_block(kbu * SCORE_UNROLL + u, rmax)
        return rmax

    rmax = lax.fori_loop(0, nks // SCORE_UNROLL, score_step, jnp.full((SUBLANE, tq), INT_MIN, I32))
    colmax = jnp.max(rmax, axis=0, keepdims=True)

    def pad_block(kb, carry):
        k0 = pl.multiple_of(kb * tks, tks)
        key_ref[pl.ds(k0, tks), :] = jnp.full((tks, tq), INT_MIN, I32)
        k16_ref[pl.ds(k0, tks), :] = jnp.full((tks, tq), I16_MIN, I16)
        return carry

    lax.fori_loop(nks, nkc * (tkc // tks), pad_block, 0)

    def tree_sum(terms):
        while len(terms) > 1:
            terms = [a + b for a, b in zip(terms[0::2], terms[1::2])]
        return terms[0]

    def count_ge(cand):
        def count_block(kb, cnt):
            k0 = pl.multiple_of(kb * tkc, tkc)
            part = tkc // COUNT_CHAINS
            sums = []
            for g in range(COUNT_CHAINS):
                ind = jnp.where(key_ref[pl.ds(k0 + g * part, part), :] >= cand, 1, 0)
                sums.append(jnp.sum(ind.reshape(part // SUBLANE, SUBLANE, tq), axis=0))
            return cnt + tree_sum(sums)

        cnt = lax.fori_loop(0, nkc, count_block, jnp.zeros((SUBLANE, tq), I32))
        return jnp.sum(cnt, axis=0, keepdims=True)

    def count_ge16(cand):
        c16 = cand.astype(I16)
        one, zero = jnp.int16(1), jnp.int16(0)

        def count_block(kb, cnt):
            k0 = pl.multiple_of(kb * tkc, tkc)
            part = tkc // COUNT_CHAINS
            sums = []
            for g in range(COUNT_CHAINS):
                ind = jnp.where(k16_ref[pl.ds(k0 + g * part, part), :] >= c16, one, zero)
                acc = ind[0:PACKED_ROWS]
                for i in range(1, part // PACKED_ROWS):
                    acc = acc + ind[i * PACKED_ROWS:(i + 1) * PACKED_ROWS]
                sums.append(acc)
            return cnt + tree_sum(sums)

        cnt = lax.fori_loop(0, nkc, count_block, jnp.zeros((PACKED_ROWS, tq), I16))
        return jnp.sum(cnt.astype(I32), axis=0, keepdims=True)

    def probe_step(count_fn, state, mid):
        lo, hi, cnt_lo, cnt_hi = state
        cnt = count_fn(mid)
        ge = cnt >= top_k
        new_hi = jnp.where(cnt == top_k, mid + 1, jnp.where(ge, hi, mid))
        return (jnp.where(ge, mid, lo), new_hi, jnp.where(ge, cnt, cnt_lo), jnp.where(ge, cnt_hi, cnt))

    def midpoint(state):
        lo, hi = state[0], state[1]
        return (lo >> 1) + (hi >> 1) + (lo & hi & 1)

    def n_active(state):
        return jnp.max(((state[0] + 1) < state[1]).astype(I32))

    binade16 = 1 << 7

    def bisect(count_fn, state, floor=None):
        def next_mid(st, it):
            mid = midpoint(st)
            if floor is None:
                return mid
            step = jnp.int32(3 * binade16) << jnp.minimum(it, 7)
            return jnp.where(st[0] == floor, jnp.maximum(st[1] - step, mid), mid)

        def body(carry):
            it, _, st = carry
            st = probe_step(count_fn, st, next_mid(st, it + 1))
            st = probe_step(count_fn, st, next_mid(st, it + 2))
            return it + 2, n_active(st), st

        return lax.while_loop(lambda c: (c[1] > 0) & (c[0] < 40), body, (jnp.int32(0), n_active(state), state))[2]

    big = jnp.int32(1 << 30)
    top16 = colmax >> 16
    state = (jnp.full((1, tq), I16_MIN, I32), top16 + 1, jnp.full((1, tq), big, I32), jnp.zeros((1, tq), I32))
    first = jnp.where(top16 > jnp.int32(I16_MIN + 4 * binade16), top16 - jnp.int32(3 * binade16),
                      jnp.int32(I16_MIN + 1))
    state = bisect(count_ge16, probe_step(count_ge16, state, first), floor=jnp.int32(I16_MIN))
    lo16, hi16, cnt_lo, cnt_hi = state
    lo = lo16 << 16
    hi = jnp.where(cnt_lo == top_k, lo + 1, jnp.where(hi16 > -(I16_MIN + 1), jnp.int32(INT_MAX), hi16 << 16))
    lo, _, cnt_lo, cnt_hi = bisect(count_ge, (lo, hi, cnt_lo, cnt_hi))

    thr = jnp.maximum(lo, jnp.int32(INT_MIN + 1))
    tie_q = (cnt_lo > top_k) & (lo > jnp.int32(INT_MIN))
    any_tie = jnp.max(tie_q.astype(I32))

    @pl.when(any_tie == 0)
    def _():
        def write_block(kb, carry):
            k0 = pl.multiple_of(kb * tks, tks)
            bias_ref[pl.ds(k0, tks), :] = jnp.where(key_ref[pl.ds(k0, tks), :] >= thr, 0.0, NEG_BIAS)
            return carry

        lax.fori_loop(0, nks, write_block, 0)

    @pl.when(any_tie != 0)
    def _():
        need = jnp.where(tie_q, top_k - cnt_hi, big).astype(F32)
        lr = lax.broadcasted_iota(I32, (LANE, LANE), 0)
        lc = lax.broadcasted_iota(I32, (LANE, LANE), 1)
        lower = (lr >= lc).astype(BF16)

        def write_block(kb, run):
            k0 = pl.multiple_of(kb * LANE, LANE)
            blk = key_ref[pl.ds(k0, LANE), :]
            eq = blk == thr
            pc = jnp.dot(lower, jnp.where(eq, 1.0, 0.0).astype(BF16), preferred_element_type=F32)
            keep_eq = jnp.where((run + pc) <= need, 0.0, NEG_BIAS)
            bias_ref[pl.ds(k0, LANE), :] = jnp.where(blk > thr, 0.0, jnp.where(eq, keep_eq, NEG_BIAS))
            return run + pc[LANE - 1:LANE, :]

        lax.fori_loop(0, nks * (tks // LANE), write_block, jnp.zeros((1, tq), F32))

    def fill_block(kb, carry):
        k0 = pl.multiple_of(kb * tks, tks)
        bias_ref[pl.ds(k0, tks), :] = jnp.full((tks, tq), NEG_BIAS, F32)
        return carry

    lax.fori_loop(nks, seq // tks, fill_block, 0)


def _indexer_bias(ki, qit, wit, top_k, tq=IDX_TQ, tks=IDX_TKS, tkc=IDX_TKC):
    bsz, seq, _ = ki.shape
    tq = min(tq, seq)
    assert seq % tkc == 0 and tkc % (tks * SCORE_UNROLL) == 0 and tks % LANE == 0
    return pl.pallas_call(
        functools.partial(_indexer_kernel, tq=tq, tks=tks, tkc=tkc, top_k=top_k, seq=seq),
        grid=(bsz, seq // tq),
        in_specs=[pl.BlockSpec((None, seq, IDX_DIM), lambda b, i: (b, 0, 0)),
                  pl.BlockSpec((None, IDX_HEADS * IDX_DIM, tq), lambda b, i: (b, 0, i)),
                  pl.BlockSpec((None, IDX_HEADS, tq), lambda b, i: (b, 0, i))],
        out_specs=pl.BlockSpec((None, seq, tq), lambda b, i: (b, 0, i)),
        out_shape=jax.ShapeDtypeStruct((bsz, seq, seq), F32),
        scratch_shapes=[pltpu.VMEM((seq, tq), I32), pltpu.VMEM((seq, tq), I16)],
        compiler_params=_cparams(("parallel", "arbitrary")),
        name="indexer_select",
    )(ki, qit, wit)


def _attn_kernel(qidx_ref, kidx_ref, qt_ref, k_ref, vt_ref, b_ref, o_ref, m_ref, acc_ref,
                 *, nheads, hd, hda, tq, tk):
    p = pl.program_id(1)
    qb = qidx_ref[p]
    kb = kidx_ref[p]
    kb_last = ((qb + 1) * tq - 1) // tk

    @pl.when(kb == 0)
    def _():
        m_ref[...] = jnp.full(m_ref.shape, M_INIT, F32)
        acc_ref[...] = jnp.zeros(acc_ref.shape, F32)

    bias = b_ref[...]
    hrows = [slice(h * hd, (h + 1) * hd) for h in range(nheads)]
    arows = [slice(h * hda, (h + 1) * hda) for h in range(nheads)]
    s = [jnp.dot(k_ref[:, r], qt_ref[r, :], preferred_element_type=F32) + bias for r in hrows]
    m_old = [m_ref[h] for h in range(nheads)]
    m_new = [jnp.maximum(m_old[h], jnp.max(s[h], axis=0, keepdims=True)) for h in range(nheads)]
    alpha = [jnp.exp2(m_old[h] - m_new[h]) for h in range(nheads)]
    pr = [jnp.exp2(s[h] - m_new[h]).astype(BF16) for h in range(nheads)]
    pv = [jnp.dot(vt_ref[arows[h], :], pr[h], preferred_element_type=F32) for h in range(nheads)]
    for h in range(nheads):
        m_ref[h] = m_new[h]
        acc_ref[arows[h], :] = alpha[h] * acc_ref[arows[h], :] + pv[h]

    @pl.when(kb == kb_last)
    def _():
        for h in range(nheads):
            a0 = h * hda
            out_t = acc_ref[a0:a0 + hd, :] / acc_ref[a0 + hd:a0 + hd + 1, :]
            o_ref[:, hrows[h]] = out_t.T.astype(o_ref.dtype)


def _masked_attention(qt, k, vt, bias_t, tq=ATTN_TQ, tk=ATTN_TK):
    bsz, width, seq = qt.shape
    nheads = width // A_HEAD_DIM
    hda = A_HEAD_DIM + ONES_ROWS
    tq = min(tq, seq)
    tk = min(tk, seq)
    nq = seq // tq
    pairs = [(i, j) for i in range(nq) for j in range(((i + 1) * tq - 1) // tk + 1)]
    qidx = jnp.asarray([pq for pq, _ in pairs], I32)
    kidx = jnp.asarray([pk for _, pk in pairs], I32)
    grid_spec = pltpu.PrefetchScalarGridSpec(
        num_scalar_prefetch=2,
        grid=(bsz, len(pairs)),
        in_specs=[pl.BlockSpec((None, width, tq), lambda b, p, qi, ki: (b, 0, qi[p])),
                  pl.BlockSpec((None, tk, width), lambda b, p, qi, ki: (b, ki[p], 0)),
                  pl.BlockSpec((None, nheads * hda, tk), lambda b, p, qi, ki: (b, 0, ki[p])),
                  pl.BlockSpec((None, tk, tq), lambda b, p, qi, ki: (b, ki[p], qi[p]))],
        out_specs=pl.BlockSpec((None, tq, width), lambda b, p, qi, ki: (b, qi[p], 0)),
        scratch_shapes=[pltpu.VMEM((nheads, 1, tq), F32),
                        pltpu.VMEM((nheads * hda, tq), F32)],
    )
    return pl.pallas_call(
        functools.partial(_attn_kernel, nheads=nheads, hd=A_HEAD_DIM, hda=hda, tq=tq, tk=tk),
        grid_spec=grid_spec,
        out_shape=jax.ShapeDtypeStruct((bsz, seq, width), BF16),
        compiler_params=_cparams(("parallel", "arbitrary")),
        name="masked_attention",
    )(qidx, kidx, qt, k, vt, bias_t)


def _retention_kernel(qk_ref, v_ref, g_ref, o_ref, state_ref, *, tc, dk, dv):
    @pl.when(pl.program_id(1) == 0)
    def _():
        state_ref[...] = jnp.zeros(state_ref.shape, F32)

    row = lax.broadcasted_iota(I32, (tc, tc), 0)
    col = lax.broadcasted_iota(I32, (tc, tc), 1)
    diff = (row - col).astype(F32)
    pos = lax.broadcasted_iota(I32, (tc, 1), 0).astype(F32)
    for h in range(R_HEADS):
        log_g = math.log1p(-(2.0 ** (-5.0 - h)))
        q = qk_ref[:, h * dk:(h + 1) * dk]
        k = qk_ref[:, (R_HEADS + h) * dk:(R_HEADS + h + 1) * dk]
        v = v_ref[:, h * dv:(h + 1) * dv]
        decay = jnp.where(diff >= 0, jnp.exp(jnp.maximum(diff, 0.0) * log_g), 0.0)
        s = lax.dot_general(q, k, (((1,), (1,)), ((), ())), preferred_element_type=F32) * decay
        intra = jnp.dot(s.astype(BF16), v, preferred_element_type=F32)
        xi = jnp.exp((pos + 1.0) * log_g)
        zeta = jnp.exp((tc - 1.0 - pos) * log_g)
        state = state_ref[h]
        cross = jnp.dot((q.astype(F32) * xi).astype(BF16), state.astype(BF16), preferred_element_type=F32)
        kz = (k.astype(F32) * zeta).astype(BF16)
        kv = lax.dot_general(kz, v, (((0,), (0,)), ((), ())), preferred_element_type=F32)
        state_ref[h] = state * math.exp(tc * log_g) + kv
        ret = intra + cross
        mu = jnp.mean(ret, axis=1, keepdims=True)
        d = ret - mu
        var = jnp.mean(d * d, axis=1, keepdims=True)
        gate = g_ref[:, h * dv:(h + 1) * dv]
        gate = gate / (1.0 + jnp.exp(-gate))
        o_ref[:, h * dv:(h + 1) * dv] = (d * lax.rsqrt(var + 1e-5) * gate).astype(o_ref.dtype)


def _retention(bqk, bv, bg, tc=RET_CHUNK):
    bsz, seq, w2 = bqk.shape
    dk = w2 // (2 * R_HEADS)
    dv = bv.shape[2] // R_HEADS
    tc = min(tc, seq)
    return pl.pallas_call(
        functools.partial(_retention_kernel, tc=tc, dk=dk, dv=dv),
        grid=(bsz, seq // tc),
        in_specs=[pl.BlockSpec((None, tc, w2), lambda b, c: (b, c, 0)),
                  pl.BlockSpec((None, tc, R_HEADS * dv), lambda b, c: (b, c, 0)),
                  pl.BlockSpec((None, tc, R_HEADS * dv), lambda b, c: (b, c, 0))],
        out_specs=pl.BlockSpec((None, tc, R_HEADS * dv), lambda b, c: (b, c, 0)),
        out_shape=jax.ShapeDtypeStruct((bsz, seq, R_HEADS * dv), BF16),
        scratch_shapes=[pltpu.VMEM((R_HEADS, dk, dv), F32)],
        compiler_params=_cparams(("parallel", "arbitrary")),
        name="retention",
    )(bqk, bv, bg)


def _head_sum_matrix(width, hd):
    r = jnp.arange(width)
    return (r[:, None] // hd == r[None, :] // hd).astype(BF16)


def _sigmoid(x):
    return 1.0 / (1.0 + jnp.exp(-x))


def _split_bf16(x):
    hi = x.astype(BF16)
    return hi, (x - hi.astype(F32)).astype(BF16)


def _dot_x3(a, b):
    ah, al = _split_bf16(a)
    bh, bl = _split_bf16(b)
    return (jnp.dot(ah, bh, preferred_element_type=F32) + jnp.dot(ah, bl, preferred_element_type=F32)
            + jnp.dot(al, bh, preferred_element_type=F32))


def _dot_sel(a, sel):
    ah, al = _split_bf16(a)
    return jnp.dot(ah, sel, preferred_element_type=F32) + jnp.dot(al, sel, preferred_element_type=F32)


def _rwkv_prep_kernel(*refs, cw, has_vres):
    if has_vres:
        (c_ref, mu_ref, vec_ref, wlb_ref, alb_ref, glb_ref, hs_ref, vlb_ref, vfirst_ref,
         r_o, lw_o, k_o, v_o, al_o, be_o, g_o, bo_o, carry_ref) = refs
    else:
        (c_ref, mu_ref, vec_ref, wlb_ref, alb_ref, glb_ref, hs_ref,
         r_o, lw_o, k_o, v_o, al_o, be_o, g_o, bo_o, carry_ref) = refs
    tr = c_ref.shape[0]

    @pl.when(pl.program_id(1) == 0)
    def _():
        carry_ref[...] = jnp.zeros(carry_ref.shape, F32)

    c = c_ref[...]
    row = lax.broadcasted_iota(I32, (tr, 1), 0)
    prev = jnp.where(row == 0, carry_ref[0:1, :], pltpu.roll(c, 1, axis=0))
    carry_ref[0:1, :] = c[tr - 1:tr, :]
    cs = c + (prev - c) * mu_ref[...]

    r = cs[:, 0:cw]
    k = cs[:, cw:2 * cw]
    v = cs[:, 2 * cw:3 * cw]
    o = 3 * cw
    wl = cs[:, o:o + LANE]
    al = cs[:, o + LANE:o + 2 * LANE]
    gl = cs[:, o + 2 * LANE:o + 2 * LANE + GATE_LORA]
    w0, a0, k_k, k_a, r_k, v0 = (vec_ref[i:i + 1, :] for i in range(6))

    z = -(w0 + jnp.dot(jnp.tanh(wl).astype(BF16), wlb_ref[...], preferred_element_type=F32))
    softplus = jnp.maximum(z, 0.0) + jnp.log(1.0 + jnp.exp(-jnp.abs(z)))
    lw = -jnp.exp(-softplus - 0.5)
    a = _sigmoid(a0 + jnp.dot(al.astype(BF16), alb_ref[...], preferred_element_type=F32))
    g = jnp.dot(_sigmoid(gl).astype(BF16), glb_ref[...], preferred_element_type=F32)
    if has_vres:
        vr = cs[:, o + 2 * LANE + GATE_LORA:o + 2 * LANE + GATE_LORA + vlb_ref.shape[0]]
        mix = _sigmoid(v0 + jnp.dot(vr.astype(BF16), vlb_ref[...], preferred_element_type=F32))
        v = v + (vfirst_ref[...] - v) * mix
    hs = hs_ref[...]
    kk = k * k_k
    ss = _dot_sel(kk * kk, hs)
    kk = kk / jnp.maximum(jnp.sqrt(ss), 1e-12)
    kh = k * (1.0 + (a - 1.0) * k_a)
    rk = _dot_sel(r * kh * r_k, hs)

    r_o[...] = r
    lw_o[...] = lw
    k_o[...] = kh
    v_o[...] = v
    al_o[...] = -kk
    be_o[...] = kk * a
    g_o[...] = g
    bo_o[...] = rk * v


def _rwkv_prep(cproj, mu, vecs, wlb, alb, glb, vlb, v_first, cw, tr=PREP_ROWS):
    bsz, seq, wc = cproj.shape
    tr = min(tr, seq)
    has_vres = vlb is not None
    hs = _head_sum_matrix(cw, W_HEAD_DIM)
    full = lambda a: pl.BlockSpec(a.shape, lambda b, t: (0,) * a.ndim)
    tok = pl.BlockSpec((None, tr, cw), lambda b, t: (b, t, 0))
    ins = [cproj, mu, vecs, wlb, alb, glb, hs]
    in_specs = [pl.BlockSpec((None, tr, wc), lambda b, t: (b, t, 0)),
                full(mu), full(vecs), full(wlb), full(alb), full(glb), full(hs)]
    if has_vres:
        ins += [vlb, v_first]
        in_specs += [full(vlb), tok]
    out = jax.ShapeDtypeStruct((bsz, seq, cw), F32)
    return pl.pallas_call(
        functools.partial(_rwkv_prep_kernel, cw=cw, has_vres=has_vres),
        grid=(bsz, seq // tr),
        in_specs=in_specs,
        out_specs=[tok] * 8,
        out_shape=[out] * 8,
        scratch_shapes=[pltpu.VMEM((SUBLANE, wc), F32)],
        compiler_params=_cparams(("parallel", "arbitrary")),
        name="rwkv_prep",
    )(*ins)


def _bdot(a, b):
    return jnp.dot(a.astype(BF16), b.astype(BF16), preferred_element_type=F32)


def _bdot_tn(a, b):
    return lax.dot_general(a.astype(BF16), b.astype(BF16), (((0,), (0,)), ((), ())),
                           preferred_element_type=F32)


def _rwkv_chunk_kernel(r_ref, lw_ref, k_ref, v_ref, al_ref, be_ref, g_o, y0_o, m_o, z0_o, *, tc, nck, npairs):
    row = lax.broadcasted_iota(I32, (tc, tc), 0)
    col = lax.broadcasted_iota(I32, (tc, tc), 1)
    incl = row >= col
    strict = row > col
    tri = incl.astype(BF16)
    eye = (row == col).astype(F32)
    lane = lax.broadcasted_iota(I32, (1, LANE), 1)
    mh0 = (lane // W_HEAD_DIM) == 0
    prow = lax.broadcasted_iota(I32, (LANE, LANE), 0)
    pcol = lax.broadcasted_iota(I32, (LANE, LANE), 1)
    same_head = (prow // W_HEAD_DIM) == (pcol // W_HEAD_DIM)

    lw_all = lw_ref[...]
    hi = lw_all.astype(BF16)
    rem = lw_all - hi.astype(F32)
    mid = rem.astype(BF16)
    lo = (rem - mid.astype(F32)).astype(BF16)
    cum_all = jnp.concatenate(
        [sum(jnp.dot(tri, part[c * tc:(c + 1) * tc], preferred_element_type=F32) for part in (hi, mid, lo))
         for c in range(nck)], axis=0)

    pairs = range(nck * npairs)
    heads = [(p, h) for p in pairs for h in range(2)]
    sls = [slice((p % npairs) * LANE, (p % npairs + 1) * LANE) for p in pairs]
    rws = [slice((p // npairs) * tc, (p // npairs + 1) * tc) for p in pairs]
    cum = [cum_all[rws[p], sls[p]] for p in pairs]
    tot = [c[tc - 1:tc, :] for c in cum]
    p_inv = [jnp.exp(-cum[p]) for p in pairs]
    p_end = [jnp.exp(tot[p] - cum[p]) for p in pairs]
    at = [al_ref[rws[p], sls[p]] * jnp.exp(cum[p] - lw_all[rws[p], sls[p]]) for p in pairs]
    rt = [r_ref[rws[p], sls[p]] * jnp.exp(cum[p]) for p in pairs]
    bh = [be_ref[rws[p], sls[p]] * p_end[p] for p in pairs]
    khat = [k_ref[rws[p], sls[p]] * p_end[p] for p in pairs]
    v = [v_ref[rws[p], sls[p]] for p in pairs]
    rhs = [jnp.concatenate([be_ref[rws[p], sls[p]] * p_inv[p], k_ref[rws[p], sls[p]] * p_inv[p]],
                           axis=0).astype(BF16) for p in pairs]

    a_ab, a_ak, qcat = {}, {}, {}
    for p, h in heads:
        mh = (lane // W_HEAD_DIM) == h
        lhs = jnp.concatenate([jnp.where(mh, at[p], 0.0), jnp.where(mh, rt[p], 0.0)], axis=0).astype(BF16)
        x = lax.dot_general(lhs, rhs[p], (((1,), (1,)), ((), ())), preferred_element_type=F32)
        a_ab[p, h] = jnp.where(strict, x[:tc, :tc], 0.0)
        a_ak[p, h] = jnp.where(strict, x[:tc, tc:], 0.0)
        qcat[p, h] = jnp.concatenate([jnp.where(incl, x[tc:, tc:], 0.0),
                                      jnp.where(incl, x[tc:, :tc], 0.0)], axis=1)

    t_inv = {ph: eye + a_ab[ph] for ph in heads}
    pw = {ph: _bdot(a_ab[ph], a_ab[ph]) for ph in heads}
    akv = {(p, h): _bdot(a_ak[p, h], v[p]) for p, h in heads}
    n = 2
    while n < tc:
        for ph in heads:
            res = _bdot(jnp.concatenate([pw[ph], t_inv[ph]], axis=0), pw[ph])
            t_inv[ph] = t_inv[ph] + res[tc:]
            pw[ph] = res[:tc]
        n *= 2
    tw = {(p, h): _bdot(t_inv[p, h], jnp.concatenate([at[p], akv[p, h]], axis=1)) for p, h in heads}

    zero = jnp.zeros((tc, LANE), F32)
    w = [jnp.where(mh0, tw[p, 0][:, :LANE], tw[p, 1][:, :LANE]) for p in pairs]
    u0 = [jnp.where(mh0, tw[p, 0][:, LANE:], tw[p, 1][:, LANE:]) for p in pairs]
    vu = [jnp.concatenate([jnp.concatenate([v[p], zero], axis=1),
                           jnp.concatenate([u0[p], w[p]], axis=1)], axis=0) for p in pairs]
    yg = {(p, h): _bdot(qcat[p, h], vu[p]) for p, h in heads}
    m_mat = [_bdot_tn(bh[p], w[p]) for p in pairs]
    z0 = [_bdot_tn(jnp.concatenate([bh[p], khat[p]], axis=0), jnp.concatenate([u0[p], v[p]], axis=0))
          for p in pairs]
    for p in pairs:
        y0_o[rws[p], sls[p]] = jnp.where(mh0, yg[p, 0][:, :LANE], yg[p, 1][:, :LANE])
        g_o[rws[p], sls[p]] = rt[p] + jnp.where(mh0, yg[p, 0][:, LANE:], yg[p, 1][:, LANE:])
        m_o[p // npairs, p % npairs] = jnp.where(same_head, m_mat[p], 0.0) + jnp.where(
            prow == pcol, jnp.broadcast_to(jnp.exp(tot[p]), (LANE, LANE)), 0.0)
        z0_o[p // npairs, p % npairs] = jnp.where(same_head, z0[p], 0.0)


def _rwkv_chunk_ops(r, lw, kh, v, alpha, beta, tc=CHUNK, nck=RWKV_CHUNKS):
    bsz, seq, cw = r.shape
    npairs = cw // LANE
    nc = seq // tc
    nck = min(nck, nc)
    tok = pl.BlockSpec((None, nck * tc, cw), lambda b, c: (b, c, 0))
    mat = pl.BlockSpec((None, nck, npairs, LANE, LANE), lambda b, c: (b, c, 0, 0, 0))
    tok_shape = jax.ShapeDtypeStruct((bsz, seq, cw), F32)
    mat_shape = jax.ShapeDtypeStruct((bsz, nc, npairs, LANE, LANE), F32)
    return pl.pallas_call(
        functools.partial(_rwkv_chunk_kernel, tc=tc, nck=nck, npairs=npairs),
        grid=(bsz, nc // nck),
        in_specs=[tok] * 6,
        out_specs=[tok, tok, mat, mat],
        out_shape=[tok_shape, tok_shape, mat_shape, mat_shape],
        compiler_params=_cparams(("parallel", "parallel")),
        name="rwkv_chunk_ops",
    )(r, lw, kh, v, alpha, beta)


def _rwkv_scan_kernel(g_ref, y0_ref, m_ref, z0_ref, bo_ref, gate_ref, ln_ref, hs_ref, o_ref, state_ref,
                      *, tc, nch, npairs):
    @pl.when(pl.program_id(1) == 0)
    def _():
        state_ref[...] = jnp.zeros(state_ref.shape, F32)

    hs = hs_ref[...]
    inv_n = 1.0 / W_HEAD_DIM
    sls = [slice(p * LANE, (p + 1) * LANE) for p in range(npairs)]
    st = [state_ref[p] for p in range(npairs)]
    for ch in range(nch):
        rows = slice(ch * tc, (ch + 1) * tc)
        gm = [_dot_x3(jnp.concatenate([g_ref[rows, sls[p]], m_ref[ch, p]], axis=0), st[p]) for p in range(npairs)]
        y = [gm[p][:tc] + y0_ref[rows, sls[p]] for p in range(npairs)]
        st = [gm[p][tc:] + z0_ref[ch, p] for p in range(npairs)]
        mu = [_dot_sel(y[p], hs) * inv_n for p in range(npairs)]
        d = [y[p] - mu[p] for p in range(npairs)]
        var = [_dot_sel(d[p] * d[p], hs) * inv_n for p in range(npairs)]
        for p in range(npairs):
            yn = d[p] * lax.rsqrt(var[p] + LNX_EPS) * ln_ref[0:1, sls[p]] + ln_ref[1:2, sls[p]]
            o_ref[rows, sls[p]] = ((yn + bo_ref[rows, sls[p]]) * gate_ref[rows, sls[p]]).astype(o_ref.dtype)
    for p in range(npairs):
        state_ref[p] = st[p]


def _rwkv_scan(g, y0, m, z0, bonus, gate, ln, tc=CHUNK, nch=RWKV_CHUNKS):
    bsz, seq, cw = g.shape
    npairs = cw // LANE
    nch = min(nch, seq // tc)
    hs = _head_sum_matrix(LANE, W_HEAD_DIM)
    tok = pl.BlockSpec((None, tc * nch, cw), lambda b, c: (b, c, 0))
    mat = pl.BlockSpec((None, nch, npairs, LANE, LANE), lambda b, c: (b, c, 0, 0, 0))
    return pl.pallas_call(
        functools.partial(_rwkv_scan_kernel, tc=tc, nch=nch, npairs=npairs),
        grid=(bsz, seq // (tc * nch)),
        in_specs=[tok, tok, mat, mat, tok, tok,
                  pl.BlockSpec(ln.shape, lambda b, c: (0, 0)),
                  pl.BlockSpec(hs.shape, lambda b, c: (0, 0))],
        out_specs=tok,
        out_shape=jax.ShapeDtypeStruct((bsz, seq, cw), BF16),
        scratch_shapes=[pltpu.VMEM((npairs, LANE, LANE), F32)],
        compiler_params=_cparams(("parallel", "arbitrary")),
        name="rwkv_scan",
    )(g, y0, m, z0, bonus, gate, ln, hs)


def _out_proj_kernel(x_ref, a_ref, b_ref, c_ref, w_ref, o_ref, *, wa, wb):
    acc = jnp.dot(a_ref[...], w_ref[0:wa, :], preferred_element_type=F32)
    acc = acc + jnp.dot(b_ref[...], w_ref[wa:wa + wb, :], preferred_element_type=F32)
    acc = acc + jnp.dot(c_ref[...], w_ref[wa + wb:, :], preferred_element_type=F32)
    o_ref[...] = x_ref[...] + acc


def _out_proj(x2d, oa, ob, oc, w, tm=TM, tn=1024):
    m, d = x2d.shape
    wa, wb, wc = oa.shape[1], ob.shape[1], oc.shape[1]
    tm = min(tm, m)
    tn = min(tn, d)
    return pl.pallas_call(
        functools.partial(_out_proj_kernel, wa=wa, wb=wb),
        grid=(m // tm, d // tn),
        in_specs=[pl.BlockSpec((tm, tn), lambda i, j: (i, j)),
                  pl.BlockSpec((tm, wa), lambda i, j: (i, 0)),
                  pl.BlockSpec((tm, wb), lambda i, j: (i, 0)),
                  pl.BlockSpec((tm, wc), lambda i, j: (i, 0)),
                  pl.BlockSpec((wa + wb + wc, tn), lambda i, j: (0, j))],
        out_specs=pl.BlockSpec((tm, tn), lambda i, j: (i, j)),
        out_shape=jax.ShapeDtypeStruct((m, d), F32),
        compiler_params=_cparams(("parallel", "arbitrary")),
        name="out_proj",
    )(x2d, oa, ob, oc, w)


def _gate_up_kernel(h_ref, wg_ref, wu_ref, o_ref):
    h = h_ref[...]
    gate = jnp.dot(h, wg_ref[...], preferred_element_type=F32)
    up = jnp.dot(h, wu_ref[...], preferred_element_type=F32)
    o_ref[...] = (gate / (1.0 + jnp.exp(-gate)) * up).astype(o_ref.dtype)


def _gate_up(h, w_gate_up, tm=TM, tf_cap=512):
    m, d = h.shape
    dff = w_gate_up.shape[1] // 2
    tm = min(tm, m)
    tf = _pick_tile(dff, tf_cap)
    nf = dff // tf
    return pl.pallas_call(
        _gate_up_kernel,
        grid=(m // tm, nf),
        in_specs=[pl.BlockSpec((tm, d), lambda i, j: (i, 0)),
                  pl.BlockSpec((d, tf), lambda i, j: (0, j)),
                  pl.BlockSpec((d, tf), lambda i, j: (0, j + nf))],
        out_specs=pl.BlockSpec((tm, tf), lambda i, j: (i, j)),
        out_shape=jax.ShapeDtypeStruct((m, dff), BF16),
        compiler_params=_cparams(("parallel", "arbitrary")),
        name="ffn_gate_up",
    )(h, w_gate_up, w_gate_up)


def _down_kernel(x_ref, a_ref, w_ref, o_ref):
    o_ref[...] = x_ref[...] + jnp.dot(a_ref[...], w_ref[...], preferred_element_type=F32)


def _down_proj(x2d, act, w, tm=TM, tn=512):
    m, d = x2d.shape
    dff = act.shape[1]
    tm = min(tm, m)
    tn = min(tn, d)
    return pl.pallas_call(
        _down_kernel,
        grid=(m // tm, d // tn),
        in_specs=[pl.BlockSpec((tm, tn), lambda i, j: (i, j)),
                  pl.BlockSpec((tm, dff), lambda i, j: (i, 0)),
                  pl.BlockSpec((dff, tn), lambda i, j: (0, j))],
        out_specs=pl.BlockSpec((tm, tn), lambda i, j: (i, j)),
        out_shape=jax.ShapeDtypeStruct((m, d), F32),
        compiler_params=_cparams(("parallel", "arbitrary")),
        name="ffn_down",
    )(x2d, act, w)


def _pad_cols(w, width):
    return jnp.pad(w, ((0, 0), (0, width - w.shape[1])))


def _pad_rows(w, height):
    return jnp.pad(w, ((0, height - w.shape[0]), (0, 0)))


def _pad_vec(v, width):
    return jnp.pad(v, (0, width - v.shape[0]))


def kernel(x, norm_mix_g, w_in, w_in_vres, rwkv_mu, rwkv_mu_vres, rwkv_w0, rwkv_w_lora_b, rwkv_a0,
           rwkv_a_lora_b, rwkv_v0, rwkv_v_lora_b, rwkv_g_lora_b, rwkv_k_k, rwkv_k_a, rwkv_r_k,
           rwkv_lnx_g, rwkv_lnx_b, w_out, norm_ffn_g, w_gate_up, w_down, final_norm_g):
    bsz, seq, d_model = x.shape
    depth = w_in.shape[0]
    m = bsz * seq
    d_mix = w_out.shape[1]
    a_w = d_mix // 2
    qi_w = IDX_HEADS * IDX_DIM
    b_v_w = d_mix // 4
    b_qk_w = b_v_w // 2
    c_w = d_mix // 4
    r_qk_dim = b_qk_w // R_HEADS
    top_k = min(TOPK_MAX, seq // 4)
    vres_pad = 2 * LANE

    o_q, o_k, o_v = 0, a_w, 2 * a_w
    o_qi = 3 * a_w
    o_ki = o_qi + qi_w
    o_wi = o_ki + IDX_DIM
    o_bq = o_wi + IDX_HEADS
    o_bk = o_bq + b_qk_w
    o_bv = o_bk + b_qk_w
    o_bg = o_bv + b_v_w
    o_c = o_bg + b_v_w
    o_wl = o_c + 3 * c_w
    o_al = o_wl + DECAY_LORA
    o_gl = o_al + AAA_LORA

    a_rot = A_HEAD_DIM // ROPE_FRAC
    tq_ = _rope_tables(seq, A_HEAD_DIM, a_rot, ROPE_THETA, scale=A_HEAD_DIM ** -0.5 * math.log2(math.e))
    tk_ = _rope_tables(seq, A_HEAD_DIM, a_rot, ROPE_THETA)
    tab_q = tuple(t[None] for t in tq_)
    tab_k = tuple(t[None] for t in tk_)
    i_rot = IDX_DIM // ROPE_FRAC
    tab_qi = tuple(t[None] for t in _rope_tables(seq, IDX_DIM, i_rot, ROPE_THETA))
    lane = jnp.arange(LANE)
    kiwi_pass = jnp.where(lane < IDX_DIM, 1.0, IDX_W_SCALE).astype(F32)
    c_kw, s1_kw, s2_kw = _rope_tables(seq, LANE, i_rot, ROPE_THETA, pass_scale=kiwi_pass)
    tab_kiwi = (c_kw[None], s1_kw[None], s2_kw[None])
    tbq = _rope_tables(seq, r_qk_dim, r_qk_dim, R_THETA)
    tbk = _rope_tables(seq, r_qk_dim, r_qk_dim, R_THETA, scale=r_qk_dim ** -0.5)
    tab_bqk = tuple(jnp.stack([a, b]) for a, b in zip(tbq, tbk))

    x2d = x.reshape(m, d_model)
    v_first = None
    for l in range(depth):
        wl_ = w_in[l]
        cols = lambda o, n: wl_[:, o:o + n]
        w_q = cols(o_q, a_w).astype(BF16)
        w_k = cols(o_k, a_w).astype(BF16)
        w_v = cols(o_v, a_w).astype(BF16)
        w_bv = cols(o_bv, b_v_w).astype(BF16)
        w_qi = cols(o_qi, qi_w).astype(BF16)
        w_kiwi = _pad_cols(cols(o_ki, IDX_DIM + IDX_HEADS), LANE).astype(BF16)
        w_bqk = cols(o_bq, 2 * b_qk_w).astype(BF16)
        w_bg = cols(o_bg, b_v_w).astype(BF16)
        vres_w = (w_in_vres[l - 1] if l > 0 else jnp.zeros((d_model, MV_LORA), F32))
        w_c = jnp.concatenate([cols(o_c, 3 * c_w), _pad_cols(cols(o_wl, DECAY_LORA), LANE),
                               _pad_cols(cols(o_al, AAA_LORA), LANE), cols(o_gl, GATE_LORA),
                               _pad_cols(vres_w, vres_pad)], axis=1).astype(BF16)
        mu = rwkv_mu[l]
        mu_vres = rwkv_mu_vres[l - 1] if l > 0 else jnp.zeros((MV_LORA,), F32)
        mu_c = jnp.concatenate([mu[:3 * c_w], _pad_vec(mu[3 * c_w:3 * c_w + DECAY_LORA], LANE),
                                _pad_vec(mu[3 * c_w + DECAY_LORA:3 * c_w + DECAY_LORA + AAA_LORA], LANE),
                                mu[3 * c_w + DECAY_LORA + AAA_LORA:], _pad_vec(mu_vres, vres_pad)])[None, :]

        h = _rmsnorm(x2d, norm_mix_g[l], BF16)
        qt = _mm_rope(h, w_q, tab_q, a_rot // 2, seq, a_w, BF16, transpose=True)
        ak = _mm_rope(h, w_k, tab_k, a_rot // 2, seq, a_w, BF16).reshape(bsz, seq, a_w)
        vt = _mm_plain(h, w_v, BF16, seq, transpose=True, pad_rows=ONES_ROWS)
        bv = _mm_plain(h, w_bv, BF16, seq, tn_cap=512).reshape(bsz, seq, b_v_w)
        qit = _mm_rope(h, w_qi, tab_qi, i_rot // 2, seq, qi_w, BF16, transpose=True)
        kiwi = _mm_rope(h, w_kiwi, tab_kiwi, i_rot // 2, seq, LANE, F32).reshape(bsz, seq, LANE)
        bqk = _mm_rope(h, w_bqk, tab_bqk, r_qk_dim // 2, seq, b_qk_w, BF16).reshape(bsz, seq, 2 * b_qk_w)
        bg = _mm_plain(h, w_bg, F32, seq).reshape(bsz, seq, b_v_w)
        cproj = _mm_plain(h, w_c, F32, seq, tn_cap=768).reshape(bsz, seq, w_c.shape[1])

        ki = kiwi[:, :, :IDX_DIM].astype(BF16)
        wit = jnp.swapaxes(kiwi[:, :, IDX_DIM:IDX_DIM + IDX_HEADS], 1, 2)
        bias_t = _indexer_bias(ki, qit, wit, top_k)
        out_a = _masked_attention(qt, ak, vt, bias_t)

        out_b = _retention(bqk, bv, bg)

        vecs = jnp.stack([rwkv_w0[l], rwkv_a0[l], rwkv_k_k[l], rwkv_k_a[l], rwkv_r_k[l].reshape(-1),
                          rwkv_v0[l - 1] if l > 0 else jnp.zeros((c_w,), F32),
                          jnp.zeros((c_w,), F32), jnp.zeros((c_w,), F32)])
        wlb = _pad_rows(rwkv_w_lora_b[l], LANE).astype(BF16)
        alb = _pad_rows(rwkv_a_lora_b[l], LANE).astype(BF16)
        glb = rwkv_g_lora_b[l].astype(BF16)
        vlb = _pad_rows(rwkv_v_lora_b[l - 1], vres_pad).astype(BF16) if l > 0 else None
        r_, lw_, kh_, v_, al_, be_, g_, bo_ = _rwkv_prep(cproj, mu_c, vecs, wlb, alb, glb, vlb, v_first, c_w)
        if l == 0:
            v_first = v_
        gm, y0, mm, z0 = _rwkv_chunk_ops(r_, lw_, kh_, v_, al_, be_)
        ln = jnp.stack([rwkv_lnx_g[l], rwkv_lnx_b[l]] + [jnp.zeros((c_w,), F32)] * 6)
        out_c = _rwkv_scan(gm, y0, mm, z0, bo_, g_, ln)

        x2d = _out_proj(x2d, out_a.reshape(m, a_w), out_b.reshape(m, b_v_w), out_c.reshape(m, c_w),
                        w_out[l].astype(BF16))

        h = _rmsnorm(x2d, norm_ffn_g[l], BF16)
        act = _gate_up(h, w_gate_up[l].astype(BF16))
        x2d = _down_proj(x2d, act, w_down[l].astype(BF16))

    return _rmsnorm(x2d, final_norm_g, F32).reshape(bsz, seq, d_model)
```

```python
import functools
import math

import jax
import jax.numpy as jnp
from jax import lax
from jax.experimental import pallas as pl
from jax.experimental.pallas import tpu as pltpu

F32 = jnp.float32
BF16 = jnp.bfloat16
I32 = jnp.int32
I16 = jnp.int16

CHUNK = 64
A_HEAD_DIM = 128
IDX_HEADS = 16
IDX_DIM = 64
IDX_W_SCALE = (IDX_HEADS * IDX_DIM) ** -0.5
TOPK_MAX = 256
ROPE_THETA = 500000.0
ROPE_FRAC = 4
R_HEADS = 4
R_THETA = 10000.0
W_HEAD_DIM = 64
DECAY_LORA = 96
AAA_LORA = 96
MV_LORA = 64
GATE_LORA = 256
LNX_EPS = 64e-5
RMS_EPS = 1e-5

LANE = 128
SUBLANE = 8
PACKED_ROWS = 16
VMEM_LIMIT = 56 * 1024 * 1024
TM = 1024
TM_NORM = 512
IDX_TQ = 256
IDX_TKS = 128
IDX_TKC = 1024
SCORE_UNROLL = 4
COUNT_CHAINS = 8
ATTN_TQ = 512
ATTN_TK = 1024
ONES_ROWS = SUBLANE
RET_CHUNK = 256
PREP_ROWS = 256
RWKV_CHUNKS = 4

NEG_BIAS = -1e30
M_INIT = -1e20
INT_MIN = -2147483648
INT_MAX = 2147483647
I16_MIN = -32768


def _cparams(sem):
    return pltpu.CompilerParams(dimension_semantics=sem, vmem_limit_bytes=VMEM_LIMIT)


def _pick_tile(n, cap):
    best = LANE
    t = LANE
    while t <= min(n, cap):
        if n % t == 0:
            best = t
        t += LANE
    return best


def _rmsnorm_kernel(x_ref, g_ref, o_ref):
    x = x_ref[...]
    ms = jnp.mean(x * x, axis=-1, keepdims=True)
    o_ref[...] = (x * lax.rsqrt(ms + RMS_EPS) * g_ref[...]).astype(o_ref.dtype)


def _rmsnorm(x2d, g, out_dtype, tm=TM_NORM):
    m, d = x2d.shape
    return pl.pallas_call(
        _rmsnorm_kernel,
        grid=(m // tm,),
        in_specs=[pl.BlockSpec((tm, d), lambda i: (i, 0)),
                  pl.BlockSpec((1, d), lambda i: (0, 0))],
        out_specs=pl.BlockSpec((tm, d), lambda i: (i, 0)),
        out_shape=jax.ShapeDtypeStruct((m, d), out_dtype),
        compiler_params=_cparams(("parallel",)),
        name="rmsnorm",
    )(x2d, g.reshape(1, d).astype(F32))


def _store_cols(o_ref, blk, x, transpose, pad_rows):
    if transpose:
        r0 = blk * (LANE + pad_rows)
        o_ref[r0:r0 + LANE, :] = x.T.astype(o_ref.dtype)
        if pad_rows:
            o_ref[r0 + LANE:r0 + LANE + pad_rows, :] = jnp.ones((pad_rows, x.shape[0]), o_ref.dtype)
    else:
        o_ref[:, blk * LANE:(blk + 1) * LANE] = x.astype(o_ref.dtype)


def _proj_out_spec(m, n, tm, tn, seq, out_dtype, transpose, pad_rows):
    if not transpose:
        return pl.BlockSpec((tm, tn), lambda i, j: (i, j)), jax.ShapeDtypeStruct((m, n), out_dtype)
    tpb = seq // tm
    rows = lambda cols: cols // LANE * (LANE + pad_rows)
    return (pl.BlockSpec((None, rows(tn), tm), lambda i, j: (i // tpb, j, i % tpb)),
            jax.ShapeDtypeStruct((m // seq, rows(n), seq), out_dtype))


def _mm_plain_kernel(a_ref, w_ref, o_ref, *, transpose, pad_rows):
    acc = jnp.dot(a_ref[...], w_ref[...], preferred_element_type=F32)
    if not transpose:
        o_ref[...] = acc.astype(o_ref.dtype)
        return
    for blk in range(acc.shape[1] // LANE):
        _store_cols(o_ref, blk, acc[:, blk * LANE:(blk + 1) * LANE], transpose, pad_rows)


def _mm_plain(a, w, out_dtype, seq, tm=TM, tn_cap=1024, transpose=False, pad_rows=0):
    m, k = a.shape
    n = w.shape[1]
    tm = min(tm, seq)
    tn = _pick_tile(n, tn_cap)
    out_spec, out_shape = _proj_out_spec(m, n, tm, tn, seq, out_dtype, transpose, pad_rows)
    return pl.pallas_call(
        functools.partial(_mm_plain_kernel, transpose=transpose, pad_rows=pad_rows),
        grid=(m // tm, n // tn),
        in_specs=[pl.BlockSpec((tm, k), lambda i, j: (i, 0)),
                  pl.BlockSpec((k, tn), lambda i, j: (0, j))],
        out_specs=out_spec,
        out_shape=out_shape,
        compiler_params=_cparams(("parallel", "arbitrary")),
        name="proj_plain",
    )(a, w)


def _mm_rope_kernel(a_ref, w_ref, c_ref, s1_ref, s2_ref, o_ref, *, half, transpose):
    acc = jnp.dot(a_ref[...], w_ref[...], preferred_element_type=F32)
    c, s1, s2 = c_ref[...], s1_ref[...], s2_ref[...]
    for blk in range(acc.shape[1] // LANE):
        x = acc[:, blk * LANE:(blk + 1) * LANE]
        up = pltpu.roll(x, LANE - half, axis=1)
        dn = pltpu.roll(x, half, axis=1)
        _store_cols(o_ref, blk, x * c + up * s1 + dn * s2, transpose, 0)


def _mm_rope(a, w, tables, half, seq, tn, out_dtype, tm=TM, transpose=False):
    m, k = a.shape
    n = w.shape[1]
    tm = min(tm, seq)
    tpb = seq // tm
    tab_spec = pl.BlockSpec((None, tm, LANE), lambda i, j: (j, i % tpb, 0))
    out_spec, out_shape = _proj_out_spec(m, n, tm, tn, seq, out_dtype, transpose, 0)
    return pl.pallas_call(
        functools.partial(_mm_rope_kernel, half=half, transpose=transpose),
        grid=(m // tm, n // tn),
        in_specs=[pl.BlockSpec((tm, k), lambda i, j: (i, 0)),
                  pl.BlockSpec((k, tn), lambda i, j: (0, j)),
                  tab_spec, tab_spec, tab_spec],
        out_specs=out_spec,
        out_shape=out_shape,
        compiler_params=_cparams(("parallel", "arbitrary")),
        name="proj_rope",
    )(a, w, *tables)


def _rope_tables(seq, group, rot_dim, theta, scale=1.0, pass_scale=None):
    half = rot_dim // 2
    freqs = jnp.power(F32(theta), -jnp.arange(half, dtype=F32) / half)
    ang = jnp.arange(seq, dtype=F32)[:, None] * freqs[None, :]
    cos, sin = jnp.cos(ang), jnp.sin(ang)
    lane = jnp.arange(LANE) % group
    idx = lane % half
    cosl, sinl = cos[:, idx], sin[:, idx]
    passv = jnp.ones((LANE,), F32) if pass_scale is None else pass_scale
    c = jnp.where(lane < rot_dim, cosl, passv[None, :])
    s1 = jnp.where(lane < half, -sinl, 0.0)
    s2 = jnp.where((lane >= half) & (lane < rot_dim), sinl, 0.0)
    return c * scale, s1 * scale, s2 * scale


def _indexer_kernel(ki_ref, qit_ref, wit_ref, bias_ref, key_ref, k16_ref, *, tq, tks, tkc, top_k, seq):
    qb = pl.program_id(1)
    span = tks * SCORE_UNROLL
    nks = ((qb + 1) * tq + span - 1) // span * SCORE_UNROLL
    nkc = (nks * tks + tkc - 1) // tkc
    q_chunk = (qb * tq + lax.broadcasted_iota(I32, (1, tq), 1)) // CHUNK
    w = wit_ref[...]

    def score_block(kb, rmax):
        k0 = pl.multiple_of(kb * tks, tks)
        kblk = ki_ref[pl.ds(k0, tks), :]
        acc = jnp.zeros((tks, tq), F32)
        for h in range(IDX_HEADS):
            x = jnp.dot(kblk, qit_ref[h * IDX_DIM:(h + 1) * IDX_DIM, :], preferred_element_type=F32)
            acc = acc + jnp.maximum(x, 0.0) * w[h:h + 1, :]
        k_chunk = (k0 + lax.broadcasted_iota(I32, (tks, 1), 0)) // CHUNK
        bits = lax.bitcast_convert_type(acc, I32)
        key = bits ^ ((bits >> 31) & jnp.int32(INT_MAX))
        key = jnp.where(k_chunk <= q_chunk, key, jnp.int32(INT_MIN))
        key_ref[pl.ds(k0, tks), :] = key
        k16_ref[pl.ds(k0, tks), :] = (key >> 16).astype(I16)
        return jnp.maximum(rmax, jnp.max(key.reshape(tks // SUBLANE, SUBLANE, tq), axis=0))

    def score_step(kbu, rmax):
        for u in range(SCORE_UNROLL):
            rmax = score_block(kbu * SCORE_UNROLL + u, rmax)
        return rmax

    rmax = lax.fori_loop(0, nks // SCORE_UNROLL, score_step, jnp.full((SUBLANE, tq), INT_MIN, I32))
    colmax = jnp.max(rmax, axis=0, keepdims=True)

    def pad_block(kb, carry):
        k0 = pl.multiple_of(kb * tks, tks)
        key_ref[pl.ds(k0, tks), :] = jnp.full((tks, tq), INT_MIN, I32)
        k16_ref[pl.ds(k0, tks), :] = jnp.full((tks, tq), I16_MIN, I16)
        return carry

    lax.fori_loop(nks, nkc * (tkc // tks), pad_block, 0)

    def tree_sum(terms):
        while len(terms) > 1:
            terms = [a + b for a, b in zip(terms[0::2], terms[1::2])]
        return terms[0]

    def count_ge16(cand):
        c16 = cand.astype(I16)
        one, zero = jnp.int16(1), jnp.int16(0)

        def count_block(kb, cnt):
            k0 = pl.multiple_of(kb * tkc, tkc)
            part = tkc // COUNT_CHAINS
            sums = []
            for g in range(COUNT_CHAINS):
                ind = jnp.where(k16_ref[pl.ds(k0 + g * part, part), :] >= c16, one, zero)
                acc = ind[0:PACKED_ROWS]
                for i in range(1, part // PACKED_ROWS):
                    acc = acc + ind[i * PACKED_ROWS:(i + 1) * PACKED_ROWS]
                sums.append(acc)
            return cnt + tree_sum(sums)

        cnt = lax.fori_loop(0, nkc, count_block, jnp.zeros((PACKED_ROWS, tq), I16))
        return jnp.sum(cnt.astype(I32), axis=0, keepdims=True)

    def probe_step(count_fn, state, mid):
        lo, hi, cnt_lo, cnt_hi = state
        cnt = count_fn(mid)
        ge = cnt >= top_k
        new_hi = jnp.where(cnt == top_k, mid + 1, jnp.where(ge, hi, mid))
        return (jnp.where(ge, mid, lo), new_hi, jnp.where(ge, cnt, cnt_lo), jnp.where(ge, cnt_hi, cnt))

    def midpoint(state):
        lo, hi = state[0], state[1]
        return (lo >> 1) + (hi >> 1) + (lo & hi & 1)

    def n_active(state):
        return jnp.max(((state[0] + 1) < state[1]).astype(I32))

    binade16 = 1 << 7

    def bisect(count_fn, state, floor=None):
        def next_mid(st, it):
            mid = midpoint(st)
            if floor is None:
                return mid
            step = jnp.int32(3 * binade16) << jnp.minimum(it, 7)
            return jnp.where(st[0] == floor, jnp.maximum(st[1] - step, mid), mid)

        def body(carry):
            it, _, st = carry
            st = probe_step(count_fn, st, next_mid(st, it + 1))
            st = probe_step(count_fn, st, next_mid(st, it + 2))
            return it + 2, n_active(st), st

        return lax.while_loop(lambda c: (c[1] > 0) & (c[0] < 40), body, (jnp.int32(0), n_active(state), state))[2]

    big = jnp.int32(1 << 30)
    top16 = colmax >> 16
    state = (jnp.full((1, tq), I16_MIN, I32), top16 + 1, jnp.full((1, tq), big, I32), jnp.zeros((1, tq), I32))
    first = jnp.where(top16 > jnp.int32(I16_MIN + 4 * binade16), top16 - jnp.int32(3 * binade16),
                      jnp.int32(I16_MIN + 1))
    state = bisect(count_ge16, probe_step(count_ge16, state, first), floor=jnp.int32(I16_MIN))
    lo16, hi16, cnt_lo, cnt_hi = state
    base = lo16 << 16
    top = jnp.where(hi16 > -(I16_MIN + 1), jnp.int32(INT_MAX), hi16 << 16)

    def rekey_block(kb, carry):
        k0 = pl.multiple_of(kb * tks, tks)
        key = key_ref[pl.ds(k0, tks), :]
        off = jnp.where(key < base, jnp.int32(I16_MIN),
                        jnp.where(key >= top, jnp.int32(-(I16_MIN + 1)), (key - base) + jnp.int32(I16_MIN)))
        k16_ref[pl.ds(k0, tks), :] = off.astype(I16)
        return carry

    lax.fori_loop(0, nkc * (tkc // tks), rekey_block, 0)
    floor16 = jnp.full((1, tq), I16_MIN, I32)
    hi_off = jnp.where(cnt_lo == top_k, floor16 + 1, (top - base) + jnp.int32(I16_MIN))
    lo_off, _, cnt_lo, cnt_hi = bisect(count_ge16, (floor16, hi_off, cnt_lo, cnt_hi))
    lo = base + (lo_off - jnp.int32(I16_MIN))

    thr = jnp.maximum(lo, jnp.int32(INT_MIN + 1))
    tie_q = (cnt_lo > top_k) & (lo > jnp.int32(INT_MIN))
    any_tie = jnp.max(tie_q.astype(I32))

    @pl.when(any_tie == 0)
    def _():
        def write_block(kb, carry):
            k0 = pl.multiple_of(kb * tks, tks)
            bias_ref[pl.ds(k0, tks), :] = jnp.where(key_ref[pl.ds(k0, tks), :] >= thr, 0.0, NEG_BIAS)
            return carry

        lax.fori_loop(0, nks, write_block, 0)

    @pl.when(any_tie != 0)
    def _():
        need = jnp.where(tie_q, top_k - cnt_hi, big).astype(F32)
        lr = lax.broadcasted_iota(I32, (LANE, LANE), 0)
        lc = lax.broadcasted_iota(I32, (LANE, LANE), 1)
        lower = (lr >= lc).astype(BF16)

        def write_block(kb, run):
            k0 = pl.multiple_of(kb * LANE, LANE)
            blk = key_ref[pl.ds(k0, LANE), :]
            eq = blk == thr
            pc = jnp.dot(lower, jnp.where(eq, 1.0, 0.0).astype(BF16), preferred_element_type=F32)
            keep_eq = jnp.where((run + pc) <= need, 0.0, NEG_BIAS)
            bias_ref[pl.ds(k0, LANE), :] = jnp.where(blk > thr, 0.0, jnp.where(eq, keep_eq, NEG_BIAS))
            return run + pc[LANE - 1:LANE, :]

        lax.fori_loop(0, nks * (tks // LANE), write_block, jnp.zeros((1, tq), F32))

    def fill_block(kb, carry):
        k0 = pl.multiple_of(kb * tks, tks)
        bias_ref[pl.ds(k0, tks), :] = jnp.full((tks, tq), NEG_BIAS, F32)
        return carry

    lax.fori_loop(nks, seq // tks, fill_block, 0)


def _indexer_bias(ki, qit, wit, top_k, tq=IDX_TQ, tks=IDX_TKS, tkc=IDX_TKC):
    bsz, seq, _ = ki.shape
    tq = min(tq, seq)
    assert seq % tkc == 0 and tkc % (tks * SCORE_UNROLL) == 0 and tks % LANE == 0
    return pl.pallas_call(
        functools.partial(_indexer_kernel, tq=tq, tks=tks, tkc=tkc, top_k=top_k, seq=seq),
        grid=(bsz, seq // tq),
        in_specs=[pl.BlockSpec((None, seq, IDX_DIM), lambda b, i: (b, 0, 0)),
                  pl.BlockSpec((None, IDX_HEADS * IDX_DIM, tq), lambda b, i: (b, 0, i)),
                  pl.BlockSpec((None, IDX_HEADS, tq), lambda b, i: (b, 0, i))],
        out_specs=pl.BlockSpec((None, seq, tq), lambda b, i: (b, 0, i)),
        out_shape=jax.ShapeDtypeStruct((bsz, seq, seq), F32),
        scratch_shapes=[pltpu.VMEM((seq, tq), I32), pltpu.VMEM((seq, tq), I16)],
        compiler_params=_cparams(("parallel", "arbitrary")),
        name="indexer_select",
    )(ki, qit, wit)


def _attn_kernel(qidx_ref, kidx_ref, qt_ref, k_ref, vt_ref, b_ref, o_ref, m_ref, acc_ref,
                 *, nheads, hd, hda, tq, tk):
    p = pl.program_id(1)
    qb = qidx_ref[p]
    kb = kidx_ref[p]
    kb_last = ((qb + 1) * tq - 1) // tk

    @pl.when(kb == 0)
    def _():
        m_ref[...] = jnp.full(m_ref.shape, M_INIT, F32)
        acc_ref[...] = jnp.zeros(acc_ref.shape, F32)

    bias = b_ref[...]
    hrows = [slice(h * hd, (h + 1) * hd) for h in range(nheads)]
    arows = [slice(h * hda, (h + 1) * hda) for h in range(nheads)]
    s = [jnp.dot(k_ref[:, r], qt_ref[r, :], preferred_element_type=F32) + bias for r in hrows]
    m_old = [m_ref[h] for h in range(nheads)]
    m_new = [jnp.maximum(m_old[h], jnp.max(s[h], axis=0, keepdims=True)) for h in range(nheads)]
    alpha = [jnp.exp2(m_old[h] - m_new[h]) for h in range(nheads)]
    pr = [jnp.exp2(s[h] - m_new[h]).astype(BF16) for h in range(nheads)]
    pv = [jnp.dot(vt_ref[arows[h], :], pr[h], preferred_element_type=F32) for h in range(nheads)]
    for h in range(nheads):
        m_ref[h] = m_new[h]
        acc_ref[arows[h], :] = alpha[h] * acc_ref[arows[h], :] + pv[h]

    @pl.when(kb == kb_last)
    def _():
        for h in range(nheads):
            a0 = h * hda
            out_t = acc_ref[a0:a0 + hd, :] / acc_ref[a0 + hd:a0 + hd + 1, :]
            o_ref[:, hrows[h]] = out_t.T.astype(o_ref.dtype)


def _masked_attention(qt, k, vt, bias_t, tq=ATTN_TQ, tk=ATTN_TK):
    bsz, width, seq = qt.shape
    nheads = width // A_HEAD_DIM
    hda = A_HEAD_DIM + ONES_ROWS
    tq = min(tq, seq)
    tk = min(tk, seq)
    nq = seq // tq
    pairs = [(i, j) for i in range(nq) for j in range(((i + 1) * tq - 1) // tk + 1)]
    qidx = jnp.asarray([pq for pq, _ in pairs], I32)
    kidx = jnp.asarray([pk for _, pk in pairs], I32)
    grid_spec = pltpu.PrefetchScalarGridSpec(
        num_scalar_prefetch=2,
        grid=(bsz, len(pairs)),
        in_specs=[pl.BlockSpec((None, width, tq), lambda b, p, qi, ki: (b, 0, qi[p])),
                  pl.BlockSpec((None, tk, width), lambda b, p, qi, ki: (b, ki[p], 0)),
                  pl.BlockSpec((None, nheads * hda, tk), lambda b, p, qi, ki: (b, 0, ki[p])),
                  pl.BlockSpec((None, tk, tq), lambda b, p, qi, ki: (b, ki[p], qi[p]))],
        out_specs=pl.BlockSpec((None, tq, width), lambda b, p, qi, ki: (b, qi[p], 0)),
        scratch_shapes=[pltpu.VMEM((nheads, 1, tq), F32),
                        pltpu.VMEM((nheads * hda, tq), F32)],
    )
    return pl.pallas_call(
        functools.partial(_attn_kernel, nheads=nheads, hd=A_HEAD_DIM, hda=hda, tq=tq, tk=tk),
        grid_spec=grid_spec,
        out_shape=jax.ShapeDtypeStruct((bsz, seq, width), BF16),
        compiler_params=_cparams(("parallel", "arbitrary")),
        name="masked_attention",
    )(qidx, kidx, qt, k, vt, bias_t)


def _retention_kernel(qk_ref, v_ref, g_ref, o_ref, state_ref, *, tc, dk, dv):
    @pl.when(pl.program_id(1) == 0)
    def _():
        state_ref[...] = jnp.zeros(state_ref.shape, F32)

    row = lax.broadcasted_iota(I32, (tc, tc), 0)
    col = lax.broadcasted_iota(I32, (tc, tc), 1)
    diff = (row - col).astype(F32)
    pos = lax.broadcasted_iota(I32, (tc, 1), 0).astype(F32)
    for h in range(R_HEADS):
        log_g = math.log1p(-(2.0 ** (-5.0 - h)))
        q = qk_ref[:, h * dk:(h + 1) * dk]
        k = qk_ref[:, (R_HEADS + h) * dk:(R_HEADS + h + 1) * dk]
        v = v_ref[:, h * dv:(h + 1) * dv]
        decay = jnp.where(diff >= 0, jnp.exp(jnp.maximum(diff, 0.0) * log_g), 0.0)
        s = lax.dot_general(q, k, (((1,), (1,)), ((), ())), preferred_element_type=F32) * decay
        intra = jnp.dot(s.astype(BF16), v, preferred_element_type=F32)
        xi = jnp.exp((pos + 1.0) * log_g)
        zeta = jnp.exp((tc - 1.0 - pos) * log_g)
        state = state_ref[h]
        cross = jnp.dot((q.astype(F32) * xi).astype(BF16), state.astype(BF16), preferred_element_type=F32)
        kz = (k.astype(F32) * zeta).astype(BF16)
        kv = lax.dot_general(kz, v, (((0,), (0,)), ((), ())), preferred_element_type=F32)
        state_ref[h] = state * math.exp(tc * log_g) + kv
        ret = intra + cross
        mu = jnp.mean(ret, axis=1, keepdims=True)
        d = ret - mu
        var = jnp.mean(d * d, axis=1, keepdims=True)
        gate = g_ref[:, h * dv:(h + 1) * dv]
        gate = gate / (1.0 + jnp.exp(-gate))
        o_ref[:, h * dv:(h + 1) * dv] = (d * lax.rsqrt(var + 1e-5) * gate).astype(o_ref.dtype)


def _retention(bqk, bv, bg, tc=RET_CHUNK):
    bsz, seq, w2 = bqk.shape
    dk = w2 // (2 * R_HEADS)
    dv = bv.shape[2] // R_HEADS
    tc = min(tc, seq)
    return pl.pallas_call(
        functools.partial(_retention_kernel, tc=tc, dk=dk, dv=dv),
        grid=(bsz, seq // tc),
        in_specs=[pl.BlockSpec((None, tc, w2), lambda b, c: (b, c, 0)),
                  pl.BlockSpec((None, tc, R_HEADS * dv), lambda b, c: (b, c, 0)),
                  pl.BlockSpec((None, tc, R_HEADS * dv), lambda b, c: (b, c, 0))],
        out_specs=pl.BlockSpec((None, tc, R_HEADS * dv), lambda b, c: (b, c, 0)),
        out_shape=jax.ShapeDtypeStruct((bsz, seq, R_HEADS * dv), BF16),
        scratch_shapes=[pltpu.VMEM((R_HEADS, dk, dv), F32)],
        compiler_params=_cparams(("parallel", "arbitrary")),
        name="retention",
    )(bqk, bv, bg)


def _head_sum_matrix(width, hd):
    r = jnp.arange(width)
    return (r[:, None] // hd == r[None, :] // hd).astype(BF16)


def _sigmoid(x):
    return 1.0 / (1.0 + jnp.exp(-x))


def _split_bf16(x):
    hi = x.astype(BF16)
    return hi, (x - hi.astype(F32)).astype(BF16)


def _dot_x3(a, b):
    ah, al = _split_bf16(a)
    bh, bl = _split_bf16(b)
    return (jnp.dot(ah, bh, preferred_element_type=F32) + jnp.dot(ah, bl, preferred_element_type=F32)
            + jnp.dot(al, bh, preferred_element_type=F32))


def _dot_sel(a, sel):
    ah, al = _split_bf16(a)
    return jnp.dot(ah, sel, preferred_element_type=F32) + jnp.dot(al, sel, preferred_element_type=F32)


def _rwkv_prep_kernel(*refs, cw, has_vres):
    if has_vres:
        (c_ref, mu_ref, vec_ref, wlb_ref, alb_ref, glb_ref, hs_ref, vlb_ref, vfirst_ref,
         r_o, lw_o, k_o, v_o, al_o, be_o, g_o, bo_o, carry_ref) = refs
    else:
        (c_ref, mu_ref, vec_ref, wlb_ref, alb_ref, glb_ref, hs_ref,
         r_o, lw_o, k_o, v_o, al_o, be_o, g_o, bo_o, carry_ref) = refs
    tr = c_ref.shape[0]

    @pl.when(pl.program_id(1) == 0)
    def _():
        carry_ref[...] = jnp.zeros(carry_ref.shape, F32)

    c = c_ref[...]
    row = lax.broadcasted_iota(I32, (tr, 1), 0)
    prev = jnp.where(row == 0, carry_ref[0:1, :], pltpu.roll(c, 1, axis=0))
    carry_ref[0:1, :] = c[tr - 1:tr, :]
    cs = c + (prev - c) * mu_ref[...]

    r = cs[:, 0:cw]
    k = cs[:, cw:2 * cw]
    v = cs[:, 2 * cw:3 * cw]
    o = 3 * cw
    wl = cs[:, o:o + LANE]
    al = cs[:, o + LANE:o + 2 * LANE]
    gl = cs[:, o + 2 * LANE:o + 2 * LANE + GATE_LORA]
    w0, a0, k_k, k_a, r_k, v0 = (vec_ref[i:i + 1, :] for i in range(6))

    z = -(w0 + jnp.dot(jnp.tanh(wl).astype(BF16), wlb_ref[...], preferred_element_type=F32))
    softplus = jnp.maximum(z, 0.0) + jnp.log(1.0 + jnp.exp(-jnp.abs(z)))
    lw = -jnp.exp(-softplus - 0.5)
    a = _sigmoid(a0 + jnp.dot(al.astype(BF16), alb_ref[...], preferred_element_type=F32))
    g = jnp.dot(_sigmoid(gl).astype(BF16), glb_ref[...], preferred_element_type=F32)
    if has_vres:
        vr = cs[:, o + 2 * LANE + GATE_LORA:o + 2 * LANE + GATE_LORA + vlb_ref.shape[0]]
        mix = _sigmoid(v0 + jnp.dot(vr.astype(BF16), vlb_ref[...], preferred_element_type=F32))
        v = v + (vfirst_ref[...] - v) * mix
    hs = hs_ref[...]
    kk = k * k_k
    ss = _dot_sel(kk * kk, hs)
    kk = kk / jnp.maximum(jnp.sqrt(ss), 1e-12)
    kh = k * (1.0 + (a - 1.0) * k_a)
    rk = _dot_sel(r * kh * r_k, hs)

    r_o[...] = r
    lw_o[...] = lw
    k_o[...] = kh
    v_o[...] = v
    al_o[...] = -kk
    be_o[...] = kk * a
    g_o[...] = g
    bo_o[...] = rk * v


def _rwkv_prep(cproj, mu, vecs, wlb, alb, glb, vlb, v_first, cw, tr=PREP_ROWS):
    bsz, seq, wc = cproj.shape
    tr = min(tr, seq)
    has_vres = vlb is not None
    hs = _head_sum_matrix(cw, W_HEAD_DIM)
    full = lambda a: pl.BlockSpec(a.shape, lambda b, t: (0,) * a.ndim)
    tok = pl.BlockSpec((None, tr, cw), lambda b, t: (b, t, 0))
    ins = [cproj, mu, vecs, wlb, alb, glb, hs]
    in_specs = [pl.BlockSpec((None, tr, wc), lambda b, t: (b, t, 0)),
                full(mu), full(vecs), full(wlb), full(alb), full(glb), full(hs)]
    if has_vres:
        ins += [vlb, v_first]
        in_specs += [full(vlb), tok]
    out = jax.ShapeDtypeStruct((bsz, seq, cw), F32)
    return pl.pallas_call(
        functools.partial(_rwkv_prep_kernel, cw=cw, has_vres=has_vres),
        grid=(bsz, seq // tr),
        in_specs=in_specs,
        out_specs=[tok] * 8,
        out_shape=[out] * 8,
        scratch_shapes=[pltpu.VMEM((SUBLANE, wc), F32)],
        compiler_params=_cparams(("parallel", "arbitrary")),
        name="rwkv_prep",
    )(*ins)


def _bdot(a, b):
    return jnp.dot(a.astype(BF16), b.astype(BF16), preferred_element_type=F32)


def _bdot_tn(a, b):
    return lax.dot_general(a.astype(BF16), b.astype(BF16), (((0,), (0,)), ((), ())),
                           preferred_element_type=F32)


def _rwkv_chunk_kernel(r_ref, lw_ref, k_ref, v_ref, al_ref, be_ref, g_o, y0_o, m_o, z0_o, *, tc, nck, npairs):
    row = lax.broadcasted_iota(I32, (tc, tc), 0)
    col = lax.broadcasted_iota(I32, (tc, tc), 1)
    incl = row >= col
    strict = row > col
    tri = incl.astype(BF16)
    eye = (row == col).astype(F32)
    lane = lax.broadcasted_iota(I32, (1, LANE), 1)
    mh0 = (lane // W_HEAD_DIM) == 0
    prow = lax.broadcasted_iota(I32, (LANE, LANE), 0)
    pcol = lax.broadcasted_iota(I32, (LANE, LANE), 1)
    same_head = (prow // W_HEAD_DIM) == (pcol // W_HEAD_DIM)

    lw_all = lw_ref[...]
    hi = lw_all.astype(BF16)
    rem = lw_all - hi.astype(F32)
    mid = rem.astype(BF16)
    lo = (rem - mid.astype(F32)).astype(BF16)
    cum_all = jnp.concatenate(
        [sum(jnp.dot(tri, part[c * tc:(c + 1) * tc], preferred_element_type=F32) for part in (hi, mid, lo))
         for c in range(nck)], axis=0)

    pairs = range(nck * npairs)
    heads = [(p, h) for p in pairs for h in range(2)]
    sls = [slice((p % npairs) * LANE, (p % npairs + 1) * LANE) for p in pairs]
    rws = [slice((p // npairs) * tc, (p // npairs + 1) * tc) for p in pairs]
    cum = [cum_all[rws[p], sls[p]] for p in pairs]
    tot = [c[tc - 1:tc, :] for c in cum]
    p_inv = [jnp.exp(-cum[p]) for p in pairs]
    p_end = [jnp.exp(tot[p] - cum[p]) for p in pairs]
    at = [al_ref[rws[p], sls[p]] * jnp.exp(cum[p] - lw_all[rws[p], sls[p]]) for p in pairs]
    rt = [r_ref[rws[p], sls[p]] * jnp.exp(cum[p]) for p in pairs]
    bh = [be_ref[rws[p], sls[p]] * p_end[p] for p in pairs]
    khat = [k_ref[rws[p], sls[p]] * p_end[p] for p in pairs]
    v = [v_ref[rws[p], sls[p]] for p in pairs]
    rhs = [jnp.concatenate([be_ref[rws[p], sls[p]] * p_inv[p], k_ref[rws[p], sls[p]] * p_inv[p]],
                           axis=0).astype(BF16) for p in pairs]

    a_ab, a_ak, qcat = {}, {}, {}
    for p, h in heads:
        mh = (lane // W_HEAD_DIM) == h
        lhs = jnp.concatenate([jnp.where(mh, at[p], 0.0), jnp.where(mh, rt[p], 0.0)], axis=0).astype(BF16)
        x = lax.dot_general(lhs, rhs[p], (((1,), (1,)), ((), ())), preferred_element_type=F32)
        a_ab[p, h] = jnp.where(strict, x[:tc, :tc], 0.0)
        a_ak[p, h] = jnp.where(strict, x[:tc, tc:], 0.0)
        qcat[p, h] = jnp.concatenate([jnp.where(incl, x[tc:, tc:], 0.0),
                                      jnp.where(incl, x[tc:, :tc], 0.0)], axis=1)

    t_inv = {ph: eye + a_ab[ph] for ph in heads}
    pw = {ph: _bdot(a_ab[ph], a_ab[ph]) for ph in heads}
    akv = {(p, h): _bdot(a_ak[p, h], v[p]) for p, h in heads}
    n = 2
    while n < tc:
        for ph in heads:
            res = _bdot(jnp.concatenate([pw[ph], t_inv[ph]], axis=0), pw[ph])
            t_inv[ph] = t_inv[ph] + res[tc:]
            pw[ph] = res[:tc]
        n *= 2
    tw = {(p, h): _bdot(t_inv[p, h], jnp.concatenate([at[p], akv[p, h]], axis=1)) for p, h in heads}

    zero = jnp.zeros((tc, LANE), F32)
    w = [jnp.where(mh0, tw[p, 0][:, :LANE], tw[p, 1][:, :LANE]) for p in pairs]
    u0 = [jnp.where(mh0, tw[p, 0][:, LANE:], tw[p, 1][:, LANE:]) for p in pairs]
    vu = [jnp.concatenate([jnp.concatenate([v[p], zero], axis=1),
                           jnp.concatenate([u0[p], w[p]], axis=1)], axis=0) for p in pairs]
    yg = {(p, h): _bdot(qcat[p, h], vu[p]) for p, h in heads}
    m_mat = [_bdot_tn(bh[p], w[p]) for p in pairs]
    z0 = [_bdot_tn(jnp.concatenate([bh[p], khat[p]], axis=0), jnp.concatenate([u0[p], v[p]], axis=0))
          for p in pairs]
    for p in pairs:
        y0_o[rws[p], sls[p]] = jnp.where(mh0, yg[p, 0][:, :LANE], yg[p, 1][:, :LANE])
        g_o[rws[p], sls[p]] = rt[p] + jnp.where(mh0, yg[p, 0][:, LANE:], yg[p, 1][:, LANE:])
        m_o[p // npairs, p % npairs] = jnp.where(same_head, m_mat[p], 0.0) + jnp.where(
            prow == pcol, jnp.broadcast_to(jnp.exp(tot[p]), (LANE, LANE)), 0.0)
        z0_o[p // npairs, p % npairs] = jnp.where(same_head, z0[p], 0.0)


def _rwkv_chunk_ops(r, lw, kh, v, alpha, beta, tc=CHUNK, nck=RWKV_CHUNKS):
    bsz, seq, cw = r.shape
    npairs = cw // LANE
    nc = seq // tc
    nck = min(nck, nc)
    tok = pl.BlockSpec((None, nck * tc, cw), lambda b, c: (b, c, 0))
    mat = pl.BlockSpec((None, nck, npairs, LANE, LANE), lambda b, c: (b, c, 0, 0, 0))
    tok_shape = jax.ShapeDtypeStruct((bsz, seq, cw), F32)
    mat_shape = jax.ShapeDtypeStruct((bsz, nc, npairs, LANE, LANE), F32)
    return pl.pallas_call(
        functools.partial(_rwkv_chunk_kernel, tc=tc, nck=nck, npairs=npairs),
        grid=(bsz, nc // nck),
        in_specs=[tok] * 6,
        out_specs=[tok, tok, mat, mat],
        out_shape=[tok_shape, tok_shape, mat_shape, mat_shape],
        compiler_params=_cparams(("parallel", "parallel")),
        name="rwkv_chunk_ops",
    )(r, lw, kh, v, alpha, beta)


def _rwkv_scan_kernel(g_ref, y0_ref, m_ref, z0_ref, bo_ref, gate_ref, ln_ref, hs_ref, o_ref, state_ref,
                      *, tc, nch, npairs):
    @pl.when(pl.program_id(1) == 0)
    def _():
        state_ref[...] = jnp.zeros(state_ref.shape, F32)

    hs = hs_ref[...]
    inv_n = 1.0 / W_HEAD_DIM
    sls = [slice(p * LANE, (p + 1) * LANE) for p in range(npairs)]
    st = [state_ref[p] for p in range(npairs)]
    for ch in range(nch):
        rows = slice(ch * tc, (ch + 1) * tc)
        gm = [_dot_x3(jnp.concatenate([g_ref[rows, sls[p]], m_ref[ch, p]], axis=0), st[p]) for p in range(npairs)]
        y = [gm[p][:tc] + y0_ref[rows, sls[p]] for p in range(npairs)]
        st = [gm[p][tc:] + z0_ref[ch, p] for p in range(npairs)]
        mu = [_dot_sel(y[p], hs) * inv_n for p in range(npairs)]
        d = [y[p] - mu[p] for p in range(npairs)]
        var = [_dot_sel(d[p] * d[p], hs) * inv_n for p in range(npairs)]
        for p in range(npairs):
            yn = d[p] * lax.rsqrt(var[p] + LNX_EPS) * ln_ref[0:1, sls[p]] + ln_ref[1:2, sls[p]]
            o_ref[rows, sls[p]] = ((yn + bo_ref[rows, sls[p]]) * gate_ref[rows, sls[p]]).astype(o_ref.dtype)
    for p in range(npairs):
        state_ref[p] = st[p]


def _rwkv_scan(g, y0, m, z0, bonus, gate, ln, tc=CHUNK, nch=RWKV_CHUNKS):
    bsz, seq, cw = g.shape
    npairs = cw // LANE
    nch = min(nch, seq // tc)
    hs = _head_sum_matrix(LANE, W_HEAD_DIM)
    tok = pl.BlockSpec((None, tc * nch, cw), lambda b, c: (b, c, 0))
    mat = pl.BlockSpec((None, nch, npairs, LANE, LANE), lambda b, c: (b, c, 0, 0, 0))
    return pl.pallas_call(
        functools.partial(_rwkv_scan_kernel, tc=tc, nch=nch, npairs=npairs),
        grid=(bsz, seq // (tc * nch)),
        in_specs=[tok, tok, mat, mat, tok, tok,
                  pl.BlockSpec(ln.shape, lambda b, c: (0, 0)),
                  pl.BlockSpec(hs.shape, lambda b, c: (0, 0))],
        out_specs=tok,
        out_shape=jax.ShapeDtypeStruct((bsz, seq, cw), BF16),
        scratch_shapes=[pltpu.VMEM((npairs, LANE, LANE), F32)],
        compiler_params=_cparams(("parallel", "arbitrary")),
        name="rwkv_scan",
    )(g, y0, m, z0, bonus, gate, ln, hs)


def _out_proj_kernel(x_ref, a_ref, b_ref, c_ref, w_ref, o_ref, *, wa, wb):
    acc = jnp.dot(a_ref[...], w_ref[0:wa, :], preferred_element_type=F32)
    acc = acc + jnp.dot(b_ref[...], w_ref[wa:wa + wb, :], preferred_element_type=F32)
    acc = acc + jnp.dot(c_ref[...], w_ref[wa + wb:, :], preferred_element_type=F32)
    o_ref[...] = x_ref[...] + acc


def _out_proj(x2d, oa, ob, oc, w, tm=TM, tn=1024):
    m, d = x2d.shape
    wa, wb, wc = oa.shape[1], ob.shape[1], oc.shape[1]
    tm = min(tm, m)
    tn = min(tn, d)
    return pl.pallas_call(
        functools.partial(_out_proj_kernel, wa=wa, wb=wb),
        grid=(m // tm, d // tn),
        in_specs=[pl.BlockSpec((tm, tn), lambda i, j: (i, j)),
                  pl.BlockSpec((tm, wa), lambda i, j: (i, 0)),
                  pl.BlockSpec((tm, wb), lambda i, j: (i, 0)),
                  pl.BlockSpec((tm, wc), lambda i, j: (i, 0)),
                  pl.BlockSpec((wa + wb + wc, tn), lambda i, j: (0, j))],
        out_specs=pl.BlockSpec((tm, tn), lambda i, j: (i, j)),
        out_shape=jax.ShapeDtypeStruct((m, d), F32),
        compiler_params=_cparams(("parallel", "arbitrary")),
        name="out_proj",
    )(x2d, oa, ob, oc, w)


def _gate_up_kernel(h_ref, wg_ref, wu_ref, o_ref):
    h = h_ref[...]
    gate = jnp.dot(h, wg_ref[...], preferred_element_type=F32)
    up = jnp.dot(h, wu_ref[...], preferred_element_type=F32)
    o_ref[...] = (gate / (1.0 + jnp.exp(-gate)) * up).astype(o_ref.dtype)


def _gate_up(h, w_gate_up, tm=TM, tf_cap=512):
    m, d = h.shape
    dff = w_gate_up.shape[1] // 2
    tm = min(tm, m)
    tf = _pick_tile(dff, tf_cap)
    nf = dff // tf
    return pl.pallas_call(
        _gate_up_kernel,
        grid=(m // tm, nf),
        in_specs=[pl.BlockSpec((tm, d), lambda i, j: (i, 0)),
                  pl.BlockSpec((d, tf), lambda i, j: (0, j)),
                  pl.BlockSpec((d, tf), lambda i, j: (0, j + nf))],
        out_specs=pl.BlockSpec((tm, tf), lambda i, j: (i, j)),
        out_shape=jax.ShapeDtypeStruct((m, dff), BF16),
        compiler_params=_cparams(("parallel", "arbitrary")),
        name="ffn_gate_up",
    )(h, w_gate_up, w_gate_up)


def _down_kernel(x_ref, a_ref, w_ref, o_ref):
    o_ref[...] = x_ref[...] + jnp.dot(a_ref[...], w_ref[...], preferred_element_type=F32)


def _down_proj(x2d, act, w, tm=TM, tn=512):
    m, d = x2d.shape
    dff = act.shape[1]
    tm = min(tm, m)
    tn = min(tn, d)
    return pl.pallas_call(
        _down_kernel,
        grid=(m // tm, d // tn),
        in_specs=[pl.BlockSpec((tm, tn), lambda i, j: (i, j)),
                  pl.BlockSpec((tm, dff), lambda i, j: (i, 0)),
                  pl.BlockSpec((dff, tn), lambda i, j: (0, j))],
        out_specs=pl.BlockSpec((tm, tn), lambda i, j: (i, j)),
        out_shape=jax.ShapeDtypeStruct((m, d), F32),
        compiler_params=_cparams(("parallel", "arbitrary")),
        name="ffn_down",
    )(x2d, act, w)


def _pad_cols(w, width):
    return jnp.pad(w, ((0, 0), (0, width - w.shape[1])))


def _pad_rows(w, height):
    return jnp.pad(w, ((0, height - w.shape[0]), (0, 0)))


def _pad_vec(v, width):
    return jnp.pad(v, (0, width - v.shape[0]))


def kernel(x, norm_mix_g, w_in, w_in_vres, rwkv_mu, rwkv_mu_vres, rwkv_w0, rwkv_w_lora_b, rwkv_a0,
           rwkv_a_lora_b, rwkv_v0, rwkv_v_lora_b, rwkv_g_lora_b, rwkv_k_k, rwkv_k_a, rwkv_r_k,
           rwkv_lnx_g, rwkv_lnx_b, w_out, norm_ffn_g, w_gate_up, w_down, final_norm_g):
    bsz, seq, d_model = x.shape
    depth = w_in.shape[0]
    m = bsz * seq
    d_mix = w_out.shape[1]
    a_w = d_mix // 2
    qi_w = IDX_HEADS * IDX_DIM
    b_v_w = d_mix // 4
    b_qk_w = b_v_w // 2
    c_w = d_mix // 4
    r_qk_dim = b_qk_w // R_HEADS
    top_k = min(TOPK_MAX, seq // 4)
    vres_pad = 2 * LANE

    o_q, o_k, o_v = 0, a_w, 2 * a_w
    o_qi = 3 * a_w
    o_ki = o_qi + qi_w
    o_wi = o_ki + IDX_DIM
    o_bq = o_wi + IDX_HEADS
    o_bk = o_bq + b_qk_w
    o_bv = o_bk + b_qk_w
    o_bg = o_bv + b_v_w
    o_c = o_bg + b_v_w
    o_wl = o_c + 3 * c_w
    o_al = o_wl + DECAY_LORA
    o_gl = o_al + AAA_LORA

    a_rot = A_HEAD_DIM // ROPE_FRAC
    tq_ = _rope_tables(seq, A_HEAD_DIM, a_rot, ROPE_THETA, scale=A_HEAD_DIM ** -0.5 * math.log2(math.e))
    tk_ = _rope_tables(seq, A_HEAD_DIM, a_rot, ROPE_THETA)
    tab_q = tuple(t[None] for t in tq_)
    tab_k = tuple(t[None] for t in tk_)
    i_rot = IDX_DIM // ROPE_FRAC
    tab_qi = tuple(t[None] for t in _rope_tables(seq, IDX_DIM, i_rot, ROPE_THETA))
    lane = jnp.arange(LANE)
    kiwi_pass = jnp.where(lane < IDX_DIM, 1.0, IDX_W_SCALE).astype(F32)
    c_kw, s1_kw, s2_kw = _rope_tables(seq, LANE, i_rot, ROPE_THETA, pass_scale=kiwi_pass)
    tab_kiwi = (c_kw[None], s1_kw[None], s2_kw[None])
    tbq = _rope_tables(seq, r_qk_dim, r_qk_dim, R_THETA)
    tbk = _rope_tables(seq, r_qk_dim, r_qk_dim, R_THETA, scale=r_qk_dim ** -0.5)
    tab_bqk = tuple(jnp.stack([a, b]) for a, b in zip(tbq, tbk))

    x2d = x.reshape(m, d_model)
    v_first = None
    for l in range(depth):
        wl_ = w_in[l]
        cols = lambda o, n: wl_[:, o:o + n]
        w_q = cols(o_q, a_w).astype(BF16)
        w_k = cols(o_k, a_w).astype(BF16)
        w_v = cols(o_v, a_w).astype(BF16)
        w_bv = cols(o_bv, b_v_w).astype(BF16)
        w_qi = cols(o_qi, qi_w).astype(BF16)
        w_kiwi = _pad_cols(cols(o_ki, IDX_DIM + IDX_HEADS), LANE).astype(BF16)
        w_bqk = cols(o_bq, 2 * b_qk_w).astype(BF16)
        w_bg = cols(o_bg, b_v_w).astype(BF16)
        vres_w = (w_in_vres[l - 1] if l > 0 else jnp.zeros((d_model, MV_LORA), F32))
        w_c = jnp.concatenate([cols(o_c, 3 * c_w), _pad_cols(cols(o_wl, DECAY_LORA), LANE),
                               _pad_cols(cols(o_al, AAA_LORA), LANE), cols(o_gl, GATE_LORA),
                               _pad_cols(vres_w, vres_pad)], axis=1).astype(BF16)
        mu = rwkv_mu[l]
        mu_vres = rwkv_mu_vres[l - 1] if l > 0 else jnp.zeros((MV_LORA,), F32)
        mu_c = jnp.concatenate([mu[:3 * c_w], _pad_vec(mu[3 * c_w:3 * c_w + DECAY_LORA], LANE),
                                _pad_vec(mu[3 * c_w + DECAY_LORA:3 * c_w + DECAY_LORA + AAA_LORA], LANE),
                                mu[3 * c_w + DECAY_LORA + AAA_LORA:], _pad_vec(mu_vres, vres_pad)])[None, :]

        h = _rmsnorm(x2d, norm_mix_g[l], BF16)
        qt = _mm_rope(h, w_q, tab_q, a_rot // 2, seq, a_w, BF16, transpose=True)
        ak = _mm_rope(h, w_k, tab_k, a_rot // 2, seq, a_w, BF16).reshape(bsz, seq, a_w)
        vt = _mm_plain(h, w_v, BF16, seq, transpose=True, pad_rows=ONES_ROWS)
        bv = _mm_plain(h, w_bv, BF16, seq, tn_cap=512).reshape(bsz, seq, b_v_w)
        qit = _mm_rope(h, w_qi, tab_qi, i_rot // 2, seq, qi_w, BF16, transpose=True)
        kiwi = _mm_rope(h, w_kiwi, tab_kiwi, i_rot // 2, seq, LANE, F32).reshape(bsz, seq, LANE)
        bqk = _mm_rope(h, w_bqk, tab_bqk, r_qk_dim // 2, seq, b_qk_w, BF16).reshape(bsz, seq, 2 * b_qk_w)
        bg = _mm_plain(h, w_bg, F32, seq).reshape(bsz, seq, b_v_w)
        cproj = _mm_plain(h, w_c, F32, seq, tn_cap=768).reshape(bsz, seq, w_c.shape[1])

        ki = kiwi[:, :, :IDX_DIM].astype(BF16)
        wit = jnp.swapaxes(kiwi[:, :, IDX_DIM:IDX_DIM + IDX_HEADS], 1, 2)
        bias_t = _indexer_bias(ki, qit, wit, top_k)
        out_a = _masked_attention(qt, ak, vt, bias_t)

        out_b = _retention(bqk, bv, bg)

        vecs = jnp.stack([rwkv_w0[l], rwkv_a0[l], rwkv_k_k[l], rwkv_k_a[l], rwkv_r_k[l].reshape(-1),
                          rwkv_v0[l - 1] if l > 0 else jnp.zeros((c_w,), F32),
                          jnp.zeros((c_w,), F32), jnp.zeros((c_w,), F32)])
        wlb = _pad_rows(rwkv_w_lora_b[l], LANE).astype(BF16)
        alb = _pad_rows(rwkv_a_lora_b[l], LANE).astype(BF16)
        glb = rwkv_g_lora_b[l].astype(BF16)
        vlb = _pad_rows(rwkv_v_lora_b[l - 1], vres_pad).astype(BF16) if l > 0 else None
        r_, lw_, kh_, v_, al_, be_, g_, bo_ = _rwkv_prep(cproj, mu_c, vecs, wlb, alb, glb, vlb, v_first, c_w)
        if l == 0:
            v_first = v_
        gm, y0, mm, z0 = _rwkv_chunk_ops(r_, lw_, kh_, v_, al_, be_)
        ln = jnp.stack([rwkv_lnx_g[l], rwkv_lnx_b[l]] + [jnp.zeros((c_w,), F32)] * 6)
        out_c = _rwkv_scan(gm, y0, mm, z0, bo_, g_, ln)

        x2d = _out_proj(x2d, out_a.reshape(m, a_w), out_b.reshape(m, b_v_w), out_c.reshape(m, c_w),
                        w_out[l].astype(BF16))

        h = _rmsnorm(x2d, norm_ffn_g[l], BF16)
        act = _gate_up(h, w_gate_up[l].astype(BF16))
        x2d = _down_proj(x2d, act, w_down[l].astype(BF16))

    return _rmsnorm(x2d, final_norm_g, F32).reshape(bsz, seq, d_model)
```

```python
import functools
import math

import jax
import jax.numpy as jnp
from jax import lax
from jax.experimental import pallas as pl
from jax.experimental.pallas import tpu as pltpu

F32 = jnp.float32
BF16 = jnp.bfloat16
I32 = jnp.int32
I16 = jnp.int16

CHUNK = 64
A_HEAD_DIM = 128
IDX_HEADS = 16
IDX_DIM = 64
IDX_W_SCALE = (IDX_HEADS * IDX_DIM) ** -0.5
TOPK_MAX = 256
ROPE_THETA = 500000.0
ROPE_FRAC = 4
R_HEADS = 4
R_THETA = 10000.0
W_HEAD_DIM = 64
DECAY_LORA = 96
AAA_LORA = 96
MV_LORA = 64
GATE_LORA = 256
LNX_EPS = 64e-5
RMS_EPS = 1e-5

LANE = 128
SUBLANE = 8
PACKED_ROWS = 16
VMEM_LIMIT = 56 * 1024 * 1024
TM = 1024
TM_NORM = 512
IDX_TQ = 256
IDX_TKS = 128
IDX_TKC = 1024
SCORE_UNROLL = 4
COUNT_CHAINS = 8
ATTN_TQ = 512
ATTN_TK = 1024
ONES_ROWS = SUBLANE
RET_CHUNK = 256
PREP_ROWS = 256
RWKV_CHUNKS = 4

NEG_BIAS = -1e30
M_INIT = -1e20
INT_MIN = -2147483648
INT_MAX = 2147483647
I16_MIN = -32768


def _cparams(sem):
    return pltpu.CompilerParams(dimension_semantics=sem, vmem_limit_bytes=VMEM_LIMIT)


def _pick_tile(n, cap):
    best = LANE
    t = LANE
    while t <= min(n, cap):
        if n % t == 0:
            best = t
        t += LANE
    return best


def _rmsnorm_kernel(x_ref, g_ref, o_ref):
    x = x_ref[...]
    ms = jnp.mean(x * x, axis=-1, keepdims=True)
    o_ref[...] = (x * lax.rsqrt(ms + RMS_EPS) * g_ref[...]).astype(o_ref.dtype)


def _rmsnorm(x2d, g, out_dtype, tm=TM_NORM):
    m, d = x2d.shape
    return pl.pallas_call(
        _rmsnorm_kernel,
        grid=(m // tm,),
        in_specs=[pl.BlockSpec((tm, d), lambda i: (i, 0)),
                  pl.BlockSpec((1, d), lambda i: (0, 0))],
        out_specs=pl.BlockSpec((tm, d), lambda i: (i, 0)),
        out_shape=jax.ShapeDtypeStruct((m, d), out_dtype),
        compiler_params=_cparams(("parallel",)),
        name="rmsnorm",
    )(x2d, g.reshape(1, d).astype(F32))


def _store_cols(o_ref, blk, x, transpose, pad_rows):
    if transpose:
        r0 = blk * (LANE + pad_rows)
        o_ref[r0:r0 + LANE, :] = x.T.astype(o_ref.dtype)
        if pad_rows:
            o_ref[r0 + LANE:r0 + LANE + pad_rows, :] = jnp.ones((pad_rows, x.shape[0]), o_ref.dtype)
    else:
        o_ref[:, blk * LANE:(blk + 1) * LANE] = x.astype(o_ref.dtype)


def _proj_out_spec(m, n, tm, tn, seq, out_dtype, transpose, pad_rows):
    if not transpose:
        return pl.BlockSpec((tm, tn), lambda i, j: (i, j)), jax.ShapeDtypeStruct((m, n), out_dtype)
    tpb = seq // tm
    rows = lambda cols: cols // LANE * (LANE + pad_rows)
    return (pl.BlockSpec((None, rows(tn), tm), lambda i, j: (i // tpb, j, i % tpb)),
            jax.ShapeDtypeStruct((m // seq, rows(n), seq), out_dtype))


def _mm_plain_kernel(a_ref, w_ref, o_ref, *, transpose, pad_rows):
    acc = jnp.dot(a_ref[...], w_ref[...], preferred_element_type=F32)
    if not transpose:
        o_ref[...] = acc.astype(o_ref.dtype)
        return
    for blk in range(acc.shape[1] // LANE):
        _store_cols(o_ref, blk, acc[:, blk * LANE:(blk + 1) * LANE], transpose, pad_rows)


def _mm_plain(a, w, out_dtype, seq, tm=TM, tn_cap=1024, transpose=False, pad_rows=0):
    m, k = a.shape
    n = w.shape[1]
    tm = min(tm, seq)
    tn = _pick_tile(n, tn_cap)
    out_spec, out_shape = _proj_out_spec(m, n, tm, tn, seq, out_dtype, transpose, pad_rows)
    return pl.pallas_call(
        functools.partial(_mm_plain_kernel, transpose=transpose, pad_rows=pad_rows),
        grid=(m // tm, n // tn),
        in_specs=[pl.BlockSpec((tm, k), lambda i, j: (i, 0)),
                  pl.BlockSpec((k, tn), lambda i, j: (0, j))],
        out_specs=out_spec,
        out_shape=out_shape,
        compiler_params=_cparams(("parallel", "arbitrary")),
        name="proj_plain",
    )(a, w)


def _mm_rope_kernel(a_ref, w_ref, c_ref, s1_ref, s2_ref, o_ref, *, half, transpose):
    acc = jnp.dot(a_ref[...], w_ref[...], preferred_element_type=F32)
    c, s1, s2 = c_ref[...], s1_ref[...], s2_ref[...]
    for blk in range(acc.shape[1] // LANE):
        x = acc[:, blk * LANE:(blk + 1) * LANE]
        up = pltpu.roll(x, LANE - half, axis=1)
        dn = pltpu.roll(x, half, axis=1)
        _store_cols(o_ref, blk, x * c + up * s1 + dn * s2, transpose, 0)


def _mm_rope(a, w, tables, half, seq, tn, out_dtype, tm=TM, transpose=False):
    m, k = a.shape
    n = w.shape[1]
    tm = min(tm, seq)
    tpb = seq // tm
    tab_spec = pl.BlockSpec((None, tm, LANE), lambda i, j: (j, i % tpb, 0))
    out_spec, out_shape = _proj_out_spec(m, n, tm, tn, seq, out_dtype, transpose, 0)
    return pl.pallas_call(
        functools.partial(_mm_rope_kernel, half=half, transpose=transpose),
        grid=(m // tm, n // tn),
        in_specs=[pl.BlockSpec((tm, k), lambda i, j: (i, 0)),
                  pl.BlockSpec((k, tn), lambda i, j: (0, j)),
                  tab_spec, tab_spec, tab_spec],
        out_specs=out_spec,
        out_shape=out_shape,
        compiler_params=_cparams(("parallel", "arbitrary")),
        name="proj_rope",
    )(a, w, *tables)


def _rope_tables(seq, group, rot_dim, theta, scale=1.0, pass_scale=None):
    half = rot_dim // 2
    freqs = jnp.power(F32(theta), -jnp.arange(half, dtype=F32) / half)
    ang = jnp.arange(seq, dtype=F32)[:, None] * freqs[None, :]
    cos, sin = jnp.cos(ang), jnp.sin(ang)
    lane = jnp.arange(LANE) % group
    idx = lane % half
    cosl, sinl = cos[:, idx], sin[:, idx]
    passv = jnp.ones((LANE,), F32) if pass_scale is None else pass_scale
    c = jnp.where(lane < rot_dim, cosl, passv[None, :])
    s1 = jnp.where(lane < half, -sinl, 0.0)
    s2 = jnp.where((lane >= half) & (lane < rot_dim), sinl, 0.0)
    return c * scale, s1 * scale, s2 * scale


def _indexer_kernel(ki_ref, qit_ref, wit_ref, bias_ref, key_ref, *, tq, tks, tkc, top_k, seq):
    qb = pl.program_id(1)
    span = tks * SCORE_UNROLL
    nks = ((qb + 1) * tq + span - 1) // span * SCORE_UNROLL
    nkc = (nks * tks + tkc - 1) // tkc
    q_chunk = (qb * tq + lax.broadcasted_iota(I32, (1, tq), 1)) // CHUNK
    w = wit_ref[...]

    def score_block(kb, rmax):
        k0 = pl.multiple_of(kb * tks, tks)
        kblk = ki_ref[pl.ds(k0, tks), :]
        acc = jnp.zeros((tks, tq), F32)
        for h in range(IDX_HEADS):
            x = jnp.dot(kblk, qit_ref[h * IDX_DIM:(h + 1) * IDX_DIM, :], preferred_element_type=F32)
            acc = acc + jnp.maximum(x, 0.0) * w[h:h + 1, :]
        k_chunk = (k0 + lax.broadcasted_iota(I32, (tks, 1), 0)) // CHUNK
        bits = lax.bitcast_convert_type(acc, I32)
        key = bits ^ ((bits >> 31) & jnp.int32(INT_MAX))
        key = jnp.where(k_chunk <= q_chunk, key, jnp.int32(INT_MIN))
        key_ref[pl.ds(k0, tks), :] = key
        return jnp.maximum(rmax, jnp.max(key.reshape(tks // SUBLANE, SUBLANE, tq), axis=0))

    def score_step(kbu, rmax):
        for u in range(SCORE_UNROLL):
            rmax = score_block(kbu * SCORE_UNROLL + u, rmax)
        return rmax

    rmax = lax.fori_loop(0, nks // SCORE_UNROLL, score_step, jnp.full((SUBLANE, tq), INT_MIN, I32))
    colmax = jnp.max(rmax, axis=0, keepdims=True)

    def pad_block(kb, carry):
        k0 = pl.multiple_of(kb * tks, tks)
        key_ref[pl.ds(k0, tks), :] = jnp.full((tks, tq), INT_MIN, I32)
        return carry

    lax.fori_loop(nks, nkc * (tkc // tks), pad_block, 0)

    def tree_sum(terms):
        while len(terms) > 1:
            terms = [a + b for a, b in zip(terms[0::2], terms[1::2])]
        return terms[0]

    def count_ge(cand):
        def count_block(kb, cnt):
            k0 = pl.multiple_of(kb * tkc, tkc)
            part = tkc // COUNT_CHAINS
            sums = []
            for g in range(COUNT_CHAINS):
                ind = jnp.where(key_ref[pl.ds(k0 + g * part, part), :] >= cand, 1, 0)
                sums.append(jnp.sum(ind.reshape(part // SUBLANE, SUBLANE, tq), axis=0))
            return cnt + tree_sum(sums)

        cnt = lax.fori_loop(0, nkc, count_block, jnp.zeros((SUBLANE, tq), I32))
        return jnp.sum(cnt, axis=0, keepdims=True)

    def probe_step(count_fn, state, mid):
        lo, hi, cnt_lo, cnt_hi = state
        cnt = count_fn(mid)
        ge = cnt >= top_k
        new_hi = jnp.where(cnt == top_k, mid + 1, jnp.where(ge, hi, mid))
        return (jnp.where(ge, mid, lo), new_hi, jnp.where(ge, cnt, cnt_lo), jnp.where(ge, cnt_hi, cnt))

    def midpoint(state):
        lo, hi = state[0], state[1]
        return (lo >> 1) + (hi >> 1) + (lo & hi & 1)

    def n_active(state):
        return jnp.max(((state[0] + 1) < state[1]).astype(I32))

    binade = 1 << 23

    def bisect(state):
        def next_mid(st, it):
            mid = midpoint(st)
            step = jnp.int32(3 * binade) << jnp.minimum(it, 6)
            descend = (st[0] == jnp.int32(INT_MIN)) & (st[1] > jnp.int32(INT_MIN) + step)
            return jnp.where(descend, jnp.maximum(st[1] - step, mid), mid)

        def body(carry):
            it, _, st = carry
            st = probe_step(count_ge, st, next_mid(st, it + 1))
            st = probe_step(count_ge, st, next_mid(st, it + 2))
            return it + 2, n_active(st), st

        return lax.while_loop(lambda c: (c[1] > 0) & (c[0] < 40), body, (jnp.int32(0), n_active(state), state))[2]

    big = jnp.int32(1 << 30)
    state = (jnp.full((1, tq), INT_MIN, I32), colmax + 1, jnp.full((1, tq), big, I32), jnp.zeros((1, tq), I32))
    first = jnp.where(colmax > jnp.int32(INT_MIN + 4 * binade), colmax - jnp.int32(3 * binade),
                      jnp.int32(INT_MIN + 1))
    lo, _, cnt_lo, cnt_hi = bisect(probe_step(count_ge, state, first))

    thr = jnp.maximum(lo, jnp.int32(INT_MIN + 1))
    tie_q = (cnt_lo > top_k) & (lo > jnp.int32(INT_MIN))
    any_tie = jnp.max(tie_q.astype(I32))

    @pl.when(any_tie == 0)
    def _():
        def write_block(kb, carry):
            k0 = pl.multiple_of(kb * tks, tks)
            bias_ref[pl.ds(k0, tks), :] = jnp.where(key_ref[pl.ds(k0, tks), :] >= thr, 0.0, NEG_BIAS)
            return carry

        lax.fori_loop(0, nks, write_block, 0)

    @pl.when(any_tie != 0)
    def _():
        need = jnp.where(tie_q, top_k - cnt_hi, big).astype(F32)
        lr = lax.broadcasted_iota(I32, (LANE, LANE), 0)
        lc = lax.broadcasted_iota(I32, (LANE, LANE), 1)
        lower = (lr >= lc).astype(BF16)

        def write_block(kb, run):
            k0 = pl.multiple_of(kb * LANE, LANE)
            blk = key_ref[pl.ds(k0, LANE), :]
            eq = blk == thr
            pc = jnp.dot(lower, jnp.where(eq, 1.0, 0.0).astype(BF16), preferred_element_type=F32)
            keep_eq = jnp.where((run + pc) <= need, 0.0, NEG_BIAS)
            bias_ref[pl.ds(k0, LANE), :] = jnp.where(blk > thr, 0.0, jnp.where(eq, keep_eq, NEG_BIAS))
            return run + pc[LANE - 1:LANE, :]

        lax.fori_loop(0, nks * (tks // LANE), write_block, jnp.zeros((1, tq), F32))

    def fill_block(kb, carry):
        k0 = pl.multiple_of(kb * tks, tks)
        bias_ref[pl.ds(k0, tks), :] = jnp.full((tks, tq), NEG_BIAS, F32)
        return carry

    lax.fori_loop(nks, seq // tks, fill_block, 0)


def _indexer_bias(ki, qit, wit, top_k, tq=IDX_TQ, tks=IDX_TKS, tkc=IDX_TKC):
    bsz, seq, _ = ki.shape
    tq = min(tq, seq)
    assert seq % tkc == 0 and tkc % (tks * SCORE_UNROLL) == 0 and tks % LANE == 0
    return pl.pallas_call(
        functools.partial(_indexer_kernel, tq=tq, tks=tks, tkc=tkc, top_k=top_k, seq=seq),
        grid=(bsz, seq // tq),
        in_specs=[pl.BlockSpec((None, seq, IDX_DIM), lambda b, i: (b, 0, 0)),
                  pl.BlockSpec((None, IDX_HEADS * IDX_DIM, tq), lambda b, i: (b, 0, i)),
                  pl.BlockSpec((None, IDX_HEADS, tq), lambda b, i: (b, 0, i))],
        out_specs=pl.BlockSpec((None, seq, tq), lambda b, i: (b, 0, i)),
        out_shape=jax.ShapeDtypeStruct((bsz, seq, seq), F32),
        scratch_shapes=[pltpu.VMEM((seq, tq), I32)],
        compiler_params=_cparams(("parallel", "arbitrary")),
        name="indexer_select",
    )(ki, qit, wit)


def _attn_kernel(qidx_ref, kidx_ref, qt_ref, k_ref, vt_ref, b_ref, o_ref, m_ref, acc_ref,
                 *, nheads, hd, hda, tq, tk):
    p = pl.program_id(1)
    qb = qidx_ref[p]
    kb = kidx_ref[p]
    kb_last = ((qb + 1) * tq - 1) // tk

    @pl.when(kb == 0)
    def _():
        m_ref[...] = jnp.full(m_ref.shape, M_INIT, F32)
        acc_ref[...] = jnp.zeros(acc_ref.shape, F32)

    bias = b_ref[...]
    hrows = [slice(h * hd, (h + 1) * hd) for h in range(nheads)]
    arows = [slice(h * hda, (h + 1) * hda) for h in range(nheads)]
    s = [jnp.dot(k_ref[:, r], qt_ref[r, :], preferred_element_type=F32) + bias for r in hrows]
    m_old = [m_ref[h] for h in range(nheads)]
    m_new = [jnp.maximum(m_old[h], jnp.max(s[h], axis=0, keepdims=True)) for h in range(nheads)]
    alpha = [jnp.exp2(m_old[h] - m_new[h]) for h in range(nheads)]
    pr = [jnp.exp2(s[h] - m_new[h]).astype(BF16) for h in range(nheads)]
    pv = [jnp.dot(vt_ref[arows[h], :], pr[h], preferred_element_type=F32) for h in range(nheads)]
    for h in range(nheads):
        m_ref[h] = m_new[h]
        acc_ref[arows[h], :] = alpha[h] * acc_ref[arows[h], :] + pv[h]

    @pl.when(kb == kb_last)
    def _():
        for h in range(nheads):
            a0 = h * hda
            out_t = acc_ref[a0:a0 + hd, :] / acc_ref[a0 + hd:a0 + hd + 1, :]
            o_ref[:, hrows[h]] = out_t.T.astype(o_ref.dtype)


def _masked_attention(qt, k, vt, bias_t, tq=ATTN_TQ, tk=ATTN_TK):
    bsz, width, seq = qt.shape
    nheads = width // A_HEAD_DIM
    hda = A_HEAD_DIM + ONES_ROWS
    tq = min(tq, seq)
    tk = min(tk, seq)
    nq = seq // tq
    pairs = [(i, j) for i in range(nq) for j in range(((i + 1) * tq - 1) // tk + 1)]
    qidx = jnp.asarray([pq for pq, _ in pairs], I32)
    kidx = jnp.asarray([pk for _, pk in pairs], I32)
    grid_spec = pltpu.PrefetchScalarGridSpec(
        num_scalar_prefetch=2,
        grid=(bsz, len(pairs)),
        in_specs=[pl.BlockSpec((None, width, tq), lambda b, p, qi, ki: (b, 0, qi[p])),
                  pl.BlockSpec((None, tk, width), lambda b, p, qi, ki: (b, ki[p], 0)),
                  pl.BlockSpec((None, nheads * hda, tk), lambda b, p, qi, ki: (b, 0, ki[p])),
                  pl.BlockSpec((None, tk, tq), lambda b, p, qi, ki: (b, ki[p], qi[p]))],
        out_specs=pl.BlockSpec((None, tq, width), lambda b, p, qi, ki: (b, qi[p], 0)),
        scratch_shapes=[pltpu.VMEM((nheads, 1, tq), F32),
                        pltpu.VMEM((nheads * hda, tq), F32)],
    )
    return pl.pallas_call(
        functools.partial(_attn_kernel, nheads=nheads, hd=A_HEAD_DIM, hda=hda, tq=tq, tk=tk),
        grid_spec=grid_spec,
        out_shape=jax.ShapeDtypeStruct((bsz, seq, width), BF16),
        compiler_params=_cparams(("parallel", "arbitrary")),
        name="masked_attention",
    )(qidx, kidx, qt, k, vt, bias_t)


def _retention_kernel(qk_ref, v_ref, g_ref, o_ref, state_ref, *, tc, dk, dv):
    @pl.when(pl.program_id(1) == 0)
    def _():
        state_ref[...] = jnp.zeros(state_ref.shape, F32)

    row = lax.broadcasted_iota(I32, (tc, tc), 0)
    col = lax.broadcasted_iota(I32, (tc, tc), 1)
    diff = (row - col).astype(F32)
    pos = lax.broadcasted_iota(I32, (tc, 1), 0).astype(F32)
    for h in range(R_HEADS):
        log_g = math.log1p(-(2.0 ** (-5.0 - h)))
        q = qk_ref[:, h * dk:(h + 1) * dk]
        k = qk_ref[:, (R_HEADS + h) * dk:(R_HEADS + h + 1) * dk]
        v = v_ref[:, h * dv:(h + 1) * dv]
        decay = jnp.where(diff >= 0, jnp.exp(jnp.maximum(diff, 0.0) * log_g), 0.0)
        s = lax.dot_general(q, k, (((1,), (1,)), ((), ())), preferred_element_type=F32) * decay
        intra = jnp.dot(s.astype(BF16), v, preferred_element_type=F32)
        xi = jnp.exp((pos + 1.0) * log_g)
        zeta = jnp.exp((tc - 1.0 - pos) * log_g)
        state = state_ref[h]
        cross = jnp.dot((q.astype(F32) * xi).astype(BF16), state.astype(BF16), preferred_element_type=F32)
        kz = (k.astype(F32) * zeta).astype(BF16)
        kv = lax.dot_general(kz, v, (((0,), (0,)), ((), ())), preferred_element_type=F32)
        state_ref[h] = state * math.exp(tc * log_g) + kv
        ret = intra + cross
        mu = jnp.mean(ret, axis=1, keepdims=True)
        d = ret - mu
        var = jnp.mean(d * d, axis=1, keepdims=True)
        gate = g_ref[:, h * dv:(h + 1) * dv]
        gate = gate / (1.0 + jnp.exp(-gate))
        o_ref[:, h * dv:(h + 1) * dv] = (d * lax.rsqrt(var + 1e-5) * gate).astype(o_ref.dtype)


def _retention(bqk, bv, bg, tc=RET_CHUNK):
    bsz, seq, w2 = bqk.shape
    dk = w2 // (2 * R_HEADS)
    dv = bv.shape[2] // R_HEADS
    tc = min(tc, seq)
    return pl.pallas_call(
        functools.partial(_retention_kernel, tc=tc, dk=dk, dv=dv),
        grid=(bsz, seq // tc),
        in_specs=[pl.BlockSpec((None, tc, w2), lambda b, c: (b, c, 0)),
                  pl.BlockSpec((None, tc, R_HEADS * dv), lambda b, c: (b, c, 0)),
                  pl.BlockSpec((None, tc, R_HEADS * dv), lambda b, c: (b, c, 0))],
        out_specs=pl.BlockSpec((None, tc, R_HEADS * dv), lambda b, c: (b, c, 0)),
        out_shape=jax.ShapeDtypeStruct((bsz, seq, R_HEADS * dv), BF16),
        scratch_shapes=[pltpu.VMEM((R_HEADS, dk, dv), F32)],
        compiler_params=_cparams(("parallel", "arbitrary")),
        name="retention",
    )(bqk, bv, bg)


def _head_sum_matrix(width, hd):
    r = jnp.arange(width)
    return (r[:, None] // hd == r[None, :] // hd).astype(BF16)


def _sigmoid(x):
    return 1.0 / (1.0 + jnp.exp(-x))


def _split_bf16(x):
    hi = x.astype(BF16)
    return hi, (x - hi.astype(F32)).astype(BF16)


def _dot_x3(a, b):
    ah, al = _split_bf16(a)
    bh, bl = _split_bf16(b)
    return (jnp.dot(ah, bh, preferred_element_type=F32) + jnp.dot(ah, bl, preferred_element_type=F32)
            + jnp.dot(al, bh, preferred_element_type=F32))


def _dot_sel(a, sel):
    ah, al = _split_bf16(a)
    return jnp.dot(ah, sel, preferred_element_type=F32) + jnp.dot(al, sel, preferred_element_type=F32)


def _rwkv_prep_kernel(*refs, cw, has_vres):
    if has_vres:
        (c_ref, mu_ref, vec_ref, wlb_ref, alb_ref, glb_ref, hs_ref, vlb_ref, vfirst_ref,
         r_o, lw_o, k_o, v_o, al_o, be_o, g_o, bo_o, carry_ref) = refs
    else:
        (c_ref, mu_ref, vec_ref, wlb_ref, alb_ref, glb_ref, hs_ref,
         r_o, lw_o, k_o, v_o, al_o, be_o, g_o, bo_o, carry_ref) = refs
    tr = c_ref.shape[0]

    @pl.when(pl.program_id(1) == 0)
    def _():
        carry_ref[...] = jnp.zeros(carry_ref.shape, F32)

    c = c_ref[...]
    row = lax.broadcasted_iota(I32, (tr, 1), 0)
    prev = jnp.where(row == 0, carry_ref[0:1, :], pltpu.roll(c, 1, axis=0))
    carry_ref[0:1, :] = c[tr - 1:tr, :]
    cs = c + (prev - c) * mu_ref[...]

    r = cs[:, 0:cw]
    k = cs[:, cw:2 * cw]
    v = cs[:, 2 * cw:3 * cw]
    o = 3 * cw
    wl = cs[:, o:o + LANE]
    al = cs[:, o + LANE:o + 2 * LANE]
    gl = cs[:, o + 2 * LANE:o + 2 * LANE + GATE_LORA]
    w0, a0, k_k, k_a, r_k, v0 = (vec_ref[i:i + 1, :] for i in range(6))

    z = -(w0 + jnp.dot(jnp.tanh(wl).astype(BF16), wlb_ref[...], preferred_element_type=F32))
    softplus = jnp.maximum(z, 0.0) + jnp.log(1.0 + jnp.exp(-jnp.abs(z)))
    lw = -jnp.exp(-softplus - 0.5)
    a = _sigmoid(a0 + jnp.dot(al.astype(BF16), alb_ref[...], preferred_element_type=F32))
    g = jnp.dot(_sigmoid(gl).astype(BF16), glb_ref[...], preferred_element_type=F32)
    if has_vres:
        vr = cs[:, o + 2 * LANE + GATE_LORA:o + 2 * LANE + GATE_LORA + vlb_ref.shape[0]]
        mix = _sigmoid(v0 + jnp.dot(vr.astype(BF16), vlb_ref[...], preferred_element_type=F32))
        v = v + (vfirst_ref[...] - v) * mix
    hs = hs_ref[...]
    kk = k * k_k
    ss = _dot_sel(kk * kk, hs)
    kk = kk / jnp.maximum(jnp.sqrt(ss), 1e-12)
    kh = k * (1.0 + (a - 1.0) * k_a)
    rk = _dot_sel(r * kh * r_k, hs)

    r_o[...] = r
    lw_o[...] = lw
    k_o[...] = kh
    v_o[...] = v
    al_o[...] = -kk
    be_o[...] = kk * a
    g_o[...] = g
    bo_o[...] = rk * v


def _rwkv_prep(cproj, mu, vecs, wlb, alb, glb, vlb, v_first, cw, tr=PREP_ROWS):
    bsz, seq, wc = cproj.shape
    tr = min(tr, seq)
    has_vres = vlb is not None
    hs = _head_sum_matrix(cw, W_HEAD_DIM)
    full = lambda a: pl.BlockSpec(a.shape, lambda b, t: (0,) * a.ndim)
    tok = pl.BlockSpec((None, tr, cw), lambda b, t: (b, t, 0))
    ins = [cproj, mu, vecs, wlb, alb, glb, hs]
    in_specs = [pl.BlockSpec((None, tr, wc), lambda b, t: (b, t, 0)),
                full(mu), full(vecs), full(wlb), full(alb), full(glb), full(hs)]
    if has_vres:
        ins += [vlb, v_first]
        in_specs += [full(vlb), tok]
    out = jax.ShapeDtypeStruct((bsz, seq, cw), F32)
    return pl.pallas_call(
        functools.partial(_rwkv_prep_kernel, cw=cw, has_vres=has_vres),
        grid=(bsz, seq // tr),
        in_specs=in_specs,
        out_specs=[tok] * 8,
        out_shape=[out] * 8,
        scratch_shapes=[pltpu.VMEM((SUBLANE, wc), F32)],
        compiler_params=_cparams(("parallel", "arbitrary")),
        name="rwkv_prep",
    )(*ins)


def _bdot(a, b):
    return jnp.dot(a.astype(BF16), b.astype(BF16), preferred_element_type=F32)


def _bdot_tn(a, b):
    return lax.dot_general(a.astype(BF16), b.astype(BF16), (((0,), (0,)), ((), ())),
                           preferred_element_type=F32)


def _rwkv_chunk_kernel(r_ref, lw_ref, k_ref, v_ref, al_ref, be_ref, g_o, y0_o, m_o, z0_o, *, tc, nck, npairs):
    row = lax.broadcasted_iota(I32, (tc, tc), 0)
    col = lax.broadcasted_iota(I32, (tc, tc), 1)
    incl = row >= col
    strict = row > col
    tri = incl.astype(BF16)
    eye = (row == col).astype(F32)
    lane = lax.broadcasted_iota(I32, (1, LANE), 1)
    mh0 = (lane // W_HEAD_DIM) == 0
    prow = lax.broadcasted_iota(I32, (LANE, LANE), 0)
    pcol = lax.broadcasted_iota(I32, (LANE, LANE), 1)
    same_head = (prow // W_HEAD_DIM) == (pcol // W_HEAD_DIM)

    lw_all = lw_ref[...]
    hi = lw_all.astype(BF16)
    rem = lw_all - hi.astype(F32)
    mid = rem.astype(BF16)
    lo = (rem - mid.astype(F32)).astype(BF16)
    cum_all = jnp.concatenate(
        [sum(jnp.dot(tri, part[c * tc:(c + 1) * tc], preferred_element_type=F32) for part in (hi, mid, lo))
         for c in range(nck)], axis=0)

    pairs = range(nck * npairs)
    heads = [(p, h) for p in pairs for h in range(2)]
    sls = [slice((p % npairs) * LANE, (p % npairs + 1) * LANE) for p in pairs]
    rws = [slice((p // npairs) * tc, (p // npairs + 1) * tc) for p in pairs]
    cum = [cum_all[rws[p], sls[p]] for p in pairs]
    tot = [c[tc - 1:tc, :] for c in cum]
    p_inv = [jnp.exp(-cum[p]) for p in pairs]
    p_end = [jnp.exp(tot[p] - cum[p]) for p in pairs]
    at = [al_ref[rws[p], sls[p]] * jnp.exp(cum[p] - lw_all[rws[p], sls[p]]) for p in pairs]
    rt = [r_ref[rws[p], sls[p]] * jnp.exp(cum[p]) for p in pairs]
    bh = [be_ref[rws[p], sls[p]] * p_end[p] for p in pairs]
    khat = [k_ref[rws[p], sls[p]] * p_end[p] for p in pairs]
    v = [v_ref[rws[p], sls[p]] for p in pairs]
    rhs = [jnp.concatenate([be_ref[rws[p], sls[p]] * p_inv[p], k_ref[rws[p], sls[p]] * p_inv[p]],
                           axis=0).astype(BF16) for p in pairs]

    a_ab, a_ak, qcat = {}, {}, {}
    for p, h in heads:
        mh = (lane // W_HEAD_DIM) == h
        lhs = jnp.concatenate([jnp.where(mh, at[p], 0.0), jnp.where(mh, rt[p], 0.0)], axis=0).astype(BF16)
        x = lax.dot_general(lhs, rhs[p], (((1,), (1,)), ((), ())), preferred_element_type=F32)
        a_ab[p, h] = jnp.where(strict, x[:tc, :tc], 0.0)
        a_ak[p, h] = jnp.where(strict, x[:tc, tc:], 0.0)
        qcat[p, h] = jnp.concatenate([jnp.where(incl, x[tc:, tc:], 0.0),
                                      jnp.where(incl, x[tc:, :tc], 0.0)], axis=1)

    t_inv = {ph: eye + a_ab[ph] for ph in heads}
    pw = {ph: _bdot(a_ab[ph], a_ab[ph]) for ph in heads}
    akv = {(p, h): _bdot(a_ak[p, h], v[p]) for p, h in heads}
    n = 2
    while n < tc:
        for ph in heads:
            res = _bdot(jnp.concatenate([pw[ph], t_inv[ph]], axis=0), pw[ph])
            t_inv[ph] = t_inv[ph] + res[tc:]
            pw[ph] = res[:tc]
        n *= 2
    tw = {(p, h): _bdot(t_inv[p, h], jnp.concatenate([at[p], akv[p, h]], axis=1)) for p, h in heads}

    zero = jnp.zeros((tc, LANE), F32)
    w = [jnp.where(mh0, tw[p, 0][:, :LANE], tw[p, 1][:, :LANE]) for p in pairs]
    u0 = [jnp.where(mh0, tw[p, 0][:, LANE:], tw[p, 1][:, LANE:]) for p in pairs]
    vu = [jnp.concatenate([jnp.concatenate([v[p], zero], axis=1),
                           jnp.concatenate([u0[p], w[p]], axis=1)], axis=0) for p in pairs]
    yg = {(p, h): _bdot(qcat[p, h], vu[p]) for p, h in heads}
    m_mat = [_bdot_tn(bh[p], w[p]) for p in pairs]
    z0 = [_bdot_tn(jnp.concatenate([bh[p], khat[p]], axis=0), jnp.concatenate([u0[p], v[p]], axis=0))
          for p in pairs]
    for p in pairs:
        y0_o[rws[p], sls[p]] = jnp.where(mh0, yg[p, 0][:, :LANE], yg[p, 1][:, :LANE])
        g_o[rws[p], sls[p]] = rt[p] + jnp.where(mh0, yg[p, 0][:, LANE:], yg[p, 1][:, LANE:])
        m_o[p // npairs, p % npairs] = jnp.where(same_head, m_mat[p], 0.0) + jnp.where(
            prow == pcol, jnp.broadcast_to(jnp.exp(tot[p]), (LANE, LANE)), 0.0)
        z0_o[p // npairs, p % npairs] = jnp.where(same_head, z0[p], 0.0)


def _rwkv_chunk_ops(r, lw, kh, v, alpha, beta, tc=CHUNK, nck=RWKV_CHUNKS):
    bsz, seq, cw = r.shape
    npairs = cw // LANE
    nc = seq // tc
    nck = min(nck, nc)
    tok = pl.BlockSpec((None, nck * tc, cw), lambda b, c: (b, c, 0))
    mat = pl.BlockSpec((None, nck, npairs, LANE, LANE), lambda b, c: (b, c, 0, 0, 0))
    tok_shape = jax.ShapeDtypeStruct((bsz, seq, cw), F32)
    mat_shape = jax.ShapeDtypeStruct((bsz, nc, npairs, LANE, LANE), F32)
    return pl.pallas_call(
        functools.partial(_rwkv_chunk_kernel, tc=tc, nck=nck, npairs=npairs),
        grid=(bsz, nc // nck),
        in_specs=[tok] * 6,
        out_specs=[tok, tok, mat, mat],
        out_shape=[tok_shape, tok_shape, mat_shape, mat_shape],
        compiler_params=_cparams(("parallel", "parallel")),
        name="rwkv_chunk_ops",
    )(r, lw, kh, v, alpha, beta)


def _rwkv_scan_kernel(g_ref, y0_ref, m_ref, z0_ref, bo_ref, gate_ref, ln_ref, hs_ref, o_ref, state_ref,
                      *, tc, nch, npairs):
    @pl.when(pl.program_id(1) == 0)
    def _():
        state_ref[...] = jnp.zeros(state_ref.shape, F32)

    hs = hs_ref[...]
    inv_n = 1.0 / W_HEAD_DIM
    sls = [slice(p * LANE, (p + 1) * LANE) for p in range(npairs)]
    st = [state_ref[p] for p in range(npairs)]
    for ch in range(nch):
        rows = slice(ch * tc, (ch + 1) * tc)
        gm = [_dot_x3(jnp.concatenate([g_ref[rows, sls[p]], m_ref[ch, p]], axis=0), st[p]) for p in range(npairs)]
        y = [gm[p][:tc] + y0_ref[rows, sls[p]] for p in range(npairs)]
        st = [gm[p][tc:] + z0_ref[ch, p] for p in range(npairs)]
        mu = [_dot_sel(y[p], hs) * inv_n for p in range(npairs)]
        d = [y[p] - mu[p] for p in range(npairs)]
        var = [_dot_sel(d[p] * d[p], hs) * inv_n for p in range(npairs)]
        for p in range(npairs):
            yn = d[p] * lax.rsqrt(var[p] + LNX_EPS) * ln_ref[0:1, sls[p]] + ln_ref[1:2, sls[p]]
            o_ref[rows, sls[p]] = ((yn + bo_ref[rows, sls[p]]) * gate_ref[rows, sls[p]]).astype(o_ref.dtype)
    for p in range(npairs):
        state_ref[p] = st[p]


def _rwkv_scan(g, y0, m, z0, bonus, gate, ln, tc=CHUNK, nch=RWKV_CHUNKS):
    bsz, seq, cw = g.shape
    npairs = cw // LANE
    nch = min(nch, seq // tc)
    hs = _head_sum_matrix(LANE, W_HEAD_DIM)
    tok = pl.BlockSpec((None, tc * nch, cw), lambda b, c: (b, c, 0))
    mat = pl.BlockSpec((None, nch, npairs, LANE, LANE), lambda b, c: (b, c, 0, 0, 0))
    return pl.pallas_call(
        functools.partial(_rwkv_scan_kernel, tc=tc, nch=nch, npairs=npairs),
        grid=(bsz, seq // (tc * nch)),
        in_specs=[tok, tok, mat, mat, tok, tok,
                  pl.BlockSpec(ln.shape, lambda b, c: (0, 0)),
                  pl.BlockSpec(hs.shape, lambda b, c: (0, 0))],
        out_specs=tok,
        out_shape=jax.ShapeDtypeStruct((bsz, seq, cw), BF16),
        scratch_shapes=[pltpu.VMEM((npairs, LANE, LANE), F32)],
        compiler_params=_cparams(("parallel", "arbitrary")),
        name="rwkv_scan",
    )(g, y0, m, z0, bonus, gate, ln, hs)


def _out_proj_kernel(x_ref, a_ref, b_ref, c_ref, w_ref, o_ref, *, wa, wb):
    acc = jnp.dot(a_ref[...], w_ref[0:wa, :], preferred_element_type=F32)
    acc = acc + jnp.dot(b_ref[...], w_ref[wa:wa + wb, :], preferred_element_type=F32)
    acc = acc + jnp.dot(c_ref[...], w_ref[wa + wb:, :], preferred_element_type=F32)
    o_ref[...] = x_ref[...] + acc


def _out_proj(x2d, oa, ob, oc, w, tm=TM, tn=1024):
    m, d = x2d.shape
    wa, wb, wc = oa.shape[1], ob.shape[1], oc.shape[1]
    tm = min(tm, m)
    tn = min(tn, d)
    return pl.pallas_call(
        functools.partial(_out_proj_kernel, wa=wa, wb=wb),
        grid=(m // tm, d // tn),
        in_specs=[pl.BlockSpec((tm, tn), lambda i, j: (i, j)),
                  pl.BlockSpec((tm, wa), lambda i, j: (i, 0)),
                  pl.BlockSpec((tm, wb), lambda i, j: (i, 0)),
                  pl.BlockSpec((tm, wc), lambda i, j: (i, 0)),
                  pl.BlockSpec((wa + wb + wc, tn), lambda i, j: (0, j))],
        out_specs=pl.BlockSpec((tm, tn), lambda i, j: (i, j)),
        out_shape=jax.ShapeDtypeStruct((m, d), F32),
        compiler_params=_cparams(("parallel", "arbitrary")),
        name="out_proj",
    )(x2d, oa, ob, oc, w)


def _gate_up_kernel(h_ref, wg_ref, wu_ref, o_ref):
    h = h_ref[...]
    gate = jnp.dot(h, wg_ref[...], preferred_element_type=F32)
    up = jnp.dot(h, wu_ref[...], preferred_element_type=F32)
    o_ref[...] = (gate / (1.0 + jnp.exp(-gate)) * up).astype(o_ref.dtype)


def _gate_up(h, w_gate_up, tm=TM, tf_cap=512):
    m, d = h.shape
    dff = w_gate_up.shape[1] // 2
    tm = min(tm, m)
    tf = _pick_tile(dff, tf_cap)
    nf = dff // tf
    return pl.pallas_call(
        _gate_up_kernel,
        grid=(m // tm, nf),
        in_specs=[pl.BlockSpec((tm, d), lambda i, j: (i, 0)),
                  pl.BlockSpec((d, tf), lambda i, j: (0, j)),
                  pl.BlockSpec((d, tf), lambda i, j: (0, j + nf))],
        out_specs=pl.BlockSpec((tm, tf), lambda i, j: (i, j)),
        out_shape=jax.ShapeDtypeStruct((m, dff), BF16),
        compiler_params=_cparams(("parallel", "arbitrary")),
        name="ffn_gate_up",
    )(h, w_gate_up, w_gate_up)


def _down_kernel(x_ref, a_ref, w_ref, o_ref):
    o_ref[...] = x_ref[...] + jnp.dot(a_ref[...], w_ref[...], preferred_element_type=F32)


def _down_proj(x2d, act, w, tm=TM, tn=512):
    m, d = x2d.shape
    dff = act.shape[1]
    tm = min(tm, m)
    tn = min(tn, d)
    return pl.pallas_call(
        _down_kernel,
        grid=(m // tm, d // tn),
        in_specs=[pl.BlockSpec((tm, tn), lambda i, j: (i, j)),
                  pl.BlockSpec((tm, dff), lambda i, j: (i, 0)),
                  pl.BlockSpec((dff, tn), lambda i, j: (0, j))],
        out_specs=pl.BlockSpec((tm, tn), lambda i, j: (i, j)),
        out_shape=jax.ShapeDtypeStruct((m, d), F32),
        compiler_params=_cparams(("parallel", "arbitrary")),
        name="ffn_down",
    )(x2d, act, w)


def _pad_cols(w, width):
    return jnp.pad(w, ((0, 0), (0, width - w.shape[1])))


def _pad_rows(w, height):
    return jnp.pad(w, ((0, height - w.shape[0]), (0, 0)))


def _pad_vec(v, width):
    return jnp.pad(v, (0, width - v.shape[0]))


def kernel(x, norm_mix_g, w_in, w_in_vres, rwkv_mu, rwkv_mu_vres, rwkv_w0, rwkv_w_lora_b, rwkv_a0,
           rwkv_a_lora_b, rwkv_v0, rwkv_v_lora_b, rwkv_g_lora_b, rwkv_k_k, rwkv_k_a, rwkv_r_k,
           rwkv_lnx_g, rwkv_lnx_b, w_out, norm_ffn_g, w_gate_up, w_down, final_norm_g):
    bsz, seq, d_model = x.shape
    depth = w_in.shape[0]
    m = bsz * seq
    d_mix = w_out.shape[1]
    a_w = d_mix // 2
    qi_w = IDX_HEADS * IDX_DIM
    b_v_w = d_mix // 4
    b_qk_w = b_v_w // 2
    c_w = d_mix // 4
    r_qk_dim = b_qk_w // R_HEADS
    top_k = min(TOPK_MAX, seq // 4)
    vres_pad = 2 * LANE

    o_q, o_k, o_v = 0, a_w, 2 * a_w
    o_qi = 3 * a_w
    o_ki = o_qi + qi_w
    o_wi = o_ki + IDX_DIM
    o_bq = o_wi + IDX_HEADS
    o_bk = o_bq + b_qk_w
    o_bv = o_bk + b_qk_w
    o_bg = o_bv + b_v_w
    o_c = o_bg + b_v_w
    o_wl = o_c + 3 * c_w
    o_al = o_wl + DECAY_LORA
    o_gl = o_al + AAA_LORA

    a_rot = A_HEAD_DIM // ROPE_FRAC
    tq_ = _rope_tables(seq, A_HEAD_DIM, a_rot, ROPE_THETA, scale=A_HEAD_DIM ** -0.5 * math.log2(math.e))
    tk_ = _rope_tables(seq, A_HEAD_DIM, a_rot, ROPE_THETA)
    tab_q = tuple(t[None] for t in tq_)
    tab_k = tuple(t[None] for t in tk_)
    i_rot = IDX_DIM // ROPE_FRAC
    tab_qi = tuple(t[None] for t in _rope_tables(seq, IDX_DIM, i_rot, ROPE_THETA))
    lane = jnp.arange(LANE)
    kiwi_pass = jnp.where(lane < IDX_DIM, 1.0, IDX_W_SCALE).astype(F32)
    c_kw, s1_kw, s2_kw = _rope_tables(seq, LANE, i_rot, ROPE_THETA, pass_scale=kiwi_pass)
    tab_kiwi = (c_kw[None], s1_kw[None], s2_kw[None])
    tbq = _rope_tables(seq, r_qk_dim, r_qk_dim, R_THETA)
    tbk = _rope_tables(seq, r_qk_dim, r_qk_dim, R_THETA, scale=r_qk_dim ** -0.5)
    tab_bqk = tuple(jnp.stack([a, b]) for a, b in zip(tbq, tbk))

    x2d = x.reshape(m, d_model)
    v_first = None
    for l in range(depth):
        wl_ = w_in[l]
        cols = lambda o, n: wl_[:, o:o + n]
        w_q = cols(o_q, a_w).astype(BF16)
        w_k = cols(o_k, a_w).astype(BF16)
        w_v = cols(o_v, a_w).astype(BF16)
        w_bv = cols(o_bv, b_v_w).astype(BF16)
        w_qi = cols(o_qi, qi_w).astype(BF16)
        w_kiwi = _pad_cols(cols(o_ki, IDX_DIM + IDX_HEADS), LANE).astype(BF16)
        w_bqk = cols(o_bq, 2 * b_qk_w).astype(BF16)
        w_bg = cols(o_bg, b_v_w).astype(BF16)
        vres_w = (w_in_vres[l - 1] if l > 0 else jnp.zeros((d_model, MV_LORA), F32))
        w_c = jnp.concatenate([cols(o_c, 3 * c_w), _pad_cols(cols(o_wl, DECAY_LORA), LANE),
                               _pad_cols(cols(o_al, AAA_LORA), LANE), cols(o_gl, GATE_LORA),
                               _pad_cols(vres_w, vres_pad)], axis=1).astype(BF16)
        mu = rwkv_mu[l]
        mu_vres = rwkv_mu_vres[l - 1] if l > 0 else jnp.zeros((MV_LORA,), F32)
        mu_c = jnp.concatenate([mu[:3 * c_w], _pad_vec(mu[3 * c_w:3 * c_w + DECAY_LORA], LANE),
                                _pad_vec(mu[3 * c_w + DECAY_LORA:3 * c_w + DECAY_LORA + AAA_LORA], LANE),
                                mu[3 * c_w + DECAY_LORA + AAA_LORA:], _pad_vec(mu_vres, vres_pad)])[None, :]

        h = _rmsnorm(x2d, norm_mix_g[l], BF16)
        qt = _mm_rope(h, w_q, tab_q, a_rot // 2, seq, a_w, BF16, transpose=True)
        ak = _mm_rope(h, w_k, tab_k, a_rot // 2, seq, a_w, BF16).reshape(bsz, seq, a_w)
        vt = _mm_plain(h, w_v, BF16, seq, transpose=True, pad_rows=ONES_ROWS)
        bv = _mm_plain(h, w_bv, BF16, seq, tn_cap=512).reshape(bsz, seq, b_v_w)
        qit = _mm_rope(h, w_qi, tab_qi, i_rot // 2, seq, qi_w, BF16, transpose=True)
        kiwi = _mm_rope(h, w_kiwi, tab_kiwi, i_rot // 2, seq, LANE, F32).reshape(bsz, seq, LANE)
        bqk = _mm_rope(h, w_bqk, tab_bqk, r_qk_dim // 2, seq, b_qk_w, BF16).reshape(bsz, seq, 2 * b_qk_w)
        bg = _mm_plain(h, w_bg, F32, seq).reshape(bsz, seq, b_v_w)
        cproj = _mm_plain(h, w_c, F32, seq, tn_cap=768).reshape(bsz, seq, w_c.shape[1])

        ki = kiwi[:, :, :IDX_DIM].astype(BF16)
        wit = jnp.swapaxes(kiwi[:, :, IDX_DIM:IDX_DIM + IDX_HEADS], 1, 2)
        bias_t = _indexer_bias(ki, qit, wit, top_k)
        out_a = _masked_attention(qt, ak, vt, bias_t)

        out_b = _retention(bqk, bv, bg)

        vecs = jnp.stack([rwkv_w0[l], rwkv_a0[l], rwkv_k_k[l], rwkv_k_a[l], rwkv_r_k[l].reshape(-1),
                          rwkv_v0[l - 1] if l > 0 else jnp.zeros((c_w,), F32),
                          jnp.zeros((c_w,), F32), jnp.zeros((c_w,), F32)])
        wlb = _pad_rows(rwkv_w_lora_b[l], LANE).astype(BF16)
        alb = _pad_rows(rwkv_a_lora_b[l], LANE).astype(BF16)
        glb = rwkv_g_lora_b[l].astype(BF16)
        vlb = _pad_rows(rwkv_v_lora_b[l - 1], vres_pad).astype(BF16) if l > 0 else None
        r_, lw_, kh_, v_, al_, be_, g_, bo_ = _rwkv_prep(cproj, mu_c, vecs, wlb, alb, glb, vlb, v_first, c_w)
        if l == 0:
            v_first = v_
        gm, y0, mm, z0 = _rwkv_chunk_ops(r_, lw_, kh_, v_, al_, be_)
        ln = jnp.stack([rwkv_lnx_g[l], rwkv_lnx_b[l]] + [jnp.zeros((c_w,), F32)] * 6)
        out_c = _rwkv_scan(gm, y0, mm, z0, bo_, g_, ln)

        x2d = _out_proj(x2d, out_a.reshape(m, a_w), out_b.reshape(m, b_v_w), out_c.reshape(m, c_w),
                        w_out[l].astype(BF16))

        h = _rmsnorm(x2d, norm_ffn_g[l], BF16)
        act = _gate_up(h, w_gate_up[l].astype(BF16))
        x2d = _down_proj(x2d, act, w_down[l].astype(BF16))

    return _rmsnorm(x2d, final_norm_g, F32).reshape(bsz, seq, d_model)
```

```python
import functools
import math

import jax
import jax.numpy as jnp
from jax import lax
from jax.experimental import pallas as pl
from jax.experimental.pallas import tpu as pltpu

F32 = jnp.float32
BF16 = jnp.bfloat16
I32 = jnp.int32

CHUNK = 64
A_HEAD_DIM = 128
IDX_HEADS = 16
IDX_DIM = 64
IDX_W_SCALE = (IDX_HEADS * IDX_DIM) ** -0.5
TOPK_MAX = 256
ROPE_THETA = 500000.0
ROPE_FRAC = 4
R_HEADS = 4
R_THETA = 10000.0
W_HEAD_DIM = 64
DECAY_LORA = 96
AAA_LORA = 96
MV_LORA = 64
GATE_LORA = 256
LNX_EPS = 64e-5
RMS_EPS = 1e-5

LANE = 128
SUBLANE = 8
VMEM_LIMIT = 56 * 1024 * 1024
TM = 1024
TM_NORM = 512
IDX_TQ = 256
IDX_TKS = 128
IDX_TKC = 1024
SCORE_UNROLL = 4
COUNT_CHAINS = 8
ATTN_TQ = 512
ATTN_TK = 1024
ONES_ROWS = SUBLANE
RET_CHUNK = 256
PREP_ROWS = 256
RWKV_CHUNKS = 4

NEG_BIAS = -1e30
M_INIT = -1e20
INT_MIN = -2147483648
INT_MAX = 2147483647


def _cparams(sem):
    return pltpu.CompilerParams(dimension_semantics=sem, vmem_limit_bytes=VMEM_LIMIT)


def _pick_tile(n, cap):
    best = LANE
    t = LANE
    while t <= min(n, cap):
        if n % t == 0:
            best = t
        t += LANE
    return best


def _rmsnorm_kernel(x_ref, g_ref, o_ref):
    x = x_ref[...]
    ms = jnp.mean(x * x, axis=-1, keepdims=True)
    o_ref[...] = (x * lax.rsqrt(ms + RMS_EPS) * g_ref[...]).astype(o_ref.dtype)


def _rmsnorm(x2d, g, out_dtype, tm=TM_NORM):
    m, d = x2d.shape
    return pl.pallas_call(
        _rmsnorm_kernel,
        grid=(m // tm,),
        in_specs=[pl.BlockSpec((tm, d), lambda i: (i, 0)),
                  pl.BlockSpec((1, d), lambda i: (0, 0))],
        out_specs=pl.BlockSpec((tm, d), lambda i: (i, 0)),
        out_shape=jax.ShapeDtypeStruct((m, d), out_dtype),
        compiler_params=_cparams(("parallel",)),
        name="rmsnorm",
    )(x2d, g.reshape(1, d).astype(F32))


def _store_cols(o_ref, blk, x, transpose, pad_rows):
    if transpose:
        r0 = blk * (LANE + pad_rows)
        o_ref[r0:r0 + LANE, :] = x.T.astype(o_ref.dtype)
        if pad_rows:
            o_ref[r0 + LANE:r0 + LANE + pad_rows, :] = jnp.ones((pad_rows, x.shape[0]), o_ref.dtype)
    else:
        o_ref[:, blk * LANE:(blk + 1) * LANE] = x.astype(o_ref.dtype)


def _proj_out_spec(m, n, tm, tn, seq, out_dtype, transpose, pad_rows):
    if not transpose:
        return pl.BlockSpec((tm, tn), lambda i, j: (i, j)), jax.ShapeDtypeStruct((m, n), out_dtype)
    tpb = seq // tm
    rows = lambda cols: cols // LANE * (LANE + pad_rows)
    return (pl.BlockSpec((None, rows(tn), tm), lambda i, j: (i // tpb, j, i % tpb)),
            jax.ShapeDtypeStruct((m // seq, rows(n), seq), out_dtype))


def _mm_plain_kernel(a_ref, w_ref, o_ref, *, transpose, pad_rows):
    acc = jnp.dot(a_ref[...], w_ref[...], preferred_element_type=F32)
    if not transpose:
        o_ref[...] = acc.astype(o_ref.dtype)
        return
    for blk in range(acc.shape[1] // LANE):
        _store_cols(o_ref, blk, acc[:, blk * LANE:(blk + 1) * LANE], transpose, pad_rows)


def _mm_plain(a, w, out_dtype, seq, tm=TM, tn_cap=1024, transpose=False, pad_rows=0):
    m, k = a.shape
    n = w.shape[1]
    tm = min(tm, seq)
    tn = _pick_tile(n, tn_cap)
    out_spec, out_shape = _proj_out_spec(m, n, tm, tn, seq, out_dtype, transpose, pad_rows)
    return pl.pallas_call(
        functools.partial(_mm_plain_kernel, transpose=transpose, pad_rows=pad_rows),
        grid=(m // tm, n // tn),
        in_specs=[pl.BlockSpec((tm, k), lambda i, j: (i, 0)),
                  pl.BlockSpec((k, tn), lambda i, j: (0, j))],
        out_specs=out_spec,
        out_shape=out_shape,
        compiler_params=_cparams(("parallel", "arbitrary")),
        name="proj_plain",
    )(a, w)


def _mm_rope_kernel(a_ref, w_ref, c_ref, s1_ref, s2_ref, o_ref, *, half, transpose):
    acc = jnp.dot(a_ref[...], w_ref[...], preferred_element_type=F32)
    c, s1, s2 = c_ref[...], s1_ref[...], s2_ref[...]
    for blk in range(acc.shape[1] // LANE):
        x = acc[:, blk * LANE:(blk + 1) * LANE]
        up = pltpu.roll(x, LANE - half, axis=1)
        dn = pltpu.roll(x, half, axis=1)
        _store_cols(o_ref, blk, x * c + up * s1 + dn * s2, transpose, 0)


def _mm_rope(a, w, tables, half, seq, tn, out_dtype, tm=TM, transpose=False):
    m, k = a.shape
    n = w.shape[1]
    tm = min(tm, seq)
    tpb = seq // tm
    tab_spec = pl.BlockSpec((None, tm, LANE), lambda i, j: (j, i % tpb, 0))
    out_spec, out_shape = _proj_out_spec(m, n, tm, tn, seq, out_dtype, transpose, 0)
    return pl.pallas_call(
        functools.partial(_mm_rope_kernel, half=half, transpose=transpose),
        grid=(m // tm, n // tn),
        in_specs=[pl.BlockSpec((tm, k), lambda i, j: (i, 0)),
                  pl.BlockSpec((k, tn), lambda i, j: (0, j)),
                  tab_spec, tab_spec, tab_spec],
        out_specs=out_spec,
        out_shape=out_shape,
        compiler_params=_cparams(("parallel", "arbitrary")),
        name="proj_rope",
    )(a, w, *tables)


def _rope_tables(seq, group, rot_dim, theta, scale=1.0, pass_scale=None):
    half = rot_dim // 2
    freqs = jnp.power(F32(theta), -jnp.arange(half, dtype=F32) / half)
    ang = jnp.arange(seq, dtype=F32)[:, None] * freqs[None, :]
    cos, sin = jnp.cos(ang), jnp.sin(ang)
    lane = jnp.arange(LANE) % group
    idx = lane % half
    cosl, sinl = cos[:, idx], sin[:, idx]
    passv = jnp.ones((LANE,), F32) if pass_scale is None else pass_scale
    c = jnp.where(lane < rot_dim, cosl, passv[None, :])
    s1 = jnp.where(lane < half, -sinl, 0.0)
    s2 = jnp.where((lane >= half) & (lane < rot_dim), sinl, 0.0)
    return c * scale, s1 * scale, s2 * scale


def _indexer_kernel(ki_ref, qit_ref, wit_ref, bias_ref, key_ref, *, tq, tks, tkc, top_k, seq):
    qb = pl.program_id(1)
    span = tks * SCORE_UNROLL
    nks = ((qb + 1) * tq + span - 1) // span * SCORE_UNROLL
    nkc = (nks * tks + tkc - 1) // tkc
    q_chunk = (qb * tq + lax.broadcasted_iota(I32, (1, tq), 1)) // CHUNK
    w = wit_ref[...]

    def score_block(kb, rmax):
        k0 = pl.multiple_of(kb * tks, tks)
        kblk = ki_ref[pl.ds(k0, tks), :]
        acc = jnp.zeros((tks, tq), F32)
        for h in range(IDX_HEADS):
            x = jnp.dot(kblk, qit_ref[h * IDX_DIM:(h + 1) * IDX_DIM, :], preferred_element_type=F32)
            acc = acc + jnp.maximum(x, 0.0) * w[h:h + 1, :]
        k_chunk = (k0 + lax.broadcasted_iota(I32, (tks, 1), 0)) // CHUNK
        bits = lax.bitcast_convert_type(acc, I32)
        key = bits ^ ((bits >> 31) & jnp.int32(INT_MAX))
        key = jnp.where(k_chunk <= q_chunk, key, jnp.int32(INT_MIN))
        key_ref[pl.ds(k0, tks), :] = key
        return jnp.maximum(rmax, jnp.max(key.reshape(tks // SUBLANE, SUBLANE, tq), axis=0))

    def score_step(kbu, rmax):
        for u in range(SCORE_UNROLL):
            rmax = score_block(kbu * SCORE_UNROLL + u, rmax)
        return rmax

    rmax = lax.fori_loop(0, nks // SCORE_UNROLL, score_step, jnp.full((SUBLANE, tq), INT_MIN, I32))
    colmax = jnp.max(rmax, axis=0, keepdims=True)

    def pad_block(kb, carry):
        k0 = pl.multiple_of(kb * tks, tks)
        key_ref[pl.ds(k0, tks), :] = jnp.full((tks, tq), INT_MIN, I32)
        return carry

    lax.fori_loop(nks, nkc * (tkc // tks), pad_block, 0)

    def tree_sum(terms):
        while len(terms) > 1:
            terms = [a + b for a, b in zip(terms[0::2], terms[1::2])]
        return terms[0]

    def count_ge(cand):
        def count_block(kb, cnt):
            k0 = pl.multiple_of(kb * tkc, tkc)
            part = tkc // COUNT_CHAINS
            sums = []
            for g in range(COUNT_CHAINS):
                ind = jnp.where(key_ref[pl.ds(k0 + g * part, part), :] >= cand, 1, 0)
                sums.append(jnp.sum(ind.reshape(part // SUBLANE, SUBLANE, tq), axis=0))
            return cnt + tree_sum(sums)

        cnt = lax.fori_loop(0, nkc, count_block, jnp.zeros((SUBLANE, tq), I32))
        return jnp.sum(cnt, axis=0, keepdims=True)

    def probe_step(count_fn, state, mid):
        lo, hi, cnt_lo, cnt_hi = state
        cnt = count_fn(mid)
        ge = cnt >= top_k
        new_hi = jnp.where(cnt == top_k, mid + 1, jnp.where(ge, hi, mid))
        return (jnp.where(ge, mid, lo), new_hi, jnp.where(ge, cnt, cnt_lo), jnp.where(ge, cnt_hi, cnt))

    def midpoint(state):
        lo, hi = state[0], state[1]
        return (lo >> 1) + (hi >> 1) + (lo & hi & 1)

    def n_active(state):
        return jnp.max(((state[0] + 1) < state[1]).astype(I32))

    binade = 1 << 23

    def bisect(state):
        def next_mid(st, it):
            mid = midpoint(st)
            step = jnp.int32(3 * binade) << jnp.minimum(it, 6)
            descend = (st[0] == jnp.int32(INT_MIN)) & (st[1] > jnp.int32(INT_MIN) + step)
            return jnp.where(descend, jnp.maximum(st[1] - step, mid), mid)

        def body(carry):
            it, _, st = carry
            st = probe_step(count_ge, st, next_mid(st, it + 1))
            st = probe_step(count_ge, st, next_mid(st, it + 2))
            return it + 2, n_active(st), st

        return lax.while_loop(lambda c: (c[1] > 0) & (c[0] < 48), body, (jnp.int32(0), n_active(state), state))[2]

    big = jnp.int32(1 << 30)
    state = (jnp.full((1, tq), INT_MIN, I32), colmax + 1, jnp.full((1, tq), big, I32), jnp.zeros((1, tq), I32))
    first = jnp.where(colmax > jnp.int32(INT_MIN + 4 * binade), colmax - jnp.int32(3 * binade),
                      jnp.int32(INT_MIN + 1))
    lo, _, cnt_lo, cnt_hi = bisect(probe_step(count_ge, state, first))

    thr = jnp.maximum(lo, jnp.int32(INT_MIN + 1))
    tie_q = (cnt_lo > top_k) & (lo > jnp.int32(INT_MIN))
    any_tie = jnp.max(tie_q.astype(I32))

    @pl.when(any_tie == 0)
    def _():
        def write_block(kb, carry):
            k0 = pl.multiple_of(kb * tks, tks)
            bias_ref[pl.ds(k0, tks), :] = jnp.where(key_ref[pl.ds(k0, tks), :] >= thr, 0.0, NEG_BIAS)
            return carry

        lax.fori_loop(0, nks, write_block, 0)

    @pl.when(any_tie != 0)
    def _():
        need = jnp.where(tie_q, top_k - cnt_hi, big).astype(F32)
        lr = lax.broadcasted_iota(I32, (LANE, LANE), 0)
        lc = lax.broadcasted_iota(I32, (LANE, LANE), 1)
        lower = (lr >= lc).astype(BF16)

        def write_block(kb, run):
            k0 = pl.multiple_of(kb * LANE, LANE)
            blk = key_ref[pl.ds(k0, LANE), :]
            eq = blk == thr
            pc = jnp.dot(lower, jnp.where(eq, 1.0, 0.0).astype(BF16), preferred_element_type=F32)
            keep_eq = jnp.where((run + pc) <= need, 0.0, NEG_BIAS)
            bias_ref[pl.ds(k0, LANE), :] = jnp.where(blk > thr, 0.0, jnp.where(eq, keep_eq, NEG_BIAS))
            return run + pc[LANE - 1:LANE, :]

        lax.fori_loop(0, nks * (tks // LANE), write_block, jnp.zeros((1, tq), F32))

    def fill_block(kb, carry):
        k0 = pl.multiple_of(kb * tks, tks)
        bias_ref[pl.ds(k0, tks), :] = jnp.full((tks, tq), NEG_BIAS, F32)
        return carry

    lax.fori_loop(nks, seq // tks, fill_block, 0)


def _indexer_bias(ki, qit, wit, top_k, tq=IDX_TQ, tks=IDX_TKS, tkc=IDX_TKC):
    bsz, seq, _ = ki.shape
    tq = min(tq, seq)
    assert seq % tkc == 0 and tkc % (tks * SCORE_UNROLL) == 0 and tks % LANE == 0
    return pl.pallas_call(
        functools.partial(_indexer_kernel, tq=tq, tks=tks, tkc=tkc, top_k=top_k, seq=seq),
        grid=(bsz, seq // tq),
        in_specs=[pl.BlockSpec((None, seq, IDX_DIM), lambda b, i: (b, 0, 0)),
                  pl.BlockSpec((None, IDX_HEADS * IDX_DIM, tq), lambda b, i: (b, 0, i)),
                  pl.BlockSpec((None, IDX_HEADS, tq), lambda b, i: (b, 0, i))],
        out_specs=pl.BlockSpec((None, seq, tq), lambda b, i: (b, 0, i)),
        out_shape=jax.ShapeDtypeStruct((bsz, seq, seq), F32),
        scratch_shapes=[pltpu.VMEM((seq, tq), I32)],
        compiler_params=_cparams(("parallel", "arbitrary")),
        name="indexer_select",
    )(ki, qit, wit)


def _attn_kernel(qidx_ref, kidx_ref, qt_ref, k_ref, vt_ref, b_ref, o_ref, m_ref, acc_ref,
                 *, nheads, hd, hda, tq, tk):
    p = pl.program_id(1)
    qb = qidx_ref[p]
    kb = kidx_ref[p]
    kb_last = ((qb + 1) * tq - 1) // tk

    @pl.when(kb == 0)
    def _():
        m_ref[...] = jnp.full(m_ref.shape, M_INIT, F32)
        acc_ref[...] = jnp.zeros(acc_ref.shape, F32)

    bias = b_ref[...]
    hrows = [slice(h * hd, (h + 1) * hd) for h in range(nheads)]
    arows = [slice(h * hda, (h + 1) * hda) for h in range(nheads)]
    s = [jnp.dot(k_ref[:, r], qt_ref[r, :], preferred_element_type=F32) + bias for r in hrows]
    m_old = [m_ref[h] for h in range(nheads)]
    m_new = [jnp.maximum(m_old[h], jnp.max(s[h], axis=0, keepdims=True)) for h in range(nheads)]
    alpha = [jnp.exp2(m_old[h] - m_new[h]) for h in range(nheads)]
    pr = [jnp.exp2(s[h] - m_new[h]).astype(BF16) for h in range(nheads)]
    pv = [jnp.dot(vt_ref[arows[h], :], pr[h], preferred_element_type=F32) for h in range(nheads)]
    for h in range(nheads):
        m_ref[h] = m_new[h]
        acc_ref[arows[h], :] = alpha[h] * acc_ref[arows[h], :] + pv[h]

    @pl.when(kb == kb_last)
    def _():
        for h in range(nheads):
            a0 = h * hda
            out_t = acc_ref[a0:a0 + hd, :] / acc_ref[a0 + hd:a0 + hd + 1, :]
            o_ref[:, hrows[h]] = out_t.T.astype(o_ref.dtype)


def _masked_attention(qt, k, vt, bias_t, tq=ATTN_TQ, tk=ATTN_TK):
    bsz, width, seq = qt.shape
    nheads = width // A_HEAD_DIM
    hda = A_HEAD_DIM + ONES_ROWS
    tq = min(tq, seq)
    tk = min(tk, seq)
    nq = seq // tq
    pairs = [(i, j) for i in range(nq) for j in range(((i + 1) * tq - 1) // tk + 1)]
    qidx = jnp.asarray([pq for pq, _ in pairs], I32)
    kidx = jnp.asarray([pk for _, pk in pairs], I32)
    grid_spec = pltpu.PrefetchScalarGridSpec(
        num_scalar_prefetch=2,
        grid=(bsz, len(pairs)),
        in_specs=[pl.BlockSpec((None, width, tq), lambda b, p, qi, ki: (b, 0, qi[p])),
                  pl.BlockSpec((None, tk, width), lambda b, p, qi, ki: (b, ki[p], 0)),
                  pl.BlockSpec((None, nheads * hda, tk), lambda b, p, qi, ki: (b, 0, ki[p])),
                  pl.BlockSpec((None, tk, tq), lambda b, p, qi, ki: (b, ki[p], qi[p]))],
        out_specs=pl.BlockSpec((None, tq, width), lambda b, p, qi, ki: (b, qi[p], 0)),
        scratch_shapes=[pltpu.VMEM((nheads, 1, tq), F32),
                        pltpu.VMEM((nheads * hda, tq), F32)],
    )
    return pl.pallas_call(
        functools.partial(_attn_kernel, nheads=nheads, hd=A_HEAD_DIM, hda=hda, tq=tq, tk=tk),
        grid_spec=grid_spec,
        out_shape=jax.ShapeDtypeStruct((bsz, seq, width), BF16),
        compiler_params=_cparams(("parallel", "arbitrary")),
        name="masked_attention",
    )(qidx, kidx, qt, k, vt, bias_t)


def _retention_kernel(qk_ref, v_ref, g_ref, o_ref, state_ref, *, tc, dk, dv):
    @pl.when(pl.program_id(1) == 0)
    def _():
        state_ref[...] = jnp.zeros(state_ref.shape, F32)

    row = lax.broadcasted_iota(I32, (tc, tc), 0)
    col = lax.broadcasted_iota(I32, (tc, tc), 1)
    diff = (row - col).astype(F32)
    pos = lax.broadcasted_iota(I32, (tc, 1), 0).astype(F32)
    for h in range(R_HEADS):
        log_g = math.log1p(-(2.0 ** (-5.0 - h)))
        q = qk_ref[:, h * dk:(h + 1) * dk]
        k = qk_ref[:, (R_HEADS + h) * dk:(R_HEADS + h + 1) * dk]
        v = v_ref[:, h * dv:(h + 1) * dv]
        decay = jnp.where(diff >= 0, jnp.exp(jnp.maximum(diff, 0.0) * log_g), 0.0)
        s = lax.dot_general(q, k, (((1,), (1,)), ((), ())), preferred_element_type=F32) * decay
        intra = jnp.dot(s.astype(BF16), v, preferred_element_type=F32)
        xi = jnp.exp((pos + 1.0) * log_g)
        zeta = jnp.exp((tc - 1.0 - pos) * log_g)
        state = state_ref[h]
        cross = jnp.dot((q.astype(F32) * xi).astype(BF16), state.astype(BF16), preferred_element_type=F32)
        kz = (k.astype(F32) * zeta).astype(BF16)
        kv = lax.dot_general(kz, v, (((0,), (0,)), ((), ())), preferred_element_type=F32)
        state_ref[h] = state * math.exp(tc * log_g) + kv
        ret = intra + cross
        mu = jnp.mean(ret, axis=1, keepdims=True)
        d = ret - mu
        var = jnp.mean(d * d, axis=1, keepdims=True)
        gate = g_ref[:, h * dv:(h + 1) * dv]
        gate = gate / (1.0 + jnp.exp(-gate))
        o_ref[:, h * dv:(h + 1) * dv] = (d * lax.rsqrt(var + 1e-5) * gate).astype(o_ref.dtype)


def _retention(bqk, bv, bg, tc=RET_CHUNK):
    bsz, seq, w2 = bqk.shape
    dk = w2 // (2 * R_HEADS)
    dv = bv.shape[2] // R_HEADS
    tc = min(tc, seq)
    return pl.pallas_call(
        functools.partial(_retention_kernel, tc=tc, dk=dk, dv=dv),
        grid=(bsz, seq // tc),
        in_specs=[pl.BlockSpec((None, tc, w2), lambda b, c: (b, c, 0)),
                  pl.BlockSpec((None, tc, R_HEADS * dv), lambda b, c: (b, c, 0)),
                  pl.BlockSpec((None, tc, R_HEADS * dv), lambda b, c: (b, c, 0))],
        out_specs=pl.BlockSpec((None, tc, R_HEADS * dv), lambda b, c: (b, c, 0)),
        out_shape=jax.ShapeDtypeStruct((bsz, seq, R_HEADS * dv), BF16),
        scratch_shapes=[pltpu.VMEM((R_HEADS, dk, dv), F32)],
        compiler_params=_cparams(("parallel", "arbitrary")),
        name="retention",
    )(bqk, bv, bg)


def _head_sum_matrix(width, hd):
    r = jnp.arange(width)
    return (r[:, None] // hd == r[None, :] // hd).astype(BF16)


def _sigmoid(x):
    return 1.0 / (1.0 + jnp.exp(-x))


def _split_bf16(x):
    hi = x.astype(BF16)
    return hi, (x - hi.astype(F32)).astype(BF16)


def _dot_x3(a, b):
    ah, al = _split_bf16(a)
    bh, bl = _split_bf16(b)
    return (jnp.dot(ah, bh, preferred_element_type=F32) + jnp.dot(ah, bl, preferred_element_type=F32)
            + jnp.dot(al, bh, preferred_element_type=F32))


def _dot_sel(a, sel):
    ah, al = _split_bf16(a)
    return jnp.dot(ah, sel, preferred_element_type=F32) + jnp.dot(al, sel, preferred_element_type=F32)


def _rwkv_prep_kernel(*refs, cw, has_vres):
    if has_vres:
        (c_ref, mu_ref, vec_ref, wlb_ref, alb_ref, glb_ref, hs_ref, vlb_ref, vfirst_ref,
         r_o, lw_o, k_o, v_o, al_o, be_o, g_o, bo_o, carry_ref) = refs
    else:
        (c_ref, mu_ref, vec_ref, wlb_ref, alb_ref, glb_ref, hs_ref,
         r_o, lw_o, k_o, v_o, al_o, be_o, g_o, bo_o, carry_ref) = refs
    tr = c_ref.shape[0]

    @pl.when(pl.program_id(1) == 0)
    def _():
        carry_ref[...] = jnp.zeros(carry_ref.shape, F32)

    c = c_ref[...]
    row = lax.broadcasted_iota(I32, (tr, 1), 0)
    prev = jnp.where(row == 0, carry_ref[0:1, :], pltpu.roll(c, 1, axis=0))
    carry_ref[0:1, :] = c[tr - 1:tr, :]
    cs = c + (prev - c) * mu_ref[...]

    r = cs[:, 0:cw]
    k = cs[:, cw:2 * cw]
    v = cs[:, 2 * cw:3 * cw]
    o = 3 * cw
    wl = cs[:, o:o + LANE]
    al = cs[:, o + LANE:o + 2 * LANE]
    gl = cs[:, o + 2 * LANE:o + 2 * LANE + GATE_LORA]
    w0, a0, k_k, k_a, r_k, v0 = (vec_ref[i:i + 1, :] for i in range(6))

    z = -(w0 + jnp.dot(jnp.tanh(wl).astype(BF16), wlb_ref[...], preferred_element_type=F32))
    softplus = jnp.maximum(z, 0.0) + jnp.log(1.0 + jnp.exp(-jnp.abs(z)))
    lw = -jnp.exp(-softplus - 0.5)
    a = _sigmoid(a0 + jnp.dot(al.astype(BF16), alb_ref[...], preferred_element_type=F32))
    g = jnp.dot(_sigmoid(gl).astype(BF16), glb_ref[...], preferred_element_type=F32)
    if has_vres:
        vr = cs[:, o + 2 * LANE + GATE_LORA:o + 2 * LANE + GATE_LORA + vlb_ref.shape[0]]
        mix = _sigmoid(v0 + jnp.dot(vr.astype(BF16), vlb_ref[...], preferred_element_type=F32))
        v = v + (vfirst_ref[...] - v) * mix
    hs = hs_ref[...]
    kk = k * k_k
    ss = _dot_sel(kk * kk, hs)
    kk = kk / jnp.maximum(jnp.sqrt(ss), 1e-12)
    kh = k * (1.0 + (a - 1.0) * k_a)
    rk = _dot_sel(r * kh * r_k, hs)

    r_o[...] = r
    lw_o[...] = lw
    k_o[...] = kh
    v_o[...] = v
    al_o[...] = -kk
    be_o[...] = kk * a
    g_o[...] = g
    bo_o[...] = rk * v


def _rwkv_prep(cproj, mu, vecs, wlb, alb, glb, vlb, v_first, cw, tr=PREP_ROWS):
    bsz, seq, wc = cproj.shape
    tr = min(tr, seq)
    has_vres = vlb is not None
    hs = _head_sum_matrix(cw, W_HEAD_DIM)
    full = lambda a: pl.BlockSpec(a.shape, lambda b, t: (0,) * a.ndim)
    tok = pl.BlockSpec((None, tr, cw), lambda b, t: (b, t, 0))
    ins = [cproj, mu, vecs, wlb, alb, glb, hs]
    in_specs = [pl.BlockSpec((None, tr, wc), lambda b, t: (b, t, 0)),
                full(mu), full(vecs), full(wlb), full(alb), full(glb), full(hs)]
    if has_vres:
        ins += [vlb, v_first]
        in_specs += [full(vlb), tok]
    out = jax.ShapeDtypeStruct((bsz, seq, cw), F32)
    return pl.pallas_call(
        functools.partial(_rwkv_prep_kernel, cw=cw, has_vres=has_vres),
        grid=(bsz, seq // tr),
        in_specs=in_specs,
        out_specs=[tok] * 8,
        out_shape=[out] * 8,
        scratch_shapes=[pltpu.VMEM((SUBLANE, wc), F32)],
        compiler_params=_cparams(("parallel", "arbitrary")),
        name="rwkv_prep",
    )(*ins)


def _bdot(a, b):
    return jnp.dot(a.astype(BF16), b.astype(BF16), preferred_element_type=F32)


def _bdot_tn(a, b):
    return lax.dot_general(a.astype(BF16), b.astype(BF16), (((0,), (0,)), ((), ())),
                           preferred_element_type=F32)


def _rwkv_chunk_kernel(r_ref, lw_ref, k_ref, v_ref, al_ref, be_ref, g_o, y0_o, m_o, z0_o, *, tc, nck, npairs):
    row = lax.broadcasted_iota(I32, (tc, tc), 0)
    col = lax.broadcasted_iota(I32, (tc, tc), 1)
    incl = row >= col
    strict = row > col
    tri = incl.astype(BF16)
    eye = (row == col).astype(F32)
    lane = lax.broadcasted_iota(I32, (1, LANE), 1)
    mh0 = (lane // W_HEAD_DIM) == 0
    prow = lax.broadcasted_iota(I32, (LANE, LANE), 0)
    pcol = lax.broadcasted_iota(I32, (LANE, LANE), 1)
    same_head = (prow // W_HEAD_DIM) == (pcol // W_HEAD_DIM)

    lw_all = lw_ref[...]
    hi = lw_all.astype(BF16)
    rem = lw_all - hi.astype(F32)
    mid = rem.astype(BF16)
    lo = (rem - mid.astype(F32)).astype(BF16)
    cum_all = jnp.concatenate(
        [sum(jnp.dot(tri, part[c * tc:(c + 1) * tc], preferred_element_type=F32) for part in (hi, mid, lo))
         for c in range(nck)], axis=0)

    pairs = range(nck * npairs)
    heads = [(p, h) for p in pairs for h in range(2)]
    sls = [slice((p % npairs) * LANE, (p % npairs + 1) * LANE) for p in pairs]
    rws = [slice((p // npairs) * tc, (p // npairs + 1) * tc) for p in pairs]
    cum = [cum_all[rws[p], sls[p]] for p in pairs]
    tot = [c[tc - 1:tc, :] for c in cum]
    p_inv = [jnp.exp(-cum[p]) for p in pairs]
    p_end = [jnp.exp(tot[p] - cum[p]) for p in pairs]
    at = [al_ref[rws[p], sls[p]] * jnp.exp(cum[p] - lw_all[rws[p], sls[p]]) for p in pairs]
    rt = [r_ref[rws[p], sls[p]] * jnp.exp(cum[p]) for p in pairs]
    bh = [be_ref[rws[p], sls[p]] * p_end[p] for p in pairs]
    khat = [k_ref[rws[p], sls[p]] * p_end[p] for p in pairs]
    v = [v_ref[rws[p], sls[p]] for p in pairs]
    rhs = [jnp.concatenate([be_ref[rws[p], sls[p]] * p_inv[p], k_ref[rws[p], sls[p]] * p_inv[p]],
                           axis=0).astype(BF16) for p in pairs]

    a_ab, a_ak, qcat = {}, {}, {}
    for p, h in heads:
        mh = (lane // W_HEAD_DIM) == h
        lhs = jnp.concatenate([jnp.where(mh, at[p], 0.0), jnp.where(mh, rt[p], 0.0)], axis=0).astype(BF16)
        x = lax.dot_general(lhs, rhs[p], (((1,), (1,)), ((), ())), preferred_element_type=F32)
        a_ab[p, h] = jnp.where(strict, x[:tc, :tc], 0.0)
        a_ak[p, h] = jnp.where(strict, x[:tc, tc:], 0.0)
        qcat[p, h] = jnp.concatenate([jnp.where(incl, x[tc:, tc:], 0.0),
                                      jnp.where(incl, x[tc:, :tc], 0.0)], axis=1)

    t_inv = {ph: eye + a_ab[ph] for ph in heads}
    pw = {ph: _bdot(a_ab[ph], a_ab[ph]) for ph in heads}
    akv = {(p, h): _bdot(a_ak[p, h], v[p]) for p, h in heads}
    n = 2
    while n < tc:
        for ph in heads:
            res = _bdot(jnp.concatenate([pw[ph], t_inv[ph]], axis=0), pw[ph])
            t_inv[ph] = t_inv[ph] + res[tc:]
            pw[ph] = res[:tc]
        n *= 2
    tw = {(p, h): _bdot(t_inv[p, h], jnp.concatenate([at[p], akv[p, h]], axis=1)) for p, h in heads}

    zero = jnp.zeros((tc, LANE), F32)
    w = [jnp.where(mh0, tw[p, 0][:, :LANE], tw[p, 1][:, :LANE]) for p in pairs]
    u0 = [jnp.where(mh0, tw[p, 0][:, LANE:], tw[p, 1][:, LANE:]) for p in pairs]
    vu = [jnp.concatenate([jnp.concatenate([v[p], zero], axis=1),
                           jnp.concatenate([u0[p], w[p]], axis=1)], axis=0) for p in pairs]
    yg = {(p, h): _bdot(qcat[p, h], vu[p]) for p, h in heads}
    m_mat = [_bdot_tn(bh[p], w[p]) for p in pairs]
    z0 = [_bdot_tn(jnp.concatenate([bh[p], khat[p]], axis=0), jnp.concatenate([u0[p], v[p]], axis=0))
          for p in pairs]
    for p in pairs:
        y0_o[rws[p], sls[p]] = jnp.where(mh0, yg[p, 0][:, :LANE], yg[p, 1][:, :LANE])
        g_o[rws[p], sls[p]] = rt[p] + jnp.where(mh0, yg[p, 0][:, LANE:], yg[p, 1][:, LANE:])
        m_o[p // npairs, p % npairs] = jnp.where(same_head, m_mat[p], 0.0) + jnp.where(
            prow == pcol, jnp.broadcast_to(jnp.exp(tot[p]), (LANE, LANE)), 0.0)
        z0_o[p // npairs, p % npairs] = jnp.where(same_head, z0[p], 0.0)


def _rwkv_chunk_ops(r, lw, kh, v, alpha, beta, tc=CHUNK, nck=RWKV_CHUNKS):
    bsz, seq, cw = r.shape
    npairs = cw // LANE
    nc = seq // tc
    nck = min(nck, nc)
    tok = pl.BlockSpec((None, nck * tc, cw), lambda b, c: (b, c, 0))
    mat = pl.BlockSpec((None, nck, npairs, LANE, LANE), lambda b, c: (b, c, 0, 0, 0))
    tok_shape = jax.ShapeDtypeStruct((bsz, seq, cw), F32)
    mat_shape = jax.ShapeDtypeStruct((bsz, nc, npairs, LANE, LANE), F32)
    return pl.pallas_call(
        functools.partial(_rwkv_chunk_kernel, tc=tc, nck=nck, npairs=npairs),
        grid=(bsz, nc // nck),
        in_specs=[tok] * 6,
        out_specs=[tok, tok, mat, mat],
        out_shape=[tok_shape, tok_shape, mat_shape, mat_shape],
        compiler_params=_cparams(("parallel", "parallel")),
        name="rwkv_chunk_ops",
    )(r, lw, kh, v, alpha, beta)


def _rwkv_scan_kernel(g_ref, y0_ref, m_ref, z0_ref, bo_ref, gate_ref, ln_ref, hs_ref, o_ref, state_ref,
                      *, tc, nch, npairs):
    @pl.when(pl.program_id(1) == 0)
    def _():
        state_ref[...] = jnp.zeros(state_ref.shape, F32)

    hs = hs_ref[...]
    inv_n = 1.0 / W_HEAD_DIM
    sls = [slice(p * LANE, (p + 1) * LANE) for p in range(npairs)]
    st = [state_ref[p] for p in range(npairs)]
    for ch in range(nch):
        rows = slice(ch * tc, (ch + 1) * tc)
        gm = [_dot_x3(jnp.concatenate([g_ref[rows, sls[p]], m_ref[ch, p]], axis=0), st[p]) for p in range(npairs)]
        y = [gm[p][:tc] + y0_ref[rows, sls[p]] for p in range(npairs)]
        st = [gm[p][tc:] + z0_ref[ch, p] for p in range(npairs)]
        mu = [_dot_sel(y[p], hs) * inv_n for p in range(npairs)]
        d = [y[p] - mu[p] for p in range(npairs)]
        var = [_dot_sel(d[p] * d[p], hs) * inv_n for p in range(npairs)]
        for p in range(npairs):
            yn = d[p] * lax.rsqrt(var[p] + LNX_EPS) * ln_ref[0:1, sls[p]] + ln_ref[1:2, sls[p]]
            o_ref[rows, sls[p]] = ((yn + bo_ref[rows, sls[p]]) * gate_ref[rows, sls[p]]).astype(o_ref.dtype)
    for p in range(npairs):
        state_ref[p] = st[p]


def _rwkv_scan(g, y0, m, z0, bonus, gate, ln, tc=CHUNK, nch=RWKV_CHUNKS):
    bsz, seq, cw = g.shape
    npairs = cw // LANE
    nch = min(nch, seq // tc)
    hs = _head_sum_matrix(LANE, W_HEAD_DIM)
    tok = pl.BlockSpec((None, tc * nch, cw), lambda b, c: (b, c, 0))
    mat = pl.BlockSpec((None, nch, npairs, LANE, LANE), lambda b, c: (b, c, 0, 0, 0))
    return pl.pallas_call(
        functools.partial(_rwkv_scan_kernel, tc=tc, nch=nch, npairs=npairs),
        grid=(bsz, seq // (tc * nch)),
        in_specs=[tok, tok, mat, mat, tok, tok,
                  pl.BlockSpec(ln.shape, lambda b, c: (0, 0)),
                  pl.BlockSpec(hs.shape, lambda b, c: (0, 0))],
        out_specs=tok,
        out_shape=jax.ShapeDtypeStruct((bsz, seq, cw), BF16),
        scratch_shapes=[pltpu.VMEM((npairs, LANE, LANE), F32)],
        compiler_params=_cparams(("parallel", "arbitrary")),
        name="rwkv_scan",
    )(g, y0, m, z0, bonus, gate, ln, hs)


def _out_proj_kernel(x_ref, a_ref, b_ref, c_ref, w_ref, o_ref, *, wa, wb):
    acc = jnp.dot(a_ref[...], w_ref[0:wa, :], preferred_element_type=F32)
    acc = acc + jnp.dot(b_ref[...], w_ref[wa:wa + wb, :], preferred_element_type=F32)
    acc = acc + jnp.dot(c_ref[...], w_ref[wa + wb:, :], preferred_element_type=F32)
    o_ref[...] = x_ref[...] + acc


def _out_proj(x2d, oa, ob, oc, w, tm=TM, tn=1024):
    m, d = x2d.shape
    wa, wb, wc = oa.shape[1], ob.shape[1], oc.shape[1]
    tm = min(tm, m)
    tn = min(tn, d)
    return pl.pallas_call(
        functools.partial(_out_proj_kernel, wa=wa, wb=wb),
        grid=(m // tm, d // tn),
        in_specs=[pl.BlockSpec((tm, tn), lambda i, j: (i, j)),
                  pl.BlockSpec((tm, wa), lambda i, j: (i, 0)),
                  pl.BlockSpec((tm, wb), lambda i, j: (i, 0)),
                  pl.BlockSpec((tm, wc), lambda i, j: (i, 0)),
                  pl.BlockSpec((wa + wb + wc, tn), lambda i, j: (0, j))],
        out_specs=pl.BlockSpec((tm, tn), lambda i, j: (i, j)),
        out_shape=jax.ShapeDtypeStruct((m, d), F32),
        compiler_params=_cparams(("parallel", "arbitrary")),
        name="out_proj",
    )(x2d, oa, ob, oc, w)


def _gate_up_kernel(h_ref, wg_ref, wu_ref, o_ref):
    h = h_ref[...]
    gate = jnp.dot(h, wg_ref[...], preferred_element_type=F32)
    up = jnp.dot(h, wu_ref[...], preferred_element_type=F32)
    o_ref[...] = (gate / (1.0 + jnp.exp(-gate)) * up).astype(o_ref.dtype)


def _gate_up(h, w_gate_up, tm=TM, tf_cap=512):
    m, d = h.shape
    dff = w_gate_up.shape[1] // 2
    tm = min(tm, m)
    tf = _pick_tile(dff, tf_cap)
    nf = dff // tf
    return pl.pallas_call(
        _gate_up_kernel,
        grid=(m // tm, nf),
        in_specs=[pl.BlockSpec((tm, d), lambda i, j: (i, 0)),
                  pl.BlockSpec((d, tf), lambda i, j: (0, j)),
                  pl.BlockSpec((d, tf), lambda i, j: (0, j + nf))],
        out_specs=pl.BlockSpec((tm, tf), lambda i, j: (i, j)),
        out_shape=jax.ShapeDtypeStruct((m, dff), BF16),
        compiler_params=_cparams(("parallel", "arbitrary")),
        name="ffn_gate_up",
    )(h, w_gate_up, w_gate_up)


def _down_kernel(x_ref, a_ref, w_ref, o_ref):
    o_ref[...] = x_ref[...] + jnp.dot(a_ref[...], w_ref[...], preferred_element_type=F32)


def _down_proj(x2d, act, w, tm=TM, tn=512):
    m, d = x2d.shape
    dff = act.shape[1]
    tm = min(tm, m)
    tn = min(tn, d)
    return pl.pallas_call(
        _down_kernel,
        grid=(m // tm, d // tn),
        in_specs=[pl.BlockSpec((tm, tn), lambda i, j: (i, j)),
                  pl.BlockSpec((tm, dff), lambda i, j: (i, 0)),
                  pl.BlockSpec((dff, tn), lambda i, j: (0, j))],
        out_specs=pl.BlockSpec((tm, tn), lambda i, j: (i, j)),
        out_shape=jax.ShapeDtypeStruct((m, d), F32),
        compiler_params=_cparams(("parallel", "arbitrary")),
        name="ffn_down",
    )(x2d, act, w)


def _pad_cols(w, width):
    return jnp.pad(w, ((0, 0), (0, width - w.shape[1])))


def _pad_rows(w, height):
    return jnp.pad(w, ((0, height - w.shape[0]), (0, 0)))


def _pad_vec(v, width):
    return jnp.pad(v, (0, width - v.shape[0]))


def kernel(x, norm_mix_g, w_in, w_in_vres, rwkv_mu, rwkv_mu_vres, rwkv_w0, rwkv_w_lora_b, rwkv_a0,
           rwkv_a_lora_b, rwkv_v0, rwkv_v_lora_b, rwkv_g_lora_b, rwkv_k_k, rwkv_k_a, rwkv_r_k,
           rwkv_lnx_g, rwkv_lnx_b, w_out, norm_ffn_g, w_gate_up, w_down, final_norm_g):
    bsz, seq, d_model = x.shape
    depth = w_in.shape[0]
    m = bsz * seq
    d_mix = w_out.shape[1]
    a_w = d_mix // 2
    qi_w = IDX_HEADS * IDX_DIM
    b_v_w = d_mix // 4
    b_qk_w = b_v_w // 2
    c_w = d_mix // 4
    r_qk_dim = b_qk_w // R_HEADS
    top_k = min(TOPK_MAX, seq // 4)
    vres_pad = 2 * LANE

    o_q, o_k, o_v = 0, a_w, 2 * a_w
    o_qi = 3 * a_w
    o_ki = o_qi + qi_w
    o_wi = o_ki + IDX_DIM
    o_bq = o_wi + IDX_HEADS
    o_bk = o_bq + b_qk_w
    o_bv = o_bk + b_qk_w
    o_bg = o_bv + b_v_w
    o_c = o_bg + b_v_w
    o_wl = o_c + 3 * c_w
    o_al = o_wl + DECAY_LORA
    o_gl = o_al + AAA_LORA

    a_rot = A_HEAD_DIM // ROPE_FRAC
    tq_ = _rope_tables(seq, A_HEAD_DIM, a_rot, ROPE_THETA, scale=A_HEAD_DIM ** -0.5 * math.log2(math.e))
    tk_ = _rope_tables(seq, A_HEAD_DIM, a_rot, ROPE_THETA)
    tab_q = tuple(t[None] for t in tq_)
    tab_k = tuple(t[None] for t in tk_)
    i_rot = IDX_DIM // ROPE_FRAC
    tab_qi = tuple(t[None] for t in _rope_tables(seq, IDX_DIM, i_rot, ROPE_THETA))
    lane = jnp.arange(LANE)
    kiwi_pass = jnp.where(lane < IDX_DIM, 1.0, IDX_W_SCALE).astype(F32)
    c_kw, s1_kw, s2_kw = _rope_tables(seq, LANE, i_rot, ROPE_THETA, pass_scale=kiwi_pass)
    tab_kiwi = (c_kw[None], s1_kw[None], s2_kw[None])
    tbq = _rope_tables(seq, r_qk_dim, r_qk_dim, R_THETA)
    tbk = _rope_tables(seq, r_qk_dim, r_qk_dim, R_THETA, scale=r_qk_dim ** -0.5)
    tab_bqk = tuple(jnp.stack([a, b]) for a, b in zip(tbq, tbk))

    x2d = x.reshape(m, d_model)
    v_first = None
    for l in range(depth):
        wl_ = w_in[l]
        cols = lambda o, n: wl_[:, o:o + n]
        w_q = cols(o_q, a_w).astype(BF16)
        w_k = cols(o_k, a_w).astype(BF16)
        w_v = cols(o_v, a_w).astype(BF16)
        w_bv = cols(o_bv, b_v_w).astype(BF16)
        w_qi = cols(o_qi, qi_w).astype(BF16)
        w_kiwi = _pad_cols(cols(o_ki, IDX_DIM + IDX_HEADS), LANE).astype(BF16)
        w_bqk = cols(o_bq, 2 * b_qk_w).astype(BF16)
        w_bg = cols(o_bg, b_v_w).astype(BF16)
        vres_w = (w_in_vres[l - 1] if l > 0 else jnp.zeros((d_model, MV_LORA), F32))
        w_c = jnp.concatenate([cols(o_c, 3 * c_w), _pad_cols(cols(o_wl, DECAY_LORA), LANE),
                               _pad_cols(cols(o_al, AAA_LORA), LANE), cols(o_gl, GATE_LORA),
                               _pad_cols(vres_w, vres_pad)], axis=1).astype(BF16)
        mu = rwkv_mu[l]
        mu_vres = rwkv_mu_vres[l - 1] if l > 0 else jnp.zeros((MV_LORA,), F32)
        mu_c = jnp.concatenate([mu[:3 * c_w], _pad_vec(mu[3 * c_w:3 * c_w + DECAY_LORA], LANE),
                                _pad_vec(mu[3 * c_w + DECAY_LORA:3 * c_w + DECAY_LORA + AAA_LORA], LANE),
                                mu[3 * c_w + DECAY_LORA + AAA_LORA:], _pad_vec(mu_vres, vres_pad)])[None, :]

        h = _rmsnorm(x2d, norm_mix_g[l], BF16)
        qt = _mm_rope(h, w_q, tab_q, a_rot // 2, seq, a_w, BF16, transpose=True)
        ak = _mm_rope(h, w_k, tab_k, a_rot // 2, seq, a_w, BF16).reshape(bsz, seq, a_w)
        vt = _mm_plain(h, w_v, BF16, seq, transpose=True, pad_rows=ONES_ROWS)
        bv = _mm_plain(h, w_bv, BF16, seq, tn_cap=512).reshape(bsz, seq, b_v_w)
        qit = _mm_rope(h, w_qi, tab_qi, i_rot // 2, seq, qi_w, BF16, transpose=True)
        kiwi = _mm_rope(h, w_kiwi, tab_kiwi, i_rot // 2, seq, LANE, F32).reshape(bsz, seq, LANE)
        bqk = _mm_rope(h, w_bqk, tab_bqk, r_qk_dim // 2, seq, b_qk_w, BF16).reshape(bsz, seq, 2 * b_qk_w)
        bg = _mm_plain(h, w_bg, F32, seq).reshape(bsz, seq, b_v_w)
        cproj = _mm_plain(h, w_c, F32, seq, tn_cap=768).reshape(bsz, seq, w_c.shape[1])

        ki = kiwi[:, :, :IDX_DIM].astype(BF16)
        wit = jnp.swapaxes(kiwi[:, :, IDX_DIM:IDX_DIM + IDX_HEADS], 1, 2)
        bias_t = _indexer_bias(ki, qit, wit, top_k)
        out_a = _masked_attention(qt, ak, vt, bias_t)

        out_b = _retention(bqk, bv, bg)

        vecs = jnp.stack([rwkv_w0[l], rwkv_a0[l], rwkv_k_k[l], rwkv_k_a[l], rwkv_r_k[l].reshape(-1),
                          rwkv_v0[l - 1] if l > 0 else jnp.zeros((c_w,), F32),
                          jnp.zeros((c_w,), F32), jnp.zeros((c_w,), F32)])
        wlb = _pad_rows(rwkv_w_lora_b[l], LANE).astype(BF16)
        alb = _pad_rows(rwkv_a_lora_b[l], LANE).astype(BF16)
        glb = rwkv_g_lora_b[l].astype(BF16)
        vlb = _pad_rows(rwkv_v_lora_b[l - 1], vres_pad).astype(BF16) if l > 0 else None
        r_, lw_, kh_, v_, al_, be_, g_, bo_ = _rwkv_prep(cproj, mu_c, vecs, wlb, alb, glb, vlb, v_first, c_w)
        if l == 0:
            v_first = v_
        gm, y0, mm, z0 = _rwkv_chunk_ops(r_, lw_, kh_, v_, al_, be_)
        ln = jnp.stack([rwkv_lnx_g[l], rwkv_lnx_b[l]] + [jnp.zeros((c_w,), F32)] * 6)
        out_c = _rwkv_scan(gm, y0, mm, z0, bo_, g_, ln)

        x2d = _out_proj(x2d, out_a.reshape(m, a_w), out_b.reshape(m, b_v_w), out_c.reshape(m, c_w),
                        w_out[l].astype(BF16))

        h = _rmsnorm(x2d, norm_ffn_g[l], BF16)
        act = _gate_up(h, w_gate_up[l].astype(BF16))
        x2d = _down_proj(x2d, act, w_down[l].astype(BF16))

    return _rmsnorm(x2d, final_norm_g, F32).reshape(bsz, seq, d_model)
```

```python
import functools
import math

import jax
import jax.numpy as jnp
from jax import lax
from jax.experimental import pallas as pl
from jax.experimental.pallas import tpu as pltpu

F32 = jnp.float32
BF16 = jnp.bfloat16
I32 = jnp.int32

CHUNK = 64
A_HEAD_DIM = 128
IDX_HEADS = 16
IDX_DIM = 64
IDX_W_SCALE = (IDX_HEADS * IDX_DIM) ** -0.5
TOPK_MAX = 256
ROPE_THETA = 500000.0
ROPE_FRAC = 4
R_HEADS = 4
R_THETA = 10000.0
W_HEAD_DIM = 64
DECAY_LORA = 96
AAA_LORA = 96
MV_LORA = 64
GATE_LORA = 256
LNX_EPS = 64e-5
RMS_EPS = 1e-5

LANE = 128
SUBLANE = 8
VMEM_LIMIT = 56 * 1024 * 1024
TM = 1024
TM_NORM = 512
IDX_TQ = 256
IDX_TKS = 128
IDX_TKC = 1024
SCORE_UNROLL = 4
COUNT_CHAINS = 8
ATTN_TQ = 512
ATTN_TK = 1024
ONES_ROWS = SUBLANE
RET_CHUNK = 256
PREP_ROWS = 256
RWKV_CHUNKS = 4

NEG_BIAS = -1e30
M_INIT = -1e20
INT_MIN = -2147483648
INT_MAX = 2147483647


def _cparams(sem):
    return pltpu.CompilerParams(dimension_semantics=sem, vmem_limit_bytes=VMEM_LIMIT)


def _pick_tile(n, cap):
    best = LANE
    t = LANE
    while t <= min(n, cap):
        if n % t == 0:
            best = t
        t += LANE
    return best


def _rmsnorm_kernel(x_ref, g_ref, o_ref):
    x = x_ref[...]
    ms = jnp.mean(x * x, axis=-1, keepdims=True)
    o_ref[...] = (x * lax.rsqrt(ms + RMS_EPS) * g_ref[...]).astype(o_ref.dtype)


def _rmsnorm(x2d, g, out_dtype, tm=TM_NORM):
    m, d = x2d.shape
    return pl.pallas_call(
        _rmsnorm_kernel,
        grid=(m // tm,),
        in_specs=[pl.BlockSpec((tm, d), lambda i: (i, 0)),
                  pl.BlockSpec((1, d), lambda i: (0, 0))],
        out_specs=pl.BlockSpec((tm, d), lambda i: (i, 0)),
        out_shape=jax.ShapeDtypeStruct((m, d), out_dtype),
        compiler_params=_cparams(("parallel",)),
        name="rmsnorm",
    )(x2d, g.reshape(1, d).astype(F32))


def _store_cols(o_ref, blk, x, transpose, pad_rows):
    if transpose:
        r0 = blk * (LANE + pad_rows)
        o_ref[r0:r0 + LANE, :] = x.T.astype(o_ref.dtype)
        if pad_rows:
            o_ref[r0 + LANE:r0 + LANE + pad_rows, :] = jnp.ones((pad_rows, x.shape[0]), o_ref.dtype)
    else:
        o_ref[:, blk * LANE:(blk + 1) * LANE] = x.astype(o_ref.dtype)


def _proj_out_spec(m, n, tm, tn, seq, out_dtype, transpose, pad_rows):
    if not transpose:
        return pl.BlockSpec((tm, tn), lambda i, j: (i, j)), jax.ShapeDtypeStruct((m, n), out_dtype)
    tpb = seq // tm
    rows = lambda cols: cols // LANE * (LANE + pad_rows)
    return (pl.BlockSpec((None, rows(tn), tm), lambda i, j: (i // tpb, j, i % tpb)),
            jax.ShapeDtypeStruct((m // seq, rows(n), seq), out_dtype))


def _mm_plain_kernel(a_ref, w_ref, o_ref, *, transpose, pad_rows):
    acc = jnp.dot(a_ref[...], w_ref[...], preferred_element_type=F32)
    if not transpose:
        o_ref[...] = acc.astype(o_ref.dtype)
        return
    for blk in range(acc.shape[1] // LANE):
        _store_cols(o_ref, blk, acc[:, blk * LANE:(blk + 1) * LANE], transpose, pad_rows)


def _mm_plain(a, w, out_dtype, seq, tm=TM, tn_cap=1024, transpose=False, pad_rows=0):
    m, k = a.shape
    n = w.shape[1]
    tm = min(tm, seq)
    tn = _pick_tile(n, tn_cap)
    out_spec, out_shape = _proj_out_spec(m, n, tm, tn, seq, out_dtype, transpose, pad_rows)
    return pl.pallas_call(
        functools.partial(_mm_plain_kernel, transpose=transpose, pad_rows=pad_rows),
        grid=(m // tm, n // tn),
        in_specs=[pl.BlockSpec((tm, k), lambda i, j: (i, 0)),
                  pl.BlockSpec((k, tn), lambda i, j: (0, j))],
        out_specs=out_spec,
        out_shape=out_shape,
        compiler_params=_cparams(("parallel", "arbitrary")),
        name="proj_plain",
    )(a, w)


def _mm_rope_kernel(a_ref, w_ref, c_ref, s1_ref, s2_ref, o_ref, *, half, transpose):
    acc = jnp.dot(a_ref[...], w_ref[...], preferred_element_type=F32)
    c, s1, s2 = c_ref[...], s1_ref[...], s2_ref[...]
    for blk in range(acc.shape[1] // LANE):
        x = acc[:, blk * LANE:(blk + 1) * LANE]
        up = pltpu.roll(x, LANE - half, axis=1)
        dn = pltpu.roll(x, half, axis=1)
        _store_cols(o_ref, blk, x * c + up * s1 + dn * s2, transpose, 0)


def _mm_rope(a, w, tables, half, seq, tn, out_dtype, tm=TM, transpose=False):
    m, k = a.shape
    n = w.shape[1]
    tm = min(tm, seq)
    tpb = seq // tm
    tab_spec = pl.BlockSpec((None, tm, LANE), lambda i, j: (j, i % tpb, 0))
    out_spec, out_shape = _proj_out_spec(m, n, tm, tn, seq, out_dtype, transpose, 0)
    return pl.pallas_call(
        functools.partial(_mm_rope_kernel, half=half, transpose=transpose),
        grid=(m // tm, n // tn),
        in_specs=[pl.BlockSpec((tm, k), lambda i, j: (i, 0)),
                  pl.BlockSpec((k, tn), lambda i, j: (0, j)),
                  tab_spec, tab_spec, tab_spec],
        out_specs=out_spec,
        out_shape=out_shape,
        compiler_params=_cparams(("parallel", "arbitrary")),
        name="proj_rope",
    )(a, w, *tables)


def _rope_tables(seq, group, rot_dim, theta, scale=1.0, pass_scale=None):
    half = rot_dim // 2
    freqs = jnp.power(F32(theta), -jnp.arange(half, dtype=F32) / half)
    ang = jnp.arange(seq, dtype=F32)[:, None] * freqs[None, :]
    cos, sin = jnp.cos(ang), jnp.sin(ang)
    lane = jnp.arange(LANE) % group
    idx = lane % half
    cosl, sinl = cos[:, idx], sin[:, idx]
    passv = jnp.ones((LANE,), F32) if pass_scale is None else pass_scale
    c = jnp.where(lane < rot_dim, cosl, passv[None, :])
    s1 = jnp.where(lane < half, -sinl, 0.0)
    s2 = jnp.where((lane >= half) & (lane < rot_dim), sinl, 0.0)
    return c * scale, s1 * scale, s2 * scale


def _indexer_kernel(ki_ref, qit_ref, wit_ref, bias_ref, key_ref, *, tq, tks, tkc, top_k, seq):
    qb = pl.program_id(1)
    span = tks * SCORE_UNROLL
    nks = ((qb + 1) * tq + span - 1) // span * SCORE_UNROLL
    nkc = (nks * tks + tkc - 1) // tkc
    q_chunk = (qb * tq + lax.broadcasted_iota(I32, (1, tq), 1)) // CHUNK
    w = wit_ref[...]

    def score_block(kb, rmax):
        k0 = pl.multiple_of(kb * tks, tks)
        kblk = ki_ref[pl.ds(k0, tks), :]
        acc = jnp.zeros((tks, tq), F32)
        for h in range(IDX_HEADS):
            x = jnp.dot(kblk, qit_ref[h * IDX_DIM:(h + 1) * IDX_DIM, :], preferred_element_type=F32)
            acc = acc + jnp.maximum(x, 0.0) * w[h:h + 1, :]
        k_chunk = (k0 + lax.broadcasted_iota(I32, (tks, 1), 0)) // CHUNK
        bits = lax.bitcast_convert_type(acc, I32)
        key = bits ^ ((bits >> 31) & jnp.int32(INT_MAX))
        key = jnp.where(k_chunk <= q_chunk, key, jnp.int32(INT_MIN))
        key_ref[pl.ds(k0, tks), :] = key
        return jnp.maximum(rmax, jnp.max(key.reshape(tks // SUBLANE, SUBLANE, tq), axis=0))

    def score_step(kbu, rmax):
        for u in range(SCORE_UNROLL):
            rmax = score_block(kbu * SCORE_UNROLL + u, rmax)
        return rmax

    rmax = lax.fori_loop(0, nks // SCORE_UNROLL, score_step, jnp.full((SUBLANE, tq), INT_MIN, I32))
    colmax = jnp.max(rmax, axis=0, keepdims=True)

    def pad_block(kb, carry):
        k0 = pl.multiple_of(kb * tks, tks)
        key_ref[pl.ds(k0, tks), :] = jnp.full((tks, tq), INT_MIN, I32)
        return carry

    lax.fori_loop(nks, nkc * (tkc // tks), pad_block, 0)

    def tree_sum(terms):
        while len(terms) > 1:
            terms = [a + b for a, b in zip(terms[0::2], terms[1::2])]
        return terms[0]

    def count_ge(cand):
        def count_block(kb, cnt):
            k0 = pl.multiple_of(kb * tkc, tkc)
            part = tkc // COUNT_CHAINS
            sums = []
            for g in range(COUNT_CHAINS):
                ind = jnp.where(key_ref[pl.ds(k0 + g * part, part), :] >= cand, 1, 0)
                sums.append(jnp.sum(ind.reshape(part // SUBLANE, SUBLANE, tq), axis=0))
            return cnt + tree_sum(sums)

        cnt = lax.fori_loop(0, nkc, count_block, jnp.zeros((SUBLANE, tq), I32))
        return jnp.sum(cnt, axis=0, keepdims=True)

    def probe_step(count_fn, state, mid):
        lo, hi, cnt_lo, cnt_hi = state
        cnt = count_fn(mid)
        ge = cnt >= top_k
        new_hi = jnp.where(cnt == top_k, mid + 1, jnp.where(ge, hi, mid))
        return (jnp.where(ge, mid, lo), new_hi, jnp.where(ge, cnt, cnt_lo), jnp.where(ge, cnt_hi, cnt))

    def midpoint(state):
        lo, hi = state[0], state[1]
        return (lo >> 1) + (hi >> 1) + (lo & hi & 1)

    def n_active(state):
        return jnp.max(((state[0] + 1) < state[1]).astype(I32))

    binade = 1 << 23

    def bisect(state):
        def next_mid(st, it):
            mid = midpoint(st)
            step = jnp.int32(3 * binade) << jnp.minimum(it, 6)
            descend = (st[0] == jnp.int32(INT_MIN)) & (st[1] > jnp.int32(INT_MIN) + step)
            return jnp.where(descend, jnp.maximum(st[1] - step, mid), mid)

        def body(carry):
            it, _, st = carry
            st = probe_step(count_ge, st, next_mid(st, it + 1))
            st = probe_step(count_ge, st, next_mid(st, it + 2))
            return it + 2, n_active(st), st

        return lax.while_loop(lambda c: (c[1] > 0) & (c[0] < 48), body, (jnp.int32(0), n_active(state), state))[2]

    big = jnp.int32(1 << 30)
    state = (jnp.full((1, tq), INT_MIN, I32), colmax + 1, jnp.full((1, tq), big, I32), jnp.zeros((1, tq), I32))
    first = jnp.where(colmax > jnp.int32(INT_MIN + 4 * binade), colmax - jnp.int32(3 * binade),
                      jnp.int32(INT_MIN + 1))
    lo, _, cnt_lo, cnt_hi = bisect(probe_step(count_ge, state, first))

    thr = jnp.maximum(lo, jnp.int32(INT_MIN + 1))
    tie_q = (cnt_lo > top_k) & (lo > jnp.int32(INT_MIN))
    any_tie = jnp.max(tie_q.astype(I32))

    @pl.when(any_tie == 0)
    def _():
        def write_block(kb, carry):
            k0 = pl.multiple_of(kb * tks, tks)
            bias_ref[pl.ds(k0, tks), :] = jnp.where(key_ref[pl.ds(k0, tks), :] >= thr, 0.0, NEG_BIAS)
            return carry

        lax.fori_loop(0, nks, write_block, 0)

    @pl.when(any_tie != 0)
    def _():
        need = jnp.where(tie_q, top_k - cnt_hi, big).astype(F32)
        lr = lax.broadcasted_iota(I32, (LANE, LANE), 0)
        lc = lax.broadcasted_iota(I32, (LANE, LANE), 1)
        lower = (lr >= lc).astype(BF16)

        def write_block(kb, run):
            k0 = pl.multiple_of(kb * LANE, LANE)
            blk = key_ref[pl.ds(k0, LANE), :]
            eq = blk == thr
            pc = jnp.dot(lower, jnp.where(eq, 1.0, 0.0).astype(BF16), preferred_element_type=F32)
            keep_eq = jnp.where((run + pc) <= need, 0.0, NEG_BIAS)
            bias_ref[pl.ds(k0, LANE), :] = jnp.where(blk > thr, 0.0, jnp.where(eq, keep_eq, NEG_BIAS))
            return run + pc[LANE - 1:LANE, :]

        lax.fori_loop(0, nks * (tks // LANE), write_block, jnp.zeros((1, tq), F32))

    def fill_block(kb, carry):
        k0 = pl.multiple_of(kb * tks, tks)
        bias_ref[pl.ds(k0, tks), :] = jnp.full((tks, tq), NEG_BIAS, F32)
        return carry

    lax.fori_loop(nks, seq // tks, fill_block, 0)


def _indexer_bias(ki, qit, wit, top_k, tq=IDX_TQ, tks=IDX_TKS, tkc=IDX_TKC):
    bsz, seq, _ = ki.shape
    tq = min(tq, seq)
    assert seq % tkc == 0 and tkc % (tks * SCORE_UNROLL) == 0 and tks % LANE == 0
    return pl.pallas_call(
        functools.partial(_indexer_kernel, tq=tq, tks=tks, tkc=tkc, top_k=top_k, seq=seq),
        grid=(bsz, seq // tq),
        in_specs=[pl.BlockSpec((None, seq, IDX_DIM), lambda b, i: (b, 0, 0)),
                  pl.BlockSpec((None, IDX_HEADS * IDX_DIM, tq), lambda b, i: (b, 0, i)),
                  pl.BlockSpec((None, IDX_HEADS, tq), lambda b, i: (b, 0, i))],
        out_specs=pl.BlockSpec((None, seq, tq), lambda b, i: (b, 0, i)),
        out_shape=jax.ShapeDtypeStruct((bsz, seq, seq), F32),
        scratch_shapes=[pltpu.VMEM((seq, tq), I32)],
        compiler_params=_cparams(("parallel", "arbitrary")),
        name="indexer_select",
    )(ki, qit, wit)


def _attn_kernel(qidx_ref, kidx_ref, qt_ref, k_ref, vt_ref, b_ref, o_ref, m_ref, acc_ref,
                 *, nheads, hd, hda, tq, tk):
    p = pl.program_id(1)
    qb = qidx_ref[p]
    kb = kidx_ref[p]
    kb_last = ((qb + 1) * tq - 1) // tk

    @pl.when(kb == 0)
    def _():
        m_ref[...] = jnp.full(m_ref.shape, M_INIT, F32)
        acc_ref[...] = jnp.zeros(acc_ref.shape, F32)

    bias = b_ref[...]
    hrows = [slice(h * hd, (h + 1) * hd) for h in range(nheads)]
    arows = [slice(h * hda, (h + 1) * hda) for h in range(nheads)]
    s = [jnp.dot(k_ref[:, r], qt_ref[r, :], preferred_element_type=F32) + bias for r in hrows]
    m_old = [m_ref[h] for h in range(nheads)]
    m_new = [jnp.maximum(m_old[h], jnp.max(s[h], axis=0, keepdims=True)) for h in range(nheads)]
    alpha = [jnp.exp2(m_old[h] - m_new[h]) for h in range(nheads)]
    pr = [jnp.exp2(s[h] - m_new[h]).astype(BF16) for h in range(nheads)]
    pv = [jnp.dot(vt_ref[arows[h], :], pr[h], preferred_element_type=F32) for h in range(nheads)]
    for h in range(nheads):
        m_ref[h] = m_new[h]
        acc_ref[arows[h], :] = alpha[h] * acc_ref[arows[h], :] + pv[h]

    @pl.when(kb == kb_last)
    def _():
        for h in range(nheads):
            a0 = h * hda
            out_t = acc_ref[a0:a0 + hd, :] / acc_ref[a0 + hd:a0 + hd + 1, :]
            o_ref[:, hrows[h]] = out_t.T.astype(o_ref.dtype)


def _masked_attention(qt, k, vt, bias_t, tq=ATTN_TQ, tk=ATTN_TK):
    bsz, width, seq = qt.shape
    nheads = width // A_HEAD_DIM
    hda = A_HEAD_DIM + ONES_ROWS
    tq = min(tq, seq)
    tk = min(tk, seq)
    nq = seq // tq
    pairs = [(i, j) for i in range(nq) for j in range(((i + 1) * tq - 1) // tk + 1)]
    qidx = jnp.asarray([pq for pq, _ in pairs], I32)
    kidx = jnp.asarray([pk for _, pk in pairs], I32)
    grid_spec = pltpu.PrefetchScalarGridSpec(
        num_scalar_prefetch=2,
        grid=(bsz, len(pairs)),
        in_specs=[pl.BlockSpec((None, width, tq), lambda b, p, qi, ki: (b, 0, qi[p])),
                  pl.BlockSpec((None, tk, width), lambda b, p, qi, ki: (b, ki[p], 0)),
                  pl.BlockSpec((None, nheads * hda, tk), lambda b, p, qi, ki: (b, 0, ki[p])),
                  pl.BlockSpec((None, tk, tq), lambda b, p, qi, ki: (b, ki[p], qi[p]))],
        out_specs=pl.BlockSpec((None, tq, width), lambda b, p, qi, ki: (b, qi[p], 0)),
        scratch_shapes=[pltpu.VMEM((nheads, 1, tq), F32),
                        pltpu.VMEM((nheads * hda, tq), F32)],
    )
    return pl.pallas_call(
        functools.partial(_attn_kernel, nheads=nheads, hd=A_HEAD_DIM, hda=hda, tq=tq, tk=tk),
        grid_spec=grid_spec,
        out_shape=jax.ShapeDtypeStruct((bsz, seq, width), BF16),
        compiler_params=_cparams(("parallel", "arbitrary")),
        name="masked_attention",
    )(qidx, kidx, qt, k, vt, bias_t)


def _retention_kernel(qk_ref, v_ref, g_ref, o_ref, state_ref, *, tc, dk, dv):
    @pl.when(pl.program_id(1) == 0)
    def _():
        state_ref[...] = jnp.zeros(state_ref.shape, F32)

    row = lax.broadcasted_iota(I32, (tc, tc), 0)
    col = lax.broadcasted_iota(I32, (tc, tc), 1)
    diff = (row - col).astype(F32)
    pos = lax.broadcasted_iota(I32, (tc, 1), 0).astype(F32)
    for h in range(R_HEADS):
        log_g = math.log1p(-(2.0 ** (-5.0 - h)))
        q = qk_ref[:, h * dk:(h + 1) * dk]
        k = qk_ref[:, (R_HEADS + h) * dk:(R_HEADS + h + 1) * dk]
        v = v_ref[:, h * dv:(h + 1) * dv]
        decay = jnp.where(diff >= 0, jnp.exp(jnp.maximum(diff, 0.0) * log_g), 0.0)
        s = lax.dot_general(q, k, (((1,), (1,)), ((), ())), preferred_element_type=F32) * decay
        intra = jnp.dot(s.astype(BF16), v, preferred_element_type=F32)
        xi = jnp.exp((pos + 1.0) * log_g)
        zeta = jnp.exp((tc - 1.0 - pos) * log_g)
        state = state_ref[h]
        cross = jnp.dot((q.astype(F32) * xi).astype(BF16), state.astype(BF16), preferred_element_type=F32)
        kz = (k.astype(F32) * zeta).astype(BF16)
        kv = lax.dot_general(kz, v, (((0,), (0,)), ((), ())), preferred_element_type=F32)
        state_ref[h] = state * math.exp(tc * log_g) + kv
        ret = intra + cross
        mu = jnp.mean(ret, axis=1, keepdims=True)
        d = ret - mu
        var = jnp.mean(d * d, axis=1, keepdims=True)
        gate = g_ref[:, h * dv:(h + 1) * dv]
        gate = gate / (1.0 + jnp.exp(-gate))
        o_ref[:, h * dv:(h + 1) * dv] = (d * lax.rsqrt(var + 1e-5) * gate).astype(o_ref.dtype)


def _retention(bqk, bv, bg, tc=RET_CHUNK):
    bsz, seq, w2 = bqk.shape
    dk = w2 // (2 * R_HEADS)
    dv = bv.shape[2] // R_HEADS
    tc = min(tc, seq)
    return pl.pallas_call(
        functools.partial(_retention_kernel, tc=tc, dk=dk, dv=dv),
        grid=(bsz, seq // tc),
        in_specs=[pl.BlockSpec((None, tc, w2), lambda b, c: (b, c, 0)),
                  pl.BlockSpec((None, tc, R_HEADS * dv), lambda b, c: (b, c, 0)),
                  pl.BlockSpec((None, tc, R_HEADS * dv), lambda b, c: (b, c, 0))],
        out_specs=pl.BlockSpec((None, tc, R_HEADS * dv), lambda b, c: (b, c, 0)),
        out_shape=jax.ShapeDtypeStruct((bsz, seq, R_HEADS * dv), BF16),
        scratch_shapes=[pltpu.VMEM((R_HEADS, dk, dv), F32)],
        compiler_params=_cparams(("parallel", "arbitrary")),
        name="retention",
    )(bqk, bv, bg)


def _head_sum_matrix(width, hd):
    r = jnp.arange(width)
    return (r[:, None] // hd == r[None, :] // hd).astype(BF16)


def _sigmoid(x):
    return 1.0 / (1.0 + jnp.exp(-x))


def _split_bf16(x):
    hi = x.astype(BF16)
    return hi, (x - hi.astype(F32)).astype(BF16)


def _dot_x3(a, b):
    ah, al = _split_bf16(a)
    bh, bl = _split_bf16(b)
    return (jnp.dot(ah, bh, preferred_element_type=F32) + jnp.dot(ah, bl, preferred_element_type=F32)
            + jnp.dot(al, bh, preferred_element_type=F32))


def _dot_sel(a, sel):
    ah, al = _split_bf16(a)
    return jnp.dot(ah, sel, preferred_element_type=F32) + jnp.dot(al, sel, preferred_element_type=F32)


def _rwkv_prep_kernel(*refs, cw, has_vres):
    if has_vres:
        (c_ref, mu_ref, vec_ref, wlb_ref, alb_ref, glb_ref, hs_ref, vlb_ref, vfirst_ref,
         r_o, lw_o, k_o, v_o, al_o, be_o, g_o, bo_o, carry_ref) = refs
    else:
        (c_ref, mu_ref, vec_ref, wlb_ref, alb_ref, glb_ref, hs_ref,
         r_o, lw_o, k_o, v_o, al_o, be_o, g_o, bo_o, carry_ref) = refs
    tr = c_ref.shape[0]

    @pl.when(pl.program_id(1) == 0)
    def _():
        carry_ref[...] = jnp.zeros(carry_ref.shape, F32)

    c = c_ref[...]
    row = lax.broadcasted_iota(I32, (tr, 1), 0)
    prev = jnp.where(row == 0, carry_ref[0:1, :], pltpu.roll(c, 1, axis=0))
    carry_ref[0:1, :] = c[tr - 1:tr, :]
    cs = c + (prev - c) * mu_ref[...]

    r = cs[:, 0:cw]
    k = cs[:, cw:2 * cw]
    v = cs[:, 2 * cw:3 * cw]
    o = 3 * cw
    wl = cs[:, o:o + LANE]
    al = cs[:, o + LANE:o + 2 * LANE]
    gl = cs[:, o + 2 * LANE:o + 2 * LANE + GATE_LORA]
    w0, a0, k_k, k_a, r_k, v0 = (vec_ref[i:i + 1, :] for i in range(6))

    z = -(w0 + jnp.dot(jnp.tanh(wl).astype(BF16), wlb_ref[...], preferred_element_type=F32))
    softplus = jnp.maximum(z, 0.0) + jnp.log(1.0 + jnp.exp(-jnp.abs(z)))
    lw = -jnp.exp(-softplus - 0.5)
    a = _sigmoid(a0 + jnp.dot(al.astype(BF16), alb_ref[...], preferred_element_type=F32))
    g = jnp.dot(_sigmoid(gl).astype(BF16), glb_ref[...], preferred_element_type=F32)
    if has_vres:
        vr = cs[:, o + 2 * LANE + GATE_LORA:o + 2 * LANE + GATE_LORA + vlb_ref.shape[0]]
        mix = _sigmoid(v0 + jnp.dot(vr.astype(BF16), vlb_ref[...], preferred_element_type=F32))
        v = v + (vfirst_ref[...] - v) * mix
    hs = hs_ref[...]
    kk = k * k_k
    ss = _dot_sel(kk * kk, hs)
    kk = kk / jnp.maximum(jnp.sqrt(ss), 1e-12)
    kh = k * (1.0 + (a - 1.0) * k_a)
    rk = _dot_sel(r * kh * r_k, hs)

    r_o[...] = r
    lw_o[...] = lw
    k_o[...] = kh
    v_o[...] = v
    al_o[...] = -kk
    be_o[...] = kk * a
    g_o[...] = g
    bo_o[...] = rk * v


def _rwkv_prep(cproj, mu, vecs, wlb, alb, glb, vlb, v_first, cw, tr=PREP_ROWS):
    bsz, seq, wc = cproj.shape
    tr = min(tr, seq)
    has_vres = vlb is not None
    hs = _head_sum_matrix(cw, W_HEAD_DIM)
    full = lambda a: pl.BlockSpec(a.shape, lambda b, t: (0,) * a.ndim)
    tok = pl.BlockSpec((None, tr, cw), lambda b, t: (b, t, 0))
    ins = [cproj, mu, vecs, wlb, alb, glb, hs]
    in_specs = [pl.BlockSpec((None, tr, wc), lambda b, t: (b, t, 0)),
                full(mu), full(vecs), full(wlb), full(alb), full(glb), full(hs)]
    if has_vres:
        ins += [vlb, v_first]
        in_specs += [full(vlb), tok]
    out = jax.ShapeDtypeStruct((bsz, seq, cw), F32)
    return pl.pallas_call(
        functools.partial(_rwkv_prep_kernel, cw=cw, has_vres=has_vres),
        grid=(bsz, seq // tr),
        in_specs=in_specs,
        out_specs=[tok] * 8,
        out_shape=[out] * 8,
        scratch_shapes=[pltpu.VMEM((SUBLANE, wc), F32)],
        compiler_params=_cparams(("parallel", "arbitrary")),
        name="rwkv_prep",
    )(*ins)


def _bdot(a, b):
    return jnp.dot(a.astype(BF16), b.astype(BF16), preferred_element_type=F32)


def _bdot_tn(a, b):
    return lax.dot_general(a.astype(BF16), b.astype(BF16), (((0,), (0,)), ((), ())),
                           preferred_element_type=F32)


def _rwkv_chunk_kernel(r_ref, lw_ref, k_ref, v_ref, al_ref, be_ref, g_o, y0_o, m_o, z0_o, *, tc, nck, npairs):
    row = lax.broadcasted_iota(I32, (tc, tc), 0)
    col = lax.broadcasted_iota(I32, (tc, tc), 1)
    incl = row >= col
    strict = row > col
    tri = incl.astype(BF16)
    eye = (row == col).astype(F32)
    lane = lax.broadcasted_iota(I32, (1, LANE), 1)
    mh0 = (lane // W_HEAD_DIM) == 0
    prow = lax.broadcasted_iota(I32, (LANE, LANE), 0)
    pcol = lax.broadcasted_iota(I32, (LANE, LANE), 1)
    same_head = (prow // W_HEAD_DIM) == (pcol // W_HEAD_DIM)

    lw_all = lw_ref[...]
    hi = lw_all.astype(BF16)
    rem = lw_all - hi.astype(F32)
    mid = rem.astype(BF16)
    lo = (rem - mid.astype(F32)).astype(BF16)
    cum_all = jnp.concatenate(
        [sum(jnp.dot(tri, part[c * tc:(c + 1) * tc], preferred_element_type=F32) for part in (hi, mid, lo))
         for c in range(nck)], axis=0)

    pairs = range(nck * npairs)
    heads = [(p, h) for p in pairs for h in range(2)]
    sls = [slice((p % npairs) * LANE, (p % npairs + 1) * LANE) for p in pairs]
    rws = [slice((p // npairs) * tc, (p // npairs + 1) * tc) for p in pairs]
    cum = [cum_all[rws[p], sls[p]] for p in pairs]
    tot = [c[tc - 1:tc, :] for c in cum]
    p_inv = [jnp.exp(-cum[p]) for p in pairs]
    p_end = [jnp.exp(tot[p] - cum[p]) for p in pairs]
    at = [al_ref[rws[p], sls[p]] * jnp.exp(cum[p] - lw_all[rws[p], sls[p]]) for p in pairs]
    rt = [r_ref[rws[p], sls[p]] * jnp.exp(cum[p]) for p in pairs]
    bh = [be_ref[rws[p], sls[p]] * p_end[p] for p in pairs]
    khat = [k_ref[rws[p], sls[p]] * p_end[p] for p in pairs]
    v = [v_ref[rws[p], sls[p]] for p in pairs]
    rhs = [jnp.concatenate([be_ref[rws[p], sls[p]] * p_inv[p], k_ref[rws[p], sls[p]] * p_inv[p]],
                           axis=0).astype(BF16) for p in pairs]

    a_ab, a_ak, qcat = {}, {}, {}
    for p, h in heads:
        mh = (lane // W_HEAD_DIM) == h
        lhs = jnp.concatenate([jnp.where(mh, at[p], 0.0), jnp.where(mh, rt[p], 0.0)], axis=0).astype(BF16)
        x = lax.dot_general(lhs, rhs[p], (((1,), (1,)), ((), ())), preferred_element_type=F32)
        a_ab[p, h] = jnp.where(strict, x[:tc, :tc], 0.0)
        a_ak[p, h] = jnp.where(strict, x[:tc, tc:], 0.0)
        qcat[p, h] = jnp.concatenate([jnp.where(incl, x[tc:, tc:], 0.0),
                                      jnp.where(incl, x[tc:, :tc], 0.0)], axis=1)

    t_inv = {ph: eye + a_ab[ph] for ph in heads}
    pw = {ph: _bdot(a_ab[ph], a_ab[ph]) for ph in heads}
    akv = {(p, h): _bdot(a_ak[p, h], v[p]) for p, h in heads}
    n = 2
    while n < tc:
        for ph in heads:
            res = _bdot(jnp.concatenate([pw[ph], t_inv[ph]], axis=0), pw[ph])
            t_inv[ph] = t_inv[ph] + res[tc:]
            pw[ph] = res[:tc]
        n *= 2
    tw = {(p, h): _bdot(t_inv[p, h], jnp.concatenate([at[p], akv[p, h]], axis=1)) for p, h in heads}

    zero = jnp.zeros((tc, LANE), F32)
    w = [jnp.where(mh0, tw[p, 0][:, :LANE], tw[p, 1][:, :LANE]) for p in pairs]
    u0 = [jnp.where(mh0, tw[p, 0][:, LANE:], tw[p, 1][:, LANE:]) for p in pairs]
    vu = [jnp.concatenate([jnp.concatenate([v[p], zero], axis=1),
                           jnp.concatenate([u0[p], w[p]], axis=1)], axis=0) for p in pairs]
    yg = {(p, h): _bdot(qcat[p, h], vu[p]) for p, h in heads}
    m_mat = [_bdot_tn(bh[p], w[p]) for p in pairs]
    z0 = [_bdot_tn(jnp.concatenate([bh[p], khat[p]], axis=0), jnp.concatenate([u0[p], v[p]], axis=0))
          for p in pairs]
    for p in pairs:
        y0_o[rws[p], sls[p]] = jnp.where(mh0, yg[p, 0][:, :LANE], yg[p, 1][:, :LANE])
        g_o[rws[p], sls[p]] = rt[p] + jnp.where(mh0, yg[p, 0][:, LANE:], yg[p, 1][:, LANE:])
        m_o[p // npairs, p % npairs] = jnp.where(same_head, m_mat[p], 0.0) + jnp.where(
            prow == pcol, jnp.broadcast_to(jnp.exp(tot[p]), (LANE, LANE)), 0.0)
        z0_o[p // npairs, p % npairs] = jnp.where(same_head, z0[p], 0.0)


def _rwkv_chunk_ops(r, lw, kh, v, alpha, beta, tc=CHUNK, nck=RWKV_CHUNKS):
    bsz, seq, cw = r.shape
    npairs = cw // LANE
    nc = seq // tc
    nck = min(nck, nc)
    tok = pl.BlockSpec((None, nck * tc, cw), lambda b, c: (b, c, 0))
    mat = pl.BlockSpec((None, nck, npairs, LANE, LANE), lambda b, c: (b, c, 0, 0, 0))
    tok_shape = jax.ShapeDtypeStruct((bsz, seq, cw), F32)
    mat_shape = jax.ShapeDtypeStruct((bsz, nc, npairs, LANE, LANE), F32)
    return pl.pallas_call(
        functools.partial(_rwkv_chunk_kernel, tc=tc, nck=nck, npairs=npairs),
        grid=(bsz, nc // nck),
        in_specs=[tok] * 6,
        out_specs=[tok, tok, mat, mat],
        out_shape=[tok_shape, tok_shape, mat_shape, mat_shape],
        compiler_params=_cparams(("parallel", "parallel")),
        name="rwkv_chunk_ops",
    )(r, lw, kh, v, alpha, beta)


def _rwkv_scan_kernel(g_ref, y0_ref, m_ref, z0_ref, bo_ref, gate_ref, ln_ref, hs_ref, o_ref, state_ref,
                      *, tc, nch, npairs):
    @pl.when(pl.program_id(1) == 0)
    def _():
        state_ref[...] = jnp.zeros(state_ref.shape, F32)

    hs = hs_ref[...]
    inv_n = 1.0 / W_HEAD_DIM
    sls = [slice(p * LANE, (p + 1) * LANE) for p in range(npairs)]
    st = [state_ref[p] for p in range(npairs)]
    for ch in range(nch):
        rows = slice(ch * tc, (ch + 1) * tc)
        gm = [_dot_x3(jnp.concatenate([g_ref[rows, sls[p]], m_ref[ch, p]], axis=0), st[p]) for p in range(npairs)]
        y = [gm[p][:tc] + y0_ref[rows, sls[p]] for p in range(npairs)]
        st = [gm[p][tc:] + z0_ref[ch, p] for p in range(npairs)]
        mu = [_dot_sel(y[p], hs) * inv_n for p in range(npairs)]
        d = [y[p] - mu[p] for p in range(npairs)]
        var = [_dot_sel(d[p] * d[p], hs) * inv_n for p in range(npairs)]
        for p in range(npairs):
            yn = d[p] * lax.rsqrt(var[p] + LNX_EPS) * ln_ref[0:1, sls[p]] + ln_ref[1:2, sls[p]]
            o_ref[rows, sls[p]] = ((yn + bo_ref[rows, sls[p]]) * gate_ref[rows, sls[p]]).astype(o_ref.dtype)
    for p in range(npairs):
        state_ref[p] = st[p]


def _rwkv_scan(g, y0, m, z0, bonus, gate, ln, tc=CHUNK, nch=RWKV_CHUNKS):
    bsz, seq, cw = g.shape
    npairs = cw // LANE
    nch = min(nch, seq // tc)
    hs = _head_sum_matrix(LANE, W_HEAD_DIM)
    tok = pl.BlockSpec((None, tc * nch, cw), lambda b, c: (b, c, 0))
    mat = pl.BlockSpec((None, nch, npairs, LANE, LANE), lambda b, c: (b, c, 0, 0, 0))
    return pl.pallas_call(
        functools.partial(_rwkv_scan_kernel, tc=tc, nch=nch, npairs=npairs),
        grid=(bsz, seq // (tc * nch)),
        in_specs=[tok, tok, mat, mat, tok, tok,
                  pl.BlockSpec(ln.shape, lambda b, c: (0, 0)),
                  pl.BlockSpec(hs.shape, lambda b, c: (0, 0))],
        out_specs=tok,
        out_shape=jax.ShapeDtypeStruct((bsz, seq, cw), BF16),
        scratch_shapes=[pltpu.VMEM((npairs, LANE, LANE), F32)],
        compiler_params=_cparams(("parallel", "arbitrary")),
        name="rwkv_scan",
    )(g, y0, m, z0, bonus, gate, ln, hs)


def _out_proj_kernel(x_ref, a_ref, b_ref, c_ref, w_ref, g_ref, o_ref, h_ref, *, wa, wb):
    acc = jnp.dot(a_ref[...], w_ref[0:wa, :], preferred_element_type=F32)
    acc = acc + jnp.dot(b_ref[...], w_ref[wa:wa + wb, :], preferred_element_type=F32)
    acc = acc + jnp.dot(c_ref[...], w_ref[wa + wb:, :], preferred_element_type=F32)
    xn = x_ref[...] + acc
    o_ref[...] = xn
    ms = jnp.mean(xn * xn, axis=-1, keepdims=True)
    h_ref[...] = (xn * lax.rsqrt(ms + RMS_EPS) * g_ref[...]).astype(h_ref.dtype)


def _out_proj(x2d, oa, ob, oc, w, g, tm=TM_NORM):
    m, d = x2d.shape
    wa, wb, wc = oa.shape[1], ob.shape[1], oc.shape[1]
    tm = min(tm, m)
    row = pl.BlockSpec((tm, d), lambda i: (i, 0))
    return pl.pallas_call(
        functools.partial(_out_proj_kernel, wa=wa, wb=wb),
        grid=(m // tm,),
        in_specs=[row,
                  pl.BlockSpec((tm, wa), lambda i: (i, 0)),
                  pl.BlockSpec((tm, wb), lambda i: (i, 0)),
                  pl.BlockSpec((tm, wc), lambda i: (i, 0)),
                  pl.BlockSpec((wa + wb + wc, d), lambda i: (0, 0)),
                  pl.BlockSpec((1, d), lambda i: (0, 0))],
        out_specs=[row, row],
        out_shape=[jax.ShapeDtypeStruct((m, d), F32), jax.ShapeDtypeStruct((m, d), BF16)],
        compiler_params=_cparams(("parallel",)),
        name="out_proj",
    )(x2d, oa, ob, oc, w, g.reshape(1, d).astype(F32))


def _gate_up_kernel(h_ref, wg_ref, wu_ref, o_ref):
    h = h_ref[...]
    gate = jnp.dot(h, wg_ref[...], preferred_element_type=F32)
    up = jnp.dot(h, wu_ref[...], preferred_element_type=F32)
    o_ref[...] = (gate / (1.0 + jnp.exp(-gate)) * up).astype(o_ref.dtype)


def _gate_up(h, w_gate_up, tm=TM, tf_cap=512):
    m, d = h.shape
    dff = w_gate_up.shape[1] // 2
    tm = min(tm, m)
    tf = _pick_tile(dff, tf_cap)
    nf = dff // tf
    return pl.pallas_call(
        _gate_up_kernel,
        grid=(m // tm, nf),
        in_specs=[pl.BlockSpec((tm, d), lambda i, j: (i, 0)),
                  pl.BlockSpec((d, tf), lambda i, j: (0, j)),
                  pl.BlockSpec((d, tf), lambda i, j: (0, j + nf))],
        out_specs=pl.BlockSpec((tm, tf), lambda i, j: (i, j)),
        out_shape=jax.ShapeDtypeStruct((m, dff), BF16),
        compiler_params=_cparams(("parallel", "arbitrary")),
        name="ffn_gate_up",
    )(h, w_gate_up, w_gate_up)


def _down_kernel(x_ref, a_ref, w_ref, o_ref):
    o_ref[...] = x_ref[...] + jnp.dot(a_ref[...], w_ref[...], preferred_element_type=F32)


def _down_proj(x2d, act, w, tm=TM, tn=512):
    m, d = x2d.shape
    dff = act.shape[1]
    tm = min(tm, m)
    tn = min(tn, d)
    return pl.pallas_call(
        _down_kernel,
        grid=(m // tm, d // tn),
        in_specs=[pl.BlockSpec((tm, tn), lambda i, j: (i, j)),
                  pl.BlockSpec((tm, dff), lambda i, j: (i, 0)),
                  pl.BlockSpec((dff, tn), lambda i, j: (0, j))],
        out_specs=pl.BlockSpec((tm, tn), lambda i, j: (i, j)),
        out_shape=jax.ShapeDtypeStruct((m, d), F32),
        compiler_params=_cparams(("parallel", "arbitrary")),
        name="ffn_down",
    )(x2d, act, w)


def _pad_cols(w, width):
    return jnp.pad(w, ((0, 0), (0, width - w.shape[1])))


def _pad_rows(w, height):
    return jnp.pad(w, ((0, height - w.shape[0]), (0, 0)))


def _pad_vec(v, width):
    return jnp.pad(v, (0, width - v.shape[0]))


def kernel(x, norm_mix_g, w_in, w_in_vres, rwkv_mu, rwkv_mu_vres, rwkv_w0, rwkv_w_lora_b, rwkv_a0,
           rwkv_a_lora_b, rwkv_v0, rwkv_v_lora_b, rwkv_g_lora_b, rwkv_k_k, rwkv_k_a, rwkv_r_k,
           rwkv_lnx_g, rwkv_lnx_b, w_out, norm_ffn_g, w_gate_up, w_down, final_norm_g):
    bsz, seq, d_model = x.shape
    depth = w_in.shape[0]
    m = bsz * seq
    d_mix = w_out.shape[1]
    a_w = d_mix // 2
    qi_w = IDX_HEADS * IDX_DIM
    b_v_w = d_mix // 4
    b_qk_w = b_v_w // 2
    c_w = d_mix // 4
    r_qk_dim = b_qk_w // R_HEADS
    top_k = min(TOPK_MAX, seq // 4)
    vres_pad = 2 * LANE

    o_q, o_k, o_v = 0, a_w, 2 * a_w
    o_qi = 3 * a_w
    o_ki = o_qi + qi_w
    o_wi = o_ki + IDX_DIM
    o_bq = o_wi + IDX_HEADS
    o_bk = o_bq + b_qk_w
    o_bv = o_bk + b_qk_w
    o_bg = o_bv + b_v_w
    o_c = o_bg + b_v_w
    o_wl = o_c + 3 * c_w
    o_al = o_wl + DECAY_LORA
    o_gl = o_al + AAA_LORA

    a_rot = A_HEAD_DIM // ROPE_FRAC
    tq_ = _rope_tables(seq, A_HEAD_DIM, a_rot, ROPE_THETA, scale=A_HEAD_DIM ** -0.5 * math.log2(math.e))
    tk_ = _rope_tables(seq, A_HEAD_DIM, a_rot, ROPE_THETA)
    tab_q = tuple(t[None] for t in tq_)
    tab_k = tuple(t[None] for t in tk_)
    i_rot = IDX_DIM // ROPE_FRAC
    tab_qi = tuple(t[None] for t in _rope_tables(seq, IDX_DIM, i_rot, ROPE_THETA))
    lane = jnp.arange(LANE)
    kiwi_pass = jnp.where(lane < IDX_DIM, 1.0, IDX_W_SCALE).astype(F32)
    c_kw, s1_kw, s2_kw = _rope_tables(seq, LANE, i_rot, ROPE_THETA, pass_scale=kiwi_pass)
    tab_kiwi = (c_kw[None], s1_kw[None], s2_kw[None])
    tbq = _rope_tables(seq, r_qk_dim, r_qk_dim, R_THETA)
    tbk = _rope_tables(seq, r_qk_dim, r_qk_dim, R_THETA, scale=r_qk_dim ** -0.5)
    tab_bqk = tuple(jnp.stack([a, b]) for a, b in zip(tbq, tbk))

    x2d = x.reshape(m, d_model)
    v_first = None
    for l in range(depth):
        wl_ = w_in[l]
        cols = lambda o, n: wl_[:, o:o + n]
        w_q = cols(o_q, a_w).astype(BF16)
        w_k = cols(o_k, a_w).astype(BF16)
        w_v = cols(o_v, a_w).astype(BF16)
        w_bv = cols(o_bv, b_v_w).astype(BF16)
        w_qi = cols(o_qi, qi_w).astype(BF16)
        w_kiwi = _pad_cols(cols(o_ki, IDX_DIM + IDX_HEADS), LANE).astype(BF16)
        w_bqk = cols(o_bq, 2 * b_qk_w).astype(BF16)
        w_bg = cols(o_bg, b_v_w).astype(BF16)
        vres_w = (w_in_vres[l - 1] if l > 0 else jnp.zeros((d_model, MV_LORA), F32))
        w_c = jnp.concatenate([cols(o_c, 3 * c_w), _pad_cols(cols(o_wl, DECAY_LORA), LANE),
                               _pad_cols(cols(o_al, AAA_LORA), LANE), cols(o_gl, GATE_LORA),
                               _pad_cols(vres_w, vres_pad)], axis=1).astype(BF16)
        mu = rwkv_mu[l]
        mu_vres = rwkv_mu_vres[l - 1] if l > 0 else jnp.zeros((MV_LORA,), F32)
        mu_c = jnp.concatenate([mu[:3 * c_w], _pad_vec(mu[3 * c_w:3 * c_w + DECAY_LORA], LANE),
                                _pad_vec(mu[3 * c_w + DECAY_LORA:3 * c_w + DECAY_LORA + AAA_LORA], LANE),
                                mu[3 * c_w + DECAY_LORA + AAA_LORA:], _pad_vec(mu_vres, vres_pad)])[None, :]

        h = _rmsnorm(x2d, norm_mix_g[l], BF16)
        qt = _mm_rope(h, w_q, tab_q, a_rot // 2, seq, a_w, BF16, transpose=True)
        ak = _mm_rope(h, w_k, tab_k, a_rot // 2, seq, a_w, BF16).reshape(bsz, seq, a_w)
        vt = _mm_plain(h, w_v, BF16, seq, transpose=True, pad_rows=ONES_ROWS)
        bv = _mm_plain(h, w_bv, BF16, seq, tn_cap=512).reshape(bsz, seq, b_v_w)
        qit = _mm_rope(h, w_qi, tab_qi, i_rot // 2, seq, qi_w, BF16, transpose=True)
        kiwi = _mm_rope(h, w_kiwi, tab_kiwi, i_rot // 2, seq, LANE, F32).reshape(bsz, seq, LANE)
        bqk = _mm_rope(h, w_bqk, tab_bqk, r_qk_dim // 2, seq, b_qk_w, BF16).reshape(bsz, seq, 2 * b_qk_w)
        bg = _mm_plain(h, w_bg, F32, seq).reshape(bsz, seq, b_v_w)
        cproj = _mm_plain(h, w_c, F32, seq, tn_cap=768).reshape(bsz, seq, w_c.shape[1])

        ki = kiwi[:, :, :IDX_DIM].astype(BF16)
        wit = jnp.swapaxes(kiwi[:, :, IDX_DIM:IDX_DIM + IDX_HEADS], 1, 2)
        bias_t = _indexer_bias(ki, qit, wit, top_k)
        out_a = _masked_attention(qt, ak, vt, bias_t)

        out_b = _retention(bqk, bv, bg)

        vecs = jnp.stack([rwkv_w0[l], rwkv_a0[l], rwkv_k_k[l], rwkv_k_a[l], rwkv_r_k[l].reshape(-1),
                          rwkv_v0[l - 1] if l > 0 else jnp.zeros((c_w,), F32),
                          jnp.zeros((c_w,), F32), jnp.zeros((c_w,), F32)])
        wlb = _pad_rows(rwkv_w_lora_b[l], LANE).astype(BF16)
        alb = _pad_rows(rwkv_a_lora_b[l], LANE).astype(BF16)
        glb = rwkv_g_lora_b[l].astype(BF16)
        vlb = _pad_rows(rwkv_v_lora_b[l - 1], vres_pad).astype(BF16) if l > 0 else None
        r_, lw_, kh_, v_, al_, be_, g_, bo_ = _rwkv_prep(cproj, mu_c, vecs, wlb, alb, glb, vlb, v_first, c_w)
        if l == 0:
            v_first = v_
        gm, y0, mm, z0 = _rwkv_chunk_ops(r_, lw_, kh_, v_, al_, be_)
        ln = jnp.stack([rwkv_lnx_g[l], rwkv_lnx_b[l]] + [jnp.zeros((c_w,), F32)] * 6)
        out_c = _rwkv_scan(gm, y0, mm, z0, bo_, g_, ln)

        x2d, h = _out_proj(x2d, out_a.reshape(m, a_w), out_b.reshape(m, b_v_w), out_c.reshape(m, c_w),
                           w_out[l].astype(BF16), norm_ffn_g[l])
        act = _gate_up(h, w_gate_up[l].astype(BF16))
        x2d = _down_proj(x2d, act, w_down[l].astype(BF16))

    return _rmsnorm(x2d, final_norm_g, F32).reshape(bsz, seq, d_model)
```
